```python
import jax, jax.numpy as jnp
from jax import lax
import numpy as np

D_MODEL = 1024
BATCH = 8
SEQ = 2048
DEPTH = 2
DEC_BATCH = 128
DEC_SEQ = 4
PAST_LEN = 16384
PAGE_SIZE = 128

HEAD_DIM = 64
RET_W = D_MODEL // 4
RET_H = RET_W // HEAD_DIM
RET_CHUNK = 128
ROPE_BASE = 10000.0
RWKV_W = D_MODEL // 2
RWKV_H = RWKV_W // HEAD_DIM
RWKV_LORA_DECAY = 32
RWKV_LORA_A = 32
RWKV_LORA_GATE = 64
RWKV_PROJ = 3 * RWKV_W + RWKV_LORA_DECAY + RWKV_LORA_A + RWKV_LORA_GATE
RWKV_GN_EPS = 64e-5
LRU_W = D_MODEL // 4
LRU_H = LRU_W // HEAD_DIM
LRU_C = 8.0
CONV_WIDTH = 4
D_MIX = RET_W + RWKV_W + LRU_W
D_PROJ = 4 * RET_W + RWKV_PROJ + 2 * LRU_W
N_EXPERTS = 32
TOP_K = 4
D_EXPERT = D_MODEL
SWIGLU_LIMIT = 7.0
SWIGLU_ALPHA = 1.702
MOE_BLOCK = 128
LN_EPS = 1e-5
DN_ALPHA = (2.0 * DEPTH) ** 0.25
DN_BETA = (8.0 * DEPTH) ** -0.25

kernel_name = 'hybrid_retention_rwkv7_rglru_moe_step'


def _normalize(x, eps):
    xf = x.astype(jnp.float32)
    xc = xf - jnp.mean(xf, axis=-1, keepdims=True)
    return xc * lax.rsqrt(jnp.mean(xc * xc, axis=-1, keepdims=True) + eps)


def _layer_norm(x, g, b):
    y = _normalize(x, LN_EPS) * g.astype(jnp.float32) + b.astype(jnp.float32)
    return y.astype(x.dtype)


def _rope(x, pos):
    half = x.shape[-1] // 2
    inv = ROPE_BASE ** (-jnp.arange(half, dtype=jnp.float32) / half)
    ang = pos[:, None] * inv[None, :]
    cos = jnp.cos(ang)[None, :, None, :]
    sin = jnp.sin(ang)[None, :, None, :]
    xf = x.astype(jnp.float32)
    x1, x2 = xf[..., :half], xf[..., half:]
    return jnp.concatenate([x1 * cos - x2 * sin, x1 * sin + x2 * cos], axis=-1)


def _retention(q, k, v, s0):
    B, T, H, _ = q.shape
    C = RET_CHUNK if T % RET_CHUNK == 0 else T
    n = T // C
    lg = jnp.log1p(-jnp.exp2(-5.0 - jnp.arange(H, dtype=jnp.float32)))
    idx = jnp.arange(C, dtype=jnp.float32)
    rel = idx[:, None] - idx[None, :]
    mask = jnp.where(rel[None] >= 0, jnp.exp(jnp.maximum(rel, 0.0)[None] * lg[:, None, None]), 0.0)
    q_dec = jnp.exp((idx[:, None] + 1.0) * lg[None, :])
    k_dec = jnp.exp((C - 1.0 - idx)[:, None] * lg[None, :])
    c_dec = jnp.exp(C * lg)

    def to_chunks(t):
        return t.reshape(B, n, C, H, t.shape[-1]).transpose(1, 0, 2, 3, 4)

    def step(S, inp):
        qc, kc, vc = inp
        sc = jnp.einsum('bihd,bjhd->bhij', qc, kc) * mask[None]
        o = jnp.einsum('bhij,bjhe->bihe', sc, vc) + jnp.einsum('bihd,bhde->bihe', qc, S) * q_dec[None, :, :, None]
        S = S * c_dec[None, :, None, None] + jnp.einsum('bjhd,bjhe->bhde', kc * k_dec[None, :, :, None], vc)
        return S, o

    S, o = lax.scan(step, s0, (to_chunks(q), to_chunks(k), to_chunks(v)))
    o = o.transpose(1, 0, 2, 3, 4).reshape(B, T, H, -1)
    return o, S


def _rwkv7_recurrence(r, w, k, v, kk, a, s0):
    def step(S, inp):
        rt, wt, kt, vt, kkt, at = inp
        sa = jnp.einsum('bhvk,bhk->bhv', S, kkt)
        S = S * wt[:, :, None, :] - sa[..., None] * (kkt * at)[:, :, None, :] + vt[..., None] * kt[:, :, None, :]
        return S, jnp.einsum('bhvk,bhk->bhv', S, rt)

    seq = tuple(t.transpose(1, 0, 2, 3) for t in (r, w, k, v, kk, a))
    S, y = lax.scan(step, s0, seq)
    return y.transpose(1, 0, 2, 3), S


def _linear_scan(a, b, h0):
    b = b.at[:, 0].add(a[:, 0] * h0)

    def comb(lhs, rhs):
        return (lhs[0] * rhs[0], rhs[0] * lhs[1] + rhs[1])

    _, h = lax.associative_scan(comb, (a, b), axis=1)
    return h


def _token_mixers(h, pos0, st, p):
    B, T, _ = h.shape
    f = lambda t: t.astype(jnp.float32)
    proj = h @ p['w_in']
    ret, rw, lru = jnp.split(proj, [4 * RET_W, 4 * RET_W + RWKV_PROJ], axis=-1)

    q, k, v, g = jnp.split(ret, 4, axis=-1)
    pos = pos0 + jnp.arange(T, dtype=jnp.float32)
    q = _rope(q.reshape(B, T, RET_H, HEAD_DIM), pos)
    k = _rope(k.reshape(B, T, RET_H, HEAD_DIM), pos) * (HEAD_DIM ** -0.5)
    o, s_ret = _retention(q, k, f(v).reshape(B, T, RET_H, HEAD_DIM), f(st['ret']))
    y_ret = (jax.nn.silu(f(g)) * _normalize(o, LN_EPS).reshape(B, T, RET_W)).astype(h.dtype)

    prev = jnp.concatenate([st['shift'][:, None, :].astype(rw.dtype), rw[:, :-1]], axis=1)
    s_shift = rw[:, -1]
    rwm = f(rw + (prev - rw) * p['mix'])
    cuts = [RWKV_W, 2 * RWKV_W, 3 * RWKV_W, 3 * RWKV_W + RWKV_LORA_DECAY, 3 * RWKV_W + RWKV_LORA_DECAY + RWKV_LORA_A]
    r, kr, vr, wd, ad, gd = jnp.split(rwm, cuts, axis=-1)
    w_log = -jax.nn.softplus(-(f(p['w0']) + jnp.tanh(wd) @ f(p['w_up']))) - 0.5
    decay = jnp.exp(-jnp.exp(w_log))
    a = jax.nn.sigmoid(f(p['a0']) + ad @ f(p['a_up']))
    gate = jax.nn.sigmoid(gd) @ f(p['g_up'])
    hd = lambda t: t.reshape(B, T, RWKV_H, HEAD_DIM)
    kk = hd(kr * f(p['k_k']))
    kk = kk * lax.rsqrt(jnp.maximum(jnp.sum(kk * kk, axis=-1, keepdims=True), 1e-24))
    kr = kr * (1.0 + (a - 1.0) * f(p['k_a']))
    rh, kh, vh = hd(r), hd(kr), hd(vr)
    yh, s_rwkv = _rwkv7_recurrence(rh, hd(decay), kh, vh, kk, hd(a), f(st['rwkv']))
    yh = _normalize(yh, RWKV_GN_EPS) * f(p['gn_g']).reshape(RWKV_H, HEAD_DIM) + f(p['gn_b']).reshape(RWKV_H, HEAD_DIM)
    bonus = jnp.sum(rh * kh * f(p['r_k']).reshape(RWKV_H, HEAD_DIM), axis=-1, keepdims=True) * vh
    y_rwkv = ((yh + bonus).reshape(B, T, RWKV_W) * gate).astype(h.dtype)

    gbr, xbr = jnp.split(lru, 2, axis=-1)
    xpad = jnp.concatenate([st['conv'].astype(xbr.dtype), xbr], axis=1)
    s_conv = xpad[:, T:]
    xc = f(p['conv_b']) + sum(f(xpad[:, j:j + T]) * f(p['conv_w'][j]) for j in range(CONV_WIDTH))
    xb = xc.reshape(B, T, LRU_H, HEAD_DIM)
    gate_a = jax.nn.sigmoid(jnp.einsum('bthi,hij->bthj', xb, f(p['wa'])).reshape(B, T, LRU_W) + f(p['ba']))
    gate_x = jax.nn.sigmoid(jnp.einsum('bthi,hij->bthj', xb, f(p['wx'])).reshape(B, T, LRU_W) + f(p['bx']))
    log_a = -LRU_C * gate_a * jax.nn.softplus(-f(p['lam']))
    hs = _linear_scan(jnp.exp(log_a), xc * gate_x * jnp.sqrt(-jnp.expm1(2.0 * log_a)), f(st['lru']))
    s_lru = hs[:, -1]
    y_lru = (hs * jax.nn.gelu(f(gbr), approximate=True)).astype(h.dtype)

    y = jnp.concatenate([y_ret, y_rwkv, y_lru], axis=-1) @ p['w_out']
    return y, (s_ret, s_rwkv, s_shift, s_lru, s_conv)


def _moe(x, p):
    B, T, D = x.shape
    xt = x.reshape(-1, D)
    N = xt.shape[0]
    A = N * TOP_K
    logits = (xt @ p['w_router'] + p['b_router']).astype(jnp.float32)
    top_v, top_e = lax.top_k(logits, TOP_K)
    gates = jax.nn.softmax(top_v, axis=-1)
    flat_e = top_e.reshape(-1)
    order = jnp.argsort(flat_e)
    sorted_e = flat_e[order]
    sorted_tok = order // TOP_K
    counts = jnp.bincount(flat_e, length=N_EXPERTS)
    padded = (counts + MOE_BLOCK - 1) // MOE_BLOCK * MOE_BLOCK
    pad_end = jnp.cumsum(padded)
    pad_start = pad_end - padded
    start = jnp.cumsum(counts) - counts
    dest_sorted = (pad_start[sorted_e] + jnp.arange(A) - start[sorted_e]).astype(jnp.int32)
    n_blocks = -(-(A + N_EXPERTS * (MOE_BLOCK - 1)) // MOE_BLOCK)
    P = n_blocks * MOE_BLOCK
    buf_tok = jnp.zeros((P,), jnp.int32).at[dest_sorted].set(sorted_tok.astype(jnp.int32))
    blk_e = jnp.minimum(jnp.searchsorted(pad_end, jnp.arange(n_blocks) * MOE_BLOCK, side='right'), N_EXPERTS - 1)

    def expert_block(args):
        tok, e = args
        gu = xt[tok] @ p['w_gu'][e] + p['b_gu'][e]
        g, u = gu[:, :D_EXPERT], gu[:, D_EXPERT:]
        g = jnp.minimum(g, SWIGLU_LIMIT)
        u = jnp.clip(u, -SWIGLU_LIMIT, SWIGLU_LIMIT)
        hdn = (u + 1.0) * g * jax.nn.sigmoid(SWIGLU_ALPHA * g)
        return hdn @ p['w_down'][e] + p['b_down'][e]

    yb = lax.map(expert_block, (buf_tok.reshape(n_blocks, MOE_BLOCK), blk_e)).reshape(P, D)
    dest = jnp.zeros((A,), jnp.int32).at[order].set(dest_sorted)
    y = jnp.einsum('nkd,nk->nd', yb[dest].reshape(N, TOP_K, D), gates.astype(yb.dtype))
    return y.reshape(B, T, D)


def _layer(x, pos0, st, p):
    y, new_st = _token_mixers(x, pos0, st, p)
    x = _layer_norm(DN_ALPHA * x + y, p['ln1_g'], p['ln1_b'])
    x = _layer_norm(DN_ALPHA * x + _moe(x, p), p['ln2_g'], p['ln2_b'])
    return x, new_st


def setup_inputs(seed: int = 0) -> dict:
    key = jax.random.key(seed)
    ks = iter(jax.random.split(key, 48))
    nrm = lambda shape, scale: jax.random.normal(next(ks), shape, jnp.float32) * scale
    L = DEPTH
    x_prompt = nrm((BATCH, SEQ, D_MODEL), 1.0)
    x_sample = nrm((DEC_BATCH, DEC_SEQ, D_MODEL), 1.0)
    state_ret = nrm((L, DEC_BATCH, RET_H, HEAD_DIM, HEAD_DIM), 0.3)
    state_rwkv = nrm((L, DEC_BATCH, RWKV_H, HEAD_DIM, HEAD_DIM), 0.3)
    state_rwkv_shift = nrm((L, DEC_BATCH, RWKV_PROJ), 1.0)
    state_lru = nrm((L, DEC_BATCH, LRU_W), 0.5)
    state_conv = nrm((L, DEC_BATCH, CONV_WIDTH - 1, LRU_W), 1.0)
    w_in = nrm((L, D_MODEL, D_PROJ), D_MODEL ** -0.5)
    w_out = nrm((L, D_MIX, D_MODEL), DN_BETA * D_MIX ** -0.5)
    ln1_g = 1.0 + nrm((L, D_MODEL), 0.02)
    ln1_b = nrm((L, D_MODEL), 0.02)
    ln2_g = 1.0 + nrm((L, D_MODEL), 0.02)
    ln2_b = nrm((L, D_MODEL), 0.02)
    rwkv_mix = jax.random.uniform(next(ks), (L, RWKV_PROJ), jnp.float32)
    rwkv_w0 = jax.random.uniform(next(ks), (L, RWKV_W), jnp.float32, -7.0, -2.0)
    rwkv_w_up = nrm((L, RWKV_LORA_DECAY, RWKV_W), 0.1)
    rwkv_a0 = nrm((L, RWKV_W), 0.1)
    rwkv_a_up = nrm((L, RWKV_LORA_A, RWKV_W), 0.1)
    rwkv_g_up = nrm((L, RWKV_LORA_GATE, RWKV_W), RWKV_LORA_GATE ** -0.5)
    rwkv_k_k = 0.85 + nrm((L, RWKV_W), 0.02)
    rwkv_k_a = 1.0 + nrm((L, RWKV_W), 0.02)
    rwkv_r_k = nrm((L, RWKV_W), 0.1)
    rwkv_gn_g = 1.0 + nrm((L, RWKV_W), 0.02)
    rwkv_gn_b = nrm((L, RWKV_W), 0.02)
    lru_conv_w = nrm((L, CONV_WIDTH, LRU_W), CONV_WIDTH ** -0.5)
    lru_conv_b = nrm((L, LRU_W), 0.02)
    lru_wa = nrm((L, LRU_H, HEAD_DIM, HEAD_DIM), HEAD_DIM ** -0.5)
    lru_ba = nrm((L, LRU_W), 0.02)
    lru_wx = nrm((L, LRU_H, HEAD_DIM, HEAD_DIM), HEAD_DIM ** -0.5)
    lru_bx = nrm((L, LRU_W), 0.02)
    u = jax.random.uniform(next(ks), (L, LRU_W), jnp.float32, 0.9, 0.999)
    a_base = u ** (1.0 / LRU_C)
    lru_lambda = jnp.log(a_base) - jnp.log1p(-a_base)
    moe_w_router = nrm((L, D_MODEL, N_EXPERTS), D_MODEL ** -0.5)
    moe_b_router = nrm((L, N_EXPERTS), 0.01)
    moe_w_gate_up = nrm((L, N_EXPERTS, D_MODEL, 2 * D_EXPERT), D_MODEL ** -0.5)
    moe_b_gate_up = nrm((L, N_EXPERTS, 2 * D_EXPERT), 0.01)
    moe_w_down = nrm((L, N_EXPERTS, D_EXPERT, D_MODEL), DN_BETA * D_EXPERT ** -0.5)
    moe_b_down = nrm((L, N_EXPERTS, D_MODEL), 0.01)
    return {'x_prompt': x_prompt, 'x_sample': x_sample, 'state_ret': state_ret, 'state_rwkv': state_rwkv,
            'state_rwkv_shift': state_rwkv_shift, 'state_lru': state_lru, 'state_conv': state_conv,
            'w_in': w_in, 'w_out': w_out, 'ln1_g': ln1_g, 'ln1_b': ln1_b, 'ln2_g': ln2_g, 'ln2_b': ln2_b,
            'rwkv_mix': rwkv_mix, 'rwkv_w0': rwkv_w0, 'rwkv_w_up': rwkv_w_up, 'rwkv_a0': rwkv_a0,
            'rwkv_a_up': rwkv_a_up, 'rwkv_g_up': rwkv_g_up, 'rwkv_k_k': rwkv_k_k, 'rwkv_k_a': rwkv_k_a,
            'rwkv_r_k': rwkv_r_k, 'rwkv_gn_g': rwkv_gn_g, 'rwkv_gn_b': rwkv_gn_b,
            'lru_conv_w': lru_conv_w, 'lru_conv_b': lru_conv_b, 'lru_wa': lru_wa, 'lru_ba': lru_ba,
            'lru_wx': lru_wx, 'lru_bx': lru_bx, 'lru_lambda': lru_lambda,
            'moe_w_router': moe_w_router, 'moe_b_router': moe_b_router, 'moe_w_gate_up': moe_w_gate_up,
            'moe_b_gate_up': moe_b_gate_up, 'moe_w_down': moe_w_down, 'moe_b_down': moe_b_down}


def reference(x_prompt, x_sample, state_ret, state_rwkv, state_rwkv_shift, state_lru, state_conv,
              w_in, w_out, ln1_g, ln1_b, ln2_g, ln2_b,
              rwkv_mix, rwkv_w0, rwkv_w_up, rwkv_a0, rwkv_a_up, rwkv_g_up, rwkv_k_k, rwkv_k_a, rwkv_r_k,
              rwkv_gn_g, rwkv_gn_b, lru_conv_w, lru_conv_b, lru_wa, lru_ba, lru_wx, lru_bx, lru_lambda,
              moe_w_router, moe_b_router, moe_w_gate_up, moe_b_gate_up, moe_w_down, moe_b_down):
    bp = x_prompt.shape[0]
    hp, hs = x_prompt, x_sample
    new_p, new_s = [], []
    for l in range(DEPTH):
        p = {'w_in': w_in[l], 'w_out': w_out[l], 'ln1_g': ln1_g[l], 'ln1_b': ln1_b[l],
             'ln2_g': ln2_g[l], 'ln2_b': ln2_b[l], 'mix': rwkv_mix[l], 'w0': rwkv_w0[l],
             'w_up': rwkv_w_up[l], 'a0': rwkv_a0[l], 'a_up': rwkv_a_up[l], 'g_up': rwkv_g_up[l],
             'k_k': rwkv_k_k[l], 'k_a': rwkv_k_a[l], 'r_k': rwkv_r_k[l], 'gn_g': rwkv_gn_g[l],
             'gn_b': rwkv_gn_b[l], 'conv_w': lru_conv_w[l], 'conv_b': lru_conv_b[l], 'wa': lru_wa[l],
             'ba': lru_ba[l], 'wx': lru_wx[l], 'bx': lru_bx[l], 'lam': lru_lambda[l],
             'w_router': moe_w_router[l], 'b_router': moe_b_router[l], 'w_gu': moe_w_gate_up[l],
             'b_gu': moe_b_gate_up[l], 'w_down': moe_w_down[l], 'b_down': moe_b_down[l]}
        st_p = {'ret': jnp.zeros((bp, RET_H, HEAD_DIM, HEAD_DIM), jnp.float32),
                'rwkv': jnp.zeros((bp, RWKV_H, HEAD_DIM, HEAD_DIM), jnp.float32),
                'shift': jnp.zeros((bp, RWKV_PROJ), x_prompt.dtype),
                'lru': jnp.zeros((bp, LRU_W), jnp.float32),
                'conv': jnp.zeros((bp, CONV_WIDTH - 1, LRU_W), x_prompt.dtype)}
        st_s = {'ret': state_ret[l], 'rwkv': state_rwkv[l], 'shift': state_rwkv_shift[l],
                'lru': state_lru[l], 'conv': state_conv[l]}
        hp, sp = _layer(hp, 0, st_p, p)
        hs, ss = _layer(hs, PAST_LEN, st_s, p)
        new_p.append(sp)
        new_s.append(ss)
    ret_p = jnp.stack([s[0] for s in new_p])
    ret_s = jnp.stack([s[0] for s in new_s])
    rwkv_p = jnp.stack([s[1] for s in new_p])
    rwkv_s = jnp.stack([s[1] for s in new_s])
    shift_p = jnp.stack([s[2] for s in new_p])
    shift_s = jnp.stack([s[2] for s in new_s])
    lru_p = jnp.stack([s[3] for s in new_p])
    lru_s = jnp.stack([s[3] for s in new_s])
    conv_p = jnp.stack([s[4] for s in new_p])
    conv_s = jnp.stack([s[4] for s in new_s])
    return (hp, hs, ret_p, ret_s, rwkv_p, rwkv_s, shift_p, shift_s, lru_p, lru_s, conv_p, conv_s)
```

```python
import functools

import jax
import jax.numpy as jnp
import numpy as np
from jax import lax
from jax.experimental import pallas as pl
from jax.experimental.pallas import tpu as pltpu

f32 = jnp.float32
bf16 = jnp.bfloat16
i32 = jnp.int32

D_MODEL = 1024
HEAD_DIM = 64
RET_W = 256
RET_H = 4
RET_CHUNK = 128
ROPE_BASE = 10000.0
RWKV_W = 512
RWKV_H = 8
RWKV_PROJ = 1664
RWKV_LORA_COL = 1536
RWKV_GN_EPS = 64e-5
RWKV_CHUNK = 64
LRU_W = 256
LRU_C = 8.0
CONV_WIDTH = 4
D_PROJ = 3200
N_EXPERTS = 32
TOP_K = 4
D_EXPERT = 1024
SWIGLU_LIMIT = 7.0
SWIGLU_ALPHA = 1.702
LN_EPS = 1e-5
DEPTH = 2
DN_ALPHA = (2.0 * DEPTH) ** 0.25

LANES = 128
SUBLANES = 8
SAMPLE_T_PAD = 16
TOKEN_TILE = 512
MOE_BLOCK = 256
COMBINE_TILE = 256
VMEM_LIMIT = 56 * 1024 * 1024

_HIGHEST = lax.Precision.HIGHEST
_NT = (((1,), (1,)), ((), ()))
_TN = (((0,), (0,)), ((), ()))


def _params(*sem):
    return pltpu.CompilerParams(dimension_semantics=sem, vmem_limit_bytes=VMEM_LIMIT)


def _mm(a, b):
    return jnp.dot(a.astype(bf16), b.astype(bf16), preferred_element_type=f32)


def _mm_nt(a, b):
    return lax.dot_general(a.astype(bf16), b.astype(bf16), _NT, preferred_element_type=f32)


def _mm_tn(a, b):
    return lax.dot_general(a.astype(bf16), b.astype(bf16), _TN, preferred_element_type=f32)


def _softplus(x):
    return jnp.maximum(x, 0.0) + jnp.log(1.0 + jnp.exp(-jnp.abs(x)))


def _sigmoid(x):
    return 1.0 / (1.0 + jnp.exp(-x))


def _half_masks():
    lane = lax.broadcasted_iota(i32, (1, LANES), 1)
    m0 = (lane < HEAD_DIM).astype(f32)
    return m0, 1.0 - m0


def _seg_mean(x, m0, m1):
    s0 = jnp.sum(x * m0, axis=-1, keepdims=True)
    s1 = jnp.sum(x * m1, axis=-1, keepdims=True)
    return (m0 * s0 + m1 * s1) * (1.0 / HEAD_DIM)


def _seg_sum(x, m0, m1):
    s0 = jnp.sum(x * m0, axis=-1, keepdims=True)
    s1 = jnp.sum(x * m1, axis=-1, keepdims=True)
    return m0 * s0 + m1 * s1


def _block_diag_mask():
    r = lax.broadcasted_iota(i32, (LANES, LANES), 0) // HEAD_DIM
    c = lax.broadcasted_iota(i32, (LANES, LANES), 1) // HEAD_DIM
    return (r == c).astype(f32)


def _layer_norm_rows(z, g, b):
    mu = jnp.mean(z, axis=-1, keepdims=True)
    zc = z - mu
    var = jnp.mean(zc * zc, axis=-1, keepdims=True)
    return zc * lax.rsqrt(var + LN_EPS) * g + b


def _in_proj_kernel(x_ref, w_ref, ret_ref, rw_ref, lru_ref):
    xb = x_ref[...].astype(bf16)
    c0, c1 = 4 * RET_W, 4 * RET_W + RWKV_PROJ
    ret_ref[...] = jnp.dot(xb, w_ref[:, :c0], preferred_element_type=f32)
    rw_ref[...] = jnp.dot(xb, w_ref[:, c0:c1], preferred_element_type=f32)
    lru_ref[...] = jnp.dot(xb, w_ref[:, c1:], preferred_element_type=f32)


def _in_proj(x, w_bf):
    n = x.shape[0]
    tm = TOKEN_TILE
    row = lambda i: (i, 0)
    return pl.pallas_call(
        _in_proj_kernel,
        grid=(n // tm,),
        in_specs=[pl.BlockSpec((tm, D_MODEL), row),
                  pl.BlockSpec((D_MODEL, D_PROJ), lambda i: (0, 0))],
        out_specs=[pl.BlockSpec((tm, 4 * RET_W), row),
                   pl.BlockSpec((tm, RWKV_PROJ), row),
                   pl.BlockSpec((tm, 2 * LRU_W), row)],
        out_shape=[jax.ShapeDtypeStruct((n, 4 * RET_W), f32),
                   jax.ShapeDtypeStruct((n, RWKV_PROJ), f32),
                   jax.ShapeDtypeStruct((n, 2 * LRU_W), f32)],
        compiler_params=_params("arbitrary"),
        name="in_proj",
    )(x, w_bf)


def _rope_tables(pos):
    half = HEAD_DIM // 2
    inv = ROPE_BASE ** (-jnp.arange(half, dtype=f32) / half)
    ang = pos[:, None] * inv[None, :]
    cos, sin = jnp.cos(ang), jnp.sin(ang)
    cos_f = jnp.tile(jnp.concatenate([cos, cos], axis=-1), (1, RET_H))
    sin_f = jnp.tile(jnp.concatenate([-sin, sin], axis=-1), (1, RET_H))
    return cos_f, sin_f


def _retention_tables(length, n_valid):
    lg = jnp.log1p(-jnp.exp2(-5.0 - jnp.arange(RET_H, dtype=f32)))
    idx = jnp.arange(length, dtype=f32)
    rel = idx[:, None] - idx[None, :]
    mask = jnp.where(rel[None] >= 0, jnp.exp(jnp.maximum(rel, 0.0)[None] * lg[:, None, None]), 0.0)
    q_dec = jnp.exp((idx[:, None] + 1.0) * lg[None, :])
    k_dec = jnp.where(idx[:, None] < n_valid, jnp.exp((n_valid - 1.0 - idx)[:, None] * lg[None, :]), 0.0)
    c_dec = jnp.exp(n_valid * lg)[None, :]
    rep = lambda t: jnp.repeat(t, HEAD_DIM, axis=-1)
    return mask, rep(q_dec), rep(k_dec), rep(c_dec)


def _retention_kernel(ret_ref, cos_ref, sin_ref, mask_ref, qdec_ref, kdec_ref, cdec_ref, s0_ref,
                      y_ref, s_ref, *, nb, length):
    @pl.when(pl.program_id(1) == 0)
    def _():
        s_ref[...] = s0_ref[...]

    m0, m1 = _half_masks()
    lane = lax.broadcasted_iota(i32, (1, LANES), 1)
    first_half = (lane % HEAD_DIM) < (HEAD_DIM // 2)
    bd = _block_diag_mask()

    def rope(x, cs, sn):
        swapped = jnp.where(first_half, pltpu.roll(x, LANES - HEAD_DIM // 2, 1), pltpu.roll(x, HEAD_DIM // 2, 1))
        return x * cs + swapped * sn

    for j in range(nb):
        rows = pl.ds(j * length, length)
        for p in range(RET_H // 2):
            cols = pl.ds(p * LANES, LANES)
            cs, sn = cos_ref[:, cols], sin_ref[:, cols]
            q2 = rope(ret_ref[rows, pl.ds(p * LANES, LANES)], cs, sn)
            k2 = rope(ret_ref[rows, pl.ds(RET_W + p * LANES, LANES)], cs, sn) * (HEAD_DIM ** -0.5)
            v2 = ret_ref[rows, pl.ds(2 * RET_W + p * LANES, LANES)]
            g2 = ret_ref[rows, pl.ds(3 * RET_W + p * LANES, LANES)]
            state = s_ref[j, p]
            o2 = _mm(q2, state) * qdec_ref[:, cols]
            for hh, m in enumerate((m0, m1)):
                sc = _mm_nt(q2 * m, k2) * mask_ref[2 * p + hh]
                o2 = o2 + _mm(sc, v2) * m
            s_ref[j, p] = state * cdec_ref[:, cols] + _mm_tn(k2 * kdec_ref[:, cols], v2) * bd
            mu = _seg_mean(o2, m0, m1)
            oc = o2 - mu
            var = _seg_mean(oc * oc, m0, m1)
            y_ref[rows, cols] = g2 * _sigmoid(g2) * oc * lax.rsqrt(var + LN_EPS)


def _retention(ret, cos_f, sin_f, tables, s0_bd, *, n_batch, n_chunks, nb, length, out_rows):
    mask, q_dec, k_dec, c_dec = tables
    blk = lambda g, c: (g * n_chunks + c, 0)
    const2 = lambda g, c: (0, 0)
    st = lambda g, c: (g, 0, 0, 0)
    return pl.pallas_call(
        functools.partial(_retention_kernel, nb=nb, length=length),
        grid=(n_batch // nb, n_chunks),
        in_specs=[pl.BlockSpec((nb * length, 4 * RET_W), blk),
                  pl.BlockSpec((length, RET_W), lambda g, c: (c, 0)),
                  pl.BlockSpec((length, RET_W), lambda g, c: (c, 0)),
                  pl.BlockSpec((RET_H, length, length), lambda g, c: (0, 0, 0)),
                  pl.BlockSpec((length, RET_W), const2),
                  pl.BlockSpec((length, RET_W), const2),
                  pl.BlockSpec((1, RET_W), const2),
                  pl.BlockSpec((nb, RET_H // 2, LANES, LANES), st)],
        out_specs=[pl.BlockSpec((nb * length, RET_W), blk),
                   pl.BlockSpec((nb, RET_H // 2, LANES, LANES), st)],
        out_shape=[jax.ShapeDtypeStruct((out_rows, RET_W), f32),
                   jax.ShapeDtypeStruct((n_batch, RET_H // 2, LANES, LANES), f32)],
        compiler_params=_params("arbitrary", "arbitrary"),
        name="retention",
    )(ret, cos_f, sin_f, mask, q_dec, k_dec, c_dec, s0_bd)


def _rwkv_kernel(rw_ref, valid_ref, mix_ref, vec_ref, lora_ref, tri_ref, shift0_ref, s0_ref,
                 y_ref, s_ref, xs_ref, *, length):
    first = pl.program_id(1) == 0

    @pl.when(first)
    def _():
        s_ref[...] = s0_ref[...]
        xs_ref[pl.ds(0, SUBLANES), :] = jnp.broadcast_to(shift0_ref[0], (SUBLANES, RWKV_PROJ))

    rw = rw_ref[...]
    xs_ref[pl.ds(SUBLANES, length), :] = rw
    prev = xs_ref[pl.ds(SUBLANES - 1, length), :]
    xs_ref[pl.ds(0, SUBLANES), :] = rw[length - SUBLANES:, :]
    rwm = rw + (prev - rw) * mix_ref[...]

    valid = valid_ref[...]
    w0, a0, k_k, k_a, r_k, gn_g, gn_b = (vec_ref[pl.ds(i, 1), :] for i in range(7))
    lo = rwm[:, RWKV_LORA_COL:]
    lw = jnp.dot(jnp.tanh(lo), lora_ref[0], precision=_HIGHEST, preferred_element_type=f32)
    la = jnp.dot(lo, lora_ref[1], precision=_HIGHEST, preferred_element_type=f32)
    gate = jnp.dot(_sigmoid(lo), lora_ref[2], precision=_HIGHEST, preferred_element_type=f32)
    logw = -jnp.exp(-_softplus(-(w0 + lw)) - 0.5) * valid
    a = _sigmoid(a0 + la)
    r = rwm[:, :RWKV_W]
    kr = rwm[:, RWKV_W:2 * RWKV_W]
    vr = rwm[:, 2 * RWKV_W:3 * RWKV_W]
    kk_raw = kr * k_k
    kp = kr * (1.0 + (a - 1.0) * k_a) * valid
    cum = jnp.dot(tri_ref[...], logw, precision=_HIGHEST, preferred_element_type=f32)
    g_incl = jnp.exp(cum)
    g_inv = jnp.exp(-cum)
    g_prev = jnp.exp(cum - logw)
    g_last = g_incl[length - 1:length, :]

    m0, m1 = _half_masks()
    bd = _block_diag_mask()
    ri = lax.broadcasted_iota(i32, (length, length), 0)
    ci = lax.broadcasted_iota(i32, (length, length), 1)
    strict = ci < ri
    incl = ci <= ri
    eye = (ci == ri).astype(f32)

    for p in range(RWKV_H // 2):
        cols = pl.ds(p * LANES, LANES)
        sl = slice(p * LANES, (p + 1) * LANES)
        kk2 = kk_raw[:, sl]
        kk2 = kk2 * lax.rsqrt(jnp.maximum(_seg_sum(kk2 * kk2, m0, m1), 1e-24)) * valid[:, sl]
        r2, v2, kp2 = r[:, sl], vr[:, sl], kp[:, sl]
        kh = kk2 * g_prev[:, sl]
        rh = r2 * g_incl[:, sl]
        bt = kk2 * a[:, sl] * g_inv[:, sl]
        kt = kp2 * g_inv[:, sl]
        state = s_ref[0, p]
        rhs = -_mm_nt(kh, state)
        y2 = _mm_nt(rh, state)
        a_b, r_b, r_k_ = [], [], []
        for m in (m0, m1):
            khm, rhm = kh * m, rh * m
            a_b.append(jnp.where(strict, _mm_nt(khm, bt), 0.0))
            rhs = rhs - _mm(jnp.where(strict, _mm_nt(khm, kt), 0.0), v2) * m
            r_b.append(jnp.where(incl, _mm_nt(rhm, bt), 0.0))
            r_k_.append(jnp.where(incl, _mm_nt(rhm, kt), 0.0))
        z2 = jnp.zeros((length, LANES), f32)
        for hh, m in enumerate((m0, m1)):
            x = -a_b[hh]
            inv = eye + x
            pw = x
            cover = 2
            while cover < length:
                pw = _mm(pw, pw)
                inv = inv + _mm(inv, pw)
                cover *= 2
            z2 = z2 + _mm(inv, rhs) * m
        for hh, m in enumerate((m0, m1)):
            y2 = y2 + (_mm(r_b[hh], z2) + _mm(r_k_[hh], v2)) * m
        gl = g_last[:, sl]
        s_ref[0, p] = state * gl + (_mm_tn(z2, bt * gl) + _mm_tn(v2, kt * gl)) * bd

        mu = _seg_mean(y2, m0, m1)
        yc = y2 - mu
        var = _seg_mean(yc * yc, m0, m1)
        yn = yc * lax.rsqrt(var + RWKV_GN_EPS) * gn_g[:, sl] + gn_b[:, sl]
        bonus = _seg_sum(r2 * kp2 * r_k[:, sl], m0, m1) * v2
        y_ref[:, cols] = (yn + bonus) * gate[:, sl]


def _rwkv(rw, valid, mix, vec, lora, tri, shift0, s0_bd, *, n_batch, n_chunks, length, out_rows):
    blk = lambda b, c: (b * n_chunks + c, 0)
    const2 = lambda b, c: (0, 0)
    st = lambda b, c: (b, 0, 0, 0)
    return pl.pallas_call(
        functools.partial(_rwkv_kernel, length=length),
        grid=(n_batch, n_chunks),
        in_specs=[pl.BlockSpec((length, RWKV_PROJ), blk),
                  pl.BlockSpec((length, RWKV_W), const2),
                  pl.BlockSpec((1, RWKV_PROJ), const2),
                  pl.BlockSpec((SUBLANES, RWKV_W), const2),
                  pl.BlockSpec((3, LANES, RWKV_W), lambda b, c: (0, 0, 0)),
                  pl.BlockSpec((length, length), const2),
                  pl.BlockSpec((1, 1, RWKV_PROJ), lambda b, c: (b, 0, 0)),
                  pl.BlockSpec((1, RWKV_H // 2, LANES, LANES), st)],
        out_specs=[pl.BlockSpec((length, RWKV_W), blk),
                   pl.BlockSpec((1, RWKV_H // 2, LANES, LANES), st)],
        out_shape=[jax.ShapeDtypeStruct((out_rows, RWKV_W), f32),
                   jax.ShapeDtypeStruct((n_batch, RWKV_H // 2, LANES, LANES), f32)],
        scratch_shapes=[pltpu.VMEM((length + SUBLANES, RWKV_PROJ), f32)],
        compiler_params=_params("arbitrary", "arbitrary"),
        name="rwkv7",
    )(rw, valid, mix, vec, lora, tri, shift0, s0_bd)


def _lru_kernel(lru_ref, valid_ref, vec_ref, wa_ref, wx_ref, conv0_ref, h0_ref,
                y_ref, h_ref, xext_ref, a_ref, b_ref, hs_ref, *, nb, length):
    @pl.when(pl.program_id(1) == 0)
    def _():
        h_ref[...] = h0_ref[...]
        xext_ref[:, pl.ds(0, SUBLANES), :] = conv0_ref[...]

    valid = valid_ref[...] > 0.5
    cw = [vec_ref[pl.ds(i, 1), :] for i in range(CONV_WIDTH)]
    cb, ba, bx, lam = (vec_ref[pl.ds(i, 1), :] for i in range(CONV_WIDTH, CONV_WIDTH + 4))
    sp = _softplus(-lam)
    for j in range(nb):
        rows = pl.ds(j * length, length)
        gbr = lru_ref[rows, pl.ds(0, LRU_W)]
        x = lru_ref[rows, pl.ds(LRU_W, LRU_W)]
        xext_ref[j, pl.ds(SUBLANES, length), :] = x
        xc = cb + x * cw[CONV_WIDTH - 1]
        for t in range(CONV_WIDTH - 1):
            xc = xc + xext_ref[j, pl.ds(SUBLANES - (CONV_WIDTH - 1) + t, length), :] * cw[t]
        xext_ref[j, pl.ds(0, SUBLANES), :] = x[length - SUBLANES:, :]
        gate_a = _sigmoid(_mm(xc, wa_ref[...]) + ba)
        gate_x = _sigmoid(_mm(xc, wx_ref[...]) + bx)
        log_a = -LRU_C * gate_a * sp
        a = jnp.exp(log_a)
        b = xc * gate_x * jnp.sqrt(1.0 - jnp.exp(2.0 * log_a))
        a_ref[...] = jnp.where(valid, a, 1.0)
        b_ref[...] = jnp.where(valid, b, 0.0)

        def step(t, h):
            h = a_ref[pl.ds(t, 1), :] * h + b_ref[pl.ds(t, 1), :]
            hs_ref[pl.ds(t, 1), :] = h
            return h

        h = lax.fori_loop(0, length, step, h_ref[j, pl.ds(0, 1), :])
        h_ref[j] = jnp.broadcast_to(h, (SUBLANES, LRU_W))
        c = 0.7978845608028654
        gelu = 0.5 * gbr * (1.0 + jnp.tanh(c * (gbr + 0.044715 * gbr * gbr * gbr)))
        y_ref[rows, :] = hs_ref[...] * gelu


def _lru(lru, valid, vec, wa_bd, wx_bd, conv0, h0, *, n_batch, n_chunks, nb, length, out_rows):
    blk = lambda g, c: (g * n_chunks + c, 0)
    const2 = lambda g, c: (0, 0)
    st = lambda g, c: (g, 0, 0)
    return pl.pallas_call(
        functools.partial(_lru_kernel, nb=nb, length=length),
        grid=(n_batch // nb, n_chunks),
        in_specs=[pl.BlockSpec((nb * length, 2 * LRU_W), blk),
                  pl.BlockSpec((length, LRU_W), const2),
                  pl.BlockSpec((SUBLANES, LRU_W), const2),
                  pl.BlockSpec((LRU_W, LRU_W), const2),
                  pl.BlockSpec((LRU_W, LRU_W), const2),
                  pl.BlockSpec((nb, SUBLANES, LRU_W), st),
                  pl.BlockSpec((nb, SUBLANES, LRU_W), st)],
        out_specs=[pl.BlockSpec((nb * length, LRU_W), blk),
                   pl.BlockSpec((nb, SUBLANES, LRU_W), st)],
        out_shape=[jax.ShapeDtypeStruct((out_rows, LRU_W), f32),
                   jax.ShapeDtypeStruct((n_batch, SUBLANES, LRU_W), f32)],
        scratch_shapes=[pltpu.VMEM((nb, length + SUBLANES, LRU_W), f32),
                        pltpu.VMEM((length, LRU_W), f32),
                        pltpu.VMEM((length, LRU_W), f32),
                        pltpu.VMEM((length, LRU_W), f32)],
        compiler_params=_params("arbitrary", "arbitrary"),
        name="rg_lru",
    )(lru, valid, vec, wa_bd, wx_bd, conv0, h0)


def _out_proj_kernel(yr_ref, yw_ref, yl_ref, x_ref, w_ref, ln_ref, wr_ref, br_ref,
                     x1_ref, e_ref, g_ref):
    mixed = (_mm(yr_ref[...], w_ref[pl.ds(0, RET_W), :])
             + _mm(yw_ref[...], w_ref[pl.ds(RET_W, RWKV_W), :])
             + _mm(yl_ref[...], w_ref[pl.ds(RET_W + RWKV_W, LRU_W), :]))
    x1 = _layer_norm_rows(DN_ALPHA * x_ref[...] + mixed, ln_ref[pl.ds(0, 1), :], ln_ref[pl.ds(1, 1), :])
    x1_ref[...] = x1
    logits = jnp.dot(x1, wr_ref[...], precision=_HIGHEST, preferred_element_type=f32) + br_ref[...]
    lane = lax.broadcasted_iota(i32, logits.shape, 1).astype(f32)
    top_e = jnp.zeros(logits.shape, f32)
    top_v = jnp.zeros(logits.shape, f32)
    work = logits
    v_max = None
    for k in range(TOP_K):
        v = jnp.max(work, axis=-1, keepdims=True)
        idx = jnp.min(jnp.where(work == v, lane, float(LANES)), axis=-1, keepdims=True)
        if k == 0:
            v_max = v
        top_e = jnp.where(lane == k, idx, top_e)
        top_v = jnp.where(lane == k, jnp.exp(v - v_max), top_v)
        work = jnp.where(lane == idx, -jnp.inf, work)
    e_ref[...] = top_e.astype(i32)
    g_ref[...] = top_v / jnp.sum(top_v, axis=-1, keepdims=True)


def _out_proj_router(y_ret, y_rwkv, y_lru, x, w_out_bf, ln, w_router, b_router):
    n = x.shape[0]
    tm = TOKEN_TILE
    row = lambda i: (i, 0)
    const = lambda i: (0, 0)
    return pl.pallas_call(
        _out_proj_kernel,
        grid=(n // tm,),
        in_specs=[pl.BlockSpec((tm, RET_W), row), pl.BlockSpec((tm, RWKV_W), row), pl.BlockSpec((tm, LRU_W), row),
                  pl.BlockSpec((tm, D_MODEL), row),
                  pl.BlockSpec((D_MODEL, D_MODEL), const),
                  pl.BlockSpec((SUBLANES, D_MODEL), const),
                  pl.BlockSpec((D_MODEL, LANES), const),
                  pl.BlockSpec((1, LANES), const)],
        out_specs=[pl.BlockSpec((tm, D_MODEL), row), pl.BlockSpec((tm, LANES), row), pl.BlockSpec((tm, LANES), row)],
        out_shape=[jax.ShapeDtypeStruct((n, D_MODEL), f32),
                   jax.ShapeDtypeStruct((n, LANES), i32),
                   jax.ShapeDtypeStruct((n, LANES), f32)],
        compiler_params=_params("arbitrary"),
        name="out_proj_router",
    )(y_ret, y_rwkv, y_lru, x, w_out_bf, ln, w_router, b_router)


def _expert_kernel(blk_e_ref, n_used_ref, tok_ref, x_hbm, wgu_ref, bgu_ref, wdn_ref, bdn_ref,
                   out_ref, xbuf, wgu_bf, wdn_bf, sem):
    i = pl.program_id(0)
    prev = jnp.maximum(i - 1, 0)
    new_expert = jnp.logical_or(i == 0, blk_e_ref[i] != blk_e_ref[prev])
    used = i < n_used_ref[0]

    @pl.when(used)
    def _():
        def issue(r, carry):
            pltpu.make_async_copy(x_hbm.at[pl.ds(tok_ref[r], 1), :], xbuf.at[pl.ds(r, 1), :], sem).start()
            return carry

        lax.fori_loop(0, MOE_BLOCK, issue, 0)

        @pl.when(new_expert)
        def _():
            wgu_bf[...] = wgu_ref[0].astype(bf16)
            wdn_bf[...] = wdn_ref[0].astype(bf16)

        pltpu.make_async_copy(x_hbm.at[pl.ds(0, MOE_BLOCK), :], xbuf, sem).wait()
        gu = jnp.dot(xbuf[...].astype(bf16), wgu_bf[...], preferred_element_type=f32) + bgu_ref[0]
        g = jnp.minimum(gu[:, :D_EXPERT], SWIGLU_LIMIT)
        u = jnp.clip(gu[:, D_EXPERT:], -SWIGLU_LIMIT, SWIGLU_LIMIT)
        hdn = (u + 1.0) * g * _sigmoid(SWIGLU_ALPHA * g)
        out_ref[...] = jnp.dot(hdn.astype(bf16), wdn_bf[...], preferred_element_type=f32) + bdn_ref[0]

    @pl.when(jnp.logical_not(used))
    def _():
        out_ref[...] = jnp.zeros(out_ref.shape, f32)


def _experts(blk_e, n_used, buf_tok, x1, w_gu, b_gu, w_down, b_down):
    n_blocks = blk_e.shape[0]
    bm = MOE_BLOCK
    by_e3 = lambda i, be, nu: (be[i], 0, 0)
    grid_spec = pltpu.PrefetchScalarGridSpec(
        num_scalar_prefetch=2,
        grid=(n_blocks,),
        in_specs=[pl.BlockSpec((bm,), lambda i, be, nu: (i,), memory_space=pltpu.SMEM),
                  pl.BlockSpec(memory_space=pl.ANY),
                  pl.BlockSpec((1, D_MODEL, 2 * D_EXPERT), by_e3),
                  pl.BlockSpec((1, 1, 2 * D_EXPERT), by_e3),
                  pl.BlockSpec((1, D_EXPERT, D_MODEL), by_e3),
                  pl.BlockSpec((1, 1, D_MODEL), by_e3)],
        out_specs=pl.BlockSpec((bm, D_MODEL), lambda i, be, nu: (i, 0)),
        scratch_shapes=[pltpu.VMEM((bm, D_MODEL), f32),
                        pltpu.VMEM((D_MODEL, 2 * D_EXPERT), bf16),
                        pltpu.VMEM((D_EXPERT, D_MODEL), bf16),
                        pltpu.SemaphoreType.DMA(())],
    )
    return pl.pallas_call(
        _expert_kernel,
        grid_spec=grid_spec,
        out_shape=jax.ShapeDtypeStruct((n_blocks * bm, D_MODEL), f32),
        compiler_params=_params("arbitrary"),
        name="moe_experts",
    )(blk_e, n_used, buf_tok, x1, w_gu, b_gu.reshape(N_EXPERTS, 1, 2 * D_EXPERT),
      w_down, b_down.reshape(N_EXPERTS, 1, D_MODEL))


def _combine_kernel(dest_ref, gates_ref, x1_ref, ln_ref, yb_hbm, out_ref, gbuf, sem):
    tm = COMBINE_TILE

    def issue(r, carry):
        for k in range(TOP_K):
            pltpu.make_async_copy(yb_hbm.at[pl.ds(dest_ref[r * TOP_K + k], 1), :],
                                  gbuf.at[k, pl.ds(r, 1), :], sem).start()
        return carry

    lax.fori_loop(0, tm, issue, 0)
    for k in range(TOP_K):
        pltpu.make_async_copy(yb_hbm.at[pl.ds(0, tm), :], gbuf.at[k], sem).wait()
    gates = gates_ref[...]
    y = gbuf[0] * gates[:, 0:1]
    for k in range(1, TOP_K):
        y = y + gbuf[k] * gates[:, k:k + 1]
    out_ref[...] = _layer_norm_rows(DN_ALPHA * x1_ref[...] + y, ln_ref[pl.ds(0, 1), :], ln_ref[pl.ds(1, 1), :])


def _combine(dest, gates, x1, ln, yb):
    n = x1.shape[0]
    tm = COMBINE_TILE
    row = lambda i: (i, 0)
    return pl.pallas_call(
        _combine_kernel,
        grid=(n // tm,),
        in_specs=[pl.BlockSpec((tm * TOP_K,), lambda i: (i,), memory_space=pltpu.SMEM),
                  pl.BlockSpec((tm, LANES), row),
                  pl.BlockSpec((tm, D_MODEL), row),
                  pl.BlockSpec((SUBLANES, D_MODEL), lambda i: (0, 0)),
                  pl.BlockSpec(memory_space=pl.ANY)],
        out_specs=pl.BlockSpec((tm, D_MODEL), row),
        out_shape=jax.ShapeDtypeStruct((n, D_MODEL), f32),
        scratch_shapes=[pltpu.VMEM((TOP_K, tm, D_MODEL), f32), pltpu.SemaphoreType.DMA(())],
        compiler_params=_params("arbitrary"),
        name="moe_combine",
    )(dest, gates, x1, ln, yb)


def _routing_tables(top_e):
    n = top_e.shape[0]
    a = n * TOP_K
    bm = MOE_BLOCK
    flat_e = top_e.reshape(-1)
    order = jnp.argsort(flat_e)
    sorted_e = flat_e[order]
    sorted_tok = (order // TOP_K).astype(i32)
    counts = jnp.bincount(flat_e, length=N_EXPERTS)
    padded = (counts + bm - 1) // bm * bm
    pad_end = jnp.cumsum(padded)
    pad_start = pad_end - padded
    start = jnp.cumsum(counts) - counts
    dest_sorted = (pad_start[sorted_e] + jnp.arange(a) - start[sorted_e]).astype(i32)
    n_blocks = -(-(a + N_EXPERTS * (bm - 1)) // bm)
    buf_tok = jnp.zeros((n_blocks * bm,), i32).at[dest_sorted].set(sorted_tok)
    blk_e = jnp.minimum(jnp.searchsorted(pad_end, jnp.arange(n_blocks) * bm, side='right'),
                        N_EXPERTS - 1).astype(i32)
    dest = jnp.zeros((a,), i32).at[order].set(dest_sorted)
    n_used = (pad_end[-1] // bm).astype(i32).reshape(1)
    return blk_e, n_used, buf_tok, dest


def _pad_time(t, n_batch, n_t, t_pad):
    w = t.shape[-1]
    return jnp.pad(t.reshape(n_batch, n_t, w), ((0, 0), (0, t_pad - n_t), (0, 0))).reshape(n_batch * t_pad, w)


def _to_block_diag(s):
    b, h = s.shape[:2]
    s = s.reshape(b, h // 2, 2, HEAD_DIM, HEAD_DIM)
    z = jnp.zeros_like(s[:, :, 0])
    top = jnp.concatenate([s[:, :, 0], z], axis=-1)
    bot = jnp.concatenate([z, s[:, :, 1]], axis=-1)
    return jnp.concatenate([top, bot], axis=-2)


def _from_block_diag(s):
    b, hp = s.shape[:2]
    return jnp.stack([s[:, :, :HEAD_DIM, :HEAD_DIM], s[:, :, HEAD_DIM:, HEAD_DIM:]], axis=2).reshape(
        b, 2 * hp, HEAD_DIM, HEAD_DIM)


def _block_diag_weight(w):
    h = w.shape[0]
    eye = jnp.eye(h, dtype=w.dtype)
    return (eye[:, None, :, None] * w[:, :, None, :]).reshape(h * HEAD_DIM, h * HEAD_DIM)


def _rows8(*rows):
    width = rows[0].shape[-1]
    m = jnp.stack([r.reshape(width) for r in rows])
    return jnp.pad(m, ((0, SUBLANES - m.shape[0]), (0, 0)))


def _layer(h, p, st_s, bp, tp, bs, ts):
    n_p, n_s = bp * tp, bs * ts
    tpad = SAMPLE_T_PAD
    ret, rw, lru = _in_proj(h, p['w_in'].astype(bf16))

    def merge(y_p, y_s):
        w = y_p.shape[-1]
        return jnp.concatenate([y_p, y_s.reshape(bs, tpad, w)[:, :ts].reshape(n_s, w)], axis=0)

    c_p = RET_CHUNK if tp % RET_CHUNK == 0 else tp
    cos_p, sin_p = _rope_tables(jnp.arange(tp, dtype=f32))
    cos_s, sin_s = _rope_tables(p['pos0_s'] + jnp.arange(tpad, dtype=f32))
    y_ret, sret_p = _retention(ret, cos_p, sin_p, _retention_tables(c_p, c_p),
                               jnp.zeros((bp, RET_H // 2, LANES, LANES), f32),
                               n_batch=bp, n_chunks=tp // c_p, nb=1, length=c_p, out_rows=n_p)
    y_ret_s, sret_s = _retention(_pad_time(ret[n_p:], bs, ts, tpad), cos_s, sin_s, _retention_tables(tpad, ts),
                                 _to_block_diag(st_s['ret']),
                                 n_batch=bs, n_chunks=1, nb=SUBLANES, length=tpad, out_rows=bs * tpad)
    y_ret = merge(y_ret, y_ret_s)

    lora = jnp.zeros((3, LANES, RWKV_W), f32)
    lora = lora.at[0, 0:32].set(p['w_up']).at[1, 32:64].set(p['a_up']).at[2, 64:128].set(p['g_up'])
    vec = _rows8(p['w0'], p['a0'], p['k_k'], p['k_a'], p['r_k'], p['gn_g'], p['gn_b'])
    mix = p['mix'].reshape(1, RWKV_PROJ)
    l_p = RWKV_CHUNK if tp % RWKV_CHUNK == 0 else tp
    tri = lambda l: jnp.tril(jnp.ones((l, l), f32))
    y_rwkv, srw_p = _rwkv(rw, jnp.ones((l_p, RWKV_W), f32), mix, vec, lora, tri(l_p),
                          jnp.zeros((bp, 1, RWKV_PROJ), f32), jnp.zeros((bp, RWKV_H // 2, LANES, LANES), f32),
                          n_batch=bp, n_chunks=tp // l_p, length=l_p, out_rows=n_p)
    valid_s = (jnp.arange(tpad) < ts).astype(f32)[:, None]
    y_rwkv_s, srw_s = _rwkv(_pad_time(rw[n_p:], bs, ts, tpad), jnp.broadcast_to(valid_s, (tpad, RWKV_W)),
                            mix, vec, lora, tri(tpad), st_s['shift'][:, None, :], _to_block_diag(st_s['rwkv']),
                            n_batch=bs, n_chunks=1, length=tpad, out_rows=bs * tpad)
    y_rwkv = merge(y_rwkv, y_rwkv_s)

    lvec = _rows8(p['conv_w'][0], p['conv_w'][1], p['conv_w'][2], p['conv_w'][3],
                  p['conv_b'], p['ba'], p['bx'], p['lam'])
    wa_bd = _block_diag_weight(p['wa']).astype(bf16)
    wx_bd = _block_diag_weight(p['wx']).astype(bf16)
    l_l = RET_CHUNK if tp % RET_CHUNK == 0 else tp
    y_lru, h_p = _lru(lru, jnp.ones((l_l, LRU_W), f32), lvec, wa_bd, wx_bd,
                      jnp.zeros((bp, SUBLANES, LRU_W), f32), jnp.zeros((bp, SUBLANES, LRU_W), f32),
                      n_batch=bp, n_chunks=tp // l_l, nb=1, length=l_l, out_rows=n_p)
    conv0_s = jnp.pad(st_s['conv'], ((0, 0), (SUBLANES - (CONV_WIDTH - 1), 0), (0, 0)))
    h0_s = jnp.broadcast_to(st_s['lru'][:, None, :], (bs, SUBLANES, LRU_W))
    y_lru_s, h_s = _lru(_pad_time(lru[n_p:], bs, ts, tpad), jnp.broadcast_to(valid_s, (tpad, LRU_W)),
                        lvec, wa_bd, wx_bd, conv0_s, h0_s,
                        n_batch=bs, n_chunks=1, nb=SUBLANES, length=tpad, out_rows=bs * tpad)
    y_lru = merge(y_lru, y_lru_s)

    w_router = jnp.pad(p['w_router'], ((0, 0), (0, LANES - N_EXPERTS)))
    b_router = jnp.pad(p['b_router'], (0, LANES - N_EXPERTS), constant_values=-1e30).reshape(1, LANES)
    x1, top_e, gates = _out_proj_router(y_ret, y_rwkv, y_lru, h, p['w_out'].astype(bf16),
                                        _rows8(p['ln1_g'], p['ln1_b']), w_router, b_router)

    blk_e, n_used, buf_tok, dest = _routing_tables(top_e[:, :TOP_K])
    yb = _experts(blk_e, n_used, buf_tok, x1, p['w_gu'], p['b_gu'], p['w_down'], p['b_down'])
    x2 = _combine(dest, gates, x1, _rows8(p['ln2_g'], p['ln2_b']), yb)

    rw_p3, rw_s3 = rw[:n_p].reshape(bp, tp, RWKV_PROJ), rw[n_p:].reshape(bs, ts, RWKV_PROJ)
    xb_p3 = lru[:n_p, LRU_W:].reshape(bp, tp, LRU_W)
    xb_s3 = lru[n_p:, LRU_W:].reshape(bs, ts, LRU_W)
    keep = CONV_WIDTH - 1
    new_p = (_from_block_diag(sret_p), _from_block_diag(srw_p), rw_p3[:, -1], h_p[:, 0], xb_p3[:, tp - keep:])
    new_s = (_from_block_diag(sret_s), _from_block_diag(srw_s), rw_s3[:, -1], h_s[:, 0], xb_s3[:, ts - keep:])
    return x2, new_p, new_s


def kernel(x_prompt, x_sample, state_ret, state_rwkv, state_rwkv_shift, state_lru, state_conv,
           w_in, w_out, ln1_g, ln1_b, ln2_g, ln2_b,
           rwkv_mix, rwkv_w0, rwkv_w_up, rwkv_a0, rwkv_a_up, rwkv_g_up, rwkv_k_k, rwkv_k_a, rwkv_r_k,
           rwkv_gn_g, rwkv_gn_b, lru_conv_w, lru_conv_b, lru_wa, lru_ba, lru_wx, lru_bx, lru_lambda,
           moe_w_router, moe_b_router, moe_w_gate_up, moe_b_gate_up, moe_w_down, moe_b_down):
    bp, tp, _ = x_prompt.shape
    bs, ts, _ = x_sample.shape
    past_len = 16384.0
    h = jnp.concatenate([x_prompt.reshape(bp * tp, D_MODEL), x_sample.reshape(bs * ts, D_MODEL)], axis=0)
    new_p, new_s = [], []
    for l in range(w_in.shape[0]):
        p = {'w_in': w_in[l], 'w_out': w_out[l], 'ln1_g': ln1_g[l], 'ln1_b': ln1_b[l],
             'ln2_g': ln2_g[l], 'ln2_b': ln2_b[l], 'mix': rwkv_mix[l], 'w0': rwkv_w0[l],
             'w_up': rwkv_w_up[l], 'a0': rwkv_a0[l], 'a_up': rwkv_a_up[l], 'g_up': rwkv_g_up[l],
             'k_k': rwkv_k_k[l], 'k_a': rwkv_k_a[l], 'r_k': rwkv_r_k[l], 'gn_g': rwkv_gn_g[l],
             'gn_b': rwkv_gn_b[l], 'conv_w': lru_conv_w[l], 'conv_b': lru_conv_b[l], 'wa': lru_wa[l],
             'ba': lru_ba[l], 'wx': lru_wx[l], 'bx': lru_bx[l], 'lam': lru_lambda[l],
             'w_router': moe_w_router[l], 'b_router': moe_b_router[l], 'w_gu': moe_w_gate_up[l],
             'b_gu': moe_b_gate_up[l], 'w_down': moe_w_down[l], 'b_down': moe_b_down[l],
             'pos0_s': past_len}
        st_s = {'ret': state_ret[l], 'rwkv': state_rwkv[l], 'shift': state_rwkv_shift[l],
                'lru': state_lru[l], 'conv': state_conv[l]}
        h, sp, ss = _layer(h, p, st_s, bp, tp, bs, ts)
        new_p.append(sp)
        new_s.append(ss)
    n_p = bp * tp
    outs = [h[:n_p].reshape(bp, tp, D_MODEL), h[n_p:].reshape(bs, ts, D_MODEL)]
    for i in range(5):
        outs.append(jnp.stack([s[i] for s in new_p]))
        outs.append(jnp.stack([s[i] for s in new_s]))
    return tuple(outs)
```

```python
import functools

import jax
import jax.numpy as jnp
from jax import lax
from jax.experimental import pallas as pl
from jax.experimental.pallas import tpu as pltpu

f32 = jnp.float32
bf16 = jnp.bfloat16
i32 = jnp.int32

D_MODEL = 1024
HEAD_DIM = 64
RET_W = 256
RET_H = 4
RET_CHUNK = 128
ROPE_BASE = 10000.0
RWKV_W = 512
RWKV_H = 8
RWKV_PROJ = 1664
RWKV_LORA_COL = 1536
RWKV_GN_EPS = 64e-5
RWKV_CHUNK = 64
RWKV_ROWS = 128
LRU_W = 256
LRU_C = 8.0
CONV_WIDTH = 4
D_PROJ = 3200
N_EXPERTS = 32
EXPERT_BITS = 5
TOP_K = 4
D_EXPERT = 1024
SWIGLU_LIMIT = 7.0
SWIGLU_ALPHA = 1.702
LN_EPS = 1e-5
PAST_LEN = 16384.0

LANES = 128
SUBLANES = 8
SAMPLE_T_PAD = 16
TOKEN_TILE = 512
MOE_BLOCK = 256
ROUTE_TILE = 256
VMEM_LIMIT = 56 * 1024 * 1024

_HIGHEST = lax.Precision.HIGHEST
_NT = (((1,), (1,)), ((), ()))
_TN = (((0,), (0,)), ((), ()))


def _params(*sem):
    return pltpu.CompilerParams(dimension_semantics=sem, vmem_limit_bytes=VMEM_LIMIT)


def _mm(a, b):
    return jnp.dot(a.astype(bf16), b.astype(bf16), preferred_element_type=f32)


def _mm_nt(a, b):
    return lax.dot_general(a.astype(bf16), b.astype(bf16), _NT, preferred_element_type=f32)


def _mm_tn(a, b):
    return lax.dot_general(a.astype(bf16), b.astype(bf16), _TN, preferred_element_type=f32)


def _softplus(x):
    return jnp.maximum(x, 0.0) + jnp.log(1.0 + jnp.exp(-jnp.abs(x)))


def _sigmoid(x):
    return 1.0 / (1.0 + jnp.exp(-x))


def _half_masks():
    lane = lax.broadcasted_iota(i32, (1, LANES), 1)
    m0 = (lane < HEAD_DIM).astype(f32)
    return m0, 1.0 - m0


def _seg_mean(x, m0, m1):
    s0 = jnp.sum(x * m0, axis=-1, keepdims=True)
    s1 = jnp.sum(x * m1, axis=-1, keepdims=True)
    return (m0 * s0 + m1 * s1) * (1.0 / HEAD_DIM)


def _seg_sum(x, m0, m1):
    s0 = jnp.sum(x * m0, axis=-1, keepdims=True)
    s1 = jnp.sum(x * m1, axis=-1, keepdims=True)
    return m0 * s0 + m1 * s1


def _block_diag_mask():
    r = lax.broadcasted_iota(i32, (LANES, LANES), 0) // HEAD_DIM
    c = lax.broadcasted_iota(i32, (LANES, LANES), 1) // HEAD_DIM
    return (r == c).astype(f32)


def _layer_norm_rows(z, g, b):
    mu = jnp.mean(z, axis=-1, keepdims=True)
    zc = z - mu
    var = jnp.mean(zc * zc, axis=-1, keepdims=True)
    return zc * lax.rsqrt(var + LN_EPS) * g + b


def _in_proj_kernel(x_ref, w_ref, ret_ref, rw_ref, lru_ref):
    xb = x_ref[...].astype(bf16)
    c0, c1 = 4 * RET_W, 4 * RET_W + RWKV_PROJ
    ret_ref[...] = jnp.dot(xb, w_ref[:, :c0], preferred_element_type=f32)
    rw_ref[...] = jnp.dot(xb, w_ref[:, c0:c1], preferred_element_type=f32)
    lru_ref[...] = jnp.dot(xb, w_ref[:, c1:], preferred_element_type=f32)


def _in_proj(x, w_bf):
    n = x.shape[0]
    tm = TOKEN_TILE
    row = lambda i: (i, 0)
    return pl.pallas_call(
        _in_proj_kernel,
        grid=(n // tm,),
        in_specs=[pl.BlockSpec((tm, D_MODEL), row),
                  pl.BlockSpec((D_MODEL, D_PROJ), lambda i: (0, 0))],
        out_specs=[pl.BlockSpec((tm, 4 * RET_W), row),
                   pl.BlockSpec((tm, RWKV_PROJ), row),
                   pl.BlockSpec((tm, 2 * LRU_W), row)],
        out_shape=[jax.ShapeDtypeStruct((n, 4 * RET_W), f32),
                   jax.ShapeDtypeStruct((n, RWKV_PROJ), f32),
                   jax.ShapeDtypeStruct((n, 2 * LRU_W), f32)],
        compiler_params=_params("arbitrary"),
        name="in_proj",
    )(x, w_bf)


def _rope_tables(pos):
    half = HEAD_DIM // 2
    inv = ROPE_BASE ** (-jnp.arange(half, dtype=f32) / half)
    ang = pos[:, None] * inv[None, :]
    cos, sin = jnp.cos(ang), jnp.sin(ang)
    cos_f = jnp.tile(jnp.concatenate([cos, cos], axis=-1), (1, RET_H))
    sin_f = jnp.tile(jnp.concatenate([-sin, sin], axis=-1), (1, RET_H))
    return cos_f, sin_f


def _retention_tables(length, n_valid):
    lg = jnp.log1p(-jnp.exp2(-5.0 - jnp.arange(RET_H, dtype=f32)))
    idx = jnp.arange(length, dtype=f32)
    rel = idx[:, None] - idx[None, :]
    mask = jnp.where(rel[None] >= 0, jnp.exp(jnp.maximum(rel, 0.0)[None] * lg[:, None, None]), 0.0)
    q_dec = jnp.exp((idx[:, None] + 1.0) * lg[None, :])
    k_dec = jnp.where(idx[:, None] < n_valid, jnp.exp((n_valid - 1.0 - idx)[:, None] * lg[None, :]), 0.0)
    c_dec = jnp.exp(n_valid * lg)[None, :]
    rep = lambda t: jnp.repeat(t, HEAD_DIM, axis=-1)
    return mask, rep(q_dec), rep(k_dec), rep(c_dec)


def _retention_kernel(ret_ref, cos_ref, sin_ref, mask_ref, qdec_ref, kdec_ref, cdec_ref, s0_ref,
                      y_ref, s_ref, *, nb, length):
    @pl.when(pl.program_id(1) == 0)
    def _():
        s_ref[...] = s0_ref[...]

    m0, m1 = _half_masks()
    lane = lax.broadcasted_iota(i32, (1, LANES), 1)
    first_half = (lane % HEAD_DIM) < (HEAD_DIM // 2)
    bd = _block_diag_mask()

    def rope(x, cs, sn):
        swapped = jnp.where(first_half, pltpu.roll(x, LANES - HEAD_DIM // 2, 1), pltpu.roll(x, HEAD_DIM // 2, 1))
        return x * cs + swapped * sn

    for j in range(nb):
        rows = pl.ds(j * length, length)
        for p in range(RET_H // 2):
            cols = pl.ds(p * LANES, LANES)
            cs, sn = cos_ref[:, cols], sin_ref[:, cols]
            q2 = rope(ret_ref[rows, pl.ds(p * LANES, LANES)], cs, sn)
            k2 = rope(ret_ref[rows, pl.ds(RET_W + p * LANES, LANES)], cs, sn) * (HEAD_DIM ** -0.5)
            v2 = ret_ref[rows, pl.ds(2 * RET_W + p * LANES, LANES)]
            g2 = ret_ref[rows, pl.ds(3 * RET_W + p * LANES, LANES)]
            state = s_ref[j, p]
            o2 = _mm(q2, state) * qdec_ref[:, cols]
            for hh, m in enumerate((m0, m1)):
                sc = _mm_nt(q2 * m, k2) * mask_ref[2 * p + hh]
                o2 = o2 + _mm(sc, v2) * m
            s_ref[j, p] = state * cdec_ref[:, cols] + _mm_tn(k2 * kdec_ref[:, cols], v2) * bd
            mu = _seg_mean(o2, m0, m1)
            oc = o2 - mu
            var = _seg_mean(oc * oc, m0, m1)
            y_ref[rows, cols] = g2 * _sigmoid(g2) * oc * lax.rsqrt(var + LN_EPS)


def _retention(ret, cos_f, sin_f, tables, s0_bd, *, n_batch, n_chunks, nb, length, out_rows):
    mask, q_dec, k_dec, c_dec = tables
    blk = lambda g, c: (g * n_chunks + c, 0)
    const2 = lambda g, c: (0, 0)
    st = lambda g, c: (g, 0, 0, 0)
    return pl.pallas_call(
        functools.partial(_retention_kernel, nb=nb, length=length),
        grid=(n_batch // nb, n_chunks),
        in_specs=[pl.BlockSpec((nb * length, 4 * RET_W), blk),
                  pl.BlockSpec((length, RET_W), lambda g, c: (c, 0)),
                  pl.BlockSpec((length, RET_W), lambda g, c: (c, 0)),
                  pl.BlockSpec((RET_H, length, length), lambda g, c: (0, 0, 0)),
                  pl.BlockSpec((length, RET_W), const2),
                  pl.BlockSpec((length, RET_W), const2),
                  pl.BlockSpec((1, RET_W), const2),
                  pl.BlockSpec((nb, RET_H // 2, LANES, LANES), st)],
        out_specs=[pl.BlockSpec((nb * length, RET_W), blk),
                   pl.BlockSpec((nb, RET_H // 2, LANES, LANES), st)],
        out_shape=[jax.ShapeDtypeStruct((out_rows, RET_W), f32),
                   jax.ShapeDtypeStruct((n_batch, RET_H // 2, LANES, LANES), f32)],
        compiler_params=_params("arbitrary", "arbitrary"),
        name="retention",
    )(ret, cos_f, sin_f, mask, q_dec, k_dec, c_dec, s0_bd)


def _rwkv_kernel(rw_ref, aux_ref, valid_ref, mix_ref, vec_ref, lora_ref, tri_ref, s0_ref,
                 y_ref, s_ref, *scratch, rows, length, chain):
    n_chunks = rows // length
    rw = rw_ref[...]
    if chain:
        xs_ref, = scratch

        @pl.when(pl.program_id(1) == 0)
        def _():
            s_ref[...] = s0_ref[...]
            xs_ref[pl.ds(0, SUBLANES), :] = jnp.broadcast_to(aux_ref[0], (SUBLANES, RWKV_PROJ))

        xs_ref[pl.ds(SUBLANES, rows), :] = rw
        prev = xs_ref[pl.ds(SUBLANES - 1, rows), :]
        xs_ref[pl.ds(0, SUBLANES), :] = rw[rows - SUBLANES:, :]
    else:
        prev = aux_ref[...]
    rwm = rw + (prev - rw) * mix_ref[...]

    valid = valid_ref[...]
    w0, a0, k_k, k_a, r_k, gn_g, gn_b = (vec_ref[pl.ds(i, 1), :] for i in range(7))
    lo = rwm[:, RWKV_LORA_COL:]
    lw = jnp.dot(jnp.tanh(lo), lora_ref[0], precision=_HIGHEST, preferred_element_type=f32)
    la = jnp.dot(lo, lora_ref[1], precision=_HIGHEST, preferred_element_type=f32)
    gate = jnp.dot(_sigmoid(lo), lora_ref[2], precision=_HIGHEST, preferred_element_type=f32)
    logw = -jnp.exp(-_softplus(-(w0 + lw)) - 0.5) * valid
    a = _sigmoid(a0 + la)
    r = rwm[:, :RWKV_W]
    kr = rwm[:, RWKV_W:2 * RWKV_W]
    vr = rwm[:, 2 * RWKV_W:3 * RWKV_W]
    kk_raw = kr * k_k
    kp = kr * (1.0 + (a - 1.0) * k_a) * valid
    cum = jnp.dot(tri_ref[...], logw, precision=_HIGHEST, preferred_element_type=f32)
    g_incl = jnp.exp(cum)
    g_inv = jnp.exp(-cum)
    g_prev = jnp.exp(cum - logw)
    g_end = jnp.concatenate(
        [jnp.broadcast_to(g_incl[(c + 1) * length - 1:(c + 1) * length, :], (length, RWKV_W)) for c in range(n_chunks)],
        axis=0)

    m0, m1 = _half_masks()
    bd = _block_diag_mask()
    ri = lax.broadcasted_iota(i32, (rows, rows), 0)
    ci = lax.broadcasted_iota(i32, (rows, rows), 1)
    same = (ri // length) == (ci // length)
    strict = jnp.logical_and(same, ci < ri)
    incl = jnp.logical_and(same, ci <= ri)
    eye = (ci == ri).astype(f32)

    for p in range(RWKV_H // 2):
        cols = pl.ds(p * LANES, LANES)
        sl = slice(p * LANES, (p + 1) * LANES)
        kk2 = kk_raw[:, sl]
        kk2 = kk2 * lax.rsqrt(jnp.maximum(_seg_sum(kk2 * kk2, m0, m1), 1e-24)) * valid[:, sl]
        r2, v2, kp2 = r[:, sl], vr[:, sl], kp[:, sl]
        kh = kk2 * g_prev[:, sl]
        rh = r2 * g_incl[:, sl]
        bt = kk2 * a[:, sl] * g_inv[:, sl]
        kt = kp2 * g_inv[:, sl]
        ge = g_end[:, sl]
        b_end = bt * ge
        k_end = kt * ge

        inv, r_b, r_k_ = [], [], []
        rhs0 = jnp.zeros((rows, LANES), f32)
        for m in (m0, m1):
            khm, rhm = kh * m, rh * m
            x = -jnp.where(strict, _mm_nt(khm, bt), 0.0)
            rhs0 = rhs0 - _mm(jnp.where(strict, _mm_nt(khm, kt), 0.0), v2) * m
            r_b.append(jnp.where(incl, _mm_nt(rhm, bt), 0.0))
            r_k_.append(jnp.where(incl, _mm_nt(rhm, kt), 0.0))
            t = eye + x
            cover = 2
            while cover < length:
                x = _mm(x, x)
                t = t + _mm(t, x)
                cover *= 2
            inv.append(t)
        z0 = jnp.zeros((rows, LANES), f32)
        q = rh
        w = jnp.zeros((rows, LANES), f32)
        for hh, m in enumerate((m0, m1)):
            z0 = z0 + _mm(inv[hh], rhs0) * m
            q = q - _mm(r_b[hh], _mm(inv[hh], kh * m))
            w = w + _mm_tn(inv[hh], b_end * m)
        y0 = jnp.zeros((rows, LANES), f32)
        for hh, m in enumerate((m0, m1)):
            y0 = y0 + (_mm(r_b[hh], z0) + _mm(r_k_[hh], v2)) * m

        state = s_ref[0, p] if chain else None
        ys = []
        for c in range(n_chunks):
            cr = slice(c * length, (c + 1) * length)
            if not chain:
                state = s0_ref[c, p]
            ys.append(_mm_nt(q[cr], state) + y0[cr])
            n_c = (_mm_tn(z0[cr], b_end[cr]) + _mm_tn(v2[cr], k_end[cr])) * bd
            kw = _mm_tn(kh[cr], w[cr]) * bd
            state = state * ge[(c + 1) * length - 1:(c + 1) * length, :] - _mm(state, kw) + n_c
            if not chain:
                s_ref[c, p] = state
        if chain:
            s_ref[0, p] = state
        y2 = jnp.concatenate(ys, axis=0) if n_chunks > 1 else ys[0]

        mu = _seg_mean(y2, m0, m1)
        yc = y2 - mu
        var = _seg_mean(yc * yc, m0, m1)
        yn = yc * lax.rsqrt(var + RWKV_GN_EPS) * gn_g[:, sl] + gn_b[:, sl]
        bonus = _seg_sum(r2 * kp2 * r_k[:, sl], m0, m1) * v2
        y_ref[:, cols] = (yn + bonus) * gate[:, sl]


def _rwkv(rw, aux, valid, mix, vec, lora, tri, s0_bd, *, n_groups, n_steps, rows, length, chain):
    blk = lambda g, c: (g * n_steps + c, 0)
    const2 = lambda g, c: (0, 0)
    st = lambda g, c: (g, 0, 0, 0)
    n_state = 1 if chain else rows // length
    aux_spec = (pl.BlockSpec((1, 1, RWKV_PROJ), lambda g, c: (g, 0, 0)) if chain
                else pl.BlockSpec((rows, RWKV_PROJ), blk))
    return pl.pallas_call(
        functools.partial(_rwkv_kernel, rows=rows, length=length, chain=chain),
        grid=(n_groups, n_steps),
        in_specs=[pl.BlockSpec((rows, RWKV_PROJ), blk),
                  aux_spec,
                  pl.BlockSpec((rows, RWKV_W), const2),
                  pl.BlockSpec((1, RWKV_PROJ), const2),
                  pl.BlockSpec((SUBLANES, RWKV_W), const2),
                  pl.BlockSpec((3, LANES, RWKV_W), lambda g, c: (0, 0, 0)),
                  pl.BlockSpec((rows, rows), const2),
                  pl.BlockSpec((n_state, RWKV_H // 2, LANES, LANES), st)],
        out_specs=[pl.BlockSpec((rows, RWKV_W), blk),
                   pl.BlockSpec((n_state, RWKV_H // 2, LANES, LANES), st)],
        out_shape=[jax.ShapeDtypeStruct((n_groups * n_steps * rows, RWKV_W), f32),
                   jax.ShapeDtypeStruct((n_groups * n_state, RWKV_H // 2, LANES, LANES), f32)],
        scratch_shapes=[pltpu.VMEM((rows + SUBLANES, RWKV_PROJ), f32)] if chain else [],
        compiler_params=_params("arbitrary", "arbitrary"),
        name="rwkv7",
    )(rw, aux, valid, mix, vec, lora, tri, s0_bd)


def _chunk_tri(rows, length):
    idx = jnp.arange(rows)
    same = (idx[:, None] // length) == (idx[None, :] // length)
    return jnp.logical_and(same, idx[None, :] <= idx[:, None]).astype(f32)


def _lru_kernel(lru_ref, valid_ref, vec_ref, wa_ref, wx_ref, conv0_ref, h0_ref,
                y_ref, h_ref, xext_ref, a_ref, b_ref, hs_ref, *, nb, length):
    @pl.when(pl.program_id(1) == 0)
    def _():
        h_ref[...] = h0_ref[...]
        xext_ref[:, pl.ds(0, SUBLANES), :] = conv0_ref[...]

    valid = valid_ref[...] > 0.5
    cw = [vec_ref[pl.ds(i, 1), :] for i in range(CONV_WIDTH)]
    cb, ba, bx, lam = (vec_ref[pl.ds(i, 1), :] for i in range(CONV_WIDTH, CONV_WIDTH + 4))
    sp = _softplus(-lam)
    for j in range(nb):
        rows = pl.ds(j * length, length)
        gbr = lru_ref[rows, pl.ds(0, LRU_W)]
        x = lru_ref[rows, pl.ds(LRU_W, LRU_W)]
        xext_ref[j, pl.ds(SUBLANES, length), :] = x
        xc = cb + x * cw[CONV_WIDTH - 1]
        for t in range(CONV_WIDTH - 1):
            xc = xc + xext_ref[j, pl.ds(SUBLANES - (CONV_WIDTH - 1) + t, length), :] * cw[t]
        xext_ref[j, pl.ds(0, SUBLANES), :] = x[length - SUBLANES:, :]
        gate_a = _sigmoid(_mm(xc, wa_ref[...]) + ba)
        gate_x = _sigmoid(_mm(xc, wx_ref[...]) + bx)
        log_a = -LRU_C * gate_a * sp
        a = jnp.exp(log_a)
        b = xc * gate_x * jnp.sqrt(1.0 - jnp.exp(2.0 * log_a))
        a_ref[...] = jnp.where(valid, a, 1.0)
        b_ref[...] = jnp.where(valid, b, 0.0)

        def step(t, h):
            h = a_ref[pl.ds(t, 1), :] * h + b_ref[pl.ds(t, 1), :]
            hs_ref[pl.ds(t, 1), :] = h
            return h

        h = lax.fori_loop(0, length, step, h_ref[j, pl.ds(0, 1), :])
        h_ref[j] = jnp.broadcast_to(h, (SUBLANES, LRU_W))
        c = 0.7978845608028654
        gelu = 0.5 * gbr * (1.0 + jnp.tanh(c * (gbr + 0.044715 * gbr * gbr * gbr)))
        y_ref[rows, :] = hs_ref[...] * gelu


def _lru(lru, valid, vec, wa_bd, wx_bd, conv0, h0, *, n_batch, n_chunks, nb, length, out_rows):
    blk = lambda g, c: (g * n_chunks + c, 0)
    const2 = lambda g, c: (0, 0)
    st = lambda g, c: (g, 0, 0)
    return pl.pallas_call(
        functools.partial(_lru_kernel, nb=nb, length=length),
        grid=(n_batch // nb, n_chunks),
        in_specs=[pl.BlockSpec((nb * length, 2 * LRU_W), blk),
                  pl.BlockSpec((length, LRU_W), const2),
                  pl.BlockSpec((SUBLANES, LRU_W), const2),
                  pl.BlockSpec((LRU_W, LRU_W), const2),
                  pl.BlockSpec((LRU_W, LRU_W), const2),
                  pl.BlockSpec((nb, SUBLANES, LRU_W), st),
                  pl.BlockSpec((nb, SUBLANES, LRU_W), st)],
        out_specs=[pl.BlockSpec((nb * length, LRU_W), blk),
                   pl.BlockSpec((nb, SUBLANES, LRU_W), st)],
        out_shape=[jax.ShapeDtypeStruct((out_rows, LRU_W), f32),
                   jax.ShapeDtypeStruct((n_batch, SUBLANES, LRU_W), f32)],
        scratch_shapes=[pltpu.VMEM((nb, length + SUBLANES, LRU_W), f32),
                        pltpu.VMEM((length, LRU_W), f32),
                        pltpu.VMEM((length, LRU_W), f32),
                        pltpu.VMEM((length, LRU_W), f32)],
        compiler_params=_params("arbitrary", "arbitrary"),
        name="rg_lru",
    )(lru, valid, vec, wa_bd, wx_bd, conv0, h0)


def _out_proj_kernel(yr_ref, yw_ref, yl_ref, x_ref, w_ref, ln_ref, wr_ref, br_ref,
                     x1_ref, code_ref, g_ref, cnt_ref, *, alpha):
    @pl.when(pl.program_id(0) == 0)
    def _():
        cnt_ref[...] = jnp.zeros(cnt_ref.shape, f32)

    mixed = (_mm(yr_ref[...], w_ref[pl.ds(0, RET_W), :])
             + _mm(yw_ref[...], w_ref[pl.ds(RET_W, RWKV_W), :])
             + _mm(yl_ref[...], w_ref[pl.ds(RET_W + RWKV_W, LRU_W), :]))
    x1 = _layer_norm_rows(alpha * x_ref[...] + mixed, ln_ref[pl.ds(0, 1), :], ln_ref[pl.ds(1, 1), :])
    x1_ref[...] = x1
    logits = jnp.dot(x1, wr_ref[...], precision=_HIGHEST, preferred_element_type=f32) + br_ref[...]
    tm = logits.shape[0]
    lane = lax.broadcasted_iota(i32, logits.shape, 1).astype(f32)
    top_v = jnp.zeros(logits.shape, f32)
    work = logits
    v_max = None
    picks, onehots = [], []
    for k in range(TOP_K):
        v = jnp.max(work, axis=-1, keepdims=True)
        idx = jnp.min(jnp.where(work == v, lane, float(LANES)), axis=-1, keepdims=True)
        if k == 0:
            v_max = v
        hit = lane == idx
        picks.append(idx)
        onehots.append(hit.astype(f32))
        top_v = jnp.where(lane == k, jnp.exp(v - v_max), top_v)
        work = jnp.where(hit, -jnp.inf, work)
    g_ref[...] = top_v / jnp.sum(top_v, axis=-1, keepdims=True)

    total = onehots[0] + onehots[1] + onehots[2] + onehots[3]
    ri = lax.broadcasted_iota(i32, (tm, tm), 0)
    ci = lax.broadcasted_iota(i32, (tm, tm), 1)
    before = _mm((ci < ri).astype(f32), total) + cnt_ref[pl.ds(0, 1), :]
    code = jnp.zeros(logits.shape, f32)
    for k in range(TOP_K):
        rank = jnp.sum(onehots[k] * before, axis=-1, keepdims=True)
        code = jnp.where(lane == k, rank * float(1 << EXPERT_BITS) + picks[k], code)
    code_ref[...] = code.astype(i32)
    cnt_ref[...] = cnt_ref[...] + jnp.sum(total, axis=0, keepdims=True)


def _out_proj_router(y_ret, y_rwkv, y_lru, x, w_out_bf, ln, w_router, b_router, alpha):
    n = x.shape[0]
    tm = TOKEN_TILE
    row = lambda i: (i, 0)
    const = lambda i: (0, 0)
    return pl.pallas_call(
        functools.partial(_out_proj_kernel, alpha=alpha),
        grid=(n // tm,),
        in_specs=[pl.BlockSpec((tm, RET_W), row), pl.BlockSpec((tm, RWKV_W), row), pl.BlockSpec((tm, LRU_W), row),
                  pl.BlockSpec((tm, D_MODEL), row),
                  pl.BlockSpec((D_MODEL, D_MODEL), const),
                  pl.BlockSpec((SUBLANES, D_MODEL), const),
                  pl.BlockSpec((D_MODEL, LANES), const),
                  pl.BlockSpec((1, LANES), const)],
        out_specs=[pl.BlockSpec((tm, D_MODEL), row), pl.BlockSpec((tm, LANES), row), pl.BlockSpec((tm, LANES), row),
                   pl.BlockSpec((SUBLANES, LANES), const)],
        out_shape=[jax.ShapeDtypeStruct((n, D_MODEL), f32),
                   jax.ShapeDtypeStruct((n, LANES), i32),
                   jax.ShapeDtypeStruct((n, LANES), f32),
                   jax.ShapeDtypeStruct((SUBLANES, LANES), f32)],
        compiler_params=_params("arbitrary"),
        name="out_proj_router",
    )(y_ret, y_rwkv, y_lru, x, w_out_bf, ln, w_router, b_router)


def _dest_row(code, pstart_ref):
    return pstart_ref[code & (N_EXPERTS - 1)] + (code >> EXPERT_BITS)


def _dispatch_kernel(pstart_ref, cnt_ref, n_used_ref, code_ref, x1_ref, xs_hbm, zero_ref, sem, zsem):
    tm = ROUTE_TILE
    bm = MOE_BLOCK

    @pl.when(pl.program_id(0) == 0)
    def _():
        zero_ref[...] = jnp.zeros(zero_ref.shape, f32)
        tail = lambda j: pltpu.make_async_copy(zero_ref, xs_hbm.at[pl.ds(j * bm, bm), :], zsem)

        def tail_start(j, carry):
            tail(j).start()
            return carry

        def tail_wait(j, carry):
            tail(j).wait()
            return carry

        n_blocks = xs_hbm.shape[0] // bm
        lax.fori_loop(n_used_ref[0], n_blocks, tail_start, 0)
        lax.fori_loop(n_used_ref[0], n_blocks, tail_wait, 0)
        for e in range(N_EXPERTS):
            lo = pstart_ref[e] + cnt_ref[e]
            hi = pstart_ref[e] + (cnt_ref[e] + bm - 1) // bm * bm
            fill = lambda r: pltpu.make_async_copy(zero_ref.at[pl.ds(0, 1), :], xs_hbm.at[pl.ds(r, 1), :], zsem)

            def start(r, carry):
                fill(r).start()
                return carry

            def wait(r, carry):
                fill(r).wait()
                return carry

            lax.fori_loop(lo, hi, start, 0)
            lax.fori_loop(lo, hi, wait, 0)

    def issue(r, carry):
        for k in range(TOP_K):
            d = _dest_row(code_ref[r * TOP_K + k], pstart_ref)
            pltpu.make_async_copy(x1_ref.at[pl.ds(r, 1), :], xs_hbm.at[pl.ds(d, 1), :], sem).start()
        return carry

    lax.fori_loop(0, tm, issue, 0)
    for k in range(TOP_K):
        pltpu.make_async_copy(x1_ref, xs_hbm.at[pl.ds(0, tm), :], sem).wait()


def _dispatch(pstart, counts, n_used, code_flat, x1, n_rows):
    n = x1.shape[0]
    tm = ROUTE_TILE
    grid_spec = pltpu.PrefetchScalarGridSpec(
        num_scalar_prefetch=3,
        grid=(n // tm,),
        in_specs=[pl.BlockSpec((tm * TOP_K,), lambda i, ps, ct, nu: (i,), memory_space=pltpu.SMEM),
                  pl.BlockSpec((tm, D_MODEL), lambda i, ps, ct, nu: (i, 0))],
        out_specs=pl.BlockSpec(memory_space=pl.ANY),
        scratch_shapes=[pltpu.VMEM((MOE_BLOCK, D_MODEL), f32),
                        pltpu.SemaphoreType.DMA(()), pltpu.SemaphoreType.DMA(())],
    )
    return pl.pallas_call(
        _dispatch_kernel,
        grid_spec=grid_spec,
        out_shape=jax.ShapeDtypeStruct((n_rows, D_MODEL), f32),
        compiler_params=_params("arbitrary"),
        name="moe_dispatch",
    )(pstart, counts, n_used, code_flat, x1)


def _expert_kernel(blk_e_ref, n_used_ref, xs_ref, wgu_ref, bgu_ref, wdn_ref, bdn_ref,
                   out_ref, wgu_bf, wdn_bf):
    i = pl.program_id(0)
    prev = jnp.maximum(i - 1, 0)
    new_expert = jnp.logical_or(i == 0, blk_e_ref[i] != blk_e_ref[prev])
    used = i < n_used_ref[0]

    @pl.when(used)
    def _():
        @pl.when(new_expert)
        def _():
            wgu_bf[...] = wgu_ref[0, 0].astype(bf16)
            wdn_bf[...] = wdn_ref[0, 0].astype(bf16)

        gu = jnp.dot(xs_ref[...].astype(bf16), wgu_bf[...], preferred_element_type=f32) + bgu_ref[0, 0]
        g = jnp.minimum(gu[:, :D_EXPERT], SWIGLU_LIMIT)
        u = jnp.clip(gu[:, D_EXPERT:], -SWIGLU_LIMIT, SWIGLU_LIMIT)
        hdn = (u + 1.0) * g * _sigmoid(SWIGLU_ALPHA * g)
        out_ref[...] = jnp.dot(hdn.astype(bf16), wdn_bf[...], preferred_element_type=f32) + bdn_ref[0, 0]

    @pl.when(jnp.logical_not(used))
    def _():
        out_ref[...] = jnp.zeros(out_ref.shape, f32)


def _experts(blk_e, n_used, xs, w_gu, b_gu, w_down, b_down, layer):
    n_blocks = blk_e.shape[0]
    bm = MOE_BLOCK
    by_e = lambda i, be, nu: (layer, be[i], 0, 0)
    x_blk = lambda i, be, nu: (jnp.minimum(i, nu[0] - 1), 0)
    grid_spec = pltpu.PrefetchScalarGridSpec(
        num_scalar_prefetch=2,
        grid=(n_blocks,),
        in_specs=[pl.BlockSpec((bm, D_MODEL), x_blk),
                  pl.BlockSpec((1, 1, D_MODEL, 2 * D_EXPERT), by_e),
                  pl.BlockSpec((1, 1, 1, 2 * D_EXPERT), by_e),
                  pl.BlockSpec((1, 1, D_EXPERT, D_MODEL), by_e),
                  pl.BlockSpec((1, 1, 1, D_MODEL), by_e)],
        out_specs=pl.BlockSpec((bm, D_MODEL), lambda i, be, nu: (i, 0)),
        scratch_shapes=[pltpu.VMEM((D_MODEL, 2 * D_EXPERT), bf16),
                        pltpu.VMEM((D_EXPERT, D_MODEL), bf16)],
    )
    depth = w_gu.shape[0]
    return pl.pallas_call(
        _expert_kernel,
        grid_spec=grid_spec,
        out_shape=jax.ShapeDtypeStruct((n_blocks * bm, D_MODEL), f32),
        compiler_params=_params("arbitrary"),
        name="moe_experts",
    )(blk_e, n_used, xs, w_gu, b_gu.reshape(depth, N_EXPERTS, 1, 2 * D_EXPERT),
      w_down, b_down.reshape(depth, N_EXPERTS, 1, D_MODEL))


def _combine_kernel(pstart_ref, code_ref, gates_ref, x1_ref, ln_ref, yb_hbm, out_ref, gbuf, sem, *, alpha):
    tm = ROUTE_TILE

    def issue(r, carry):
        for k in range(TOP_K):
            d = _dest_row(code_ref[r * TOP_K + k], pstart_ref)
            pltpu.make_async_copy(yb_hbm.at[pl.ds(d, 1), :], gbuf.at[k, pl.ds(r, 1), :], sem).start()
        return carry

    lax.fori_loop(0, tm, issue, 0)
    for k in range(TOP_K):
        pltpu.make_async_copy(yb_hbm.at[pl.ds(0, tm), :], gbuf.at[k], sem).wait()
    gates = gates_ref[...]
    y = gbuf[0] * gates[:, 0:1]
    for k in range(1, TOP_K):
        y = y + gbuf[k] * gates[:, k:k + 1]
    out_ref[...] = _layer_norm_rows(alpha * x1_ref[...] + y, ln_ref[pl.ds(0, 1), :], ln_ref[pl.ds(1, 1), :])


def _combine(pstart, code_flat, gates, x1, ln, yb, alpha):
    n = x1.shape[0]
    tm = ROUTE_TILE
    row = lambda i, ps: (i, 0)
    grid_spec = pltpu.PrefetchScalarGridSpec(
        num_scalar_prefetch=1,
        grid=(n // tm,),
        in_specs=[pl.BlockSpec((tm * TOP_K,), lambda i, ps: (i,), memory_space=pltpu.SMEM),
                  pl.BlockSpec((tm, LANES), row),
                  pl.BlockSpec((tm, D_MODEL), row),
                  pl.BlockSpec((SUBLANES, D_MODEL), lambda i, ps: (0, 0)),
                  pl.BlockSpec(memory_space=pl.ANY)],
        out_specs=pl.BlockSpec((tm, D_MODEL), row),
        scratch_shapes=[pltpu.VMEM((TOP_K, tm, D_MODEL), f32), pltpu.SemaphoreType.DMA(())],
    )
    return pl.pallas_call(
        functools.partial(_combine_kernel, alpha=alpha),
        grid_spec=grid_spec,
        out_shape=jax.ShapeDtypeStruct((n, D_MODEL), f32),
        compiler_params=_params("arbitrary"),
        name="moe_combine",
    )(pstart, code_flat, gates, x1, ln, yb)


def _block_tables(counts, n_tokens):
    bm = MOE_BLOCK
    padded = (counts + bm - 1) // bm * bm
    pad_end = jnp.cumsum(padded)
    pstart = (pad_end - padded).astype(i32)
    n_blocks = -(-(n_tokens * TOP_K + N_EXPERTS * (bm - 1)) // bm)
    blk_e = jnp.minimum(jnp.searchsorted(pad_end, jnp.arange(n_blocks) * bm, side='right'),
                        N_EXPERTS - 1).astype(i32)
    n_used = (pad_end[-1] // bm).astype(i32).reshape(1)
    return pstart, blk_e, n_used, n_blocks * bm


def _pad_time(t, n_batch, n_t, t_pad):
    w = t.shape[-1]
    return jnp.pad(t.reshape(n_batch, n_t, w), ((0, 0), (0, t_pad - n_t), (0, 0))).reshape(n_batch * t_pad, w)


def _to_block_diag(s):
    b, h = s.shape[:2]
    s = s.reshape(b, h // 2, 2, HEAD_DIM, HEAD_DIM)
    z = jnp.zeros_like(s[:, :, 0])
    top = jnp.concatenate([s[:, :, 0], z], axis=-1)
    bot = jnp.concatenate([z, s[:, :, 1]], axis=-1)
    return jnp.concatenate([top, bot], axis=-2)


def _from_block_diag(s):
    b, hp = s.shape[:2]
    return jnp.stack([s[:, :, :HEAD_DIM, :HEAD_DIM], s[:, :, HEAD_DIM:, HEAD_DIM:]], axis=2).reshape(
        b, 2 * hp, HEAD_DIM, HEAD_DIM)


def _block_diag_weight(w):
    h = w.shape[0]
    eye = jnp.eye(h, dtype=w.dtype)
    return (eye[:, None, :, None] * w[:, :, None, :]).reshape(h * HEAD_DIM, h * HEAD_DIM)


def _rows8(*rows):
    width = rows[0].shape[-1]
    m = jnp.stack([r.reshape(width) for r in rows])
    return jnp.pad(m, ((0, SUBLANES - m.shape[0]), (0, 0)))


def _last_rows(t, start, n_seq, seq_len, k):
    return jnp.stack([lax.slice_in_dim(t, start + seq_len - k + j, start + n_seq * seq_len, seq_len, axis=0)
                      for j in range(k)], axis=1)


def _layer(h, p, moe, layer, alpha, st_s, bp, tp, bs, ts):
    n_p, n_s = bp * tp, bs * ts
    tpad = SAMPLE_T_PAD
    ret, rw, lru = _in_proj(h, p['w_in'].astype(bf16))

    def merge(y_p, y_s):
        w = y_p.shape[-1]
        return jnp.concatenate([y_p, y_s.reshape(bs, tpad, w)[:, :ts].reshape(n_s, w)], axis=0)

    c_p = RET_CHUNK if tp % RET_CHUNK == 0 else tp
    cos_p, sin_p = _rope_tables(jnp.arange(tp, dtype=f32))
    cos_s, sin_s = _rope_tables(PAST_LEN + jnp.arange(tpad, dtype=f32))
    y_ret, sret_p = _retention(ret, cos_p, sin_p, _retention_tables(c_p, c_p),
                               jnp.zeros((bp, RET_H // 2, LANES, LANES), f32),
                               n_batch=bp, n_chunks=tp // c_p, nb=1, length=c_p, out_rows=n_p)
    y_ret_s, sret_s = _retention(_pad_time(ret[n_p:], bs, ts, tpad), cos_s, sin_s, _retention_tables(tpad, ts),
                                 _to_block_diag(st_s['ret']),
                                 n_batch=bs, n_chunks=1, nb=SUBLANES, length=tpad, out_rows=bs * tpad)
    y_ret = merge(y_ret, y_ret_s)

    lora = jnp.zeros((3, LANES, RWKV_W), f32)
    lora = lora.at[0, 0:32].set(p['w_up']).at[1, 32:64].set(p['a_up']).at[2, 64:128].set(p['g_up'])
    vec = _rows8(p['w0'], p['a0'], p['k_k'], p['k_a'], p['r_k'], p['gn_g'], p['gn_b'])
    mix = p['mix'].reshape(1, RWKV_PROJ)
    rows = RWKV_ROWS
    y_rwkv, srw_p = _rwkv(rw, jnp.zeros((bp, 1, RWKV_PROJ), f32), jnp.ones((rows, RWKV_W), f32), mix, vec, lora,
                          _chunk_tri(rows, RWKV_CHUNK), jnp.zeros((bp, RWKV_H // 2, LANES, LANES), f32),
                          n_groups=bp, n_steps=tp // rows, rows=rows, length=RWKV_CHUNK, chain=True)
    valid_s = (jnp.arange(tpad) < ts).astype(f32)[:, None]
    rw_s3 = rw[n_p:].reshape(bs, ts, RWKV_PROJ)
    prev_s = jnp.concatenate([st_s['shift'][:, None, :], rw_s3[:, :-1]], axis=1).reshape(n_s, RWKV_PROJ)
    seq_per_blk = rows // tpad
    y_rwkv_s, srw_s = _rwkv(_pad_time(rw[n_p:], bs, ts, tpad), _pad_time(prev_s, bs, ts, tpad),
                            jnp.tile(jnp.broadcast_to(valid_s, (tpad, RWKV_W)), (seq_per_blk, 1)),
                            mix, vec, lora, _chunk_tri(rows, tpad), _to_block_diag(st_s['rwkv']),
                            n_groups=bs // seq_per_blk, n_steps=1, rows=rows, length=tpad, chain=False)
    y_rwkv = merge(y_rwkv, y_rwkv_s)

    lvec = _rows8(p['conv_w'][0], p['conv_w'][1], p['conv_w'][2], p['conv_w'][3],
                  p['conv_b'], p['ba'], p['bx'], p['lam'])
    wa_bd = _block_diag_weight(p['wa']).astype(bf16)
    wx_bd = _block_diag_weight(p['wx']).astype(bf16)
    l_l = RET_CHUNK if tp % RET_CHUNK == 0 else tp
    y_lru, h_p = _lru(lru, jnp.ones((l_l, LRU_W), f32), lvec, wa_bd, wx_bd,
                      jnp.zeros((bp, SUBLANES, LRU_W), f32), jnp.zeros((bp, SUBLANES, LRU_W), f32),
                      n_batch=bp, n_chunks=tp // l_l, nb=1, length=l_l, out_rows=n_p)
    conv0_s = jnp.pad(st_s['conv'], ((0, 0), (SUBLANES - (CONV_WIDTH - 1), 0), (0, 0)))
    h0_s = jnp.broadcast_to(st_s['lru'][:, None, :], (bs, SUBLANES, LRU_W))
    y_lru_s, h_s = _lru(_pad_time(lru[n_p:], bs, ts, tpad), jnp.broadcast_to(valid_s, (tpad, LRU_W)),
                        lvec, wa_bd, wx_bd, conv0_s, h0_s,
                        n_batch=bs, n_chunks=1, nb=SUBLANES, length=tpad, out_rows=bs * tpad)
    y_lru = merge(y_lru, y_lru_s)

    w_router = jnp.pad(p['w_router'], ((0, 0), (0, LANES - N_EXPERTS)))
    b_router = jnp.pad(p['b_router'], (0, LANES - N_EXPERTS), constant_values=-1e30).reshape(1, LANES)
    x1, code, gates, cnt = _out_proj_router(y_ret, y_rwkv, y_lru, h, p['w_out'].astype(bf16),
                                            _rows8(p['ln1_g'], p['ln1_b']), w_router, b_router, alpha)

    counts = cnt[0, :N_EXPERTS].astype(i32)
    pstart, blk_e, n_used, n_rows = _block_tables(counts, n_p + n_s)
    code_flat = code[:, :TOP_K].reshape(-1)
    xs = _dispatch(pstart, counts, n_used, code_flat, x1, n_rows)
    yb = _experts(blk_e, n_used, xs, moe['w_gu'], moe['b_gu'], moe['w_down'], moe['b_down'], layer)
    x2 = _combine(pstart, code_flat, gates, x1, _rows8(p['ln2_g'], p['ln2_b']), yb, alpha)

    keep = CONV_WIDTH - 1
    xbr = lru[:, LRU_W:]
    new_p = (_from_block_diag(sret_p), _from_block_diag(srw_p), _last_rows(rw, 0, bp, tp, 1)[:, 0], h_p[:, 0],
             _last_rows(xbr, 0, bp, tp, keep))
    new_s = (_from_block_diag(sret_s), _from_block_diag(srw_s), _last_rows(rw, n_p, bs, ts, 1)[:, 0], h_s[:, 0],
             _last_rows(xbr, n_p, bs, ts, keep))
    return x2, new_p, new_s


def kernel(x_prompt, x_sample, state_ret, state_rwkv, state_rwkv_shift, state_lru, state_conv,
           w_in, w_out, ln1_g, ln1_b, ln2_g, ln2_b,
           rwkv_mix, rwkv_w0, rwkv_w_up, rwkv_a0, rwkv_a_up, rwkv_g_up, rwkv_k_k, rwkv_k_a, rwkv_r_k,
           rwkv_gn_g, rwkv_gn_b, lru_conv_w, lru_conv_b, lru_wa, lru_ba, lru_wx, lru_bx, lru_lambda,
           moe_w_router, moe_b_router, moe_w_gate_up, moe_b_gate_up, moe_w_down, moe_b_down):
    bp, tp, _ = x_prompt.shape
    bs, ts, _ = x_sample.shape
    depth = w_in.shape[0]
    alpha = (2.0 * depth) ** 0.25
    moe = {'w_gu': moe_w_gate_up, 'b_gu': moe_b_gate_up, 'w_down': moe_w_down, 'b_down': moe_b_down}
    h = jnp.concatenate([x_prompt.reshape(bp * tp, D_MODEL), x_sample.reshape(bs * ts, D_MODEL)], axis=0)
    new_p, new_s = [], []
    for l in range(depth):
        p = {'w_in': w_in[l], 'w_out': w_out[l], 'ln1_g': ln1_g[l], 'ln1_b': ln1_b[l],
             'ln2_g': ln2_g[l], 'ln2_b': ln2_b[l], 'mix': rwkv_mix[l], 'w0': rwkv_w0[l],
             'w_up': rwkv_w_up[l], 'a0': rwkv_a0[l], 'a_up': rwkv_a_up[l], 'g_up': rwkv_g_up[l],
             'k_k': rwkv_k_k[l], 'k_a': rwkv_k_a[l], 'r_k': rwkv_r_k[l], 'gn_g': rwkv_gn_g[l],
             'gn_b': rwkv_gn_b[l], 'conv_w': lru_conv_w[l], 'conv_b': lru_conv_b[l], 'wa': lru_wa[l],
             'ba': lru_ba[l], 'wx': lru_wx[l], 'bx': lru_bx[l], 'lam': lru_lambda[l],
             'w_router': moe_w_router[l], 'b_router': moe_b_router[l]}
        st_s = {'ret': state_ret[l], 'rwkv': state_rwkv[l], 'shift': state_rwkv_shift[l],
                'lru': state_lru[l], 'conv': state_conv[l]}
        h, sp, ss = _layer(h, p, moe, l, alpha, st_s, bp, tp, bs, ts)
        new_p.append(sp)
        new_s.append(ss)
    n_p = bp * tp
    outs = [h[:n_p].reshape(bp, tp, D_MODEL), h[n_p:].reshape(bs, ts, D_MODEL)]
    for i in range(5):
        outs.append(jnp.stack([s[i] for s in new_p]))
        outs.append(jnp.stack([s[i] for s in new_s]))
    return tuple(outs)
```

```python
import functools

import jax
import jax.numpy as jnp
from jax import lax
from jax.experimental import pallas as pl
from jax.experimental.pallas import tpu as pltpu

f32 = jnp.float32
bf16 = jnp.bfloat16
i32 = jnp.int32

D_MODEL = 1024
HEAD_DIM = 64
RET_W = 256
RET_H = 4
RET_CHUNK = 128
ROPE_BASE = 10000.0
RWKV_W = 512
RWKV_H = 8
RWKV_PROJ = 1664
RWKV_LORA_COL = 1536
RWKV_GN_EPS = 64e-5
RWKV_CHUNK = 64
RWKV_ROWS = 128
LRU_W = 256
LRU_C = 8.0
CONV_WIDTH = 4
D_PROJ = 3200
N_EXPERTS = 32
EXPERT_BITS = 5
TOP_K = 4
D_EXPERT = 1024
SWIGLU_LIMIT = 7.0
SWIGLU_ALPHA = 1.702
LN_EPS = 1e-5
PAST_LEN = 16384.0

LANES = 128
SUBLANES = 8
SAMPLE_T_PAD = 16
TOKEN_TILE = 512
MOE_BLOCK = 256
ROUTE_TILE = 256
VMEM_LIMIT = 56 * 1024 * 1024

_HIGHEST = lax.Precision.HIGHEST
_NT = (((1,), (1,)), ((), ()))
_TN = (((0,), (0,)), ((), ()))


def _params(*sem):
    return pltpu.CompilerParams(dimension_semantics=sem, vmem_limit_bytes=VMEM_LIMIT)


def _mm(a, b):
    return jnp.dot(a.astype(bf16), b.astype(bf16), preferred_element_type=f32)


def _mm_nt(a, b):
    return lax.dot_general(a.astype(bf16), b.astype(bf16), _NT, preferred_element_type=f32)


def _mm_tn(a, b):
    return lax.dot_general(a.astype(bf16), b.astype(bf16), _TN, preferred_element_type=f32)


def _softplus(x):
    return jnp.maximum(x, 0.0) + jnp.log(1.0 + jnp.exp(-jnp.abs(x)))


def _sigmoid(x):
    return 1.0 / (1.0 + jnp.exp(-x))


def _half_masks():
    lane = lax.broadcasted_iota(i32, (1, LANES), 1)
    m0 = (lane < HEAD_DIM).astype(f32)
    return m0, 1.0 - m0


def _seg_mean(x, m0, m1):
    s0 = jnp.sum(x * m0, axis=-1, keepdims=True)
    s1 = jnp.sum(x * m1, axis=-1, keepdims=True)
    return (m0 * s0 + m1 * s1) * (1.0 / HEAD_DIM)


def _seg_sum(x, m0, m1):
    s0 = jnp.sum(x * m0, axis=-1, keepdims=True)
    s1 = jnp.sum(x * m1, axis=-1, keepdims=True)
    return m0 * s0 + m1 * s1


def _block_diag_mask():
    r = lax.broadcasted_iota(i32, (LANES, LANES), 0) // HEAD_DIM
    c = lax.broadcasted_iota(i32, (LANES, LANES), 1) // HEAD_DIM
    return (r == c).astype(f32)


def _layer_norm_rows(z, g, b):
    mu = jnp.mean(z, axis=-1, keepdims=True)
    zc = z - mu
    var = jnp.mean(zc * zc, axis=-1, keepdims=True)
    return zc * lax.rsqrt(var + LN_EPS) * g + b


def _in_proj_kernel(x_ref, w_ref, ret_ref, rw_ref, lru_ref):
    xb = x_ref[...].astype(bf16)
    c0, c1 = 4 * RET_W, 4 * RET_W + RWKV_PROJ
    ret_ref[...] = jnp.dot(xb, w_ref[:, :c0], preferred_element_type=f32)
    rw_ref[...] = jnp.dot(xb, w_ref[:, c0:c1], preferred_element_type=f32)
    lru_ref[...] = jnp.dot(xb, w_ref[:, c1:], preferred_element_type=f32)


def _in_proj(x, w_bf):
    n = x.shape[0]
    tm = TOKEN_TILE
    row = lambda i: (i, 0)
    return pl.pallas_call(
        _in_proj_kernel,
        grid=(n // tm,),
        in_specs=[pl.BlockSpec((tm, D_MODEL), row),
                  pl.BlockSpec((D_MODEL, D_PROJ), lambda i: (0, 0))],
        out_specs=[pl.BlockSpec((tm, 4 * RET_W), row),
                   pl.BlockSpec((tm, RWKV_PROJ), row),
                   pl.BlockSpec((tm, 2 * LRU_W), row)],
        out_shape=[jax.ShapeDtypeStruct((n, 4 * RET_W), f32),
                   jax.ShapeDtypeStruct((n, RWKV_PROJ), f32),
                   jax.ShapeDtypeStruct((n, 2 * LRU_W), f32)],
        compiler_params=_params("arbitrary"),
        name="in_proj",
    )(x, w_bf)


def _rope_tables(pos):
    half = HEAD_DIM // 2
    inv = ROPE_BASE ** (-jnp.arange(half, dtype=f32) / half)
    ang = pos[:, None] * inv[None, :]
    cos, sin = jnp.cos(ang), jnp.sin(ang)
    cos_f = jnp.tile(jnp.concatenate([cos, cos], axis=-1), (1, RET_H))
    sin_f = jnp.tile(jnp.concatenate([-sin, sin], axis=-1), (1, RET_H))
    return cos_f, sin_f


def _retention_tables(length, n_valid):
    lg = jnp.log1p(-jnp.exp2(-5.0 - jnp.arange(RET_H, dtype=f32)))
    idx = jnp.arange(length, dtype=f32)
    rel = idx[:, None] - idx[None, :]
    mask = jnp.where(rel[None] >= 0, jnp.exp(jnp.maximum(rel, 0.0)[None] * lg[:, None, None]), 0.0)
    q_dec = jnp.exp((idx[:, None] + 1.0) * lg[None, :])
    k_dec = jnp.where(idx[:, None] < n_valid, jnp.exp((n_valid - 1.0 - idx)[:, None] * lg[None, :]), 0.0)
    c_dec = jnp.exp(n_valid * lg)[None, :]
    rep = lambda t: jnp.repeat(t, HEAD_DIM, axis=-1)
    return mask, rep(q_dec), rep(k_dec), rep(c_dec)


def _retention_kernel(ret_ref, cos_ref, sin_ref, mask_ref, qdec_ref, kdec_ref, cdec_ref, s0_ref,
                      y_ref, s_ref, *, nb, length):
    @pl.when(pl.program_id(1) == 0)
    def _():
        s_ref[...] = s0_ref[...]

    m0, m1 = _half_masks()
    lane = lax.broadcasted_iota(i32, (1, LANES), 1)
    first_half = (lane % HEAD_DIM) < (HEAD_DIM // 2)
    bd = _block_diag_mask()

    def rope(x, cs, sn):
        swapped = jnp.where(first_half, pltpu.roll(x, LANES - HEAD_DIM // 2, 1), pltpu.roll(x, HEAD_DIM // 2, 1))
        return x * cs + swapped * sn

    for j in range(nb):
        rows = pl.ds(j * length, length)
        for p in range(RET_H // 2):
            cols = pl.ds(p * LANES, LANES)
            cs, sn = cos_ref[:, cols], sin_ref[:, cols]
            q2 = rope(ret_ref[rows, pl.ds(p * LANES, LANES)], cs, sn)
            k2 = rope(ret_ref[rows, pl.ds(RET_W + p * LANES, LANES)], cs, sn) * (HEAD_DIM ** -0.5)
            v2 = ret_ref[rows, pl.ds(2 * RET_W + p * LANES, LANES)]
            g2 = ret_ref[rows, pl.ds(3 * RET_W + p * LANES, LANES)]
            state = s_ref[j, p]
            o2 = _mm(q2, state) * qdec_ref[:, cols]
            for hh, m in enumerate((m0, m1)):
                sc = _mm_nt(q2 * m, k2) * mask_ref[2 * p + hh]
                o2 = o2 + _mm(sc, v2) * m
            s_ref[j, p] = state * cdec_ref[:, cols] + _mm_tn(k2 * kdec_ref[:, cols], v2) * bd
            mu = _seg_mean(o2, m0, m1)
            oc = o2 - mu
            var = _seg_mean(oc * oc, m0, m1)
            y_ref[rows, cols] = g2 * _sigmoid(g2) * oc * lax.rsqrt(var + LN_EPS)


def _retention(ret, cos_f, sin_f, tables, s0_bd, *, n_batch, n_chunks, nb, length, out_rows):
    mask, q_dec, k_dec, c_dec = tables
    blk = lambda g, c: (g * n_chunks + c, 0)
    const2 = lambda g, c: (0, 0)
    st = lambda g, c: (g, 0, 0, 0)
    return pl.pallas_call(
        functools.partial(_retention_kernel, nb=nb, length=length),
        grid=(n_batch // nb, n_chunks),
        in_specs=[pl.BlockSpec((nb * length, 4 * RET_W), blk),
                  pl.BlockSpec((length, RET_W), lambda g, c: (c, 0)),
                  pl.BlockSpec((length, RET_W), lambda g, c: (c, 0)),
                  pl.BlockSpec((RET_H, length, length), lambda g, c: (0, 0, 0)),
                  pl.BlockSpec((length, RET_W), const2),
                  pl.BlockSpec((length, RET_W), const2),
                  pl.BlockSpec((1, RET_W), const2),
                  pl.BlockSpec((nb, RET_H // 2, LANES, LANES), st)],
        out_specs=[pl.BlockSpec((nb * length, RET_W), blk),
                   pl.BlockSpec((nb, RET_H // 2, LANES, LANES), st)],
        out_shape=[jax.ShapeDtypeStruct((out_rows, RET_W), f32),
                   jax.ShapeDtypeStruct((n_batch, RET_H // 2, LANES, LANES), f32)],
        compiler_params=_params("arbitrary", "arbitrary"),
        name="retention",
    )(ret, cos_f, sin_f, mask, q_dec, k_dec, c_dec, s0_bd)


def _rwkv_kernel(rw_ref, aux_ref, valid_ref, mix_ref, vec_ref, lora_ref, tri_ref, s0_ref,
                 y_ref, s_ref, *scratch, rows, length, chain):
    n_chunks = rows // length
    rw = rw_ref[...]
    if chain:
        xs_ref, = scratch

        @pl.when(pl.program_id(1) == 0)
        def _():
            s_ref[...] = s0_ref[...]
            xs_ref[pl.ds(0, SUBLANES), :] = jnp.broadcast_to(aux_ref[0], (SUBLANES, RWKV_PROJ))

        xs_ref[pl.ds(SUBLANES, rows), :] = rw
        prev = xs_ref[pl.ds(SUBLANES - 1, rows), :]
        xs_ref[pl.ds(0, SUBLANES), :] = rw[rows - SUBLANES:, :]
    else:
        prev = aux_ref[...]
    rwm = rw + (prev - rw) * mix_ref[...]

    valid = valid_ref[...]
    w0, a0, k_k, k_a, r_k, gn_g, gn_b = (vec_ref[pl.ds(i, 1), :] for i in range(7))
    lo = rwm[:, RWKV_LORA_COL:]
    lw = jnp.dot(jnp.tanh(lo), lora_ref[0], precision=_HIGHEST, preferred_element_type=f32)
    la = jnp.dot(lo, lora_ref[1], precision=_HIGHEST, preferred_element_type=f32)
    gate = jnp.dot(_sigmoid(lo), lora_ref[2], precision=_HIGHEST, preferred_element_type=f32)
    logw = -jnp.exp(-_softplus(-(w0 + lw)) - 0.5) * valid
    a = _sigmoid(a0 + la)
    r = rwm[:, :RWKV_W]
    kr = rwm[:, RWKV_W:2 * RWKV_W]
    vr = rwm[:, 2 * RWKV_W:3 * RWKV_W]
    kk_raw = kr * k_k
    kp = kr * (1.0 + (a - 1.0) * k_a) * valid
    cum = jnp.dot(tri_ref[...], logw, precision=_HIGHEST, preferred_element_type=f32)
    g_incl = jnp.exp(cum)
    g_inv = jnp.exp(-cum)
    g_prev = jnp.exp(cum - logw)
    g_end = jnp.concatenate(
        [jnp.broadcast_to(g_incl[(c + 1) * length - 1:(c + 1) * length, :], (length, RWKV_W)) for c in range(n_chunks)],
        axis=0)

    m0, m1 = _half_masks()
    bd = _block_diag_mask()
    stacked = 2 * rows
    ri = lax.broadcasted_iota(i32, (stacked, stacked), 0)
    ci = lax.broadcasted_iota(i32, (stacked, stacked), 1)
    same = (ri // length) == (ci // length)
    strict = jnp.logical_and(same, ci < ri)
    incl = jnp.logical_and(same, ci <= ri)
    eye = (ci == ri).astype(f32)
    stack = lambda t: jnp.concatenate([t * m0, t * m1], axis=0)
    unstack = lambda t: t[:rows] + t[rows:]
    pairs = range(RWKV_H // 2)

    pre = []
    for p in pairs:
        sl = slice(p * LANES, (p + 1) * LANES)
        kk2 = kk_raw[:, sl]
        kk2 = kk2 * lax.rsqrt(jnp.maximum(_seg_sum(kk2 * kk2, m0, m1), 1e-24)) * valid[:, sl]
        d = dict(sl=sl, r2=r[:, sl], v2=vr[:, sl], kp2=kp[:, sl], ge=g_end[:, sl])
        d['kh'] = kk2 * g_prev[:, sl]
        rh = d['r2'] * g_incl[:, sl]
        bt = kk2 * a[:, sl] * g_inv[:, sl]
        kt = d['kp2'] * g_inv[:, sl]
        d['b_end'] = bt * d['ge']
        d['k_end'] = kt * d['ge']
        d['kh_s'], d['rh_s'], d['v_s'] = stack(d['kh']), stack(rh), stack(d['v2'])
        gram = _mm_nt(jnp.concatenate([d['kh_s'], d['rh_s']], axis=0), jnp.concatenate([stack(bt), stack(kt)], axis=0))
        d['x'] = -jnp.where(strict, gram[:stacked, :stacked], 0.0)
        d['a_k'] = jnp.where(strict, gram[:stacked, stacked:], 0.0)
        d['r_b'] = jnp.where(incl, gram[stacked:, :stacked], 0.0)
        d['r_k'] = jnp.where(incl, gram[stacked:, stacked:], 0.0)
        d['t'] = eye + d['x']
        pre.append(d)
    cover = 2
    while cover < length:
        for d in pre:
            d['x'] = _mm(d['x'], d['x'])
            d['t'] = d['t'] + _mm(d['t'], d['x'])
        cover *= 2
    for d in pre:
        rhs0 = -_mm(d['a_k'], d['v_s'])
        tz = _mm(d['t'], jnp.concatenate([rhs0, d['kh_s']], axis=1))
        z0_s = tz[:, :LANES]
        rbz = _mm(d['r_b'], tz)
        d['z0'] = unstack(z0_s)
        d['q'] = unstack(d['rh_s'] - rbz[:, LANES:])
        d['y0'] = unstack(rbz[:, :LANES] + _mm(d['r_k'], d['v_s']))
        d['w'] = unstack(_mm_tn(d['t'], stack(d['b_end'])))

    for p, d in zip(pairs, pre):
        sl, ge, kh, v2 = d['sl'], d['ge'], d['kh'], d['v2']
        state = s_ref[0, p] if chain else None
        ys = []
        for c in range(n_chunks):
            cr = slice(c * length, (c + 1) * length)
            if not chain:
                state = s0_ref[c, p]
            ys.append(_mm_nt(d['q'][cr], state) + d['y0'][cr])
            n_c = (_mm_tn(d['z0'][cr], d['b_end'][cr]) + _mm_tn(v2[cr], d['k_end'][cr])) * bd
            kw = _mm_tn(kh[cr], d['w'][cr]) * bd
            state = state * ge[(c + 1) * length - 1:(c + 1) * length, :] - _mm(state, kw) + n_c
            if not chain:
                s_ref[c, p] = state
        if chain:
            s_ref[0, p] = state
        y2 = jnp.concatenate(ys, axis=0) if n_chunks > 1 else ys[0]

        mu = _seg_mean(y2, m0, m1)
        yc = y2 - mu
        var = _seg_mean(yc * yc, m0, m1)
        yn = yc * lax.rsqrt(var + RWKV_GN_EPS) * gn_g[:, sl] + gn_b[:, sl]
        bonus = _seg_sum(d['r2'] * d['kp2'] * r_k[:, sl], m0, m1) * v2
        y_ref[:, pl.ds(p * LANES, LANES)] = (yn + bonus) * gate[:, sl]


def _rwkv(rw, aux, valid, mix, vec, lora, tri, s0_bd, *, n_groups, n_steps, rows, length, chain):
    blk = lambda g, c: (g * n_steps + c, 0)
    const2 = lambda g, c: (0, 0)
    st = lambda g, c: (g, 0, 0, 0)
    n_state = 1 if chain else rows // length
    aux_spec = (pl.BlockSpec((1, 1, RWKV_PROJ), lambda g, c: (g, 0, 0)) if chain
                else pl.BlockSpec((rows, RWKV_PROJ), blk))
    return pl.pallas_call(
        functools.partial(_rwkv_kernel, rows=rows, length=length, chain=chain),
        grid=(n_groups, n_steps),
        in_specs=[pl.BlockSpec((rows, RWKV_PROJ), blk),
                  aux_spec,
                  pl.BlockSpec((rows, RWKV_W), const2),
                  pl.BlockSpec((1, RWKV_PROJ), const2),
                  pl.BlockSpec((SUBLANES, RWKV_W), const2),
                  pl.BlockSpec((3, LANES, RWKV_W), lambda g, c: (0, 0, 0)),
                  pl.BlockSpec((rows, rows), const2),
                  pl.BlockSpec((n_state, RWKV_H // 2, LANES, LANES), st)],
        out_specs=[pl.BlockSpec((rows, RWKV_W), blk),
                   pl.BlockSpec((n_state, RWKV_H // 2, LANES, LANES), st)],
        out_shape=[jax.ShapeDtypeStruct((n_groups * n_steps * rows, RWKV_W), f32),
                   jax.ShapeDtypeStruct((n_groups * n_state, RWKV_H // 2, LANES, LANES), f32)],
        scratch_shapes=[pltpu.VMEM((rows + SUBLANES, RWKV_PROJ), f32)] if chain else [],
        compiler_params=_params("arbitrary", "arbitrary"),
        name="rwkv7",
    )(rw, aux, valid, mix, vec, lora, tri, s0_bd)


def _chunk_tri(rows, length):
    idx = jnp.arange(rows)
    same = (idx[:, None] // length) == (idx[None, :] // length)
    return jnp.logical_and(same, idx[None, :] <= idx[:, None]).astype(f32)


def _lru_kernel(lru_ref, valid_ref, vec_ref, wa_ref, wx_ref, conv0_ref, h0_ref,
                y_ref, h_ref, xext_ref, a_ref, b_ref, hs_ref, *, nb, length):
    @pl.when(pl.program_id(1) == 0)
    def _():
        h_ref[...] = h0_ref[...]
        xext_ref[:, pl.ds(0, SUBLANES), :] = conv0_ref[...]

    valid = valid_ref[...] > 0.5
    cw = [vec_ref[pl.ds(i, 1), :] for i in range(CONV_WIDTH)]
    cb, ba, bx, lam = (vec_ref[pl.ds(i, 1), :] for i in range(CONV_WIDTH, CONV_WIDTH + 4))
    sp = _softplus(-lam)
    for j in range(nb):
        rows = pl.ds(j * length, length)
        gbr = lru_ref[rows, pl.ds(0, LRU_W)]
        x = lru_ref[rows, pl.ds(LRU_W, LRU_W)]
        xext_ref[j, pl.ds(SUBLANES, length), :] = x
        xc = cb + x * cw[CONV_WIDTH - 1]
        for t in range(CONV_WIDTH - 1):
            xc = xc + xext_ref[j, pl.ds(SUBLANES - (CONV_WIDTH - 1) + t, length), :] * cw[t]
        xext_ref[j, pl.ds(0, SUBLANES), :] = x[length - SUBLANES:, :]
        gate_a = _sigmoid(_mm(xc, wa_ref[...]) + ba)
        gate_x = _sigmoid(_mm(xc, wx_ref[...]) + bx)
        log_a = -LRU_C * gate_a * sp
        a = jnp.exp(log_a)
        b = xc * gate_x * jnp.sqrt(1.0 - jnp.exp(2.0 * log_a))
        a_ref[...] = jnp.where(valid, a, 1.0)
        b_ref[...] = jnp.where(valid, b, 0.0)

        def step(t, h):
            h = a_ref[pl.ds(t, 1), :] * h + b_ref[pl.ds(t, 1), :]
            hs_ref[pl.ds(t, 1), :] = h
            return h

        h = lax.fori_loop(0, length, step, h_ref[j, pl.ds(0, 1), :])
        h_ref[j] = jnp.broadcast_to(h, (SUBLANES, LRU_W))
        c = 0.7978845608028654
        gelu = 0.5 * gbr * (1.0 + jnp.tanh(c * (gbr + 0.044715 * gbr * gbr * gbr)))
        y_ref[rows, :] = hs_ref[...] * gelu


def _lru(lru, valid, vec, wa_bd, wx_bd, conv0, h0, *, n_batch, n_chunks, nb, length, out_rows):
    blk = lambda g, c: (g * n_chunks + c, 0)
    const2 = lambda g, c: (0, 0)
    st = lambda g, c: (g, 0, 0)
    return pl.pallas_call(
        functools.partial(_lru_kernel, nb=nb, length=length),
        grid=(n_batch // nb, n_chunks),
        in_specs=[pl.BlockSpec((nb * length, 2 * LRU_W), blk),
                  pl.BlockSpec((length, LRU_W), const2),
                  pl.BlockSpec((SUBLANES, LRU_W), const2),
                  pl.BlockSpec((LRU_W, LRU_W), const2),
                  pl.BlockSpec((LRU_W, LRU_W), const2),
                  pl.BlockSpec((nb, SUBLANES, LRU_W), st),
                  pl.BlockSpec((nb, SUBLANES, LRU_W), st)],
        out_specs=[pl.BlockSpec((nb * length, LRU_W), blk),
                   pl.BlockSpec((nb, SUBLANES, LRU_W), st)],
        out_shape=[jax.ShapeDtypeStruct((out_rows, LRU_W), f32),
                   jax.ShapeDtypeStruct((n_batch, SUBLANES, LRU_W), f32)],
        scratch_shapes=[pltpu.VMEM((nb, length + SUBLANES, LRU_W), f32),
                        pltpu.VMEM((length, LRU_W), f32),
                        pltpu.VMEM((length, LRU_W), f32),
                        pltpu.VMEM((length, LRU_W), f32)],
        compiler_params=_params("arbitrary", "arbitrary"),
        name="rg_lru",
    )(lru, valid, vec, wa_bd, wx_bd, conv0, h0)


def _out_proj_kernel(yr_ref, yw_ref, yl_ref, x_ref, w_ref, ln_ref, wr_ref, br_ref,
                     x1_ref, code_ref, g_ref, cnt_ref, *, alpha):
    @pl.when(pl.program_id(0) == 0)
    def _():
        cnt_ref[...] = jnp.zeros(cnt_ref.shape, f32)

    mixed = (_mm(yr_ref[...], w_ref[pl.ds(0, RET_W), :])
             + _mm(yw_ref[...], w_ref[pl.ds(RET_W, RWKV_W), :])
             + _mm(yl_ref[...], w_ref[pl.ds(RET_W + RWKV_W, LRU_W), :]))
    x1 = _layer_norm_rows(alpha * x_ref[...] + mixed, ln_ref[pl.ds(0, 1), :], ln_ref[pl.ds(1, 1), :])
    x1_ref[...] = x1
    logits = jnp.dot(x1, wr_ref[...], precision=_HIGHEST, preferred_element_type=f32) + br_ref[...]
    tm = logits.shape[0]
    lane = lax.broadcasted_iota(i32, logits.shape, 1).astype(f32)
    top_v = jnp.zeros(logits.shape, f32)
    work = logits
    v_max = None
    picks, onehots = [], []
    for k in range(TOP_K):
        v = jnp.max(work, axis=-1, keepdims=True)
        idx = jnp.min(jnp.where(work == v, lane, float(LANES)), axis=-1, keepdims=True)
        if k == 0:
            v_max = v
        hit = lane == idx
        picks.append(idx)
        onehots.append(hit.astype(f32))
        top_v = jnp.where(lane == k, jnp.exp(v - v_max), top_v)
        work = jnp.where(hit, -jnp.inf, work)
    g_ref[...] = top_v / jnp.sum(top_v, axis=-1, keepdims=True)

    total = onehots[0] + onehots[1] + onehots[2] + onehots[3]
    ri = lax.broadcasted_iota(i32, (tm, tm), 0)
    ci = lax.broadcasted_iota(i32, (tm, tm), 1)
    before = _mm((ci < ri).astype(f32), total) + cnt_ref[pl.ds(0, 1), :]
    code = jnp.zeros(logits.shape, f32)
    for k in range(TOP_K):
        rank = jnp.sum(onehots[k] * before, axis=-1, keepdims=True)
        code = jnp.where(lane == k, rank * float(1 << EXPERT_BITS) + picks[k], code)
    code_ref[...] = code.astype(i32)
    cnt_ref[...] = cnt_ref[...] + jnp.sum(total, axis=0, keepdims=True)


def _out_proj_router(y_ret, y_rwkv, y_lru, x, w_out_bf, ln, w_router, b_router, alpha):
    n = x.shape[0]
    tm = TOKEN_TILE
    row = lambda i: (i, 0)
    const = lambda i: (0, 0)
    return pl.pallas_call(
        functools.partial(_out_proj_kernel, alpha=alpha),
        grid=(n // tm,),
        in_specs=[pl.BlockSpec((tm, RET_W), row), pl.BlockSpec((tm, RWKV_W), row), pl.BlockSpec((tm, LRU_W), row),
                  pl.BlockSpec((tm, D_MODEL), row),
                  pl.BlockSpec((D_MODEL, D_MODEL), const),
                  pl.BlockSpec((SUBLANES, D_MODEL), const),
                  pl.BlockSpec((D_MODEL, LANES), const),
                  pl.BlockSpec((1, LANES), const)],
        out_specs=[pl.BlockSpec((tm, D_MODEL), row), pl.BlockSpec((tm, LANES), row), pl.BlockSpec((tm, LANES), row),
                   pl.BlockSpec((SUBLANES, LANES), const)],
        out_shape=[jax.ShapeDtypeStruct((n, D_MODEL), f32),
                   jax.ShapeDtypeStruct((n, LANES), i32),
                   jax.ShapeDtypeStruct((n, LANES), f32),
                   jax.ShapeDtypeStruct((SUBLANES, LANES), f32)],
        compiler_params=_params("arbitrary"),
        name="out_proj_router",
    )(y_ret, y_rwkv, y_lru, x, w_out_bf, ln, w_router, b_router)


def _dest_row(code, pstart_ref):
    return pstart_ref[code & (N_EXPERTS - 1)] + (code >> EXPERT_BITS)


def _dispatch_kernel(pstart_ref, cnt_ref, n_used_ref, code_ref, x1_ref, xs_hbm, zero_ref, sem, zsem):
    tm = ROUTE_TILE
    bm = MOE_BLOCK

    @pl.when(pl.program_id(0) == 0)
    def _():
        zero_ref[...] = jnp.zeros(zero_ref.shape, f32)
        tail = lambda j: pltpu.make_async_copy(zero_ref, xs_hbm.at[pl.ds(j * bm, bm), :], zsem)

        def tail_start(j, carry):
            tail(j).start()
            return carry

        def tail_wait(j, carry):
            tail(j).wait()
            return carry

        n_blocks = xs_hbm.shape[0] // bm
        lax.fori_loop(n_used_ref[0], n_blocks, tail_start, 0)
        lax.fori_loop(n_used_ref[0], n_blocks, tail_wait, 0)
        for e in range(N_EXPERTS):
            lo = pstart_ref[e] + cnt_ref[e]
            hi = pstart_ref[e] + (cnt_ref[e] + bm - 1) // bm * bm
            fill = lambda r: pltpu.make_async_copy(zero_ref.at[pl.ds(0, 1), :], xs_hbm.at[pl.ds(r, 1), :], zsem)

            def start(r, carry):
                fill(r).start()
                return carry

            def wait(r, carry):
                fill(r).wait()
                return carry

            lax.fori_loop(lo, hi, start, 0)
            lax.fori_loop(lo, hi, wait, 0)

    def issue(r, carry):
        for k in range(TOP_K):
            d = _dest_row(code_ref[r * TOP_K + k], pstart_ref)
            pltpu.make_async_copy(x1_ref.at[pl.ds(r, 1), :], xs_hbm.at[pl.ds(d, 1), :], sem).start()
        return carry

    lax.fori_loop(0, tm, issue, 0)
    for k in range(TOP_K):
        pltpu.make_async_copy(x1_ref, xs_hbm.at[pl.ds(0, tm), :], sem).wait()


def _dispatch(pstart, counts, n_used, code_flat, x1, n_rows):
    n = x1.shape[0]
    tm = ROUTE_TILE
    grid_spec = pltpu.PrefetchScalarGridSpec(
        num_scalar_prefetch=3,
        grid=(n // tm,),
        in_specs=[pl.BlockSpec((tm * TOP_K,), lambda i, ps, ct, nu: (i,), memory_space=pltpu.SMEM),
                  pl.BlockSpec((tm, D_MODEL), lambda i, ps, ct, nu: (i, 0))],
        out_specs=pl.BlockSpec(memory_space=pl.ANY),
        scratch_shapes=[pltpu.VMEM((MOE_BLOCK, D_MODEL), f32),
                        pltpu.SemaphoreType.DMA(()), pltpu.SemaphoreType.DMA(())],
    )
    return pl.pallas_call(
        _dispatch_kernel,
        grid_spec=grid_spec,
        out_shape=jax.ShapeDtypeStruct((n_rows, D_MODEL), f32),
        compiler_params=_params("arbitrary"),
        name="moe_dispatch",
    )(pstart, counts, n_used, code_flat, x1)


def _expert_kernel(blk_e_ref, n_used_ref, xs_ref, wgu_ref, bgu_ref, wdn_ref, bdn_ref,
                   out_ref, wgu_bf, wdn_bf):
    i = pl.program_id(0)
    prev = jnp.maximum(i - 1, 0)
    new_expert = jnp.logical_or(i == 0, blk_e_ref[i] != blk_e_ref[prev])
    used = i < n_used_ref[0]

    @pl.when(used)
    def _():
        @pl.when(new_expert)
        def _():
            wgu_bf[...] = wgu_ref[0, 0].astype(bf16)
            wdn_bf[...] = wdn_ref[0, 0].astype(bf16)

        gu = jnp.dot(xs_ref[...].astype(bf16), wgu_bf[...], preferred_element_type=f32) + bgu_ref[0, 0]
        g = jnp.minimum(gu[:, :D_EXPERT], SWIGLU_LIMIT)
        u = jnp.clip(gu[:, D_EXPERT:], -SWIGLU_LIMIT, SWIGLU_LIMIT)
        hdn = (u + 1.0) * g * _sigmoid(SWIGLU_ALPHA * g)
        out_ref[...] = jnp.dot(hdn.astype(bf16), wdn_bf[...], preferred_element_type=f32) + bdn_ref[0, 0]

    @pl.when(jnp.logical_not(used))
    def _():
        out_ref[...] = jnp.zeros(out_ref.shape, f32)


def _experts(blk_e, n_used, xs, w_gu, b_gu, w_down, b_down, layer):
    n_blocks = blk_e.shape[0]
    bm = MOE_BLOCK
    by_e = lambda i, be, nu: (layer, be[i], 0, 0)
    x_blk = lambda i, be, nu: (jnp.minimum(i, nu[0] - 1), 0)
    grid_spec = pltpu.PrefetchScalarGridSpec(
        num_scalar_prefetch=2,
        grid=(n_blocks,),
        in_specs=[pl.BlockSpec((bm, D_MODEL), x_blk),
                  pl.BlockSpec((1, 1, D_MODEL, 2 * D_EXPERT), by_e),
                  pl.BlockSpec((1, 1, 1, 2 * D_EXPERT), by_e),
                  pl.BlockSpec((1, 1, D_EXPERT, D_MODEL), by_e),
                  pl.BlockSpec((1, 1, 1, D_MODEL), by_e)],
        out_specs=pl.BlockSpec((bm, D_MODEL), lambda i, be, nu: (i, 0)),
        scratch_shapes=[pltpu.VMEM((D_MODEL, 2 * D_EXPERT), bf16),
                        pltpu.VMEM((D_EXPERT, D_MODEL), bf16)],
    )
    depth = w_gu.shape[0]
    return pl.pallas_call(
        _expert_kernel,
        grid_spec=grid_spec,
        out_shape=jax.ShapeDtypeStruct((n_blocks * bm, D_MODEL), f32),
        compiler_params=_params("arbitrary"),
        name="moe_experts",
    )(blk_e, n_used, xs, w_gu, b_gu.reshape(depth, N_EXPERTS, 1, 2 * D_EXPERT),
      w_down, b_down.reshape(depth, N_EXPERTS, 1, D_MODEL))


def _combine_kernel(pstart_ref, code_ref, gates_ref, x1_ref, ln_ref, yb_hbm, out_ref, gbuf, sem, *, alpha):
    tm = ROUTE_TILE

    def issue(r, carry):
        for k in range(TOP_K):
            d = _dest_row(code_ref[r * TOP_K + k], pstart_ref)
            pltpu.make_async_copy(yb_hbm.at[pl.ds(d, 1), :], gbuf.at[k, pl.ds(r, 1), :], sem).start()
        return carry

    lax.fori_loop(0, tm, issue, 0)
    for k in range(TOP_K):
        pltpu.make_async_copy(yb_hbm.at[pl.ds(0, tm), :], gbuf.at[k], sem).wait()
    gates = gates_ref[...]
    y = gbuf[0] * gates[:, 0:1]
    for k in range(1, TOP_K):
        y = y + gbuf[k] * gates[:, k:k + 1]
    out_ref[...] = _layer_norm_rows(alpha * x1_ref[...] + y, ln_ref[pl.ds(0, 1), :], ln_ref[pl.ds(1, 1), :])


def _combine(pstart, code_flat, gates, x1, ln, yb, alpha):
    n = x1.shape[0]
    tm = ROUTE_TILE
    row = lambda i, ps: (i, 0)
    grid_spec = pltpu.PrefetchScalarGridSpec(
        num_scalar_prefetch=1,
        grid=(n // tm,),
        in_specs=[pl.BlockSpec((tm * TOP_K,), lambda i, ps: (i,), memory_space=pltpu.SMEM),
                  pl.BlockSpec((tm, LANES), row),
                  pl.BlockSpec((tm, D_MODEL), row),
                  pl.BlockSpec((SUBLANES, D_MODEL), lambda i, ps: (0, 0)),
                  pl.BlockSpec(memory_space=pl.ANY)],
        out_specs=pl.BlockSpec((tm, D_MODEL), row),
        scratch_shapes=[pltpu.VMEM((TOP_K, tm, D_MODEL), f32), pltpu.SemaphoreType.DMA(())],
    )
    return pl.pallas_call(
        functools.partial(_combine_kernel, alpha=alpha),
        grid_spec=grid_spec,
        out_shape=jax.ShapeDtypeStruct((n, D_MODEL), f32),
        compiler_params=_params("arbitrary"),
        name="moe_combine",
    )(pstart, code_flat, gates, x1, ln, yb)


def _block_tables(counts, n_tokens):
    bm = MOE_BLOCK
    padded = (counts + bm - 1) // bm * bm
    pad_end = jnp.cumsum(padded)
    pstart = (pad_end - padded).astype(i32)
    n_blocks = -(-(n_tokens * TOP_K + N_EXPERTS * (bm - 1)) // bm)
    first_row = jnp.arange(n_blocks, dtype=pad_end.dtype) * bm
    blk_e = jnp.minimum(jnp.sum(pad_end[None, :] <= first_row[:, None], axis=1), N_EXPERTS - 1).astype(i32)
    n_used = (pad_end[-1] // bm).astype(i32).reshape(1)
    return pstart, blk_e, n_used, n_blocks * bm


def _pad_time(t, n_batch, n_t, t_pad):
    w = t.shape[-1]
    return jnp.pad(t.reshape(n_batch, n_t, w), ((0, 0), (0, t_pad - n_t), (0, 0))).reshape(n_batch * t_pad, w)


def _to_block_diag(s):
    b, h = s.shape[:2]
    s = s.reshape(b, h // 2, 2, HEAD_DIM, HEAD_DIM)
    z = jnp.zeros_like(s[:, :, 0])
    top = jnp.concatenate([s[:, :, 0], z], axis=-1)
    bot = jnp.concatenate([z, s[:, :, 1]], axis=-1)
    return jnp.concatenate([top, bot], axis=-2)


def _from_block_diag(s):
    b, hp = s.shape[:2]
    return jnp.stack([s[:, :, :HEAD_DIM, :HEAD_DIM], s[:, :, HEAD_DIM:, HEAD_DIM:]], axis=2).reshape(
        b, 2 * hp, HEAD_DIM, HEAD_DIM)


def _block_diag_weight(w):
    h = w.shape[0]
    eye = jnp.eye(h, dtype=w.dtype)
    return (eye[:, None, :, None] * w[:, :, None, :]).reshape(h * HEAD_DIM, h * HEAD_DIM)


def _rows8(*rows):
    width = rows[0].shape[-1]
    m = jnp.stack([r.reshape(width) for r in rows])
    return jnp.pad(m, ((0, SUBLANES - m.shape[0]), (0, 0)))


def _last_rows(t, start, n_seq, seq_len, k):
    return jnp.stack([lax.slice_in_dim(t, start + seq_len - k + j, start + n_seq * seq_len, seq_len, axis=0)
                      for j in range(k)], axis=1)


def _layer(h, p, moe, layer, alpha, st_s, bp, tp, bs, ts):
    n_p, n_s = bp * tp, bs * ts
    tpad = SAMPLE_T_PAD
    ret, rw, lru = _in_proj(h, p['w_in'].astype(bf16))

    def merge(y_p, y_s):
        w = y_p.shape[-1]
        return jnp.concatenate([y_p, y_s.reshape(bs, tpad, w)[:, :ts].reshape(n_s, w)], axis=0)

    c_p = RET_CHUNK if tp % RET_CHUNK == 0 else tp
    cos_p, sin_p = _rope_tables(jnp.arange(tp, dtype=f32))
    cos_s, sin_s = _rope_tables(PAST_LEN + jnp.arange(tpad, dtype=f32))
    y_ret, sret_p = _retention(ret, cos_p, sin_p, _retention_tables(c_p, c_p),
                               jnp.zeros((bp, RET_H // 2, LANES, LANES), f32),
                               n_batch=bp, n_chunks=tp // c_p, nb=1, length=c_p, out_rows=n_p)
    y_ret_s, sret_s = _retention(_pad_time(ret[n_p:], bs, ts, tpad), cos_s, sin_s, _retention_tables(tpad, ts),
                                 _to_block_diag(st_s['ret']),
                                 n_batch=bs, n_chunks=1, nb=SUBLANES, length=tpad, out_rows=bs * tpad)
    y_ret = merge(y_ret, y_ret_s)

    lora = jnp.zeros((3, LANES, RWKV_W), f32)
    lora = lora.at[0, 0:32].set(p['w_up']).at[1, 32:64].set(p['a_up']).at[2, 64:128].set(p['g_up'])
    vec = _rows8(p['w0'], p['a0'], p['k_k'], p['k_a'], p['r_k'], p['gn_g'], p['gn_b'])
    mix = p['mix'].reshape(1, RWKV_PROJ)
    rows = RWKV_ROWS
    y_rwkv, srw_p = _rwkv(rw, jnp.zeros((bp, 1, RWKV_PROJ), f32), jnp.ones((rows, RWKV_W), f32), mix, vec, lora,
                          _chunk_tri(rows, RWKV_CHUNK), jnp.zeros((bp, RWKV_H // 2, LANES, LANES), f32),
                          n_groups=bp, n_steps=tp // rows, rows=rows, length=RWKV_CHUNK, chain=True)
    valid_s = (jnp.arange(tpad) < ts).astype(f32)[:, None]
    rw_s3 = rw[n_p:].reshape(bs, ts, RWKV_PROJ)
    prev_s = jnp.concatenate([st_s['shift'][:, None, :], rw_s3[:, :-1]], axis=1).reshape(n_s, RWKV_PROJ)
    seq_per_blk = rows // tpad
    y_rwkv_s, srw_s = _rwkv(_pad_time(rw[n_p:], bs, ts, tpad), _pad_time(prev_s, bs, ts, tpad),
                            jnp.tile(jnp.broadcast_to(valid_s, (tpad, RWKV_W)), (seq_per_blk, 1)),
                            mix, vec, lora, _chunk_tri(rows, tpad), _to_block_diag(st_s['rwkv']),
                            n_groups=bs // seq_per_blk, n_steps=1, rows=rows, length=tpad, chain=False)
    y_rwkv = merge(y_rwkv, y_rwkv_s)

    lvec = _rows8(p['conv_w'][0], p['conv_w'][1], p['conv_w'][2], p['conv_w'][3],
                  p['conv_b'], p['ba'], p['bx'], p['lam'])
    wa_bd = _block_diag_weight(p['wa']).astype(bf16)
    wx_bd = _block_diag_weight(p['wx']).astype(bf16)
    l_l = RET_CHUNK if tp % RET_CHUNK == 0 else tp
    y_lru, h_p = _lru(lru, jnp.ones((l_l, LRU_W), f32), lvec, wa_bd, wx_bd,
                      jnp.zeros((bp, SUBLANES, LRU_W), f32), jnp.zeros((bp, SUBLANES, LRU_W), f32),
                      n_batch=bp, n_chunks=tp // l_l, nb=1, length=l_l, out_rows=n_p)
    conv0_s = jnp.pad(st_s['conv'], ((0, 0), (SUBLANES - (CONV_WIDTH - 1), 0), (0, 0)))
    h0_s = jnp.broadcast_to(st_s['lru'][:, None, :], (bs, SUBLANES, LRU_W))
    y_lru_s, h_s = _lru(_pad_time(lru[n_p:], bs, ts, tpad), jnp.broadcast_to(valid_s, (tpad, LRU_W)),
                        lvec, wa_bd, wx_bd, conv0_s, h0_s,
                        n_batch=bs, n_chunks=1, nb=SUBLANES, length=tpad, out_rows=bs * tpad)
    y_lru = merge(y_lru, y_lru_s)

    w_router = jnp.pad(p['w_router'], ((0, 0), (0, LANES - N_EXPERTS)))
    b_router = jnp.pad(p['b_router'], (0, LANES - N_EXPERTS), constant_values=-1e30).reshape(1, LANES)
    x1, code, gates, cnt = _out_proj_router(y_ret, y_rwkv, y_lru, h, p['w_out'].astype(bf16),
                                            _rows8(p['ln1_g'], p['ln1_b']), w_router, b_router, alpha)

    counts = cnt[0, :N_EXPERTS].astype(i32)
    pstart, blk_e, n_used, n_rows = _block_tables(counts, n_p + n_s)
    code_flat = code[:, :TOP_K].reshape(-1)
    xs = _dispatch(pstart, counts, n_used, code_flat, x1, n_rows)
    yb = _experts(blk_e, n_used, xs, moe['w_gu'], moe['b_gu'], moe['w_down'], moe['b_down'], layer)
    x2 = _combine(pstart, code_flat, gates, x1, _rows8(p['ln2_g'], p['ln2_b']), yb, alpha)

    keep = CONV_WIDTH - 1
    xbr = lru[:, LRU_W:]
    new_p = (_from_block_diag(sret_p), _from_block_diag(srw_p), _last_rows(rw, 0, bp, tp, 1)[:, 0], h_p[:, 0],
             _last_rows(xbr, 0, bp, tp, keep))
    new_s = (_from_block_diag(sret_s), _from_block_diag(srw_s), _last_rows(rw, n_p, bs, ts, 1)[:, 0], h_s[:, 0],
             _last_rows(xbr, n_p, bs, ts, keep))
    return x2, new_p, new_s


def kernel(x_prompt, x_sample, state_ret, state_rwkv, state_rwkv_shift, state_lru, state_conv,
           w_in, w_out, ln1_g, ln1_b, ln2_g, ln2_b,
           rwkv_mix, rwkv_w0, rwkv_w_up, rwkv_a0, rwkv_a_up, rwkv_g_up, rwkv_k_k, rwkv_k_a, rwkv_r_k,
           rwkv_gn_g, rwkv_gn_b, lru_conv_w, lru_conv_b, lru_wa, lru_ba, lru_wx, lru_bx, lru_lambda,
           moe_w_router, moe_b_router, moe_w_gate_up, moe_b_gate_up, moe_w_down, moe_b_down):
    bp, tp, _ = x_prompt.shape
    bs, ts, _ = x_sample.shape
    depth = w_in.shape[0]
    alpha = (2.0 * depth) ** 0.25
    moe = {'w_gu': moe_w_gate_up, 'b_gu': moe_b_gate_up, 'w_down': moe_w_down, 'b_down': moe_b_down}
    h = jnp.concatenate([x_prompt.reshape(bp * tp, D_MODEL), x_sample.reshape(bs * ts, D_MODEL)], axis=0)
    new_p, new_s = [], []
    for l in range(depth):
        p = {'w_in': w_in[l], 'w_out': w_out[l], 'ln1_g': ln1_g[l], 'ln1_b': ln1_b[l],
             'ln2_g': ln2_g[l], 'ln2_b': ln2_b[l], 'mix': rwkv_mix[l], 'w0': rwkv_w0[l],
             'w_up': rwkv_w_up[l], 'a0': rwkv_a0[l], 'a_up': rwkv_a_up[l], 'g_up': rwkv_g_up[l],
             'k_k': rwkv_k_k[l], 'k_a': rwkv_k_a[l], 'r_k': rwkv_r_k[l], 'gn_g': rwkv_gn_g[l],
             'gn_b': rwkv_gn_b[l], 'conv_w': lru_conv_w[l], 'conv_b': lru_conv_b[l], 'wa': lru_wa[l],
             'ba': lru_ba[l], 'wx': lru_wx[l], 'bx': lru_bx[l], 'lam': lru_lambda[l],
             'w_router': moe_w_router[l], 'b_router': moe_b_router[l]}
        st_s = {'ret': state_ret[l], 'rwkv': state_rwkv[l], 'shift': state_rwkv_shift[l],
                'lru': state_lru[l], 'conv': state_conv[l]}
        h, sp, ss = _layer(h, p, moe, l, alpha, st_s, bp, tp, bs, ts)
        new_p.append(sp)
        new_s.append(ss)
    n_p = bp * tp
    outs = [h[:n_p].reshape(bp, tp, D_MODEL), h[n_p:].reshape(bs, ts, D_MODEL)]
    for i in range(5):
        outs.append(jnp.stack([s[i] for s in new_p]))
        outs.append(jnp.stack([s[i] for s in new_s]))
    return tuple(outs)
```

```python
import functools

import jax
import jax.numpy as jnp
from jax import lax
from jax.experimental import pallas as pl
from jax.experimental.pallas import tpu as pltpu

f32 = jnp.float32
bf16 = jnp.bfloat16
i32 = jnp.int32

D_MODEL = 1024
HEAD_DIM = 64
RET_W = 256
RET_H = 4
RET_CHUNK = 128
ROPE_BASE = 10000.0
RWKV_W = 512
RWKV_H = 8
RWKV_PROJ = 1664
RWKV_LORA_COL = 1536
RWKV_GN_EPS = 64e-5
RWKV_CHUNK = 64
RWKV_ROWS = 128
LRU_W = 256
LRU_C = 8.0
CONV_WIDTH = 4
D_PROJ = 3200
N_EXPERTS = 32
TOP_K = 4
D_EXPERT = 1024
SWIGLU_LIMIT = 7.0
SWIGLU_ALPHA = 1.702
LN_EPS = 1e-5
PAST_LEN = 16384.0

LANES = 128
SUBLANES = 8
SAMPLE_T_PAD = 16
TOKEN_TILE = 512
MOE_BLOCK = 512
LOCAL_ROWS = TOP_K * TOKEN_TILE + N_EXPERTS * SUBLANES
RUN_PIECES = tuple(SUBLANES << j for j in reversed(range((TOKEN_TILE // SUBLANES).bit_length())))
WAIT_PIECES = tuple(SUBLANES << j for j in reversed(range((LOCAL_ROWS // SUBLANES).bit_length())))
VMEM_LIMIT = 56 * 1024 * 1024

_HIGHEST = lax.Precision.HIGHEST
_NT = (((1,), (1,)), ((), ()))
_TN = (((0,), (0,)), ((), ()))


def _params(*sem):
    return pltpu.CompilerParams(dimension_semantics=sem, vmem_limit_bytes=VMEM_LIMIT)


def _mm(a, b):
    return jnp.dot(a.astype(bf16), b.astype(bf16), preferred_element_type=f32)


def _mm_nt(a, b):
    return lax.dot_general(a.astype(bf16), b.astype(bf16), _NT, preferred_element_type=f32)


def _mm_tn(a, b):
    return lax.dot_general(a.astype(bf16), b.astype(bf16), _TN, preferred_element_type=f32)


def _softplus(x):
    return jnp.maximum(x, 0.0) + jnp.log(1.0 + jnp.exp(-jnp.abs(x)))


def _sigmoid(x):
    return 1.0 / (1.0 + jnp.exp(-x))


def _half_masks():
    lane = lax.broadcasted_iota(i32, (1, LANES), 1)
    m0 = (lane < HEAD_DIM).astype(f32)
    return m0, 1.0 - m0


def _seg_mean(x, m0, m1):
    s0 = jnp.sum(x * m0, axis=-1, keepdims=True)
    s1 = jnp.sum(x * m1, axis=-1, keepdims=True)
    return (m0 * s0 + m1 * s1) * (1.0 / HEAD_DIM)


def _seg_sum(x, m0, m1):
    s0 = jnp.sum(x * m0, axis=-1, keepdims=True)
    s1 = jnp.sum(x * m1, axis=-1, keepdims=True)
    return m0 * s0 + m1 * s1


def _block_diag_mask():
    r = lax.broadcasted_iota(i32, (LANES, LANES), 0) // HEAD_DIM
    c = lax.broadcasted_iota(i32, (LANES, LANES), 1) // HEAD_DIM
    return (r == c).astype(f32)


def _layer_norm_rows(z, g, b):
    mu = jnp.mean(z, axis=-1, keepdims=True)
    zc = z - mu
    var = jnp.mean(zc * zc, axis=-1, keepdims=True)
    return zc * lax.rsqrt(var + LN_EPS) * g + b


def _in_proj_kernel(x_ref, w_ref, ret_ref, rw_ref, lru_ref):
    xb = x_ref[...].astype(bf16)
    c0, c1 = 4 * RET_W, 4 * RET_W + RWKV_PROJ
    ret_ref[...] = jnp.dot(xb, w_ref[:, :c0], preferred_element_type=f32)
    rw_ref[...] = jnp.dot(xb, w_ref[:, c0:c1], preferred_element_type=f32)
    lru_ref[...] = jnp.dot(xb, w_ref[:, c1:], preferred_element_type=f32)


def _in_proj(x, w_bf):
    n = x.shape[0]
    tm = TOKEN_TILE
    row = lambda i: (i, 0)
    return pl.pallas_call(
        _in_proj_kernel,
        grid=(n // tm,),
        in_specs=[pl.BlockSpec((tm, D_MODEL), row),
                  pl.BlockSpec((D_MODEL, D_PROJ), lambda i: (0, 0))],
        out_specs=[pl.BlockSpec((tm, 4 * RET_W), row),
                   pl.BlockSpec((tm, RWKV_PROJ), row),
                   pl.BlockSpec((tm, 2 * LRU_W), row)],
        out_shape=[jax.ShapeDtypeStruct((n, 4 * RET_W), f32),
                   jax.ShapeDtypeStruct((n, RWKV_PROJ), f32),
                   jax.ShapeDtypeStruct((n, 2 * LRU_W), f32)],
        compiler_params=_params("arbitrary"),
        name="in_proj",
    )(x, w_bf)


def _rope_tables(pos):
    half = HEAD_DIM // 2
    inv = ROPE_BASE ** (-jnp.arange(half, dtype=f32) / half)
    ang = pos[:, None] * inv[None, :]
    cos, sin = jnp.cos(ang), jnp.sin(ang)
    cos_f = jnp.tile(jnp.concatenate([cos, cos], axis=-1), (1, RET_H))
    sin_f = jnp.tile(jnp.concatenate([-sin, sin], axis=-1), (1, RET_H))
    return cos_f, sin_f


def _retention_tables(length, n_valid):
    lg = jnp.log1p(-jnp.exp2(-5.0 - jnp.arange(RET_H, dtype=f32)))
    idx = jnp.arange(length, dtype=f32)
    rel = idx[:, None] - idx[None, :]
    mask = jnp.where(rel[None] >= 0, jnp.exp(jnp.maximum(rel, 0.0)[None] * lg[:, None, None]), 0.0)
    q_dec = jnp.exp((idx[:, None] + 1.0) * lg[None, :])
    k_dec = jnp.where(idx[:, None] < n_valid, jnp.exp((n_valid - 1.0 - idx)[:, None] * lg[None, :]), 0.0)
    c_dec = jnp.exp(n_valid * lg)[None, :]
    rep = lambda t: jnp.repeat(t, HEAD_DIM, axis=-1)
    return mask, rep(q_dec), rep(k_dec), rep(c_dec)


def _retention_kernel(ret_ref, cos_ref, sin_ref, mask_ref, qdec_ref, kdec_ref, cdec_ref, s0_ref,
                      y_ref, s_ref, *, nb, length):
    @pl.when(pl.program_id(1) == 0)
    def _():
        s_ref[...] = s0_ref[...]

    m0, m1 = _half_masks()
    lane = lax.broadcasted_iota(i32, (1, LANES), 1)
    first_half = (lane % HEAD_DIM) < (HEAD_DIM // 2)
    bd = _block_diag_mask()

    def rope(x, cs, sn):
        swapped = jnp.where(first_half, pltpu.roll(x, LANES - HEAD_DIM // 2, 1), pltpu.roll(x, HEAD_DIM // 2, 1))
        return x * cs + swapped * sn

    for j in range(nb):
        rows = pl.ds(j * length, length)
        for p in range(RET_H // 2):
            cols = pl.ds(p * LANES, LANES)
            cs, sn = cos_ref[:, cols], sin_ref[:, cols]
            q2 = rope(ret_ref[rows, pl.ds(p * LANES, LANES)], cs, sn)
            k2 = rope(ret_ref[rows, pl.ds(RET_W + p * LANES, LANES)], cs, sn) * (HEAD_DIM ** -0.5)
            v2 = ret_ref[rows, pl.ds(2 * RET_W + p * LANES, LANES)]
            g2 = ret_ref[rows, pl.ds(3 * RET_W + p * LANES, LANES)]
            state = s_ref[j, p]
            o2 = _mm(q2, state) * qdec_ref[:, cols]
            for hh, m in enumerate((m0, m1)):
                sc = _mm_nt(q2 * m, k2) * mask_ref[2 * p + hh]
                o2 = o2 + _mm(sc, v2) * m
            s_ref[j, p] = state * cdec_ref[:, cols] + _mm_tn(k2 * kdec_ref[:, cols], v2) * bd
            mu = _seg_mean(o2, m0, m1)
            oc = o2 - mu
            var = _seg_mean(oc * oc, m0, m1)
            y_ref[rows, cols] = g2 * _sigmoid(g2) * oc * lax.rsqrt(var + LN_EPS)


def _retention(ret, cos_f, sin_f, tables, s0_bd, *, n_batch, n_chunks, nb, length, out_rows):
    mask, q_dec, k_dec, c_dec = tables
    blk = lambda g, c: (g * n_chunks + c, 0)
    const2 = lambda g, c: (0, 0)
    st = lambda g, c: (g, 0, 0, 0)
    return pl.pallas_call(
        functools.partial(_retention_kernel, nb=nb, length=length),
        grid=(n_batch // nb, n_chunks),
        in_specs=[pl.BlockSpec((nb * length, 4 * RET_W), blk),
                  pl.BlockSpec((length, RET_W), lambda g, c: (c, 0)),
                  pl.BlockSpec((length, RET_W), lambda g, c: (c, 0)),
                  pl.BlockSpec((RET_H, length, length), lambda g, c: (0, 0, 0)),
                  pl.BlockSpec((length, RET_W), const2),
                  pl.BlockSpec((length, RET_W), const2),
                  pl.BlockSpec((1, RET_W), const2),
                  pl.BlockSpec((nb, RET_H // 2, LANES, LANES), st)],
        out_specs=[pl.BlockSpec((nb * length, RET_W), blk),
                   pl.BlockSpec((nb, RET_H // 2, LANES, LANES), st)],
        out_shape=[jax.ShapeDtypeStruct((out_rows, RET_W), f32),
                   jax.ShapeDtypeStruct((n_batch, RET_H // 2, LANES, LANES), f32)],
        compiler_params=_params("arbitrary", "arbitrary"),
        name="retention",
    )(ret, cos_f, sin_f, mask, q_dec, k_dec, c_dec, s0_bd)


def _rwkv_kernel(rw_ref, aux_ref, valid_ref, mix_ref, vec_ref, lora_ref, tri_ref, s0_ref,
                 y_ref, s_ref, *scratch, rows, length, chain):
    n_chunks = rows // length
    rw = rw_ref[...]
    if chain:
        xs_ref, = scratch

        @pl.when(pl.program_id(1) == 0)
        def _():
            s_ref[...] = s0_ref[...]
            xs_ref[pl.ds(0, SUBLANES), :] = jnp.broadcast_to(aux_ref[0], (SUBLANES, RWKV_PROJ))

        xs_ref[pl.ds(SUBLANES, rows), :] = rw
        prev = xs_ref[pl.ds(SUBLANES - 1, rows), :]
        xs_ref[pl.ds(0, SUBLANES), :] = rw[rows - SUBLANES:, :]
    else:
        prev = aux_ref[...]
    rwm = rw + (prev - rw) * mix_ref[...]

    valid = valid_ref[...]
    w0, a0, k_k, k_a, r_k, gn_g, gn_b = (vec_ref[pl.ds(i, 1), :] for i in range(7))
    lo = rwm[:, RWKV_LORA_COL:]
    lw = jnp.dot(jnp.tanh(lo), lora_ref[0], precision=_HIGHEST, preferred_element_type=f32)
    la = jnp.dot(lo, lora_ref[1], precision=_HIGHEST, preferred_element_type=f32)
    gate = jnp.dot(_sigmoid(lo), lora_ref[2], precision=_HIGHEST, preferred_element_type=f32)
    logw = -jnp.exp(-_softplus(-(w0 + lw)) - 0.5) * valid
    a = _sigmoid(a0 + la)
    r = rwm[:, :RWKV_W]
    kr = rwm[:, RWKV_W:2 * RWKV_W]
    vr = rwm[:, 2 * RWKV_W:3 * RWKV_W]
    kk_raw = kr * k_k
    kp = kr * (1.0 + (a - 1.0) * k_a) * valid
    cum = jnp.dot(tri_ref[...], logw, precision=_HIGHEST, preferred_element_type=f32)
    g_incl = jnp.exp(cum)
    g_inv = jnp.exp(-cum)
    g_prev = jnp.exp(cum - logw)
    g_end = jnp.concatenate(
        [jnp.broadcast_to(g_incl[(c + 1) * length - 1:(c + 1) * length, :], (length, RWKV_W)) for c in range(n_chunks)],
        axis=0)

    m0, m1 = _half_masks()
    bd = _block_diag_mask()
    stacked = 2 * rows
    ri = lax.broadcasted_iota(i32, (stacked, stacked), 0)
    ci = lax.broadcasted_iota(i32, (stacked, stacked), 1)
    same = (ri // length) == (ci // length)
    strict = jnp.logical_and(same, ci < ri)
    incl = jnp.logical_and(same, ci <= ri)
    eye = (ci == ri).astype(f32)
    stack = lambda t: jnp.concatenate([t * m0, t * m1], axis=0)
    unstack = lambda t: t[:rows] + t[rows:]
    pairs = range(RWKV_H // 2)

    pre = []
    for p in pairs:
        sl = slice(p * LANES, (p + 1) * LANES)
        kk2 = kk_raw[:, sl]
        kk2 = kk2 * lax.rsqrt(jnp.maximum(_seg_sum(kk2 * kk2, m0, m1), 1e-24)) * valid[:, sl]
        d = dict(sl=sl, r2=r[:, sl], v2=vr[:, sl], kp2=kp[:, sl], ge=g_end[:, sl])
        d['kh'] = kk2 * g_prev[:, sl]
        rh = d['r2'] * g_incl[:, sl]
        bt = kk2 * a[:, sl] * g_inv[:, sl]
        kt = d['kp2'] * g_inv[:, sl]
        d['b_end'] = bt * d['ge']
        d['k_end'] = kt * d['ge']
        d['kh_s'], d['rh_s'], d['v_s'] = stack(d['kh']), stack(rh), stack(d['v2'])
        gram = _mm_nt(jnp.concatenate([d['kh_s'], d['rh_s']], axis=0), jnp.concatenate([stack(bt), stack(kt)], axis=0))
        d['x'] = -jnp.where(strict, gram[:stacked, :stacked], 0.0)
        d['a_k'] = jnp.where(strict, gram[:stacked, stacked:], 0.0)
        d['r_b'] = jnp.where(incl, gram[stacked:, :stacked], 0.0)
        d['r_k'] = jnp.where(incl, gram[stacked:, stacked:], 0.0)
        d['t'] = eye + d['x']
        pre.append(d)
    cover = 2
    while cover < length:
        for d in pre:
            d['x'] = _mm(d['x'], d['x'])
            d['t'] = d['t'] + _mm(d['t'], d['x'])
        cover *= 2
    for d in pre:
        rhs0 = -_mm(d['a_k'], d['v_s'])
        tz = _mm(d['t'], jnp.concatenate([rhs0, d['kh_s']], axis=1))
        z0_s = tz[:, :LANES]
        rbz = _mm(d['r_b'], tz)
        d['z0'] = unstack(z0_s)
        d['q'] = unstack(d['rh_s'] - rbz[:, LANES:])
        d['y0'] = unstack(rbz[:, :LANES] + _mm(d['r_k'], d['v_s']))
        d['w'] = unstack(_mm_tn(d['t'], stack(d['b_end'])))

    for p, d in zip(pairs, pre):
        sl, ge, kh, v2 = d['sl'], d['ge'], d['kh'], d['v2']
        state = s_ref[0, p] if chain else None
        ys = []
        for c in range(n_chunks):
            cr = slice(c * length, (c + 1) * length)
            if not chain:
                state = s0_ref[c, p]
            ys.append(_mm_nt(d['q'][cr], state) + d['y0'][cr])
            n_c = (_mm_tn(d['z0'][cr], d['b_end'][cr]) + _mm_tn(v2[cr], d['k_end'][cr])) * bd
            kw = _mm_tn(kh[cr], d['w'][cr]) * bd
            state = state * ge[(c + 1) * length - 1:(c + 1) * length, :] - _mm(state, kw) + n_c
            if not chain:
                s_ref[c, p] = state
        if chain:
            s_ref[0, p] = state
        y2 = jnp.concatenate(ys, axis=0) if n_chunks > 1 else ys[0]

        mu = _seg_mean(y2, m0, m1)
        yc = y2 - mu
        var = _seg_mean(yc * yc, m0, m1)
        yn = yc * lax.rsqrt(var + RWKV_GN_EPS) * gn_g[:, sl] + gn_b[:, sl]
        bonus = _seg_sum(d['r2'] * d['kp2'] * r_k[:, sl], m0, m1) * v2
        y_ref[:, pl.ds(p * LANES, LANES)] = (yn + bonus) * gate[:, sl]


def _rwkv(rw, aux, valid, mix, vec, lora, tri, s0_bd, *, n_groups, n_steps, rows, length, chain):
    blk = lambda g, c: (g * n_steps + c, 0)
    const2 = lambda g, c: (0, 0)
    st = lambda g, c: (g, 0, 0, 0)
    n_state = 1 if chain else rows // length
    aux_spec = (pl.BlockSpec((1, 1, RWKV_PROJ), lambda g, c: (g, 0, 0)) if chain
                else pl.BlockSpec((rows, RWKV_PROJ), blk))
    return pl.pallas_call(
        functools.partial(_rwkv_kernel, rows=rows, length=length, chain=chain),
        grid=(n_groups, n_steps),
        in_specs=[pl.BlockSpec((rows, RWKV_PROJ), blk),
                  aux_spec,
                  pl.BlockSpec((rows, RWKV_W), const2),
                  pl.BlockSpec((1, RWKV_PROJ), const2),
                  pl.BlockSpec((SUBLANES, RWKV_W), const2),
                  pl.BlockSpec((3, LANES, RWKV_W), lambda g, c: (0, 0, 0)),
                  pl.BlockSpec((rows, rows), const2),
                  pl.BlockSpec((n_state, RWKV_H // 2, LANES, LANES), st)],
        out_specs=[pl.BlockSpec((rows, RWKV_W), blk),
                   pl.BlockSpec((n_state, RWKV_H // 2, LANES, LANES), st)],
        out_shape=[jax.ShapeDtypeStruct((n_groups * n_steps * rows, RWKV_W), f32),
                   jax.ShapeDtypeStruct((n_groups * n_state, RWKV_H // 2, LANES, LANES), f32)],
        scratch_shapes=[pltpu.VMEM((rows + SUBLANES, RWKV_PROJ), f32)] if chain else [],
        compiler_params=_params("arbitrary", "arbitrary"),
        name="rwkv7",
    )(rw, aux, valid, mix, vec, lora, tri, s0_bd)


def _chunk_tri(rows, length):
    idx = jnp.arange(rows)
    same = (idx[:, None] // length) == (idx[None, :] // length)
    return jnp.logical_and(same, idx[None, :] <= idx[:, None]).astype(f32)


def _lru_kernel(lru_ref, valid_ref, vec_ref, wa_ref, wx_ref, conv0_ref, h0_ref,
                y_ref, h_ref, xext_ref, a_ref, b_ref, hs_ref, *, nb, length):
    @pl.when(pl.program_id(1) == 0)
    def _():
        h_ref[...] = h0_ref[...]
        xext_ref[:, pl.ds(0, SUBLANES), :] = conv0_ref[...]

    valid = valid_ref[...] > 0.5
    cw = [vec_ref[pl.ds(i, 1), :] for i in range(CONV_WIDTH)]
    cb, ba, bx, lam = (vec_ref[pl.ds(i, 1), :] for i in range(CONV_WIDTH, CONV_WIDTH + 4))
    sp = _softplus(-lam)
    for j in range(nb):
        rows = pl.ds(j * length, length)
        gbr = lru_ref[rows, pl.ds(0, LRU_W)]
        x = lru_ref[rows, pl.ds(LRU_W, LRU_W)]
        xext_ref[j, pl.ds(SUBLANES, length), :] = x
        xc = cb + x * cw[CONV_WIDTH - 1]
        for t in range(CONV_WIDTH - 1):
            xc = xc + xext_ref[j, pl.ds(SUBLANES - (CONV_WIDTH - 1) + t, length), :] * cw[t]
        xext_ref[j, pl.ds(0, SUBLANES), :] = x[length - SUBLANES:, :]
        gate_a = _sigmoid(_mm(xc, wa_ref[...]) + ba)
        gate_x = _sigmoid(_mm(xc, wx_ref[...]) + bx)
        log_a = -LRU_C * gate_a * sp
        a = jnp.exp(log_a)
        b = xc * gate_x * jnp.sqrt(1.0 - jnp.exp(2.0 * log_a))
        a_ref[...] = jnp.where(valid, a, 1.0)
        b_ref[...] = jnp.where(valid, b, 0.0)

        def step(t, h):
            h = a_ref[pl.ds(t, 1), :] * h + b_ref[pl.ds(t, 1), :]
            hs_ref[pl.ds(t, 1), :] = h
            return h

        h = lax.fori_loop(0, length, step, h_ref[j, pl.ds(0, 1), :])
        h_ref[j] = jnp.broadcast_to(h, (SUBLANES, LRU_W))
        c = 0.7978845608028654
        gelu = 0.5 * gbr * (1.0 + jnp.tanh(c * (gbr + 0.044715 * gbr * gbr * gbr)))
        y_ref[rows, :] = hs_ref[...] * gelu


def _lru(lru, valid, vec, wa_bd, wx_bd, conv0, h0, *, n_batch, n_chunks, nb, length, out_rows):
    blk = lambda g, c: (g * n_chunks + c, 0)
    const2 = lambda g, c: (0, 0)
    st = lambda g, c: (g, 0, 0)
    return pl.pallas_call(
        functools.partial(_lru_kernel, nb=nb, length=length),
        grid=(n_batch // nb, n_chunks),
        in_specs=[pl.BlockSpec((nb * length, 2 * LRU_W), blk),
                  pl.BlockSpec((length, LRU_W), const2),
                  pl.BlockSpec((SUBLANES, LRU_W), const2),
                  pl.BlockSpec((LRU_W, LRU_W), const2),
                  pl.BlockSpec((LRU_W, LRU_W), const2),
                  pl.BlockSpec((nb, SUBLANES, LRU_W), st),
                  pl.BlockSpec((nb, SUBLANES, LRU_W), st)],
        out_specs=[pl.BlockSpec((nb * length, LRU_W), blk),
                   pl.BlockSpec((nb, SUBLANES, LRU_W), st)],
        out_shape=[jax.ShapeDtypeStruct((out_rows, LRU_W), f32),
                   jax.ShapeDtypeStruct((n_batch, SUBLANES, LRU_W), f32)],
        scratch_shapes=[pltpu.VMEM((nb, length + SUBLANES, LRU_W), f32),
                        pltpu.VMEM((length, LRU_W), f32),
                        pltpu.VMEM((length, LRU_W), f32),
                        pltpu.VMEM((length, LRU_W), f32)],
        compiler_params=_params("arbitrary", "arbitrary"),
        name="rg_lru",
    )(lru, valid, vec, wa_bd, wx_bd, conv0, h0)


def _out_proj_kernel(yr_ref, yw_ref, yl_ref, x_ref, w_ref, ln_ref, wr_ref, br_ref,
                     x1_ref, lpos_ref, g_ref, meta_ref, cnt_ref, *, alpha):
    @pl.when(pl.program_id(0) == 0)
    def _():
        cnt_ref[...] = jnp.zeros(cnt_ref.shape, f32)

    mixed = (_mm(yr_ref[...], w_ref[pl.ds(0, RET_W), :])
             + _mm(yw_ref[...], w_ref[pl.ds(RET_W, RWKV_W), :])
             + _mm(yl_ref[...], w_ref[pl.ds(RET_W + RWKV_W, LRU_W), :]))
    x1 = _layer_norm_rows(alpha * x_ref[...] + mixed, ln_ref[pl.ds(0, 1), :], ln_ref[pl.ds(1, 1), :])
    x1_ref[...] = x1
    logits = jnp.dot(x1, wr_ref[...], precision=_HIGHEST, preferred_element_type=f32) + br_ref[...]
    tm = logits.shape[0]
    lane = lax.broadcasted_iota(i32, logits.shape, 1).astype(f32)
    top_v = jnp.zeros(logits.shape, f32)
    work = logits
    v_max = None
    onehots = []
    for k in range(TOP_K):
        v = jnp.max(work, axis=-1, keepdims=True)
        idx = jnp.min(jnp.where(work == v, lane, float(LANES)), axis=-1, keepdims=True)
        if k == 0:
            v_max = v
        hit = lane == idx
        onehots.append(hit.astype(f32))
        top_v = jnp.where(lane == k, jnp.exp(v - v_max), top_v)
        work = jnp.where(hit, -jnp.inf, work)
    g_ref[...] = top_v / jnp.sum(top_v, axis=-1, keepdims=True)

    total = onehots[0] + onehots[1] + onehots[2] + onehots[3]
    ri = lax.broadcasted_iota(i32, (tm, tm), 0)
    ci = lax.broadcasted_iota(i32, (tm, tm), 1)
    before = _mm((ci < ri).astype(f32), total)
    groups = jnp.floor((jnp.sum(total, axis=0, keepdims=True) + (SUBLANES - 1.0)) * (1.0 / SUBLANES))
    er = lax.broadcasted_iota(i32, (LANES, LANES), 0)
    ec = lax.broadcasted_iota(i32, (LANES, LANES), 1)
    run_len = groups * SUBLANES
    run_start = _mm(jnp.broadcast_to(groups, (SUBLANES, LANES)), (er < ec).astype(f32))[0:1] * SUBLANES
    lpos = jnp.full(logits.shape, -1.0, f32)
    for k in range(TOP_K):
        pos = jnp.sum(onehots[k] * (before + run_start), axis=-1, keepdims=True)
        lpos = jnp.where(lane == k, pos, lpos)
    lpos_ref[...] = lpos
    row = lax.broadcasted_iota(i32, (SUBLANES, LANES), 0)
    meta = jnp.where(row == 0, run_start, jnp.where(row == 1, run_len, jnp.where(row == 2, cnt_ref[...], 0.0)))
    meta_ref[...] = meta.astype(i32)
    cnt_ref[...] = cnt_ref[...] + run_len


def _out_proj_router(y_ret, y_rwkv, y_lru, x, w_out_bf, ln, w_router, b_router, alpha):
    n = x.shape[0]
    tm = TOKEN_TILE
    row = lambda i: (i, 0)
    const = lambda i: (0, 0)
    return pl.pallas_call(
        functools.partial(_out_proj_kernel, alpha=alpha),
        grid=(n // tm,),
        in_specs=[pl.BlockSpec((tm, RET_W), row), pl.BlockSpec((tm, RWKV_W), row), pl.BlockSpec((tm, LRU_W), row),
                  pl.BlockSpec((tm, D_MODEL), row),
                  pl.BlockSpec((D_MODEL, D_MODEL), const),
                  pl.BlockSpec((SUBLANES, D_MODEL), const),
                  pl.BlockSpec((D_MODEL, LANES), const),
                  pl.BlockSpec((1, LANES), const)],
        out_specs=[pl.BlockSpec((tm, D_MODEL), row), pl.BlockSpec((tm, LANES), row), pl.BlockSpec((tm, LANES), row),
                   pl.BlockSpec((SUBLANES, LANES), row), pl.BlockSpec((SUBLANES, LANES), const)],
        out_shape=[jax.ShapeDtypeStruct((n, D_MODEL), f32),
                   jax.ShapeDtypeStruct((n, LANES), f32),
                   jax.ShapeDtypeStruct((n, LANES), f32),
                   jax.ShapeDtypeStruct((n // tm * SUBLANES, LANES), i32),
                   jax.ShapeDtypeStruct((SUBLANES, LANES), f32)],
        compiler_params=_params("arbitrary"),
        name="out_proj_router",
    )(y_ret, y_rwkv, y_lru, x, w_out_bf, ln, w_router, b_router)


def _for_each_run_piece(meta_ref, pstart_ref, fn):
    for e in range(N_EXPERTS):
        start, n = meta_ref[0, e], meta_ref[1, e]
        base = pstart_ref[e] + meta_ref[2, e]
        for sz in RUN_PIECES:
            done = n & ~(2 * sz - 1)

            @pl.when((n & sz) != 0)
            def _():
                fn(pl.multiple_of(start + done, SUBLANES), pl.multiple_of(base + done, SUBLANES), sz)


def _wait_run_rows(meta_ref, make_copy):
    total = meta_ref[1, 0]
    for e in range(1, N_EXPERTS):
        total = total + meta_ref[1, e]
    for sz in WAIT_PIECES:
        @pl.when((total & sz) != 0)
        def _():
            make_copy(sz).wait()


def _dispatch_kernel(pstart_ref, cnt_ref, n_used_ref, meta_ref, meta_prev_ref, lpos_ref, x1_ref, xs_hbm,
                     xl_ref, zero_ref, sems, zsem):
    tm = TOKEN_TILE
    bm = MOE_BLOCK
    i = pl.program_id(0)
    slot = i % 2

    @pl.when(i == 0)
    def _():
        zero_ref[...] = jnp.zeros(zero_ref.shape, f32)
        tail = lambda j: pltpu.make_async_copy(zero_ref, xs_hbm.at[pl.ds(j * bm, bm), :], zsem)

        def tail_start(j, carry):
            tail(j).start()
            return carry

        def tail_wait(j, carry):
            tail(j).wait()
            return carry

        n_blocks = xs_hbm.shape[0] // bm
        lax.fori_loop(n_used_ref[0], n_blocks, tail_start, 0)
        lax.fori_loop(n_used_ref[0], n_blocks, tail_wait, 0)
        for e in range(N_EXPERTS):
            lo = pstart_ref[e] + cnt_ref[e]
            n_groups = ((cnt_ref[e] + bm - 1) // bm * bm - cnt_ref[e]) // SUBLANES
            fill = lambda g: pltpu.make_async_copy(
                zero_ref.at[pl.ds(0, SUBLANES), :],
                xs_hbm.at[pl.ds(pl.multiple_of(lo + g * SUBLANES, SUBLANES), SUBLANES), :], zsem)

            def start(g, carry):
                fill(g).start()
                return carry

            def wait(g, carry):
                fill(g).wait()
                return carry

            lax.fori_loop(0, n_groups, start, 0)
            lax.fori_loop(0, n_groups, wait, 0)

    lpos_t = lpos_ref[...].T
    srow = lax.broadcasted_iota(i32, (LOCAL_ROWS, tm), 0).astype(f32)
    perm = jnp.zeros((LOCAL_ROWS, tm), f32)
    for k in range(TOP_K):
        perm = perm + (srow == lpos_t[k:k + 1, :]).astype(f32)
    xl_ref[slot] = jnp.dot(perm.astype(bf16), x1_ref[...].astype(bf16), preferred_element_type=f32)

    def send(local_row, sorted_row, n_rows):
        pltpu.make_async_copy(xl_ref.at[slot, pl.ds(local_row, n_rows), :],
                              xs_hbm.at[pl.ds(sorted_row, n_rows), :], sems.at[slot]).start()

    _for_each_run_piece(meta_ref, pstart_ref, send)

    def sent(s):
        return lambda n_rows: pltpu.make_async_copy(xl_ref.at[s, pl.ds(0, n_rows), :],
                                                    xs_hbm.at[pl.ds(0, n_rows), :], sems.at[s])

    @pl.when(i > 0)
    def _():
        _wait_run_rows(meta_prev_ref, sent(1 - slot))

    @pl.when(i == pl.num_programs(0) - 1)
    def _():
        _wait_run_rows(meta_ref, sent(slot))


def _dispatch(pstart, counts, n_used, meta, lpos, x1, n_rows):
    n = x1.shape[0]
    tm = TOKEN_TILE
    smem_tile = lambda f: pl.BlockSpec((SUBLANES, LANES), f, memory_space=pltpu.SMEM)
    grid_spec = pltpu.PrefetchScalarGridSpec(
        num_scalar_prefetch=3,
        grid=(n // tm,),
        in_specs=[smem_tile(lambda i, ps, ct, nu: (i, 0)),
                  smem_tile(lambda i, ps, ct, nu: (jnp.maximum(i - 1, 0), 0)),
                  pl.BlockSpec((tm, LANES), lambda i, ps, ct, nu: (i, 0)),
                  pl.BlockSpec((tm, D_MODEL), lambda i, ps, ct, nu: (i, 0))],
        out_specs=pl.BlockSpec(memory_space=pl.ANY),
        scratch_shapes=[pltpu.VMEM((2, LOCAL_ROWS, D_MODEL), f32),
                        pltpu.VMEM((MOE_BLOCK, D_MODEL), f32),
                        pltpu.SemaphoreType.DMA((2,)), pltpu.SemaphoreType.DMA(())],
    )
    return pl.pallas_call(
        _dispatch_kernel,
        grid_spec=grid_spec,
        out_shape=jax.ShapeDtypeStruct((n_rows, D_MODEL), f32),
        compiler_params=_params("arbitrary"),
        name="moe_dispatch",
    )(pstart, counts, n_used, meta, meta, lpos, x1)


def _expert_kernel(blk_e_ref, n_used_ref, xs_ref, wgu_ref, bgu_ref, wdn_ref, bdn_ref,
                   out_ref, wgu_bf, wdn_bf):
    i = pl.program_id(0)
    prev = jnp.maximum(i - 1, 0)
    new_expert = jnp.logical_or(i == 0, blk_e_ref[i] != blk_e_ref[prev])
    used = i < n_used_ref[0]

    @pl.when(used)
    def _():
        @pl.when(new_expert)
        def _():
            wgu_bf[...] = wgu_ref[0, 0].astype(bf16)
            wdn_bf[...] = wdn_ref[0, 0].astype(bf16)

        gu = jnp.dot(xs_ref[...].astype(bf16), wgu_bf[...], preferred_element_type=f32) + bgu_ref[0, 0]
        g = jnp.minimum(gu[:, :D_EXPERT], SWIGLU_LIMIT)
        u = jnp.clip(gu[:, D_EXPERT:], -SWIGLU_LIMIT, SWIGLU_LIMIT)
        hdn = (u + 1.0) * g * _sigmoid(SWIGLU_ALPHA * g)
        out_ref[...] = jnp.dot(hdn.astype(bf16), wdn_bf[...], preferred_element_type=f32) + bdn_ref[0, 0]

    @pl.when(jnp.logical_not(used))
    def _():
        out_ref[...] = jnp.zeros(out_ref.shape, f32)


def _experts(blk_e, n_used, xs, w_gu, b_gu, w_down, b_down, layer):
    n_blocks = blk_e.shape[0]
    bm = MOE_BLOCK
    by_e = lambda i, be, nu: (layer, be[i], 0, 0)
    x_blk = lambda i, be, nu: (jnp.minimum(i, nu[0] - 1), 0)
    grid_spec = pltpu.PrefetchScalarGridSpec(
        num_scalar_prefetch=2,
        grid=(n_blocks,),
        in_specs=[pl.BlockSpec((bm, D_MODEL), x_blk),
                  pl.BlockSpec((1, 1, D_MODEL, 2 * D_EXPERT), by_e),
                  pl.BlockSpec((1, 1, 1, 2 * D_EXPERT), by_e),
                  pl.BlockSpec((1, 1, D_EXPERT, D_MODEL), by_e),
                  pl.BlockSpec((1, 1, 1, D_MODEL), by_e)],
        out_specs=pl.BlockSpec((bm, D_MODEL), lambda i, be, nu: (i, 0)),
        scratch_shapes=[pltpu.VMEM((D_MODEL, 2 * D_EXPERT), bf16),
                        pltpu.VMEM((D_EXPERT, D_MODEL), bf16)],
    )
    depth = w_gu.shape[0]
    return pl.pallas_call(
        _expert_kernel,
        grid_spec=grid_spec,
        out_shape=jax.ShapeDtypeStruct((n_blocks * bm, D_MODEL), f32),
        compiler_params=_params("arbitrary"),
        name="moe_experts",
    )(blk_e, n_used, xs, w_gu, b_gu.reshape(depth, N_EXPERTS, 1, 2 * D_EXPERT),
      w_down, b_down.reshape(depth, N_EXPERTS, 1, D_MODEL))


def _combine_kernel(pstart_ref, meta_ref, meta_next_ref, lpos_ref, gates_ref, x1_ref, ln_ref, yb_hbm,
                    out_ref, yl_ref, sems, *, alpha):
    tm = TOKEN_TILE
    i = pl.program_id(0)
    slot = i % 2

    def fetch(meta, s):
        def recv(local_row, sorted_row, n_rows):
            pltpu.make_async_copy(yb_hbm.at[pl.ds(sorted_row, n_rows), :],
                                  yl_ref.at[s, pl.ds(local_row, n_rows), :], sems.at[s]).start()

        _for_each_run_piece(meta, pstart_ref, recv)

    @pl.when(i == 0)
    def _():
        yl_ref[...] = jnp.zeros(yl_ref.shape, f32)
        fetch(meta_ref, slot)

    @pl.when(i + 1 < pl.num_programs(0))
    def _():
        fetch(meta_next_ref, 1 - slot)

    _wait_run_rows(meta_ref, lambda n_rows: pltpu.make_async_copy(
        yb_hbm.at[pl.ds(0, n_rows), :], yl_ref.at[slot, pl.ds(0, n_rows), :], sems.at[slot]))

    lpos = lpos_ref[...]
    gates = gates_ref[...]
    scol = lax.broadcasted_iota(i32, (tm, LOCAL_ROWS), 1).astype(f32)
    weight = jnp.zeros((tm, LOCAL_ROWS), f32)
    for k in range(TOP_K):
        weight = weight + jnp.where(scol == lpos[:, k:k + 1], gates[:, k:k + 1], 0.0)
    y = jnp.dot(weight.astype(bf16), yl_ref[slot].astype(bf16), preferred_element_type=f32)
    out_ref[...] = _layer_norm_rows(alpha * x1_ref[...] + y, ln_ref[pl.ds(0, 1), :], ln_ref[pl.ds(1, 1), :])


def _combine(pstart, meta, lpos, gates, x1, ln, yb, alpha):
    n = x1.shape[0]
    tm = TOKEN_TILE
    n_tiles = n // tm
    row = lambda i, ps: (i, 0)
    smem_tile = lambda f: pl.BlockSpec((SUBLANES, LANES), f, memory_space=pltpu.SMEM)
    grid_spec = pltpu.PrefetchScalarGridSpec(
        num_scalar_prefetch=1,
        grid=(n_tiles,),
        in_specs=[smem_tile(row),
                  smem_tile(lambda i, ps: (jnp.minimum(i + 1, n_tiles - 1), 0)),
                  pl.BlockSpec((tm, LANES), row),
                  pl.BlockSpec((tm, LANES), row),
                  pl.BlockSpec((tm, D_MODEL), row),
                  pl.BlockSpec((SUBLANES, D_MODEL), lambda i, ps: (0, 0)),
                  pl.BlockSpec(memory_space=pl.ANY)],
        out_specs=pl.BlockSpec((tm, D_MODEL), row),
        scratch_shapes=[pltpu.VMEM((2, LOCAL_ROWS, D_MODEL), f32), pltpu.SemaphoreType.DMA((2,))],
    )
    return pl.pallas_call(
        functools.partial(_combine_kernel, alpha=alpha),
        grid_spec=grid_spec,
        out_shape=jax.ShapeDtypeStruct((n, D_MODEL), f32),
        compiler_params=_params("arbitrary"),
        name="moe_combine",
    )(pstart, meta, meta, lpos, gates, x1, ln, yb)


def _block_tables(counts, n_tokens):
    bm = MOE_BLOCK
    padded = (counts + bm - 1) // bm * bm
    pad_end = jnp.cumsum(padded)
    pstart = (pad_end - padded).astype(i32)
    max_used = n_tokens * TOP_K + (n_tokens // TOKEN_TILE) * N_EXPERTS * (SUBLANES - 1)
    n_blocks = -(-(max_used + N_EXPERTS * (bm - 1)) // bm)
    first_row = jnp.arange(n_blocks, dtype=pad_end.dtype) * bm
    blk_e = jnp.minimum(jnp.sum(pad_end[None, :] <= first_row[:, None], axis=1), N_EXPERTS - 1).astype(i32)
    n_used = (pad_end[-1] // bm).astype(i32).reshape(1)
    return pstart, blk_e, n_used, n_blocks * bm


def _pad_time(t, n_batch, n_t, t_pad):
    w = t.shape[-1]
    return jnp.pad(t.reshape(n_batch, n_t, w), ((0, 0), (0, t_pad - n_t), (0, 0))).reshape(n_batch * t_pad, w)


def _to_block_diag(s):
    b, h = s.shape[:2]
    s = s.reshape(b, h // 2, 2, HEAD_DIM, HEAD_DIM)
    z = jnp.zeros_like(s[:, :, 0])
    top = jnp.concatenate([s[:, :, 0], z], axis=-1)
    bot = jnp.concatenate([z, s[:, :, 1]], axis=-1)
    return jnp.concatenate([top, bot], axis=-2)


def _from_block_diag(s):
    b, hp = s.shape[:2]
    return jnp.stack([s[:, :, :HEAD_DIM, :HEAD_DIM], s[:, :, HEAD_DIM:, HEAD_DIM:]], axis=2).reshape(
        b, 2 * hp, HEAD_DIM, HEAD_DIM)


def _block_diag_weight(w):
    h = w.shape[0]
    eye = jnp.eye(h, dtype=w.dtype)
    return (eye[:, None, :, None] * w[:, :, None, :]).reshape(h * HEAD_DIM, h * HEAD_DIM)


def _rows8(*rows):
    width = rows[0].shape[-1]
    m = jnp.stack([r.reshape(width) for r in rows])
    return jnp.pad(m, ((0, SUBLANES - m.shape[0]), (0, 0)))


def _last_rows(t, start, n_seq, seq_len, k):
    return jnp.stack([lax.slice_in_dim(t, start + seq_len - k + j, start + n_seq * seq_len, seq_len, axis=0)
                      for j in range(k)], axis=1)


def _layer(h, p, moe, layer, alpha, st_s, bp, tp, bs, ts):
    n_p, n_s = bp * tp, bs * ts
    tpad = SAMPLE_T_PAD
    ret, rw, lru = _in_proj(h, p['w_in'].astype(bf16))

    def merge(y_p, y_s):
        w = y_p.shape[-1]
        return jnp.concatenate([y_p, y_s.reshape(bs, tpad, w)[:, :ts].reshape(n_s, w)], axis=0)

    c_p = RET_CHUNK if tp % RET_CHUNK == 0 else tp
    cos_p, sin_p = _rope_tables(jnp.arange(tp, dtype=f32))
    cos_s, sin_s = _rope_tables(PAST_LEN + jnp.arange(tpad, dtype=f32))
    y_ret, sret_p = _retention(ret, cos_p, sin_p, _retention_tables(c_p, c_p),
                               jnp.zeros((bp, RET_H // 2, LANES, LANES), f32),
                               n_batch=bp, n_chunks=tp // c_p, nb=1, length=c_p, out_rows=n_p)
    y_ret_s, sret_s = _retention(_pad_time(ret[n_p:], bs, ts, tpad), cos_s, sin_s, _retention_tables(tpad, ts),
                                 _to_block_diag(st_s['ret']),
                                 n_batch=bs, n_chunks=1, nb=SUBLANES, length=tpad, out_rows=bs * tpad)
    y_ret = merge(y_ret, y_ret_s)

    lora = jnp.zeros((3, LANES, RWKV_W), f32)
    lora = lora.at[0, 0:32].set(p['w_up']).at[1, 32:64].set(p['a_up']).at[2, 64:128].set(p['g_up'])
    vec = _rows8(p['w0'], p['a0'], p['k_k'], p['k_a'], p['r_k'], p['gn_g'], p['gn_b'])
    mix = p['mix'].reshape(1, RWKV_PROJ)
    rows = RWKV_ROWS
    y_rwkv, srw_p = _rwkv(rw, jnp.zeros((bp, 1, RWKV_PROJ), f32), jnp.ones((rows, RWKV_W), f32), mix, vec, lora,
                          _chunk_tri(rows, RWKV_CHUNK), jnp.zeros((bp, RWKV_H // 2, LANES, LANES), f32),
                          n_groups=bp, n_steps=tp // rows, rows=rows, length=RWKV_CHUNK, chain=True)
    valid_s = (jnp.arange(tpad) < ts).astype(f32)[:, None]
    rw_s3 = rw[n_p:].reshape(bs, ts, RWKV_PROJ)
    prev_s = jnp.concatenate([st_s['shift'][:, None, :], rw_s3[:, :-1]], axis=1).reshape(n_s, RWKV_PROJ)
    seq_per_blk = rows // tpad
    y_rwkv_s, srw_s = _rwkv(_pad_time(rw[n_p:], bs, ts, tpad), _pad_time(prev_s, bs, ts, tpad),
                            jnp.tile(jnp.broadcast_to(valid_s, (tpad, RWKV_W)), (seq_per_blk, 1)),
                            mix, vec, lora, _chunk_tri(rows, tpad), _to_block_diag(st_s['rwkv']),
                            n_groups=bs // seq_per_blk, n_steps=1, rows=rows, length=tpad, chain=False)
    y_rwkv = merge(y_rwkv, y_rwkv_s)

    lvec = _rows8(p['conv_w'][0], p['conv_w'][1], p['conv_w'][2], p['conv_w'][3],
                  p['conv_b'], p['ba'], p['bx'], p['lam'])
    wa_bd = _block_diag_weight(p['wa']).astype(bf16)
    wx_bd = _block_diag_weight(p['wx']).astype(bf16)
    l_l = RET_CHUNK if tp % RET_CHUNK == 0 else tp
    y_lru, h_p = _lru(lru, jnp.ones((l_l, LRU_W), f32), lvec, wa_bd, wx_bd,
                      jnp.zeros((bp, SUBLANES, LRU_W), f32), jnp.zeros((bp, SUBLANES, LRU_W), f32),
                      n_batch=bp, n_chunks=tp // l_l, nb=1, length=l_l, out_rows=n_p)
    conv0_s = jnp.pad(st_s['conv'], ((0, 0), (SUBLANES - (CONV_WIDTH - 1), 0), (0, 0)))
    h0_s = jnp.broadcast_to(st_s['lru'][:, None, :], (bs, SUBLANES, LRU_W))
    y_lru_s, h_s = _lru(_pad_time(lru[n_p:], bs, ts, tpad), jnp.broadcast_to(valid_s, (tpad, LRU_W)),
                        lvec, wa_bd, wx_bd, conv0_s, h0_s,
                        n_batch=bs, n_chunks=1, nb=SUBLANES, length=tpad, out_rows=bs * tpad)
    y_lru = merge(y_lru, y_lru_s)

    w_router = jnp.pad(p['w_router'], ((0, 0), (0, LANES - N_EXPERTS)))
    b_router = jnp.pad(p['b_router'], (0, LANES - N_EXPERTS), constant_values=-1e30).reshape(1, LANES)
    x1, lpos, gates, meta, cnt = _out_proj_router(y_ret, y_rwkv, y_lru, h, p['w_out'].astype(bf16),
                                                  _rows8(p['ln1_g'], p['ln1_b']), w_router, b_router, alpha)

    counts = cnt[0, :N_EXPERTS].astype(i32)
    pstart, blk_e, n_used, n_rows = _block_tables(counts, n_p + n_s)
    xs = _dispatch(pstart, counts, n_used, meta, lpos, x1, n_rows)
    yb = _experts(blk_e, n_used, xs, moe['w_gu'], moe['b_gu'], moe['w_down'], moe['b_down'], layer)
    x2 = _combine(pstart, meta, lpos, gates, x1, _rows8(p['ln2_g'], p['ln2_b']), yb, alpha)

    keep = CONV_WIDTH - 1
    xbr = lru[:, LRU_W:]
    new_p = (_from_block_diag(sret_p), _from_block_diag(srw_p), _last_rows(rw, 0, bp, tp, 1)[:, 0], h_p[:, 0],
             _last_rows(xbr, 0, bp, tp, keep))
    new_s = (_from_block_diag(sret_s), _from_block_diag(srw_s), _last_rows(rw, n_p, bs, ts, 1)[:, 0], h_s[:, 0],
             _last_rows(xbr, n_p, bs, ts, keep))
    return x2, new_p, new_s


def kernel(x_prompt, x_sample, state_ret, state_rwkv, state_rwkv_shift, state_lru, state_conv,
           w_in, w_out, ln1_g, ln1_b, ln2_g, ln2_b,
           rwkv_mix, rwkv_w0, rwkv_w_up, rwkv_a0, rwkv_a_up, rwkv_g_up, rwkv_k_k, rwkv_k_a, rwkv_r_k,
           rwkv_gn_g, rwkv_gn_b, lru_conv_w, lru_conv_b, lru_wa, lru_ba, lru_wx, lru_bx, lru_lambda,
           moe_w_router, moe_b_router, moe_w_gate_up, moe_b_gate_up, moe_w_down, moe_b_down):
    bp, tp, _ = x_prompt.shape
    bs, ts, _ = x_sample.shape
    depth = w_in.shape[0]
    alpha = (2.0 * depth) ** 0.25
    moe = {'w_gu': moe_w_gate_up, 'b_gu': moe_b_gate_up, 'w_down': moe_w_down, 'b_down': moe_b_down}
    h = jnp.concatenate([x_prompt.reshape(bp * tp, D_MODEL), x_sample.reshape(bs * ts, D_MODEL)], axis=0)
    new_p, new_s = [], []
    for l in range(depth):
        p = {'w_in': w_in[l], 'w_out': w_out[l], 'ln1_g': ln1_g[l], 'ln1_b': ln1_b[l],
             'ln2_g': ln2_g[l], 'ln2_b': ln2_b[l], 'mix': rwkv_mix[l], 'w0': rwkv_w0[l],
             'w_up': rwkv_w_up[l], 'a0': rwkv_a0[l], 'a_up': rwkv_a_up[l], 'g_up': rwkv_g_up[l],
             'k_k': rwkv_k_k[l], 'k_a': rwkv_k_a[l], 'r_k': rwkv_r_k[l], 'gn_g': rwkv_gn_g[l],
             'gn_b': rwkv_gn_b[l], 'conv_w': lru_conv_w[l], 'conv_b': lru_conv_b[l], 'wa': lru_wa[l],
             'ba': lru_ba[l], 'wx': lru_wx[l], 'bx': lru_bx[l], 'lam': lru_lambda[l],
             'w_router': moe_w_router[l], 'b_router': moe_b_router[l]}
        st_s = {'ret': state_ret[l], 'rwkv': state_rwkv[l], 'shift': state_rwkv_shift[l],
                'lru': state_lru[l], 'conv': state_conv[l]}
        h, sp, ss = _layer(h, p, moe, l, alpha, st_s, bp, tp, bs, ts)
        new_p.append(sp)
        new_s.append(ss)
    n_p = bp * tp
    outs = [h[:n_p].reshape(bp, tp, D_MODEL), h[n_p:].reshape(bs, ts, D_MODEL)]
    for i in range(5):
        outs.append(jnp.stack([s[i] for s in new_p]))
        outs.append(jnp.stack([s[i] for s in new_s]))
    return tuple(outs)
```

```python
import functools

import jax
import jax.numpy as jnp
from jax import lax
from jax.experimental import pallas as pl
from jax.experimental.pallas import tpu as pltpu

f32 = jnp.float32
bf16 = jnp.bfloat16
i32 = jnp.int32

D_MODEL = 1024
HEAD_DIM = 64
RET_W = 256
RET_H = 4
RET_CHUNK = 128
ROPE_BASE = 10000.0
RWKV_W = 512
RWKV_H = 8
RWKV_PROJ = 1664
RWKV_LORA_COL = 1536
RWKV_GN_EPS = 64e-5
RWKV_CHUNK = 64
RWKV_ROWS = 128
LRU_W = 256
LRU_C = 8.0
CONV_WIDTH = 4
D_PROJ = 3200
N_EXPERTS = 32
TOP_K = 4
D_EXPERT = 1024
SWIGLU_LIMIT = 7.0
SWIGLU_ALPHA = 1.702
LN_EPS = 1e-5
PAST_LEN = 16384.0

LANES = 128
SUBLANES = 8
SAMPLE_T_PAD = 16
TOKEN_TILE = 512
MOE_BLOCK = 512
LOCAL_ROWS = TOP_K * TOKEN_TILE + N_EXPERTS * SUBLANES
RUN_PIECES = tuple(SUBLANES << j for j in reversed(range((TOKEN_TILE // SUBLANES).bit_length())))
WAIT_PIECES = tuple(SUBLANES << j for j in reversed(range((LOCAL_ROWS // SUBLANES).bit_length())))
VMEM_LIMIT = 56 * 1024 * 1024

_NT = (((1,), (1,)), ((), ()))
_TN = (((0,), (0,)), ((), ()))


def _params(*sem):
    return pltpu.CompilerParams(dimension_semantics=sem, vmem_limit_bytes=VMEM_LIMIT)


def _mm(a, b):
    return jnp.dot(a.astype(bf16), b.astype(bf16), preferred_element_type=f32)


def _mm_nt(a, b):
    return lax.dot_general(a.astype(bf16), b.astype(bf16), _NT, preferred_element_type=f32)


def _mm_tn(a, b):
    return lax.dot_general(a.astype(bf16), b.astype(bf16), _TN, preferred_element_type=f32)


def _mm3(a, b):
    a_hi, b_hi = a.astype(bf16), b.astype(bf16)
    a_lo = (a - a_hi.astype(f32)).astype(bf16)
    b_lo = (b - b_hi.astype(f32)).astype(bf16)
    dot = functools.partial(jnp.dot, preferred_element_type=f32)
    return dot(a_hi, b_hi) + dot(a_hi, b_lo) + dot(a_lo, b_hi)


def _softplus(x):
    return jnp.maximum(x, 0.0) + jnp.log(1.0 + jnp.exp(-jnp.abs(x)))


def _sigmoid(x):
    return 1.0 / (1.0 + jnp.exp(-x))


def _half_masks():
    lane = lax.broadcasted_iota(i32, (1, LANES), 1)
    m0 = (lane < HEAD_DIM).astype(f32)
    return m0, 1.0 - m0


def _seg_mean(x, m0, m1):
    s0 = jnp.sum(x * m0, axis=-1, keepdims=True)
    s1 = jnp.sum(x * m1, axis=-1, keepdims=True)
    return (m0 * s0 + m1 * s1) * (1.0 / HEAD_DIM)


def _seg_sum(x, m0, m1):
    s0 = jnp.sum(x * m0, axis=-1, keepdims=True)
    s1 = jnp.sum(x * m1, axis=-1, keepdims=True)
    return m0 * s0 + m1 * s1


def _block_diag_mask():
    r = lax.broadcasted_iota(i32, (LANES, LANES), 0) // HEAD_DIM
    c = lax.broadcasted_iota(i32, (LANES, LANES), 1) // HEAD_DIM
    return (r == c).astype(f32)


def _layer_norm_rows(z, g, b):
    mu = jnp.mean(z, axis=-1, keepdims=True)
    zc = z - mu
    var = jnp.mean(zc * zc, axis=-1, keepdims=True)
    return zc * lax.rsqrt(var + LN_EPS) * g + b


def _group_specs(width, n_prompt_tiles):
    return [pl.BlockSpec((TOKEN_TILE, width), lambda i, *_: (jnp.minimum(i, n_prompt_tiles - 1), 0)),
            pl.BlockSpec((TOKEN_TILE, width), lambda i, *_: (jnp.maximum(i - n_prompt_tiles, 0), 0))]


def _pick_group(p_ref, s_ref, n_prompt_tiles):
    return jnp.where(pl.program_id(0) >= n_prompt_tiles, s_ref[...], p_ref[...])


def _n_tiles(h_p, h_s):
    assert h_p.shape[0] % TOKEN_TILE == 0 and h_s.shape[0] % TOKEN_TILE == 0, (h_p.shape, h_s.shape)
    return h_p.shape[0] // TOKEN_TILE, h_s.shape[0] // TOKEN_TILE


def _in_proj_kernel(xp_ref, xs_ref, w_ref, ret_ref, rw_ref, lru_ref, *, n_prompt_tiles):
    xb = _pick_group(xp_ref, xs_ref, n_prompt_tiles).astype(bf16)
    c0, c1 = 4 * RET_W, 4 * RET_W + RWKV_PROJ
    ret_ref[...] = jnp.dot(xb, w_ref[:, :c0], preferred_element_type=f32)
    rw_ref[...] = jnp.dot(xb, w_ref[:, c0:c1], preferred_element_type=f32)
    lru_ref[...] = jnp.dot(xb, w_ref[:, c1:], preferred_element_type=f32)


def _in_proj(h_p, h_s, w_bf):
    npt, nst = _n_tiles(h_p, h_s)
    n = h_p.shape[0] + h_s.shape[0]
    tm = TOKEN_TILE
    row = lambda i: (i, 0)
    return pl.pallas_call(
        functools.partial(_in_proj_kernel, n_prompt_tiles=npt),
        grid=(npt + nst,),
        in_specs=_group_specs(D_MODEL, npt) + [
                  pl.BlockSpec((D_MODEL, D_PROJ), lambda i: (0, 0))],
        out_specs=[pl.BlockSpec((tm, 4 * RET_W), row),
                   pl.BlockSpec((tm, RWKV_PROJ), row),
                   pl.BlockSpec((tm, 2 * LRU_W), row)],
        out_shape=[jax.ShapeDtypeStruct((n, 4 * RET_W), f32),
                   jax.ShapeDtypeStruct((n, RWKV_PROJ), f32),
                   jax.ShapeDtypeStruct((n, 2 * LRU_W), f32)],
        compiler_params=_params("arbitrary"),
        name="in_proj",
    )(h_p, h_s, w_bf)


def _rope_tables(pos):
    half = HEAD_DIM // 2
    inv = ROPE_BASE ** (-jnp.arange(half, dtype=f32) / half)
    ang = pos[:, None] * inv[None, :]
    cos, sin = jnp.cos(ang), jnp.sin(ang)
    cos_f = jnp.tile(jnp.concatenate([cos, cos], axis=-1), (1, RET_H))
    sin_f = jnp.tile(jnp.concatenate([-sin, sin], axis=-1), (1, RET_H))
    return cos_f, sin_f


def _retention_tables(length, n_valid):
    lg = jnp.log1p(-jnp.exp2(-5.0 - jnp.arange(RET_H, dtype=f32)))
    idx = jnp.arange(length, dtype=f32)
    rel = idx[:, None] - idx[None, :]
    mask = jnp.where(rel[None] >= 0, jnp.exp(jnp.maximum(rel, 0.0)[None] * lg[:, None, None]), 0.0)
    q_dec = jnp.exp((idx[:, None] + 1.0) * lg[None, :])
    k_dec = jnp.where(idx[:, None] < n_valid, jnp.exp((n_valid - 1.0 - idx)[:, None] * lg[None, :]), 0.0)
    c_dec = jnp.exp(n_valid * lg)[None, :]
    rep = lambda t: jnp.repeat(t, HEAD_DIM, axis=-1)
    return mask, rep(q_dec), rep(k_dec), rep(c_dec)


def _retention_kernel(ret_ref, cos_ref, sin_ref, mask_ref, qdec_ref, kdec_ref, cdec_ref, s0_ref,
                      y_ref, s_ref, *, nb, length):
    @pl.when(pl.program_id(1) == 0)
    def _():
        s_ref[...] = s0_ref[...]

    m0, m1 = _half_masks()
    lane = lax.broadcasted_iota(i32, (1, LANES), 1)
    first_half = (lane % HEAD_DIM) < (HEAD_DIM // 2)
    bd = _block_diag_mask()

    def rope(x, cs, sn):
        swapped = jnp.where(first_half, pltpu.roll(x, LANES - HEAD_DIM // 2, 1), pltpu.roll(x, HEAD_DIM // 2, 1))
        return x * cs + swapped * sn

    for j in range(nb):
        rows = pl.ds(j * length, length)
        for p in range(RET_H // 2):
            cols = pl.ds(p * LANES, LANES)
            cs, sn = cos_ref[:, cols], sin_ref[:, cols]
            q2 = rope(ret_ref[rows, pl.ds(p * LANES, LANES)], cs, sn)
            k2 = rope(ret_ref[rows, pl.ds(RET_W + p * LANES, LANES)], cs, sn) * (HEAD_DIM ** -0.5)
            v2 = ret_ref[rows, pl.ds(2 * RET_W + p * LANES, LANES)]
            g2 = ret_ref[rows, pl.ds(3 * RET_W + p * LANES, LANES)]
            state = s_ref[j, p]
            o2 = _mm(q2, state) * qdec_ref[:, cols]
            for hh, m in enumerate((m0, m1)):
                sc = _mm_nt(q2 * m, k2) * mask_ref[2 * p + hh]
                o2 = o2 + _mm(sc, v2) * m
            s_ref[j, p] = state * cdec_ref[:, cols] + _mm_tn(k2 * kdec_ref[:, cols], v2) * bd
            mu = _seg_mean(o2, m0, m1)
            oc = o2 - mu
            var = _seg_mean(oc * oc, m0, m1)
            y_ref[rows, cols] = g2 * _sigmoid(g2) * oc * lax.rsqrt(var + LN_EPS)


def _retention(ret, cos_f, sin_f, tables, s0_bd, *, n_batch, n_chunks, nb, length, out_rows):
    mask, q_dec, k_dec, c_dec = tables
    blk = lambda g, c: (g * n_chunks + c, 0)
    const2 = lambda g, c: (0, 0)
    st = lambda g, c: (g, 0, 0, 0)
    return pl.pallas_call(
        functools.partial(_retention_kernel, nb=nb, length=length),
        grid=(n_batch // nb, n_chunks),
        in_specs=[pl.BlockSpec((nb * length, 4 * RET_W), blk),
                  pl.BlockSpec((length, RET_W), lambda g, c: (c, 0)),
                  pl.BlockSpec((length, RET_W), lambda g, c: (c, 0)),
                  pl.BlockSpec((RET_H, length, length), lambda g, c: (0, 0, 0)),
                  pl.BlockSpec((length, RET_W), const2),
                  pl.BlockSpec((length, RET_W), const2),
                  pl.BlockSpec((1, RET_W), const2),
                  pl.BlockSpec((nb, RET_H // 2, LANES, LANES), st)],
        out_specs=[pl.BlockSpec((nb * length, RET_W), blk),
                   pl.BlockSpec((nb, RET_H // 2, LANES, LANES), st)],
        out_shape=[jax.ShapeDtypeStruct((out_rows, RET_W), f32),
                   jax.ShapeDtypeStruct((n_batch, RET_H // 2, LANES, LANES), f32)],
        compiler_params=_params("arbitrary", "arbitrary"),
        name="retention",
    )(ret, cos_f, sin_f, mask, q_dec, k_dec, c_dec, s0_bd)


def _rwkv_kernel(rw_ref, aux_ref, valid_ref, mix_ref, vec_ref, lora_ref, tri_ref, s0_ref,
                 y_ref, s_ref, *rest, rows, length, chain):
    n_chunks = rows // length
    rw = rw_ref[...]
    if chain:
        last_ref, xs_ref = rest

        @pl.when(pl.program_id(1) == 0)
        def _():
            s_ref[...] = s0_ref[...]
            xs_ref[pl.ds(0, SUBLANES), :] = jnp.broadcast_to(aux_ref[0], (SUBLANES, RWKV_PROJ))

        xs_ref[pl.ds(SUBLANES, rows), :] = rw
        prev = xs_ref[pl.ds(SUBLANES - 1, rows), :]
        xs_ref[pl.ds(0, SUBLANES), :] = rw[rows - SUBLANES:, :]
        last_ref[0] = rw[rows - 1:rows, :]
    else:
        prev = aux_ref[...]
    rwm = rw + (prev - rw) * mix_ref[...]

    valid = valid_ref[...]
    w0, a0, k_k, k_a, r_k, gn_g, gn_b = (vec_ref[pl.ds(i, 1), :] for i in range(7))
    lo = rwm[:, RWKV_LORA_COL:]
    lw = _mm3(jnp.tanh(lo), lora_ref[0])
    la = _mm3(lo, lora_ref[1])
    gate = _mm3(_sigmoid(lo), lora_ref[2])
    logw = -jnp.exp(-_softplus(-(w0 + lw)) - 0.5) * valid
    a = _sigmoid(a0 + la)
    r = rwm[:, :RWKV_W]
    kr = rwm[:, RWKV_W:2 * RWKV_W]
    vr = rwm[:, 2 * RWKV_W:3 * RWKV_W]
    kk_raw = kr * k_k
    kp = kr * (1.0 + (a - 1.0) * k_a) * valid
    cum = _mm3(tri_ref[...], logw)
    g_incl = jnp.exp(cum)
    g_inv = jnp.exp(-cum)
    g_prev = jnp.exp(cum - logw)
    g_end = jnp.concatenate(
        [jnp.broadcast_to(g_incl[(c + 1) * length - 1:(c + 1) * length, :], (length, RWKV_W)) for c in range(n_chunks)],
        axis=0)

    m0, m1 = _half_masks()
    bd = _block_diag_mask()
    stacked = 2 * rows
    ri = lax.broadcasted_iota(i32, (stacked, stacked), 0)
    ci = lax.broadcasted_iota(i32, (stacked, stacked), 1)
    same = (ri // length) == (ci // length)
    strict = jnp.logical_and(same, ci < ri)
    incl = jnp.logical_and(same, ci <= ri)
    eye = (ci == ri).astype(f32)
    stack = lambda t: jnp.concatenate([t * m0, t * m1], axis=0)
    unstack = lambda t: t[:rows] + t[rows:]
    pairs = range(RWKV_H // 2)

    pre = []
    for p in pairs:
        sl = slice(p * LANES, (p + 1) * LANES)
        kk2 = kk_raw[:, sl]
        kk2 = kk2 * lax.rsqrt(jnp.maximum(_seg_sum(kk2 * kk2, m0, m1), 1e-24)) * valid[:, sl]
        d = dict(sl=sl, r2=r[:, sl], v2=vr[:, sl], kp2=kp[:, sl], ge=g_end[:, sl])
        d['kh'] = kk2 * g_prev[:, sl]
        rh = d['r2'] * g_incl[:, sl]
        bt = kk2 * a[:, sl] * g_inv[:, sl]
        kt = d['kp2'] * g_inv[:, sl]
        d['b_end'] = bt * d['ge']
        d['k_end'] = kt * d['ge']
        d['kh_s'], d['rh_s'], d['v_s'] = stack(d['kh']), stack(rh), stack(d['v2'])
        gram = _mm_nt(jnp.concatenate([d['kh_s'], d['rh_s']], axis=0), jnp.concatenate([stack(bt), stack(kt)], axis=0))
        d['x'] = -jnp.where(strict, gram[:stacked, :stacked], 0.0)
        d['a_k'] = jnp.where(strict, gram[:stacked, stacked:], 0.0)
        d['r_b'] = jnp.where(incl, gram[stacked:, :stacked], 0.0)
        d['r_k'] = jnp.where(incl, gram[stacked:, stacked:], 0.0)
        d['t'] = eye + d['x']
        pre.append(d)
    cover = 2
    while cover < length:
        for d in pre:
            d['x'] = _mm(d['x'], d['x'])
            d['t'] = d['t'] + _mm(d['t'], d['x'])
        cover *= 2
    for d in pre:
        rhs0 = -_mm(d['a_k'], d['v_s'])
        tz = _mm(d['t'], jnp.concatenate([rhs0, d['kh_s']], axis=1))
        z0_s = tz[:, :LANES]
        rbz = _mm(d['r_b'], tz)
        d['z0'] = unstack(z0_s)
        d['q'] = unstack(d['rh_s'] - rbz[:, LANES:])
        d['y0'] = unstack(rbz[:, :LANES] + _mm(d['r_k'], d['v_s']))
        d['w'] = unstack(_mm_tn(d['t'], stack(d['b_end'])))

    for p, d in zip(pairs, pre):
        sl, ge, kh, v2 = d['sl'], d['ge'], d['kh'], d['v2']
        state = s_ref[0, p] if chain else None
        ys = []
        for c in range(n_chunks):
            cr = slice(c * length, (c + 1) * length)
            if not chain:
                state = s0_ref[c, p]
            ys.append(_mm_nt(d['q'][cr], state) + d['y0'][cr])
            n_c = (_mm_tn(d['z0'][cr], d['b_end'][cr]) + _mm_tn(v2[cr], d['k_end'][cr])) * bd
            kw = _mm_tn(kh[cr], d['w'][cr]) * bd
            state = state * ge[(c + 1) * length - 1:(c + 1) * length, :] - _mm(state, kw) + n_c
            if not chain:
                s_ref[c, p] = state
        if chain:
            s_ref[0, p] = state
        y2 = jnp.concatenate(ys, axis=0) if n_chunks > 1 else ys[0]

        mu = _seg_mean(y2, m0, m1)
        yc = y2 - mu
        var = _seg_mean(yc * yc, m0, m1)
        yn = yc * lax.rsqrt(var + RWKV_GN_EPS) * gn_g[:, sl] + gn_b[:, sl]
        bonus = _seg_sum(d['r2'] * d['kp2'] * r_k[:, sl], m0, m1) * v2
        y_ref[:, pl.ds(p * LANES, LANES)] = (yn + bonus) * gate[:, sl]


def _rwkv(rw, aux, valid, mix, vec, lora, tri, s0_bd, *, n_groups, n_steps, rows, length, chain):
    blk = lambda g, c: (g * n_steps + c, 0)
    const2 = lambda g, c: (0, 0)
    st = lambda g, c: (g, 0, 0, 0)
    n_state = 1 if chain else rows // length
    aux_spec = (pl.BlockSpec((1, 1, RWKV_PROJ), lambda g, c: (g, 0, 0)) if chain
                else pl.BlockSpec((rows, RWKV_PROJ), blk))
    return pl.pallas_call(
        functools.partial(_rwkv_kernel, rows=rows, length=length, chain=chain),
        grid=(n_groups, n_steps),
        in_specs=[pl.BlockSpec((rows, RWKV_PROJ), blk),
                  aux_spec,
                  pl.BlockSpec((rows, RWKV_W), const2),
                  pl.BlockSpec((1, RWKV_PROJ), const2),
                  pl.BlockSpec((SUBLANES, RWKV_W), const2),
                  pl.BlockSpec((3, LANES, RWKV_W), lambda g, c: (0, 0, 0)),
                  pl.BlockSpec((rows, rows), const2),
                  pl.BlockSpec((n_state, RWKV_H // 2, LANES, LANES), st)],
        out_specs=[pl.BlockSpec((rows, RWKV_W), blk),
                   pl.BlockSpec((n_state, RWKV_H // 2, LANES, LANES), st)]
        + ([pl.BlockSpec((1, 1, RWKV_PROJ), lambda g, c: (g, 0, 0))] if chain else []),
        out_shape=[jax.ShapeDtypeStruct((n_groups * n_steps * rows, RWKV_W), f32),
                   jax.ShapeDtypeStruct((n_groups * n_state, RWKV_H // 2, LANES, LANES), f32)]
        + ([jax.ShapeDtypeStruct((n_groups, 1, RWKV_PROJ), f32)] if chain else []),
        scratch_shapes=[pltpu.VMEM((rows + SUBLANES, RWKV_PROJ), f32)] if chain else [],
        compiler_params=_params("arbitrary", "arbitrary"),
        name="rwkv7",
    )(rw, aux, valid, mix, vec, lora, tri, s0_bd)


def _chunk_tri(rows, length):
    idx = jnp.arange(rows)
    same = (idx[:, None] // length) == (idx[None, :] // length)
    return jnp.logical_and(same, idx[None, :] <= idx[:, None]).astype(f32)


def _lru_kernel(lru_ref, valid_ref, vec_ref, wa_ref, wx_ref, conv0_ref, h0_ref,
                y_ref, h_ref, tail_ref, xext_ref, a_ref, b_ref, hs_ref, *, nb, length):
    @pl.when(pl.program_id(1) == 0)
    def _():
        h_ref[...] = h0_ref[...]
        xext_ref[:, pl.ds(0, SUBLANES), :] = conv0_ref[...]

    valid = valid_ref[...] > 0.5
    cw = [vec_ref[pl.ds(i, 1), :] for i in range(CONV_WIDTH)]
    cb, ba, bx, lam = (vec_ref[pl.ds(i, 1), :] for i in range(CONV_WIDTH, CONV_WIDTH + 4))
    sp = _softplus(-lam)
    for j in range(nb):
        rows = pl.ds(j * length, length)
        gbr = lru_ref[rows, pl.ds(0, LRU_W)]
        x = lru_ref[rows, pl.ds(LRU_W, LRU_W)]
        xext_ref[j, pl.ds(SUBLANES, length), :] = x
        xc = cb + x * cw[CONV_WIDTH - 1]
        for t in range(CONV_WIDTH - 1):
            xc = xc + xext_ref[j, pl.ds(SUBLANES - (CONV_WIDTH - 1) + t, length), :] * cw[t]
        xext_ref[j, pl.ds(0, SUBLANES), :] = x[length - SUBLANES:, :]
        tail_ref[j] = x[length - SUBLANES:, :]
        gate_a = _sigmoid(_mm(xc, wa_ref[...]) + ba)
        gate_x = _sigmoid(_mm(xc, wx_ref[...]) + bx)
        log_a = -LRU_C * gate_a * sp
        a = jnp.exp(log_a)
        b = xc * gate_x * jnp.sqrt(1.0 - jnp.exp(2.0 * log_a))
        a_ref[...] = jnp.where(valid, a, 1.0)
        b_ref[...] = jnp.where(valid, b, 0.0)

        def step(t, h):
            h = a_ref[pl.ds(t, 1), :] * h + b_ref[pl.ds(t, 1), :]
            hs_ref[pl.ds(t, 1), :] = h
            return h

        h = lax.fori_loop(0, length, step, h_ref[j, pl.ds(0, 1), :])
        h_ref[j] = jnp.broadcast_to(h, (SUBLANES, LRU_W))
        c = 0.7978845608028654
        gelu = 0.5 * gbr * (1.0 + jnp.tanh(c * (gbr + 0.044715 * gbr * gbr * gbr)))
        y_ref[rows, :] = hs_ref[...] * gelu


def _lru(lru, valid, vec, wa_bd, wx_bd, conv0, h0, *, n_batch, n_chunks, nb, length, out_rows):
    blk = lambda g, c: (g * n_chunks + c, 0)
    const2 = lambda g, c: (0, 0)
    st = lambda g, c: (g, 0, 0)
    return pl.pallas_call(
        functools.partial(_lru_kernel, nb=nb, length=length),
        grid=(n_batch // nb, n_chunks),
        in_specs=[pl.BlockSpec((nb * length, 2 * LRU_W), blk),
                  pl.BlockSpec((length, LRU_W), const2),
                  pl.BlockSpec((SUBLANES, LRU_W), const2),
                  pl.BlockSpec((LRU_W, LRU_W), const2),
                  pl.BlockSpec((LRU_W, LRU_W), const2),
                  pl.BlockSpec((nb, SUBLANES, LRU_W), st),
                  pl.BlockSpec((nb, SUBLANES, LRU_W), st)],
        out_specs=[pl.BlockSpec((nb * length, LRU_W), blk),
                   pl.BlockSpec((nb, SUBLANES, LRU_W), st),
                   pl.BlockSpec((nb, SUBLANES, LRU_W), st)],
        out_shape=[jax.ShapeDtypeStruct((out_rows, LRU_W), f32),
                   jax.ShapeDtypeStruct((n_batch, SUBLANES, LRU_W), f32),
                   jax.ShapeDtypeStruct((n_batch, SUBLANES, LRU_W), f32)],
        scratch_shapes=[pltpu.VMEM((nb, length + SUBLANES, LRU_W), f32),
                        pltpu.VMEM((length, LRU_W), f32),
                        pltpu.VMEM((length, LRU_W), f32),
                        pltpu.VMEM((length, LRU_W), f32)],
        compiler_params=_params("arbitrary", "arbitrary"),
        name="rg_lru",
    )(lru, valid, vec, wa_bd, wx_bd, conv0, h0)


def _out_proj_kernel(yrp_ref, yrs_ref, ywp_ref, yws_ref, ylp_ref, yls_ref, xp_ref, xs_ref,
                     w_ref, ln_ref, wr_ref, br_ref,
                     x1_ref, lpos_ref, g_ref, meta_ref, cnt_ref, *, alpha, n_prompt_tiles):
    @pl.when(pl.program_id(0) == 0)
    def _():
        cnt_ref[...] = jnp.zeros(cnt_ref.shape, f32)

    pick = functools.partial(_pick_group, n_prompt_tiles=n_prompt_tiles)
    mixed = (_mm(pick(yrp_ref, yrs_ref), w_ref[pl.ds(0, RET_W), :])
             + _mm(pick(ywp_ref, yws_ref), w_ref[pl.ds(RET_W, RWKV_W), :])
             + _mm(pick(ylp_ref, yls_ref), w_ref[pl.ds(RET_W + RWKV_W, LRU_W), :]))
    x1 = _layer_norm_rows(alpha * pick(xp_ref, xs_ref) + mixed, ln_ref[pl.ds(0, 1), :], ln_ref[pl.ds(1, 1), :])
    x1_ref[...] = x1
    logits = _mm3(x1, wr_ref[...]) + br_ref[...]
    tm = logits.shape[0]
    lane = lax.broadcasted_iota(i32, logits.shape, 1).astype(f32)
    top_v = jnp.zeros(logits.shape, f32)
    work = logits
    v_max = None
    onehots = []
    for k in range(TOP_K):
        v = jnp.max(work, axis=-1, keepdims=True)
        idx = jnp.min(jnp.where(work == v, lane, float(LANES)), axis=-1, keepdims=True)
        if k == 0:
            v_max = v
        hit = lane == idx
        onehots.append(hit.astype(f32))
        top_v = jnp.where(lane == k, jnp.exp(v - v_max), top_v)
        work = jnp.where(hit, -jnp.inf, work)
    g_ref[...] = top_v / jnp.sum(top_v, axis=-1, keepdims=True)

    total = onehots[0] + onehots[1] + onehots[2] + onehots[3]
    ri = lax.broadcasted_iota(i32, (tm, tm), 0)
    ci = lax.broadcasted_iota(i32, (tm, tm), 1)
    before = _mm((ci < ri).astype(f32), total)
    groups = jnp.floor((jnp.sum(total, axis=0, keepdims=True) + (SUBLANES - 1.0)) * (1.0 / SUBLANES))
    er = lax.broadcasted_iota(i32, (LANES, LANES), 0)
    ec = lax.broadcasted_iota(i32, (LANES, LANES), 1)
    run_len = groups * SUBLANES
    run_start = _mm(jnp.broadcast_to(groups, (SUBLANES, LANES)), (er < ec).astype(f32))[0:1] * SUBLANES
    lpos = jnp.full(logits.shape, -1.0, f32)
    for k in range(TOP_K):
        pos = jnp.sum(onehots[k] * (before + run_start), axis=-1, keepdims=True)
        lpos = jnp.where(lane == k, pos, lpos)
    lpos_ref[...] = lpos
    row = lax.broadcasted_iota(i32, (SUBLANES, LANES), 0)
    meta = jnp.where(row == 0, run_start, jnp.where(row == 1, run_len, jnp.where(row == 2, cnt_ref[...], 0.0)))
    meta_ref[...] = meta.astype(i32)
    cnt_ref[...] = cnt_ref[...] + run_len


def _out_proj_router(y_ret, y_rwkv, y_lru, h, w_out_bf, ln, w_router, b_router, alpha):
    npt, nst = _n_tiles(*h)
    n = h[0].shape[0] + h[1].shape[0]
    tm = TOKEN_TILE
    row = lambda i: (i, 0)
    const = lambda i: (0, 0)
    return pl.pallas_call(
        functools.partial(_out_proj_kernel, alpha=alpha, n_prompt_tiles=npt),
        grid=(npt + nst,),
        in_specs=_group_specs(RET_W, npt) + _group_specs(RWKV_W, npt) + _group_specs(LRU_W, npt)
        + _group_specs(D_MODEL, npt) + [
                  pl.BlockSpec((D_MODEL, D_MODEL), const),
                  pl.BlockSpec((SUBLANES, D_MODEL), const),
                  pl.BlockSpec((D_MODEL, LANES), const),
                  pl.BlockSpec((1, LANES), const)],
        out_specs=[pl.BlockSpec((tm, D_MODEL), row), pl.BlockSpec((tm, LANES), row), pl.BlockSpec((tm, LANES), row),
                   pl.BlockSpec((SUBLANES, LANES), row), pl.BlockSpec((SUBLANES, LANES), const)],
        out_shape=[jax.ShapeDtypeStruct((n, D_MODEL), f32),
                   jax.ShapeDtypeStruct((n, LANES), f32),
                   jax.ShapeDtypeStruct((n, LANES), f32),
                   jax.ShapeDtypeStruct((n // tm * SUBLANES, LANES), i32),
                   jax.ShapeDtypeStruct((SUBLANES, LANES), f32)],
        compiler_params=_params("arbitrary"),
        name="out_proj_router",
    )(*y_ret, *y_rwkv, *y_lru, *h, w_out_bf, ln, w_router, b_router)


def _for_each_run_piece(meta_ref, pstart_ref, fn):
    for e in range(N_EXPERTS):
        start, n = meta_ref[0, e], meta_ref[1, e]
        base = pstart_ref[e] + meta_ref[2, e]
        for sz in RUN_PIECES:
            done = n & ~(2 * sz - 1)

            @pl.when((n & sz) != 0)
            def _():
                fn(pl.multiple_of(start + done, SUBLANES), pl.multiple_of(base + done, SUBLANES), sz)


def _wait_run_rows(meta_ref, make_copy):
    total = meta_ref[1, 0]
    for e in range(1, N_EXPERTS):
        total = total + meta_ref[1, e]
    for sz in WAIT_PIECES:
        @pl.when((total & sz) != 0)
        def _():
            make_copy(sz).wait()


def _dispatch_kernel(pstart_ref, cnt_ref, n_used_ref, meta_ref, meta_prev_ref, lpos_ref, x1_ref, xs_hbm,
                     xl_ref, zero_ref, sems, zsem):
    tm = TOKEN_TILE
    bm = MOE_BLOCK
    i = pl.program_id(0)
    slot = i % 2

    @pl.when(i == 0)
    def _():
        zero_ref[...] = jnp.zeros(zero_ref.shape, f32)
        tail = lambda j: pltpu.make_async_copy(zero_ref, xs_hbm.at[pl.ds(j * bm, bm), :], zsem)

        def tail_start(j, carry):
            tail(j).start()
            return carry

        def tail_wait(j, carry):
            tail(j).wait()
            return carry

        n_blocks = xs_hbm.shape[0] // bm
        lax.fori_loop(n_used_ref[0], n_blocks, tail_start, 0)
        lax.fori_loop(n_used_ref[0], n_blocks, tail_wait, 0)
        for e in range(N_EXPERTS):
            lo = pstart_ref[e] + cnt_ref[e]
            n_groups = ((cnt_ref[e] + bm - 1) // bm * bm - cnt_ref[e]) // SUBLANES
            fill = lambda g: pltpu.make_async_copy(
                zero_ref.at[pl.ds(0, SUBLANES), :],
                xs_hbm.at[pl.ds(pl.multiple_of(lo + g * SUBLANES, SUBLANES), SUBLANES), :], zsem)

            def start(g, carry):
                fill(g).start()
                return carry

            def wait(g, carry):
                fill(g).wait()
                return carry

            lax.fori_loop(0, n_groups, start, 0)
            lax.fori_loop(0, n_groups, wait, 0)

    lpos_t = lpos_ref[...].T
    srow = lax.broadcasted_iota(i32, (LOCAL_ROWS, tm), 0).astype(f32)
    perm = jnp.zeros((LOCAL_ROWS, tm), f32)
    for k in range(TOP_K):
        perm = perm + (srow == lpos_t[k:k + 1, :]).astype(f32)
    xl_ref[slot] = jnp.dot(perm.astype(bf16), x1_ref[...].astype(bf16), preferred_element_type=f32)

    def send(local_row, sorted_row, n_rows):
        pltpu.make_async_copy(xl_ref.at[slot, pl.ds(local_row, n_rows), :],
                              xs_hbm.at[pl.ds(sorted_row, n_rows), :], sems.at[slot]).start()

    _for_each_run_piece(meta_ref, pstart_ref, send)

    def sent(s):
        return lambda n_rows: pltpu.make_async_copy(xl_ref.at[s, pl.ds(0, n_rows), :],
                                                    xs_hbm.at[pl.ds(0, n_rows), :], sems.at[s])

    @pl.when(i > 0)
    def _():
        _wait_run_rows(meta_prev_ref, sent(1 - slot))

    @pl.when(i == pl.num_programs(0) - 1)
    def _():
        _wait_run_rows(meta_ref, sent(slot))


def _dispatch(pstart, counts, n_used, meta, lpos, x1, n_rows):
    n = x1.shape[0]
    tm = TOKEN_TILE
    smem_tile = lambda f: pl.BlockSpec((SUBLANES, LANES), f, memory_space=pltpu.SMEM)
    grid_spec = pltpu.PrefetchScalarGridSpec(
        num_scalar_prefetch=3,
        grid=(n // tm,),
        in_specs=[smem_tile(lambda i, ps, ct, nu: (i, 0)),
                  smem_tile(lambda i, ps, ct, nu: (jnp.maximum(i - 1, 0), 0)),
                  pl.BlockSpec((tm, LANES), lambda i, ps, ct, nu: (i, 0)),
                  pl.BlockSpec((tm, D_MODEL), lambda i, ps, ct, nu: (i, 0))],
        out_specs=pl.BlockSpec(memory_space=pl.ANY),
        scratch_shapes=[pltpu.VMEM((2, LOCAL_ROWS, D_MODEL), f32),
                        pltpu.VMEM((MOE_BLOCK, D_MODEL), f32),
                        pltpu.SemaphoreType.DMA((2,)), pltpu.SemaphoreType.DMA(())],
    )
    return pl.pallas_call(
        _dispatch_kernel,
        grid_spec=grid_spec,
        out_shape=jax.ShapeDtypeStruct((n_rows, D_MODEL), f32),
        compiler_params=_params("arbitrary"),
        name="moe_dispatch",
    )(pstart, counts, n_used, meta, meta, lpos, x1)


def _expert_kernel(blk_e_ref, n_used_ref, xs_ref, wgu_ref, bgu_ref, wdn_ref, bdn_ref,
                   out_ref, wgu_bf, wdn_bf):
    i = pl.program_id(0)
    prev = jnp.maximum(i - 1, 0)
    new_expert = jnp.logical_or(i == 0, blk_e_ref[i] != blk_e_ref[prev])
    used = i < n_used_ref[0]

    @pl.when(used)
    def _():
        @pl.when(new_expert)
        def _():
            wgu_bf[...] = wgu_ref[0, 0].astype(bf16)
            wdn_bf[...] = wdn_ref[0, 0].astype(bf16)

        gu = jnp.dot(xs_ref[...].astype(bf16), wgu_bf[...], preferred_element_type=f32) + bgu_ref[0, 0]
        g = jnp.minimum(gu[:, :D_EXPERT], SWIGLU_LIMIT)
        u = jnp.clip(gu[:, D_EXPERT:], -SWIGLU_LIMIT, SWIGLU_LIMIT)
        hdn = (u + 1.0) * g * _sigmoid(SWIGLU_ALPHA * g)
        out_ref[...] = jnp.dot(hdn.astype(bf16), wdn_bf[...], preferred_element_type=f32) + bdn_ref[0, 0]

    @pl.when(jnp.logical_not(used))
    def _():
        out_ref[...] = jnp.zeros(out_ref.shape, f32)


def _experts(blk_e, n_used, xs, w_gu, b_gu, w_down, b_down, layer):
    n_blocks = blk_e.shape[0]
    bm = MOE_BLOCK
    by_e = lambda i, be, nu: (layer, be[i], 0, 0)
    x_blk = lambda i, be, nu: (jnp.minimum(i, nu[0] - 1), 0)
    grid_spec = pltpu.PrefetchScalarGridSpec(
        num_scalar_prefetch=2,
        grid=(n_blocks,),
        in_specs=[pl.BlockSpec((bm, D_MODEL), x_blk),
                  pl.BlockSpec((1, 1, D_MODEL, 2 * D_EXPERT), by_e),
                  pl.BlockSpec((1, 1, 1, 2 * D_EXPERT), by_e),
                  pl.BlockSpec((1, 1, D_EXPERT, D_MODEL), by_e),
                  pl.BlockSpec((1, 1, 1, D_MODEL), by_e)],
        out_specs=pl.BlockSpec((bm, D_MODEL), lambda i, be, nu: (i, 0)),
        scratch_shapes=[pltpu.VMEM((D_MODEL, 2 * D_EXPERT), bf16),
                        pltpu.VMEM((D_EXPERT, D_MODEL), bf16)],
    )
    depth = w_gu.shape[0]
    return pl.pallas_call(
        _expert_kernel,
        grid_spec=grid_spec,
        out_shape=jax.ShapeDtypeStruct((n_blocks * bm, D_MODEL), f32),
        compiler_params=_params("arbitrary"),
        name="moe_experts",
    )(blk_e, n_used, xs, w_gu, b_gu.reshape(depth, N_EXPERTS, 1, 2 * D_EXPERT),
      w_down, b_down.reshape(depth, N_EXPERTS, 1, D_MODEL))


def _combine_kernel(pstart_ref, meta_ref, meta_next_ref, lpos_ref, gates_ref, x1_ref, ln_ref, yb_hbm,
                    outp_ref, outs_ref, yl_ref, sems, *, alpha, n_prompt_tiles):
    tm = TOKEN_TILE
    i = pl.program_id(0)
    slot = i % 2

    def fetch(meta, s):
        def recv(local_row, sorted_row, n_rows):
            pltpu.make_async_copy(yb_hbm.at[pl.ds(sorted_row, n_rows), :],
                                  yl_ref.at[s, pl.ds(local_row, n_rows), :], sems.at[s]).start()

        _for_each_run_piece(meta, pstart_ref, recv)

    @pl.when(i == 0)
    def _():
        yl_ref[...] = jnp.zeros(yl_ref.shape, f32)
        fetch(meta_ref, slot)

    @pl.when(i + 1 < pl.num_programs(0))
    def _():
        fetch(meta_next_ref, 1 - slot)

    _wait_run_rows(meta_ref, lambda n_rows: pltpu.make_async_copy(
        yb_hbm.at[pl.ds(0, n_rows), :], yl_ref.at[slot, pl.ds(0, n_rows), :], sems.at[slot]))

    lpos = lpos_ref[...]
    gates = gates_ref[...]
    scol = lax.broadcasted_iota(i32, (tm, LOCAL_ROWS), 1).astype(f32)
    weight = jnp.zeros((tm, LOCAL_ROWS), f32)
    for k in range(TOP_K):
        weight = weight + jnp.where(scol == lpos[:, k:k + 1], gates[:, k:k + 1], 0.0)
    y = jnp.dot(weight.astype(bf16), yl_ref[slot].astype(bf16), preferred_element_type=f32)
    x2 = _layer_norm_rows(alpha * x1_ref[...] + y, ln_ref[pl.ds(0, 1), :], ln_ref[pl.ds(1, 1), :])

    @pl.when(i < n_prompt_tiles)
    def _():
        outp_ref[...] = x2

    @pl.when(i >= n_prompt_tiles)
    def _():
        outs_ref[...] = x2


def _combine(pstart, meta, lpos, gates, x1, ln, yb, alpha, n_prompt_tiles):
    n = x1.shape[0]
    tm = TOKEN_TILE
    n_tiles = n // tm
    row = lambda i, ps: (i, 0)
    smem_tile = lambda f: pl.BlockSpec((SUBLANES, LANES), f, memory_space=pltpu.SMEM)
    grid_spec = pltpu.PrefetchScalarGridSpec(
        num_scalar_prefetch=1,
        grid=(n_tiles,),
        in_specs=[smem_tile(row),
                  smem_tile(lambda i, ps: (jnp.minimum(i + 1, n_tiles - 1), 0)),
                  pl.BlockSpec((tm, LANES), row),
                  pl.BlockSpec((tm, LANES), row),
                  pl.BlockSpec((tm, D_MODEL), row),
                  pl.BlockSpec((SUBLANES, D_MODEL), lambda i, ps: (0, 0)),
                  pl.BlockSpec(memory_space=pl.ANY)],
        out_specs=_group_specs(D_MODEL, n_prompt_tiles),
        scratch_shapes=[pltpu.VMEM((2, LOCAL_ROWS, D_MODEL), f32), pltpu.SemaphoreType.DMA((2,))],
    )
    return pl.pallas_call(
        functools.partial(_combine_kernel, alpha=alpha, n_prompt_tiles=n_prompt_tiles),
        grid_spec=grid_spec,
        out_shape=[jax.ShapeDtypeStruct((n_prompt_tiles * tm, D_MODEL), f32),
                   jax.ShapeDtypeStruct((n - n_prompt_tiles * tm, D_MODEL), f32)],
        compiler_params=_params("arbitrary"),
        name="moe_combine",
    )(pstart, meta, meta, lpos, gates, x1, ln, yb)


def _block_tables(counts, n_tokens):
    bm = MOE_BLOCK
    padded = (counts + bm - 1) // bm * bm
    pad_end = jnp.cumsum(padded)
    pstart = (pad_end - padded).astype(i32)
    max_used = n_tokens * TOP_K + (n_tokens // TOKEN_TILE) * N_EXPERTS * (SUBLANES - 1)
    n_blocks = -(-(max_used + N_EXPERTS * (bm - 1)) // bm)
    first_row = jnp.arange(n_blocks, dtype=pad_end.dtype) * bm
    blk_e = jnp.minimum(jnp.sum(pad_end[None, :] <= first_row[:, None], axis=1), N_EXPERTS - 1).astype(i32)
    n_used = (pad_end[-1] // bm).astype(i32).reshape(1)
    return pstart, blk_e, n_used, n_blocks * bm


def _pad_time(t, n_batch, n_t, t_pad):
    w = t.shape[-1]
    return jnp.pad(t.reshape(n_batch, n_t, w), ((0, 0), (0, t_pad - n_t), (0, 0))).reshape(n_batch * t_pad, w)


def _to_block_diag(s):
    b, h = s.shape[:2]
    s = s.reshape(b, h // 2, 2, HEAD_DIM, HEAD_DIM)
    z = jnp.zeros_like(s[:, :, 0])
    top = jnp.concatenate([s[:, :, 0], z], axis=-1)
    bot = jnp.concatenate([z, s[:, :, 1]], axis=-1)
    return jnp.concatenate([top, bot], axis=-2)


def _from_block_diag(s):
    b, hp = s.shape[:2]
    return jnp.stack([s[:, :, :HEAD_DIM, :HEAD_DIM], s[:, :, HEAD_DIM:, HEAD_DIM:]], axis=2).reshape(
        b, 2 * hp, HEAD_DIM, HEAD_DIM)


def _block_diag_weight(w):
    h = w.shape[0]
    eye = jnp.eye(h, dtype=w.dtype)
    return (eye[:, None, :, None] * w[:, :, None, :]).reshape(h * HEAD_DIM, h * HEAD_DIM)


def _rows8(*rows):
    width = rows[0].shape[-1]
    m = jnp.stack([r.reshape(width) for r in rows])
    return jnp.pad(m, ((0, SUBLANES - m.shape[0]), (0, 0)))


def _layer(h, p, moe, layer, alpha, st_s, bp, tp, bs, ts):
    n_p, n_s = bp * tp, bs * ts
    tpad = SAMPLE_T_PAD
    ret, rw, lru = _in_proj(*h, p['w_in'].astype(bf16))

    def merge(y_p, y_s):
        w = y_p.shape[-1]
        return y_p, y_s.reshape(bs, tpad, w)[:, :ts].reshape(n_s, w)

    c_p = RET_CHUNK if tp % RET_CHUNK == 0 else tp
    cos_p, sin_p = _rope_tables(jnp.arange(tp, dtype=f32))
    cos_s, sin_s = _rope_tables(PAST_LEN + jnp.arange(tpad, dtype=f32))
    y_ret, sret_p = _retention(ret, cos_p, sin_p, _retention_tables(c_p, c_p),
                               jnp.zeros((bp, RET_H // 2, LANES, LANES), f32),
                               n_batch=bp, n_chunks=tp // c_p, nb=1, length=c_p, out_rows=n_p)
    y_ret_s, sret_s = _retention(_pad_time(ret[n_p:], bs, ts, tpad), cos_s, sin_s, _retention_tables(tpad, ts),
                                 _to_block_diag(st_s['ret']),
                                 n_batch=bs, n_chunks=1, nb=SUBLANES, length=tpad, out_rows=bs * tpad)
    y_ret = merge(y_ret, y_ret_s)

    lora = jnp.zeros((3, LANES, RWKV_W), f32)
    lora = lora.at[0, 0:32].set(p['w_up']).at[1, 32:64].set(p['a_up']).at[2, 64:128].set(p['g_up'])
    vec = _rows8(p['w0'], p['a0'], p['k_k'], p['k_a'], p['r_k'], p['gn_g'], p['gn_b'])
    mix = p['mix'].reshape(1, RWKV_PROJ)
    rows = RWKV_ROWS
    y_rwkv, srw_p, shift_p = _rwkv(rw, jnp.zeros((bp, 1, RWKV_PROJ), f32), jnp.ones((rows, RWKV_W), f32), mix, vec,
                                   lora, _chunk_tri(rows, RWKV_CHUNK),
                                   jnp.zeros((bp, RWKV_H // 2, LANES, LANES), f32),
                                   n_groups=bp, n_steps=tp // rows, rows=rows, length=RWKV_CHUNK, chain=True)
    valid_s = (jnp.arange(tpad) < ts).astype(f32)[:, None]
    rw_s3 = rw[n_p:].reshape(bs, ts, RWKV_PROJ)
    prev_s = jnp.concatenate([st_s['shift'][:, None, :], rw_s3[:, :-1]], axis=1).reshape(n_s, RWKV_PROJ)
    seq_per_blk = rows // tpad
    y_rwkv_s, srw_s = _rwkv(_pad_time(rw[n_p:], bs, ts, tpad), _pad_time(prev_s, bs, ts, tpad),
                            jnp.tile(jnp.broadcast_to(valid_s, (tpad, RWKV_W)), (seq_per_blk, 1)),
                            mix, vec, lora, _chunk_tri(rows, tpad), _to_block_diag(st_s['rwkv']),
                            n_groups=bs // seq_per_blk, n_steps=1, rows=rows, length=tpad, chain=False)
    y_rwkv = merge(y_rwkv, y_rwkv_s)

    lvec = _rows8(p['conv_w'][0], p['conv_w'][1], p['conv_w'][2], p['conv_w'][3],
                  p['conv_b'], p['ba'], p['bx'], p['lam'])
    wa_bd = _block_diag_weight(p['wa']).astype(bf16)
    wx_bd = _block_diag_weight(p['wx']).astype(bf16)
    l_l = RET_CHUNK if tp % RET_CHUNK == 0 else tp
    y_lru, h_p, tail_p = _lru(lru, jnp.ones((l_l, LRU_W), f32), lvec, wa_bd, wx_bd,
                              jnp.zeros((bp, SUBLANES, LRU_W), f32), jnp.zeros((bp, SUBLANES, LRU_W), f32),
                              n_batch=bp, n_chunks=tp // l_l, nb=1, length=l_l, out_rows=n_p)
    conv0_s = jnp.pad(st_s['conv'], ((0, 0), (SUBLANES - (CONV_WIDTH - 1), 0), (0, 0)))
    h0_s = jnp.broadcast_to(st_s['lru'][:, None, :], (bs, SUBLANES, LRU_W))
    y_lru_s, h_s, _ = _lru(_pad_time(lru[n_p:], bs, ts, tpad), jnp.broadcast_to(valid_s, (tpad, LRU_W)),
                           lvec, wa_bd, wx_bd, conv0_s, h0_s,
                           n_batch=bs, n_chunks=1, nb=SUBLANES, length=tpad, out_rows=bs * tpad)
    y_lru = merge(y_lru, y_lru_s)

    w_router = jnp.pad(p['w_router'], ((0, 0), (0, LANES - N_EXPERTS)))
    b_router = jnp.pad(p['b_router'], (0, LANES - N_EXPERTS), constant_values=-1e30).reshape(1, LANES)
    x1, lpos, gates, meta, cnt = _out_proj_router(y_ret, y_rwkv, y_lru, h, p['w_out'].astype(bf16),
                                                  _rows8(p['ln1_g'], p['ln1_b']), w_router, b_router, alpha)

    counts = cnt[0, :N_EXPERTS].astype(i32)
    pstart, blk_e, n_used, n_rows = _block_tables(counts, n_p + n_s)
    xs = _dispatch(pstart, counts, n_used, meta, lpos, x1, n_rows)
    yb = _experts(blk_e, n_used, xs, moe['w_gu'], moe['b_gu'], moe['w_down'], moe['b_down'], layer)
    x2 = _combine(pstart, meta, lpos, gates, x1, _rows8(p['ln2_g'], p['ln2_b']), yb, alpha, n_p // TOKEN_TILE)

    keep = CONV_WIDTH - 1
    assert tp >= SUBLANES and ts >= keep
    conv_s = lru[n_p:].reshape(bs, ts, 2 * LRU_W)[:, ts - keep:, LRU_W:]
    new_p = (_from_block_diag(sret_p), _from_block_diag(srw_p), shift_p[:, 0], h_p[:, 0],
             tail_p[:, SUBLANES - keep:])
    new_s = (_from_block_diag(sret_s), _from_block_diag(srw_s), rw_s3[:, -1], h_s[:, 0], conv_s)
    return x2, new_p, new_s


def kernel(x_prompt, x_sample, state_ret, state_rwkv, state_rwkv_shift, state_lru, state_conv,
           w_in, w_out, ln1_g, ln1_b, ln2_g, ln2_b,
           rwkv_mix, rwkv_w0, rwkv_w_up, rwkv_a0, rwkv_a_up, rwkv_g_up, rwkv_k_k, rwkv_k_a, rwkv_r_k,
           rwkv_gn_g, rwkv_gn_b, lru_conv_w, lru_conv_b, lru_wa, lru_ba, lru_wx, lru_bx, lru_lambda,
           moe_w_router, moe_b_router, moe_w_gate_up, moe_b_gate_up, moe_w_down, moe_b_down):
    bp, tp, _ = x_prompt.shape
    bs, ts, _ = x_sample.shape
    depth = w_in.shape[0]
    alpha = (2.0 * depth) ** 0.25
    moe = {'w_gu': moe_w_gate_up, 'b_gu': moe_b_gate_up, 'w_down': moe_w_down, 'b_down': moe_b_down}
    h = (x_prompt.reshape(bp * tp, D_MODEL), x_sample.reshape(bs * ts, D_MODEL))
    new_p, new_s = [], []
    for l in range(depth):
        p = {'w_in': w_in[l], 'w_out': w_out[l], 'ln1_g': ln1_g[l], 'ln1_b': ln1_b[l],
             'ln2_g': ln2_g[l], 'ln2_b': ln2_b[l], 'mix': rwkv_mix[l], 'w0': rwkv_w0[l],
             'w_up': rwkv_w_up[l], 'a0': rwkv_a0[l], 'a_up': rwkv_a_up[l], 'g_up': rwkv_g_up[l],
             'k_k': rwkv_k_k[l], 'k_a': rwkv_k_a[l], 'r_k': rwkv_r_k[l], 'gn_g': rwkv_gn_g[l],
             'gn_b': rwkv_gn_b[l], 'conv_w': lru_conv_w[l], 'conv_b': lru_conv_b[l], 'wa': lru_wa[l],
             'ba': lru_ba[l], 'wx': lru_wx[l], 'bx': lru_bx[l], 'lam': lru_lambda[l],
             'w_router': moe_w_router[l], 'b_router': moe_b_router[l]}
        st_s = {'ret': state_ret[l], 'rwkv': state_rwkv[l], 'shift': state_rwkv_shift[l],
                'lru': state_lru[l], 'conv': state_conv[l]}
        h, sp, ss = _layer(h, p, moe, l, alpha, st_s, bp, tp, bs, ts)
        new_p.append(sp)
        new_s.append(ss)
    outs = [h[0].reshape(bp, tp, D_MODEL), h[1].reshape(bs, ts, D_MODEL)]
    for i in range(5):
        outs.append(jnp.stack([s[i] for s in new_p]))
        outs.append(jnp.stack([s[i] for s in new_s]))
    return tuple(outs)
```

```python
import functools

import jax
import jax.numpy as jnp
from jax import lax
from jax.experimental import pallas as pl
from jax.experimental.pallas import tpu as pltpu

f32 = jnp.float32
bf16 = jnp.bfloat16
i32 = jnp.int32

D_MODEL = 1024
HEAD_DIM = 64
RET_W = 256
RET_H = 4
RET_CHUNK = 128
ROPE_BASE = 10000.0
RWKV_W = 512
RWKV_H = 8
RWKV_PROJ = 1664
RWKV_LORA_COL = 1536
RWKV_GN_EPS = 64e-5
RWKV_CHUNK = 64
RWKV_ROWS = 128
LRU_W = 256
LRU_C = 8.0
CONV_WIDTH = 4
D_PROJ = 3200
N_EXPERTS = 32
TOP_K = 4
D_EXPERT = 1024
SWIGLU_LIMIT = 7.0
SWIGLU_ALPHA = 1.702
LN_EPS = 1e-5
PAST_LEN = 16384.0

LANES = 128
SUBLANES = 8
SAMPLE_T_PAD = 8
TOKEN_TILE = 512
MOE_BLOCK = 512
LOCAL_ROWS = TOP_K * TOKEN_TILE + N_EXPERTS * SUBLANES
RUN_PIECES = tuple(SUBLANES << j for j in reversed(range((TOKEN_TILE // SUBLANES).bit_length())))
WAIT_PIECES = tuple(SUBLANES << j for j in reversed(range((LOCAL_ROWS // SUBLANES).bit_length())))
VMEM_LIMIT = 56 * 1024 * 1024

_NT = (((1,), (1,)), ((), ()))
_TN = (((0,), (0,)), ((), ()))


def _params(*sem):
    return pltpu.CompilerParams(dimension_semantics=sem, vmem_limit_bytes=VMEM_LIMIT)


def _mm(a, b):
    return jnp.dot(a.astype(bf16), b.astype(bf16), preferred_element_type=f32)


def _mm_nt(a, b):
    return lax.dot_general(a.astype(bf16), b.astype(bf16), _NT, preferred_element_type=f32)


def _mm_tn(a, b):
    return lax.dot_general(a.astype(bf16), b.astype(bf16), _TN, preferred_element_type=f32)


def _mm3(a, b):
    a_hi, b_hi = a.astype(bf16), b.astype(bf16)
    a_lo = (a - a_hi.astype(f32)).astype(bf16)
    b_lo = (b - b_hi.astype(f32)).astype(bf16)
    dot = functools.partial(jnp.dot, preferred_element_type=f32)
    return dot(a_hi, b_hi) + dot(a_hi, b_lo) + dot(a_lo, b_hi)


def _softplus(x):
    return jnp.maximum(x, 0.0) + jnp.log(1.0 + jnp.exp(-jnp.abs(x)))


def _sigmoid(x):
    return 1.0 / (1.0 + jnp.exp(-x))


def _half_masks():
    lane = lax.broadcasted_iota(i32, (1, LANES), 1)
    m0 = (lane < HEAD_DIM).astype(f32)
    return m0, 1.0 - m0


def _seg_mean(x, m0, m1):
    s0 = jnp.sum(x * m0, axis=-1, keepdims=True)
    s1 = jnp.sum(x * m1, axis=-1, keepdims=True)
    return (m0 * s0 + m1 * s1) * (1.0 / HEAD_DIM)


def _seg_sum(x, m0, m1):
    s0 = jnp.sum(x * m0, axis=-1, keepdims=True)
    s1 = jnp.sum(x * m1, axis=-1, keepdims=True)
    return m0 * s0 + m1 * s1


def _block_diag_mask():
    r = lax.broadcasted_iota(i32, (LANES, LANES), 0) // HEAD_DIM
    c = lax.broadcasted_iota(i32, (LANES, LANES), 1) // HEAD_DIM
    return (r == c).astype(f32)


def _layer_norm_rows(z, g, b):
    mu = jnp.mean(z, axis=-1, keepdims=True)
    zc = z - mu
    var = jnp.mean(zc * zc, axis=-1, keepdims=True)
    return zc * lax.rsqrt(var + LN_EPS) * g + b


def _group_specs(width, n_prompt_tiles):
    return [pl.BlockSpec((TOKEN_TILE, width), lambda i, *_: (jnp.minimum(i, n_prompt_tiles - 1), 0)),
            pl.BlockSpec((TOKEN_TILE, width), lambda i, *_: (jnp.maximum(i - n_prompt_tiles, 0), 0))]


def _pick_group(p_ref, s_ref, n_prompt_tiles):
    return jnp.where(pl.program_id(0) >= n_prompt_tiles, s_ref[...], p_ref[...])


def _n_tiles(h_p, h_s):
    assert h_p.shape[0] % TOKEN_TILE == 0 and h_s.shape[0] % TOKEN_TILE == 0, (h_p.shape, h_s.shape)
    return h_p.shape[0] // TOKEN_TILE, h_s.shape[0] // TOKEN_TILE


def _store_group(p_ref, s_ref, value, n_prompt_tiles):
    @pl.when(pl.program_id(0) < n_prompt_tiles)
    def _():
        p_ref[...] = value

    @pl.when(pl.program_id(0) >= n_prompt_tiles)
    def _():
        s_ref[...] = value


def _in_proj_kernel(xp_ref, xs_ref, w_ref, retp_ref, rets_ref, rwp_ref, rws_ref, lrup_ref, lrus_ref, *,
                    n_prompt_tiles):
    xb = _pick_group(xp_ref, xs_ref, n_prompt_tiles).astype(bf16)
    c0, c1 = 4 * RET_W, 4 * RET_W + RWKV_PROJ
    dot = functools.partial(jnp.dot, preferred_element_type=f32)
    _store_group(retp_ref, rets_ref, dot(xb, w_ref[:, :c0]), n_prompt_tiles)
    _store_group(rwp_ref, rws_ref, dot(xb, w_ref[:, c0:c1]), n_prompt_tiles)
    _store_group(lrup_ref, lrus_ref, dot(xb, w_ref[:, c1:]), n_prompt_tiles)


def _in_proj(h_p, h_s, w_bf):
    npt, nst = _n_tiles(h_p, h_s)
    widths = (4 * RET_W, RWKV_PROJ, 2 * LRU_W)
    out = pl.pallas_call(
        functools.partial(_in_proj_kernel, n_prompt_tiles=npt),
        grid=(npt + nst,),
        in_specs=_group_specs(D_MODEL, npt) + [
                  pl.BlockSpec((D_MODEL, D_PROJ), lambda i: (0, 0))],
        out_specs=[s for w in widths for s in _group_specs(w, npt)],
        out_shape=[jax.ShapeDtypeStruct((rows, w), f32) for w in widths for rows in (h_p.shape[0], h_s.shape[0])],
        compiler_params=_params("arbitrary"),
        name="in_proj",
    )(h_p, h_s, w_bf)
    return out[0:2], out[2:4], out[4:6]


def _rope_tables(pos):
    half = HEAD_DIM // 2
    inv = ROPE_BASE ** (-jnp.arange(half, dtype=f32) / half)
    ang = pos[:, None] * inv[None, :]
    cos, sin = jnp.cos(ang), jnp.sin(ang)
    cos_f = jnp.tile(jnp.concatenate([cos, cos], axis=-1), (1, RET_H))
    sin_f = jnp.tile(jnp.concatenate([-sin, sin], axis=-1), (1, RET_H))
    return cos_f, sin_f


def _retention_tables(length, n_valid):
    lg = jnp.log1p(-jnp.exp2(-5.0 - jnp.arange(RET_H, dtype=f32)))
    idx = jnp.arange(length, dtype=f32)
    rel = idx[:, None] - idx[None, :]
    mask = jnp.where(rel[None] >= 0, jnp.exp(jnp.maximum(rel, 0.0)[None] * lg[:, None, None]), 0.0)
    q_dec = jnp.exp((idx[:, None] + 1.0) * lg[None, :])
    k_dec = jnp.where(idx[:, None] < n_valid, jnp.exp((n_valid - 1.0 - idx)[:, None] * lg[None, :]), 0.0)
    c_dec = jnp.exp(n_valid * lg)[None, :]
    rep = lambda t: jnp.repeat(t, HEAD_DIM, axis=-1)
    return mask, rep(q_dec), rep(k_dec), rep(c_dec)


def _retention_kernel(ret_ref, cos_ref, sin_ref, mask_ref, qdec_ref, kdec_ref, cdec_ref, s0_ref,
                      y_ref, s_ref, pair_ref, *, nb):
    @pl.when(pl.program_id(1) == 0)
    def _():
        _load_head_pairs(pair_ref, s0_ref)

    m0, m1 = _half_masks()
    lane = lax.broadcasted_iota(i32, (1, LANES), 1)
    first_half = (lane % HEAD_DIM) < (HEAD_DIM // 2)
    bd = _block_diag_mask()

    def rope(x, cs, sn):
        swapped = jnp.where(first_half, pltpu.roll(x, LANES - HEAD_DIM // 2, 1), pltpu.roll(x, HEAD_DIM // 2, 1))
        return x * cs + swapped * sn

    for j in range(nb):
        for p in range(RET_H // 2):
            cols = pl.ds(p * LANES, LANES)
            cs, sn = cos_ref[:, cols], sin_ref[:, cols]
            q2 = rope(ret_ref[j, :, pl.ds(p * LANES, LANES)], cs, sn)
            k2 = rope(ret_ref[j, :, pl.ds(RET_W + p * LANES, LANES)], cs, sn) * (HEAD_DIM ** -0.5)
            v2 = ret_ref[j, :, pl.ds(2 * RET_W + p * LANES, LANES)]
            g2 = ret_ref[j, :, pl.ds(3 * RET_W + p * LANES, LANES)]
            state = pair_ref[j, p]
            o2 = _mm(q2, state) * qdec_ref[:, cols]
            for hh, m in enumerate((m0, m1)):
                sc = _mm_nt(q2 * m, k2) * mask_ref[2 * p + hh]
                o2 = o2 + _mm(sc, v2) * m
            pair_ref[j, p] = state * cdec_ref[:, cols] + _mm_tn(k2 * kdec_ref[:, cols], v2) * bd
            mu = _seg_mean(o2, m0, m1)
            oc = o2 - mu
            var = _seg_mean(oc * oc, m0, m1)
            y_ref[j, :, cols] = g2 * _sigmoid(g2) * oc * lax.rsqrt(var + LN_EPS)

    @pl.when(pl.program_id(1) == pl.num_programs(1) - 1)
    def _():
        _store_head_pairs(s_ref, pair_ref)


def _load_head_pairs(pair_ref, s_ref):
    pair_ref[...] = jnp.zeros(pair_ref.shape, f32)
    for j in range(pair_ref.shape[0]):
        for p in range(pair_ref.shape[1]):
            pair_ref[j, p, pl.ds(0, HEAD_DIM), pl.ds(0, HEAD_DIM)] = s_ref[j, 2 * p]
            pair_ref[j, p, pl.ds(HEAD_DIM, HEAD_DIM), pl.ds(HEAD_DIM, HEAD_DIM)] = s_ref[j, 2 * p + 1]


def _store_head_pairs(s_ref, pair_ref):
    for j in range(pair_ref.shape[0]):
        for p in range(pair_ref.shape[1]):
            s_ref[j, 2 * p] = pair_ref[j, p, pl.ds(0, HEAD_DIM), pl.ds(0, HEAD_DIM)]
            s_ref[j, 2 * p + 1] = pair_ref[j, p, pl.ds(HEAD_DIM, HEAD_DIM), pl.ds(HEAD_DIM, HEAD_DIM)]


def _retention(ret, cos_f, sin_f, tables, s0, *, nb, length):
    mask, q_dec, k_dec, c_dec = tables
    n_batch, t_len, _ = ret.shape
    blk = lambda g, c: (g, c, 0)
    const2 = lambda g, c: (0, 0)
    st = lambda g, c: (g, 0, 0, 0)
    return pl.pallas_call(
        functools.partial(_retention_kernel, nb=nb),
        grid=(n_batch // nb, t_len // length),
        in_specs=[pl.BlockSpec((nb, length, 4 * RET_W), blk),
                  pl.BlockSpec((length, RET_W), lambda g, c: (c, 0)),
                  pl.BlockSpec((length, RET_W), lambda g, c: (c, 0)),
                  pl.BlockSpec((RET_H, length, length), lambda g, c: (0, 0, 0)),
                  pl.BlockSpec((length, RET_W), const2),
                  pl.BlockSpec((length, RET_W), const2),
                  pl.BlockSpec((1, RET_W), const2),
                  pl.BlockSpec((nb, RET_H, HEAD_DIM, HEAD_DIM), st)],
        out_specs=[pl.BlockSpec((nb, length, RET_W), blk),
                   pl.BlockSpec((nb, RET_H, HEAD_DIM, HEAD_DIM), st)],
        out_shape=[jax.ShapeDtypeStruct((n_batch, t_len, RET_W), f32),
                   jax.ShapeDtypeStruct((n_batch, RET_H, HEAD_DIM, HEAD_DIM), f32)],
        scratch_shapes=[pltpu.VMEM((nb, RET_H // 2, LANES, LANES), f32)],
        compiler_params=_params("arbitrary", "arbitrary"),
        name="retention",
    )(ret, cos_f, sin_f, mask, q_dec, k_dec, c_dec, s0)


def _rwkv_kernel(rw_ref, aux_ref, valid_ref, mix_ref, vec_ref, lora_ref, tri_ref, s0_ref,
                 y_ref, s_ref, *rest, rows, length, chain):
    n_chunks = rows // length
    rw = rw_ref[...]
    if chain:
        last_ref, xs_ref, pair_ref = rest

        @pl.when(pl.program_id(1) == 0)
        def _():
            _load_head_pairs(pair_ref, s0_ref)
            xs_ref[pl.ds(0, SUBLANES), :] = jnp.broadcast_to(aux_ref[0], (SUBLANES, RWKV_PROJ))

        xs_ref[pl.ds(SUBLANES, rows), :] = rw
        prev = xs_ref[pl.ds(SUBLANES - 1, rows), :]
        xs_ref[pl.ds(0, SUBLANES), :] = rw[rows - SUBLANES:, :]
        last_ref[0] = rw[rows - 1:rows, :]
    else:
        pair_ref, = rest
        _load_head_pairs(pair_ref, s0_ref)
        prev = aux_ref[...]
    rwm = rw + (prev - rw) * mix_ref[...]

    valid = valid_ref[...]
    w0, a0, k_k, k_a, r_k, gn_g, gn_b = (vec_ref[pl.ds(i, 1), :] for i in range(7))
    lo = rwm[:, RWKV_LORA_COL:]
    lw = _mm3(jnp.tanh(lo), lora_ref[0])
    la = _mm3(lo, lora_ref[1])
    gate = _mm3(_sigmoid(lo), lora_ref[2])
    logw = -jnp.exp(-_softplus(-(w0 + lw)) - 0.5) * valid
    a = _sigmoid(a0 + la)
    r = rwm[:, :RWKV_W]
    kr = rwm[:, RWKV_W:2 * RWKV_W]
    vr = rwm[:, 2 * RWKV_W:3 * RWKV_W]
    kk_raw = kr * k_k
    kp = kr * (1.0 + (a - 1.0) * k_a) * valid
    cum = _mm3(tri_ref[...], logw)
    g_incl = jnp.exp(cum)
    g_inv = jnp.exp(-cum)
    g_prev = jnp.exp(cum - logw)
    g_end = jnp.concatenate(
        [jnp.broadcast_to(g_incl[(c + 1) * length - 1:(c + 1) * length, :], (length, RWKV_W)) for c in range(n_chunks)],
        axis=0)

    m0, m1 = _half_masks()
    bd = _block_diag_mask()
    stacked = 2 * rows
    ri = lax.broadcasted_iota(i32, (stacked, stacked), 0)
    ci = lax.broadcasted_iota(i32, (stacked, stacked), 1)
    same = (ri // length) == (ci // length)
    strict = jnp.logical_and(same, ci < ri)
    incl = jnp.logical_and(same, ci <= ri)
    eye = (ci == ri).astype(f32)
    stack = lambda t: jnp.concatenate([t * m0, t * m1], axis=0)
    unstack = lambda t: t[:rows] + t[rows:]
    pairs = range(RWKV_H // 2)

    pre = []
    for p in pairs:
        sl = slice(p * LANES, (p + 1) * LANES)
        kk2 = kk_raw[:, sl]
        kk2 = kk2 * lax.rsqrt(jnp.maximum(_seg_sum(kk2 * kk2, m0, m1), 1e-24)) * valid[:, sl]
        d = dict(sl=sl, r2=r[:, sl], v2=vr[:, sl], kp2=kp[:, sl], ge=g_end[:, sl])
        d['kh'] = kk2 * g_prev[:, sl]
        rh = d['r2'] * g_incl[:, sl]
        bt = kk2 * a[:, sl] * g_inv[:, sl]
        kt = d['kp2'] * g_inv[:, sl]
        d['b_end'] = bt * d['ge']
        d['k_end'] = kt * d['ge']
        d['kh_s'], d['rh_s'], d['v_s'] = stack(d['kh']), stack(rh), stack(d['v2'])
        gram = _mm_nt(jnp.concatenate([d['kh_s'], d['rh_s']], axis=0), jnp.concatenate([stack(bt), stack(kt)], axis=0))
        d['x'] = -jnp.where(strict, gram[:stacked, :stacked], 0.0)
        d['a_k'] = jnp.where(strict, gram[:stacked, stacked:], 0.0)
        d['r_b'] = jnp.where(incl, gram[stacked:, :stacked], 0.0)
        d['r_k'] = jnp.where(incl, gram[stacked:, stacked:], 0.0)
        d['t'] = eye + d['x']
        pre.append(d)
    cover = 2
    while cover < length:
        for d in pre:
            d['x'] = _mm(d['x'], d['x'])
            d['t'] = d['t'] + _mm(d['t'], d['x'])
        cover *= 2
    for d in pre:
        rhs0 = -_mm(d['a_k'], d['v_s'])
        tz = _mm(d['t'], jnp.concatenate([rhs0, d['kh_s']], axis=1))
        z0_s = tz[:, :LANES]
        rbz = _mm(d['r_b'], tz)
        d['z0'] = unstack(z0_s)
        d['q'] = unstack(d['rh_s'] - rbz[:, LANES:])
        d['y0'] = unstack(rbz[:, :LANES] + _mm(d['r_k'], d['v_s']))
        d['w'] = unstack(_mm_tn(d['t'], stack(d['b_end'])))

    for p, d in zip(pairs, pre):
        sl, ge, kh, v2 = d['sl'], d['ge'], d['kh'], d['v2']
        state = pair_ref[0, p] if chain else None
        ys = []
        for c in range(n_chunks):
            cr = slice(c * length, (c + 1) * length)
            if not chain:
                state = pair_ref[c, p]
            ys.append(_mm_nt(d['q'][cr], state) + d['y0'][cr])
            n_c = (_mm_tn(d['z0'][cr], d['b_end'][cr]) + _mm_tn(v2[cr], d['k_end'][cr])) * bd
            kw = _mm_tn(kh[cr], d['w'][cr]) * bd
            state = state * ge[(c + 1) * length - 1:(c + 1) * length, :] - _mm(state, kw) + n_c
            if not chain:
                pair_ref[c, p] = state
        if chain:
            pair_ref[0, p] = state
        y2 = jnp.concatenate(ys, axis=0) if n_chunks > 1 else ys[0]

        mu = _seg_mean(y2, m0, m1)
        yc = y2 - mu
        var = _seg_mean(yc * yc, m0, m1)
        yn = yc * lax.rsqrt(var + RWKV_GN_EPS) * gn_g[:, sl] + gn_b[:, sl]
        bonus = _seg_sum(d['r2'] * d['kp2'] * r_k[:, sl], m0, m1) * v2
        y_ref[:, pl.ds(p * LANES, LANES)] = (yn + bonus) * gate[:, sl]

    if chain:
        @pl.when(pl.program_id(1) == pl.num_programs(1) - 1)
        def _():
            _store_head_pairs(s_ref, pair_ref)
    else:
        _store_head_pairs(s_ref, pair_ref)


def _rwkv(rw, aux, valid, mix, vec, lora, tri, s0, *, n_groups, n_steps, rows, length, chain):
    blk = lambda g, c: (g * n_steps + c, 0)
    const2 = lambda g, c: (0, 0)
    st = lambda g, c: (g, 0, 0, 0)
    n_state = 1 if chain else rows // length
    aux_spec = (pl.BlockSpec((1, 1, RWKV_PROJ), lambda g, c: (g, 0, 0)) if chain
                else pl.BlockSpec((rows, RWKV_PROJ), blk))
    return pl.pallas_call(
        functools.partial(_rwkv_kernel, rows=rows, length=length, chain=chain),
        grid=(n_groups, n_steps),
        in_specs=[pl.BlockSpec((rows, RWKV_PROJ), blk),
                  aux_spec,
                  pl.BlockSpec((rows, RWKV_W), const2),
                  pl.BlockSpec((1, RWKV_PROJ), const2),
                  pl.BlockSpec((SUBLANES, RWKV_W), const2),
                  pl.BlockSpec((3, LANES, RWKV_W), lambda g, c: (0, 0, 0)),
                  pl.BlockSpec((rows, rows), const2),
                  pl.BlockSpec((n_state, RWKV_H, HEAD_DIM, HEAD_DIM), st)],
        out_specs=[pl.BlockSpec((rows, RWKV_W), blk),
                   pl.BlockSpec((n_state, RWKV_H, HEAD_DIM, HEAD_DIM), st)]
        + ([pl.BlockSpec((1, 1, RWKV_PROJ), lambda g, c: (g, 0, 0))] if chain else []),
        out_shape=[jax.ShapeDtypeStruct((n_groups * n_steps * rows, RWKV_W), f32),
                   jax.ShapeDtypeStruct((n_groups * n_state, RWKV_H, HEAD_DIM, HEAD_DIM), f32)]
        + ([jax.ShapeDtypeStruct((n_groups, 1, RWKV_PROJ), f32)] if chain else []),
        scratch_shapes=([pltpu.VMEM((rows + SUBLANES, RWKV_PROJ), f32)] if chain else [])
        + [pltpu.VMEM((n_state, RWKV_H // 2, LANES, LANES), f32)],
        compiler_params=_params("arbitrary", "arbitrary"),
        name="rwkv7",
    )(rw, aux, valid, mix, vec, lora, tri, s0)


def _chunk_tri(rows, length):
    idx = jnp.arange(rows)
    same = (idx[:, None] // length) == (idx[None, :] // length)
    return jnp.logical_and(same, idx[None, :] <= idx[:, None]).astype(f32)


def _lru_kernel(lru_ref, valid_ref, vec_ref, wa_ref, wx_ref, conv0_ref, h0_ref,
                y_ref, h_ref, tail_ref, xext_ref, a_ref, b_ref, hs_ref, *, nb, length):
    @pl.when(pl.program_id(1) == 0)
    def _():
        h_ref[...] = h0_ref[...]
        xext_ref[:, pl.ds(0, SUBLANES), :] = conv0_ref[...]

    valid = valid_ref[...] > 0.5
    cw = [vec_ref[pl.ds(i, 1), :] for i in range(CONV_WIDTH)]
    cb, ba, bx, lam = (vec_ref[pl.ds(i, 1), :] for i in range(CONV_WIDTH, CONV_WIDTH + 4))
    sp = _softplus(-lam)
    for j in range(nb):
        rows = pl.ds(j * length, length)
        gbr = lru_ref[rows, pl.ds(0, LRU_W)]
        x = lru_ref[rows, pl.ds(LRU_W, LRU_W)]
        xext_ref[j, pl.ds(SUBLANES, length), :] = x
        xc = cb + x * cw[CONV_WIDTH - 1]
        for t in range(CONV_WIDTH - 1):
            xc = xc + xext_ref[j, pl.ds(SUBLANES - (CONV_WIDTH - 1) + t, length), :] * cw[t]
        xext_ref[j, pl.ds(0, SUBLANES), :] = x[length - SUBLANES:, :]
        tail_ref[j] = x[length - SUBLANES:, :]
        gate_a = _sigmoid(_mm(xc, wa_ref[...]) + ba)
        gate_x = _sigmoid(_mm(xc, wx_ref[...]) + bx)
        log_a = -LRU_C * gate_a * sp
        a = jnp.exp(log_a)
        b = xc * gate_x * jnp.sqrt(1.0 - jnp.exp(2.0 * log_a))
        a_ref[...] = jnp.where(valid, a, 1.0)
        b_ref[...] = jnp.where(valid, b, 0.0)

        def step(t, h):
            h = a_ref[pl.ds(t, 1), :] * h + b_ref[pl.ds(t, 1), :]
            hs_ref[pl.ds(t, 1), :] = h
            return h

        h = lax.fori_loop(0, length, step, h_ref[j, pl.ds(0, 1), :])
        h_ref[j] = jnp.broadcast_to(h, (SUBLANES, LRU_W))
        c = 0.7978845608028654
        gelu = 0.5 * gbr * (1.0 + jnp.tanh(c * (gbr + 0.044715 * gbr * gbr * gbr)))
        y_ref[rows, :] = hs_ref[...] * gelu


def _lru(lru, valid, vec, wa_bd, wx_bd, conv0, h0, *, n_batch, n_chunks, nb, length, out_rows):
    blk = lambda g, c: (g * n_chunks + c, 0)
    const2 = lambda g, c: (0, 0)
    st = lambda g, c: (g, 0, 0)
    return pl.pallas_call(
        functools.partial(_lru_kernel, nb=nb, length=length),
        grid=(n_batch // nb, n_chunks),
        in_specs=[pl.BlockSpec((nb * length, 2 * LRU_W), blk),
                  pl.BlockSpec((length, LRU_W), const2),
                  pl.BlockSpec((SUBLANES, LRU_W), const2),
                  pl.BlockSpec((LRU_W, LRU_W), const2),
                  pl.BlockSpec((LRU_W, LRU_W), const2),
                  pl.BlockSpec((nb, SUBLANES, LRU_W), st),
                  pl.BlockSpec((nb, SUBLANES, LRU_W), st)],
        out_specs=[pl.BlockSpec((nb * length, LRU_W), blk),
                   pl.BlockSpec((nb, SUBLANES, LRU_W), st),
                   pl.BlockSpec((nb, SUBLANES, LRU_W), st)],
        out_shape=[jax.ShapeDtypeStruct((out_rows, LRU_W), f32),
                   jax.ShapeDtypeStruct((n_batch, SUBLANES, LRU_W), f32),
                   jax.ShapeDtypeStruct((n_batch, SUBLANES, LRU_W), f32)],
        scratch_shapes=[pltpu.VMEM((nb, length + SUBLANES, LRU_W), f32),
                        pltpu.VMEM((length, LRU_W), f32),
                        pltpu.VMEM((length, LRU_W), f32),
                        pltpu.VMEM((length, LRU_W), f32)],
        compiler_params=_params("arbitrary", "arbitrary"),
        name="rg_lru",
    )(lru, valid, vec, wa_bd, wx_bd, conv0, h0)


def _out_proj_kernel(yrp_ref, yrs_ref, ywp_ref, yws_ref, ylp_ref, yls_ref, xp_ref, xs_ref,
                     w_ref, ln_ref, wr_ref, br_ref,
                     x1_ref, lpos_ref, g_ref, meta_ref, cnt_ref, *, alpha, n_prompt_tiles):
    @pl.when(pl.program_id(0) == 0)
    def _():
        cnt_ref[...] = jnp.zeros(cnt_ref.shape, f32)

    pick = functools.partial(_pick_group, n_prompt_tiles=n_prompt_tiles)
    mixed = (_mm(pick(yrp_ref, yrs_ref), w_ref[pl.ds(0, RET_W), :])
             + _mm(pick(ywp_ref, yws_ref), w_ref[pl.ds(RET_W, RWKV_W), :])
             + _mm(pick(ylp_ref, yls_ref), w_ref[pl.ds(RET_W + RWKV_W, LRU_W), :]))
    x1 = _layer_norm_rows(alpha * pick(xp_ref, xs_ref) + mixed, ln_ref[pl.ds(0, 1), :], ln_ref[pl.ds(1, 1), :])
    x1_ref[...] = x1
    logits = _mm3(x1, wr_ref[...]) + br_ref[...]
    tm = logits.shape[0]
    lane = lax.broadcasted_iota(i32, logits.shape, 1).astype(f32)
    top_v = jnp.zeros(logits.shape, f32)
    work = logits
    v_max = None
    onehots = []
    for k in range(TOP_K):
        v = jnp.max(work, axis=-1, keepdims=True)
        idx = jnp.min(jnp.where(work == v, lane, float(LANES)), axis=-1, keepdims=True)
        if k == 0:
            v_max = v
        hit = lane == idx
        onehots.append(hit.astype(f32))
        top_v = jnp.where(lane == k, jnp.exp(v - v_max), top_v)
        work = jnp.where(hit, -jnp.inf, work)
    g_ref[...] = top_v / jnp.sum(top_v, axis=-1, keepdims=True)

    total = onehots[0] + onehots[1] + onehots[2] + onehots[3]
    ri = lax.broadcasted_iota(i32, (tm, tm), 0)
    ci = lax.broadcasted_iota(i32, (tm, tm), 1)
    before = _mm((ci < ri).astype(f32), total)
    groups = jnp.floor((jnp.sum(total, axis=0, keepdims=True) + (SUBLANES - 1.0)) * (1.0 / SUBLANES))
    er = lax.broadcasted_iota(i32, (LANES, LANES), 0)
    ec = lax.broadcasted_iota(i32, (LANES, LANES), 1)
    run_len = groups * SUBLANES
    run_start = _mm(jnp.broadcast_to(groups, (SUBLANES, LANES)), (er < ec).astype(f32))[0:1] * SUBLANES
    lpos = jnp.full(logits.shape, -1.0, f32)
    for k in range(TOP_K):
        pos = jnp.sum(onehots[k] * (before + run_start), axis=-1, keepdims=True)
        lpos = jnp.where(lane == k, pos, lpos)
    lpos_ref[...] = lpos
    row = lax.broadcasted_iota(i32, (SUBLANES, LANES), 0)
    meta = jnp.where(row == 0, run_start, jnp.where(row == 1, run_len, jnp.where(row == 2, cnt_ref[...], 0.0)))
    meta_ref[...] = meta.astype(i32)
    cnt_ref[...] = cnt_ref[...] + run_len


def _out_proj_router(y_ret, y_rwkv, y_lru, h, w_out_bf, ln, w_router, b_router, alpha):
    npt, nst = _n_tiles(*h)
    n = h[0].shape[0] + h[1].shape[0]
    tm = TOKEN_TILE
    row = lambda i: (i, 0)
    const = lambda i: (0, 0)
    return pl.pallas_call(
        functools.partial(_out_proj_kernel, alpha=alpha, n_prompt_tiles=npt),
        grid=(npt + nst,),
        in_specs=_group_specs(RET_W, npt) + _group_specs(RWKV_W, npt) + _group_specs(LRU_W, npt)
        + _group_specs(D_MODEL, npt) + [
                  pl.BlockSpec((D_MODEL, D_MODEL), const),
                  pl.BlockSpec((SUBLANES, D_MODEL), const),
                  pl.BlockSpec((D_MODEL, LANES), const),
                  pl.BlockSpec((1, LANES), const)],
        out_specs=[pl.BlockSpec((tm, D_MODEL), row), pl.BlockSpec((tm, LANES), row), pl.BlockSpec((tm, LANES), row),
                   pl.BlockSpec((SUBLANES, LANES), row), pl.BlockSpec((SUBLANES, LANES), const)],
        out_shape=[jax.ShapeDtypeStruct((n, D_MODEL), f32),
                   jax.ShapeDtypeStruct((n, LANES), f32),
                   jax.ShapeDtypeStruct((n, LANES), f32),
                   jax.ShapeDtypeStruct((n // tm * SUBLANES, LANES), i32),
                   jax.ShapeDtypeStruct((SUBLANES, LANES), f32)],
        compiler_params=_params("arbitrary"),
        name="out_proj_router",
    )(*y_ret, *y_rwkv, *y_lru, *h, w_out_bf, ln, w_router, b_router)


def _for_each_run_piece(meta_ref, pstart_ref, fn):
    for e in range(N_EXPERTS):
        start, n = meta_ref[0, e], meta_ref[1, e]
        base = pstart_ref[e] + meta_ref[2, e]
        for sz in RUN_PIECES:
            done = n & ~(2 * sz - 1)

            @pl.when((n & sz) != 0)
            def _():
                fn(pl.multiple_of(start + done, SUBLANES), pl.multiple_of(base + done, SUBLANES), sz)


def _wait_run_rows(meta_ref, make_copy):
    total = meta_ref[1, 0]
    for e in range(1, N_EXPERTS):
        total = total + meta_ref[1, e]
    for sz in WAIT_PIECES:
        @pl.when((total & sz) != 0)
        def _():
            make_copy(sz).wait()


def _dispatch_kernel(pstart_ref, cnt_ref, n_used_ref, meta_ref, meta_prev_ref, lpos_ref, x1_ref, xs_hbm,
                     xl_ref, zero_ref, sems, zsem):
    tm = TOKEN_TILE
    bm = MOE_BLOCK
    i = pl.program_id(0)
    slot = i % 2

    @pl.when(i == 0)
    def _():
        zero_ref[...] = jnp.zeros(zero_ref.shape, f32)
        tail = lambda j: pltpu.make_async_copy(zero_ref, xs_hbm.at[pl.ds(j * bm, bm), :], zsem)

        def tail_start(j, carry):
            tail(j).start()
            return carry

        def tail_wait(j, carry):
            tail(j).wait()
            return carry

        n_blocks = xs_hbm.shape[0] // bm
        lax.fori_loop(n_used_ref[0], n_blocks, tail_start, 0)
        lax.fori_loop(n_used_ref[0], n_blocks, tail_wait, 0)
        for e in range(N_EXPERTS):
            lo = pstart_ref[e] + cnt_ref[e]
            n_groups = ((cnt_ref[e] + bm - 1) // bm * bm - cnt_ref[e]) // SUBLANES
            fill = lambda g: pltpu.make_async_copy(
                zero_ref.at[pl.ds(0, SUBLANES), :],
                xs_hbm.at[pl.ds(pl.multiple_of(lo + g * SUBLANES, SUBLANES), SUBLANES), :], zsem)

            def start(g, carry):
                fill(g).start()
                return carry

            def wait(g, carry):
                fill(g).wait()
                return carry

            lax.fori_loop(0, n_groups, start, 0)
            lax.fori_loop(0, n_groups, wait, 0)

    lpos_t = lpos_ref[...].T
    srow = lax.broadcasted_iota(i32, (LOCAL_ROWS, tm), 0).astype(f32)
    perm = jnp.zeros((LOCAL_ROWS, tm), f32)
    for k in range(TOP_K):
        perm = perm + (srow == lpos_t[k:k + 1, :]).astype(f32)
    xl_ref[slot] = jnp.dot(perm.astype(bf16), x1_ref[...].astype(bf16), preferred_element_type=f32)

    def send(local_row, sorted_row, n_rows):
        pltpu.make_async_copy(xl_ref.at[slot, pl.ds(local_row, n_rows), :],
                              xs_hbm.at[pl.ds(sorted_row, n_rows), :], sems.at[slot]).start()

    _for_each_run_piece(meta_ref, pstart_ref, send)

    def sent(s):
        return lambda n_rows: pltpu.make_async_copy(xl_ref.at[s, pl.ds(0, n_rows), :],
                                                    xs_hbm.at[pl.ds(0, n_rows), :], sems.at[s])

    @pl.when(i > 0)
    def _():
        _wait_run_rows(meta_prev_ref, sent(1 - slot))

    @pl.when(i == pl.num_programs(0) - 1)
    def _():
        _wait_run_rows(meta_ref, sent(slot))


def _dispatch(pstart, counts, n_used, meta, lpos, x1, n_rows):
    n = x1.shape[0]
    tm = TOKEN_TILE
    smem_tile = lambda f: pl.BlockSpec((SUBLANES, LANES), f, memory_space=pltpu.SMEM)
    grid_spec = pltpu.PrefetchScalarGridSpec(
        num_scalar_prefetch=3,
        grid=(n // tm,),
        in_specs=[smem_tile(lambda i, ps, ct, nu: (i, 0)),
                  smem_tile(lambda i, ps, ct, nu: (jnp.maximum(i - 1, 0), 0)),
                  pl.BlockSpec((tm, LANES), lambda i, ps, ct, nu: (i, 0)),
                  pl.BlockSpec((tm, D_MODEL), lambda i, ps, ct, nu: (i, 0))],
        out_specs=pl.BlockSpec(memory_space=pl.ANY),
        scratch_shapes=[pltpu.VMEM((2, LOCAL_ROWS, D_MODEL), f32),
                        pltpu.VMEM((MOE_BLOCK, D_MODEL), f32),
                        pltpu.SemaphoreType.DMA((2,)), pltpu.SemaphoreType.DMA(())],
    )
    return pl.pallas_call(
        _dispatch_kernel,
        grid_spec=grid_spec,
        out_shape=jax.ShapeDtypeStruct((n_rows, D_MODEL), f32),
        compiler_params=_params("arbitrary"),
        name="moe_dispatch",
    )(pstart, counts, n_used, meta, meta, lpos, x1)


def _expert_kernel(blk_e_ref, n_used_ref, xs_ref, wgu_ref, bgu_ref, wdn_ref, bdn_ref,
                   out_ref, wgu_bf, wdn_bf):
    i = pl.program_id(0)
    prev = jnp.maximum(i - 1, 0)
    new_expert = jnp.logical_or(i == 0, blk_e_ref[i] != blk_e_ref[prev])
    used = i < n_used_ref[0]

    @pl.when(used)
    def _():
        @pl.when(new_expert)
        def _():
            wgu_bf[...] = wgu_ref[0, 0].astype(bf16)
            wdn_bf[...] = wdn_ref[0, 0].astype(bf16)

        gu = jnp.dot(xs_ref[...].astype(bf16), wgu_bf[...], preferred_element_type=f32) + bgu_ref[0, 0]
        g = jnp.minimum(gu[:, :D_EXPERT], SWIGLU_LIMIT)
        u = jnp.clip(gu[:, D_EXPERT:], -SWIGLU_LIMIT, SWIGLU_LIMIT)
        hdn = (u + 1.0) * g * _sigmoid(SWIGLU_ALPHA * g)
        out_ref[...] = jnp.dot(hdn.astype(bf16), wdn_bf[...], preferred_element_type=f32) + bdn_ref[0, 0]

    @pl.when(jnp.logical_not(used))
    def _():
        out_ref[...] = jnp.zeros(out_ref.shape, f32)


def _experts(blk_e, n_used, xs, w_gu, b_gu, w_down, b_down, layer):
    n_blocks = blk_e.shape[0]
    bm = MOE_BLOCK
    by_e = lambda i, be, nu: (layer, be[i], 0, 0)
    x_blk = lambda i, be, nu: (jnp.minimum(i, nu[0] - 1), 0)
    grid_spec = pltpu.PrefetchScalarGridSpec(
        num_scalar_prefetch=2,
        grid=(n_blocks,),
        in_specs=[pl.BlockSpec((bm, D_MODEL), x_blk),
                  pl.BlockSpec((1, 1, D_MODEL, 2 * D_EXPERT), by_e),
                  pl.BlockSpec((1, 1, 1, 2 * D_EXPERT), by_e),
                  pl.BlockSpec((1, 1, D_EXPERT, D_MODEL), by_e),
                  pl.BlockSpec((1, 1, 1, D_MODEL), by_e)],
        out_specs=pl.BlockSpec((bm, D_MODEL), lambda i, be, nu: (i, 0)),
        scratch_shapes=[pltpu.VMEM((D_MODEL, 2 * D_EXPERT), bf16),
                        pltpu.VMEM((D_EXPERT, D_MODEL), bf16)],
    )
    depth = w_gu.shape[0]
    return pl.pallas_call(
        _expert_kernel,
        grid_spec=grid_spec,
        out_shape=jax.ShapeDtypeStruct((n_blocks * bm, D_MODEL), f32),
        compiler_params=_params("arbitrary"),
        name="moe_experts",
    )(blk_e, n_used, xs, w_gu, b_gu.reshape(depth, N_EXPERTS, 1, 2 * D_EXPERT),
      w_down, b_down.reshape(depth, N_EXPERTS, 1, D_MODEL))


def _combine_kernel(pstart_ref, meta_ref, meta_next_ref, lpos_ref, gates_ref, x1_ref, ln_ref, yb_hbm,
                    outp_ref, outs_ref, yl_ref, sems, *, alpha, n_prompt_tiles):
    tm = TOKEN_TILE
    i = pl.program_id(0)
    slot = i % 2

    def fetch(meta, s):
        def recv(local_row, sorted_row, n_rows):
            pltpu.make_async_copy(yb_hbm.at[pl.ds(sorted_row, n_rows), :],
                                  yl_ref.at[s, pl.ds(local_row, n_rows), :], sems.at[s]).start()

        _for_each_run_piece(meta, pstart_ref, recv)

    @pl.when(i == 0)
    def _():
        yl_ref[...] = jnp.zeros(yl_ref.shape, f32)
        fetch(meta_ref, slot)

    @pl.when(i + 1 < pl.num_programs(0))
    def _():
        fetch(meta_next_ref, 1 - slot)

    _wait_run_rows(meta_ref, lambda n_rows: pltpu.make_async_copy(
        yb_hbm.at[pl.ds(0, n_rows), :], yl_ref.at[slot, pl.ds(0, n_rows), :], sems.at[slot]))

    lpos = lpos_ref[...]
    gates = gates_ref[...]
    scol = lax.broadcasted_iota(i32, (tm, LOCAL_ROWS), 1).astype(f32)
    weight = jnp.zeros((tm, LOCAL_ROWS), f32)
    for k in range(TOP_K):
        weight = weight + jnp.where(scol == lpos[:, k:k + 1], gates[:, k:k + 1], 0.0)
    y = jnp.dot(weight.astype(bf16), yl_ref[slot].astype(bf16), preferred_element_type=f32)
    x2 = _layer_norm_rows(alpha * x1_ref[...] + y, ln_ref[pl.ds(0, 1), :], ln_ref[pl.ds(1, 1), :])
    _store_group(outp_ref, outs_ref, x2, n_prompt_tiles)


def _combine(pstart, meta, lpos, gates, x1, ln, yb, alpha, n_prompt_tiles):
    n = x1.shape[0]
    tm = TOKEN_TILE
    n_tiles = n // tm
    row = lambda i, ps: (i, 0)
    smem_tile = lambda f: pl.BlockSpec((SUBLANES, LANES), f, memory_space=pltpu.SMEM)
    grid_spec = pltpu.PrefetchScalarGridSpec(
        num_scalar_prefetch=1,
        grid=(n_tiles,),
        in_specs=[smem_tile(row),
                  smem_tile(lambda i, ps: (jnp.minimum(i + 1, n_tiles - 1), 0)),
                  pl.BlockSpec((tm, LANES), row),
                  pl.BlockSpec((tm, LANES), row),
                  pl.BlockSpec((tm, D_MODEL), row),
                  pl.BlockSpec((SUBLANES, D_MODEL), lambda i, ps: (0, 0)),
                  pl.BlockSpec(memory_space=pl.ANY)],
        out_specs=_group_specs(D_MODEL, n_prompt_tiles),
        scratch_shapes=[pltpu.VMEM((2, LOCAL_ROWS, D_MODEL), f32), pltpu.SemaphoreType.DMA((2,))],
    )
    return pl.pallas_call(
        functools.partial(_combine_kernel, alpha=alpha, n_prompt_tiles=n_prompt_tiles),
        grid_spec=grid_spec,
        out_shape=[jax.ShapeDtypeStruct((n_prompt_tiles * tm, D_MODEL), f32),
                   jax.ShapeDtypeStruct((n - n_prompt_tiles * tm, D_MODEL), f32)],
        compiler_params=_params("arbitrary"),
        name="moe_combine",
    )(pstart, meta, meta, lpos, gates, x1, ln, yb)


def _block_tables(counts, n_tokens):
    bm = MOE_BLOCK
    padded = (counts + bm - 1) // bm * bm
    pad_end = jnp.cumsum(padded)
    pstart = (pad_end - padded).astype(i32)
    max_used = n_tokens * TOP_K + (n_tokens // TOKEN_TILE) * N_EXPERTS * (SUBLANES - 1)
    n_blocks = -(-(max_used + N_EXPERTS * (bm - 1)) // bm)
    first_row = jnp.arange(n_blocks, dtype=pad_end.dtype) * bm
    blk_e = jnp.minimum(jnp.sum(pad_end[None, :] <= first_row[:, None], axis=1), N_EXPERTS - 1).astype(i32)
    n_used = (pad_end[-1] // bm).astype(i32).reshape(1)
    return pstart, blk_e, n_used, n_blocks * bm


def _pad_time(t, n_batch, n_t, t_pad):
    w = t.shape[-1]
    return jnp.pad(t.reshape(n_batch, n_t, w), ((0, 0), (0, t_pad - n_t), (0, 0))).reshape(n_batch * t_pad, w)


def _block_diag_weight(w):
    h = w.shape[0]
    eye = jnp.eye(h, dtype=w.dtype)
    return (eye[:, None, :, None] * w[:, :, None, :]).reshape(h * HEAD_DIM, h * HEAD_DIM)


def _rows8(*rows):
    width = rows[0].shape[-1]
    m = jnp.stack([r.reshape(width) for r in rows])
    return jnp.pad(m, ((0, SUBLANES - m.shape[0]), (0, 0)))


def _layer(h, p, moe, layer, alpha, st_s, bp, tp, bs, ts):
    n_p, n_s = bp * tp, bs * ts
    tpad = SAMPLE_T_PAD
    (ret_p, ret_s), (rw_p, rw_s), (lru_p, lru_s) = _in_proj(*h, p['w_in'].astype(bf16))
    unpad = lambda y: y.reshape(bs, tpad, -1)[:, :ts].reshape(n_s, -1)

    c_p = RET_CHUNK if tp % RET_CHUNK == 0 else tp
    cos_p, sin_p = _rope_tables(jnp.arange(tp, dtype=f32))
    cos_s, sin_s = _rope_tables(PAST_LEN + jnp.arange(tpad, dtype=f32))
    y_ret_p, sret_p = _retention(ret_p.reshape(bp, tp, 4 * RET_W), cos_p, sin_p, _retention_tables(c_p, c_p),
                                 jnp.zeros((bp, RET_H, HEAD_DIM, HEAD_DIM), f32),
                                 nb=2 if bp % 2 == 0 else 1, length=c_p)
    y_ret_s, sret_s = _retention(_pad_time(ret_s, bs, ts, tpad).reshape(bs, tpad, 4 * RET_W), cos_s, sin_s,
                                 _retention_tables(tpad, ts), st_s['ret'], nb=SUBLANES, length=tpad)
    y_ret = (y_ret_p.reshape(n_p, RET_W), unpad(y_ret_s))

    lora = jnp.zeros((3, LANES, RWKV_W), f32)
    lora = lora.at[0, 0:32].set(p['w_up']).at[1, 32:64].set(p['a_up']).at[2, 64:128].set(p['g_up'])
    vec = _rows8(p['w0'], p['a0'], p['k_k'], p['k_a'], p['r_k'], p['gn_g'], p['gn_b'])
    mix = p['mix'].reshape(1, RWKV_PROJ)
    rows = RWKV_ROWS
    y_rwkv_p, srw_p, shift_p = _rwkv(rw_p, jnp.zeros((bp, 1, RWKV_PROJ), f32), jnp.ones((rows, RWKV_W), f32), mix,
                                     vec, lora, _chunk_tri(rows, RWKV_CHUNK),
                                     jnp.zeros((bp, RWKV_H, HEAD_DIM, HEAD_DIM), f32),
                                     n_groups=bp, n_steps=tp // rows, rows=rows, length=RWKV_CHUNK, chain=True)
    valid_s = (jnp.arange(tpad) < ts).astype(f32)[:, None]
    rw_s3 = rw_s.reshape(bs, ts, RWKV_PROJ)
    prev_s = jnp.concatenate([st_s['shift'][:, None, :], rw_s3[:, :-1]], axis=1).reshape(n_s, RWKV_PROJ)
    seq_per_blk = rows // tpad
    y_rwkv_s, srw_s = _rwkv(_pad_time(rw_s, bs, ts, tpad), _pad_time(prev_s, bs, ts, tpad),
                            jnp.tile(jnp.broadcast_to(valid_s, (tpad, RWKV_W)), (seq_per_blk, 1)),
                            mix, vec, lora, _chunk_tri(rows, tpad), st_s['rwkv'],
                            n_groups=bs // seq_per_blk, n_steps=1, rows=rows, length=tpad, chain=False)
    y_rwkv = (y_rwkv_p, unpad(y_rwkv_s))

    lvec = _rows8(p['conv_w'][0], p['conv_w'][1], p['conv_w'][2], p['conv_w'][3],
                  p['conv_b'], p['ba'], p['bx'], p['lam'])
    wa_bd = _block_diag_weight(p['wa']).astype(bf16)
    wx_bd = _block_diag_weight(p['wx']).astype(bf16)
    l_l = RET_CHUNK if tp % RET_CHUNK == 0 else tp
    y_lru_p, h_p, tail_p = _lru(lru_p, jnp.ones((l_l, LRU_W), f32), lvec, wa_bd, wx_bd,
                                jnp.zeros((bp, SUBLANES, LRU_W), f32), jnp.zeros((bp, SUBLANES, LRU_W), f32),
                                n_batch=bp, n_chunks=tp // l_l, nb=1, length=l_l, out_rows=n_p)
    conv0_s = jnp.pad(st_s['conv'], ((0, 0), (SUBLANES - (CONV_WIDTH - 1), 0), (0, 0)))
    h0_s = jnp.broadcast_to(st_s['lru'][:, None, :], (bs, SUBLANES, LRU_W))
    y_lru_s, h_s, _ = _lru(_pad_time(lru_s, bs, ts, tpad), jnp.broadcast_to(valid_s, (tpad, LRU_W)),
                           lvec, wa_bd, wx_bd, conv0_s, h0_s,
                           n_batch=bs, n_chunks=1, nb=SUBLANES, length=tpad, out_rows=bs * tpad)
    y_lru = (y_lru_p, unpad(y_lru_s))

    w_router = jnp.pad(p['w_router'], ((0, 0), (0, LANES - N_EXPERTS)))
    b_router = jnp.pad(p['b_router'], (0, LANES - N_EXPERTS), constant_values=-1e30).reshape(1, LANES)
    x1, lpos, gates, meta, cnt = _out_proj_router(y_ret, y_rwkv, y_lru, h, p['w_out'].astype(bf16),
                                                  _rows8(p['ln1_g'], p['ln1_b']), w_router, b_router, alpha)

    counts = cnt[0, :N_EXPERTS].astype(i32)
    pstart, blk_e, n_used, n_rows = _block_tables(counts, n_p + n_s)
    xs = _dispatch(pstart, counts, n_used, meta, lpos, x1, n_rows)
    yb = _experts(blk_e, n_used, xs, moe['w_gu'], moe['b_gu'], moe['w_down'], moe['b_down'], layer)
    x2 = _combine(pstart, meta, lpos, gates, x1, _rows8(p['ln2_g'], p['ln2_b']), yb, alpha, n_p // TOKEN_TILE)

    keep = CONV_WIDTH - 1
    assert tp >= SUBLANES and ts >= keep
    conv_s = lru_s.reshape(bs, ts, 2 * LRU_W)[:, ts - keep:, LRU_W:]
    new_p = (sret_p, srw_p, shift_p[:, 0], h_p[:, 0], tail_p[:, SUBLANES - keep:])
    new_s = (sret_s, srw_s, rw_s3[:, -1], h_s[:, 0], conv_s)
    return x2, new_p, new_s


def kernel(x_prompt, x_sample, state_ret, state_rwkv, state_rwkv_shift, state_lru, state_conv,
           w_in, w_out, ln1_g, ln1_b, ln2_g, ln2_b,
           rwkv_mix, rwkv_w0, rwkv_w_up, rwkv_a0, rwkv_a_up, rwkv_g_up, rwkv_k_k, rwkv_k_a, rwkv_r_k,
           rwkv_gn_g, rwkv_gn_b, lru_conv_w, lru_conv_b, lru_wa, lru_ba, lru_wx, lru_bx, lru_lambda,
           moe_w_router, moe_b_router, moe_w_gate_up, moe_b_gate_up, moe_w_down, moe_b_down):
    bp, tp, _ = x_prompt.shape
    bs, ts, _ = x_sample.shape
    depth = w_in.shape[0]
    alpha = (2.0 * depth) ** 0.25
    moe = {'w_gu': moe_w_gate_up, 'b_gu': moe_b_gate_up, 'w_down': moe_w_down, 'b_down': moe_b_down}
    h = (x_prompt.reshape(bp * tp, D_MODEL), x_sample.reshape(bs * ts, D_MODEL))
    new_p, new_s = [], []
    for l in range(depth):
        p = {'w_in': w_in[l], 'w_out': w_out[l], 'ln1_g': ln1_g[l], 'ln1_b': ln1_b[l],
             'ln2_g': ln2_g[l], 'ln2_b': ln2_b[l], 'mix': rwkv_mix[l], 'w0': rwkv_w0[l],
             'w_up': rwkv_w_up[l], 'a0': rwkv_a0[l], 'a_up': rwkv_a_up[l], 'g_up': rwkv_g_up[l],
             'k_k': rwkv_k_k[l], 'k_a': rwkv_k_a[l], 'r_k': rwkv_r_k[l], 'gn_g': rwkv_gn_g[l],
             'gn_b': rwkv_gn_b[l], 'conv_w': lru_conv_w[l], 'conv_b': lru_conv_b[l], 'wa': lru_wa[l],
             'ba': lru_ba[l], 'wx': lru_wx[l], 'bx': lru_bx[l], 'lam': lru_lambda[l],
             'w_router': moe_w_router[l], 'b_router': moe_b_router[l]}
        st_s = {'ret': state_ret[l], 'rwkv': state_rwkv[l], 'shift': state_rwkv_shift[l],
                'lru': state_lru[l], 'conv': state_conv[l]}
        h, sp, ss = _layer(h, p, moe, l, alpha, st_s, bp, tp, bs, ts)
        new_p.append(sp)
        new_s.append(ss)
    outs = [h[0].reshape(bp, tp, D_MODEL), h[1].reshape(bs, ts, D_MODEL)]
    for i in range(5):
        outs.append(jnp.stack([s[i] for s in new_p]))
        outs.append(jnp.stack([s[i] for s in new_s]))
    return tuple(outs)
```

```python
import functools

import jax
import jax.numpy as jnp
from jax import lax
from jax.experimental import pallas as pl
from jax.experimental.pallas import tpu as pltpu

f32 = jnp.float32
bf16 = jnp.bfloat16
i32 = jnp.int32

D_MODEL = 1024
HEAD_DIM = 64
RET_W = 256
RET_H = 4
RET_CHUNK = 128
ROPE_BASE = 10000.0
RWKV_W = 512
RWKV_H = 8
RWKV_PROJ = 1664
RWKV_LORA_COL = 1536
RWKV_GN_EPS = 64e-5
RWKV_CHUNK = 64
RWKV_ROWS = 128
LRU_W = 256
LRU_C = 8.0
LRU_CHUNK = 256
CONV_WIDTH = 4
D_PROJ = 3200
N_EXPERTS = 32
TOP_K = 4
D_EXPERT = 1024
SWIGLU_LIMIT = 7.0
SWIGLU_ALPHA = 1.702
LN_EPS = 1e-5
PAST_LEN = 16384.0

LANES = 128
SUBLANES = 8
SAMPLE_T_PAD = 8
TOKEN_TILE = 512
MOE_BLOCK = 512
LOCAL_ROWS = TOP_K * TOKEN_TILE + N_EXPERTS * SUBLANES
RUN_PIECES = tuple(SUBLANES << j for j in reversed(range((TOKEN_TILE // SUBLANES).bit_length())))
WAIT_PIECES = tuple(SUBLANES << j for j in reversed(range((LOCAL_ROWS // SUBLANES).bit_length())))
VMEM_LIMIT = 56 * 1024 * 1024

_NT = (((1,), (1,)), ((), ()))
_TN = (((0,), (0,)), ((), ()))


def _params(*sem):
    return pltpu.CompilerParams(dimension_semantics=sem, vmem_limit_bytes=VMEM_LIMIT)


def _mm(a, b):
    return jnp.dot(a.astype(bf16), b.astype(bf16), preferred_element_type=f32)


def _mm_nt(a, b):
    return lax.dot_general(a.astype(bf16), b.astype(bf16), _NT, preferred_element_type=f32)


def _mm_tn(a, b):
    return lax.dot_general(a.astype(bf16), b.astype(bf16), _TN, preferred_element_type=f32)


def _mm3(a, b):
    a_hi, b_hi = a.astype(bf16), b.astype(bf16)
    a_lo = (a - a_hi.astype(f32)).astype(bf16)
    b_lo = (b - b_hi.astype(f32)).astype(bf16)
    dot = functools.partial(jnp.dot, preferred_element_type=f32)
    return dot(a_hi, b_hi) + dot(a_hi, b_lo) + dot(a_lo, b_hi)


def _softplus(x):
    return jnp.maximum(x, 0.0) + jnp.log(1.0 + jnp.exp(-jnp.abs(x)))


def _sigmoid(x):
    return 1.0 / (1.0 + jnp.exp(-x))


def _half_masks():
    lane = lax.broadcasted_iota(i32, (1, LANES), 1)
    m0 = (lane < HEAD_DIM).astype(f32)
    return m0, 1.0 - m0


def _seg_mean(x, m0, m1):
    s0 = jnp.sum(x * m0, axis=-1, keepdims=True)
    s1 = jnp.sum(x * m1, axis=-1, keepdims=True)
    return (m0 * s0 + m1 * s1) * (1.0 / HEAD_DIM)


def _seg_sum(x, m0, m1):
    s0 = jnp.sum(x * m0, axis=-1, keepdims=True)
    s1 = jnp.sum(x * m1, axis=-1, keepdims=True)
    return m0 * s0 + m1 * s1


def _block_diag_mask():
    r = lax.broadcasted_iota(i32, (LANES, LANES), 0) // HEAD_DIM
    c = lax.broadcasted_iota(i32, (LANES, LANES), 1) // HEAD_DIM
    return (r == c).astype(f32)


def _layer_norm_rows(z, g, b):
    mu = jnp.mean(z, axis=-1, keepdims=True)
    zc = z - mu
    var = jnp.mean(zc * zc, axis=-1, keepdims=True)
    return zc * lax.rsqrt(var + LN_EPS) * g + b


def _group_specs(width, n_prompt_tiles):
    return [pl.BlockSpec((TOKEN_TILE, width), lambda i, *_: (jnp.minimum(i, n_prompt_tiles - 1), 0)),
            pl.BlockSpec((TOKEN_TILE, width), lambda i, *_: (jnp.maximum(i - n_prompt_tiles, 0), 0))]


def _pick_group(p_ref, s_ref, n_prompt_tiles):
    return jnp.where(pl.program_id(0) >= n_prompt_tiles, s_ref[...], p_ref[...])


def _n_tiles(h_p, h_s):
    assert h_p.shape[0] % TOKEN_TILE == 0 and h_s.shape[0] % TOKEN_TILE == 0, (h_p.shape, h_s.shape)
    return h_p.shape[0] // TOKEN_TILE, h_s.shape[0] // TOKEN_TILE


def _store_group(p_ref, s_ref, value, n_prompt_tiles):
    @pl.when(pl.program_id(0) < n_prompt_tiles)
    def _():
        p_ref[...] = value

    @pl.when(pl.program_id(0) >= n_prompt_tiles)
    def _():
        s_ref[...] = value


def _in_proj_kernel(xp_ref, xs_ref, w_ref, retp_ref, rets_ref, rwp_ref, rws_ref, lrup_ref, lrus_ref, *,
                    n_prompt_tiles):
    c0, c1 = 4 * RET_W, 4 * RET_W + RWKV_PROJ
    dot = functools.partial(jnp.dot, preferred_element_type=f32)

    def project(x_ref, ret_ref, rw_ref, lru_ref):
        xb = x_ref[...].astype(bf16)
        ret_ref[...] = dot(xb, w_ref[:, :c0])
        rw_ref[...] = dot(xb, w_ref[:, c0:c1])
        lru_ref[...] = dot(xb, w_ref[:, c1:])

    @pl.when(pl.program_id(0) < n_prompt_tiles)
    def _():
        project(xp_ref, retp_ref, rwp_ref, lrup_ref)

    @pl.when(pl.program_id(0) >= n_prompt_tiles)
    def _():
        project(xs_ref, rets_ref, rws_ref, lrus_ref)


def _in_proj(h_p, h_s, w_bf):
    npt, nst = _n_tiles(h_p, h_s)
    widths = (4 * RET_W, RWKV_PROJ, 2 * LRU_W)
    out = pl.pallas_call(
        functools.partial(_in_proj_kernel, n_prompt_tiles=npt),
        grid=(npt + nst,),
        in_specs=_group_specs(D_MODEL, npt) + [
                  pl.BlockSpec((D_MODEL, D_PROJ), lambda i: (0, 0))],
        out_specs=[s for w in widths for s in _group_specs(w, npt)],
        out_shape=[jax.ShapeDtypeStruct((rows, w), f32) for w in widths for rows in (h_p.shape[0], h_s.shape[0])],
        compiler_params=_params("arbitrary"),
        name="in_proj",
    )(h_p, h_s, w_bf)
    return out[0:2], out[2:4], out[4:6]


def _rope_tables(pos):
    half = HEAD_DIM // 2
    inv = ROPE_BASE ** (-jnp.arange(half, dtype=f32) / half)
    ang = pos[:, None] * inv[None, :]
    cos, sin = jnp.cos(ang), jnp.sin(ang)
    cos_f = jnp.tile(jnp.concatenate([cos, cos], axis=-1), (1, RET_H))
    sin_f = jnp.tile(jnp.concatenate([-sin, sin], axis=-1), (1, RET_H))
    return cos_f, sin_f


def _retention_tables(length, n_valid):
    lg = jnp.log1p(-jnp.exp2(-5.0 - jnp.arange(RET_H, dtype=f32)))
    idx = jnp.arange(length, dtype=f32)
    rel = idx[:, None] - idx[None, :]
    mask = jnp.where(rel[None] >= 0, jnp.exp(jnp.maximum(rel, 0.0)[None] * lg[:, None, None]), 0.0)
    q_dec = jnp.exp((idx[:, None] + 1.0) * lg[None, :])
    k_dec = jnp.where(idx[:, None] < n_valid, jnp.exp((n_valid - 1.0 - idx)[:, None] * lg[None, :]), 0.0)
    c_dec = jnp.exp(n_valid * lg)[None, :]
    rep = lambda t: jnp.repeat(t, HEAD_DIM, axis=-1)
    return mask, rep(q_dec), rep(k_dec), rep(c_dec)


def _retention_kernel(ret_ref, cos_ref, sin_ref, mask_ref, qdec_ref, kdec_ref, cdec_ref, s0_ref,
                      y_ref, s_ref, pair_ref, *, nb):
    @pl.when(pl.program_id(1) == 0)
    def _():
        _load_head_pairs(pair_ref, s0_ref)

    m0, m1 = _half_masks()
    lane = lax.broadcasted_iota(i32, (1, LANES), 1)
    first_half = (lane % HEAD_DIM) < (HEAD_DIM // 2)
    bd = _block_diag_mask()

    def rope(x, cs, sn):
        swapped = jnp.where(first_half, pltpu.roll(x, LANES - HEAD_DIM // 2, 1), pltpu.roll(x, HEAD_DIM // 2, 1))
        return x * cs + swapped * sn

    for j in range(nb):
        for p in range(RET_H // 2):
            cols = pl.ds(p * LANES, LANES)
            cs, sn = cos_ref[:, cols], sin_ref[:, cols]
            q2 = rope(ret_ref[j, :, pl.ds(p * LANES, LANES)], cs, sn)
            k2 = rope(ret_ref[j, :, pl.ds(RET_W + p * LANES, LANES)], cs, sn) * (HEAD_DIM ** -0.5)
            v2 = ret_ref[j, :, pl.ds(2 * RET_W + p * LANES, LANES)]
            g2 = ret_ref[j, :, pl.ds(3 * RET_W + p * LANES, LANES)]
            state = pair_ref[j, p]
            o2 = _mm(q2, state) * qdec_ref[:, cols]
            for hh, m in enumerate((m0, m1)):
                sc = _mm_nt(q2 * m, k2) * mask_ref[2 * p + hh]
                o2 = o2 + _mm(sc, v2) * m
            pair_ref[j, p] = state * cdec_ref[:, cols] + _mm_tn(k2 * kdec_ref[:, cols], v2) * bd
            mu = _seg_mean(o2, m0, m1)
            oc = o2 - mu
            var = _seg_mean(oc * oc, m0, m1)
            y_ref[j, :, cols] = g2 * _sigmoid(g2) * oc * lax.rsqrt(var + LN_EPS)

    @pl.when(pl.program_id(1) == pl.num_programs(1) - 1)
    def _():
        _store_head_pairs(s_ref, pair_ref)


def _load_head_pairs(pair_ref, s_ref):
    pair_ref[...] = jnp.zeros(pair_ref.shape, f32)
    for j in range(pair_ref.shape[0]):
        for p in range(pair_ref.shape[1]):
            pair_ref[j, p, pl.ds(0, HEAD_DIM), pl.ds(0, HEAD_DIM)] = s_ref[j, 2 * p]
            pair_ref[j, p, pl.ds(HEAD_DIM, HEAD_DIM), pl.ds(HEAD_DIM, HEAD_DIM)] = s_ref[j, 2 * p + 1]


def _store_head_pairs(s_ref, pair_ref):
    for j in range(pair_ref.shape[0]):
        for p in range(pair_ref.shape[1]):
            s_ref[j, 2 * p] = pair_ref[j, p, pl.ds(0, HEAD_DIM), pl.ds(0, HEAD_DIM)]
            s_ref[j, 2 * p + 1] = pair_ref[j, p, pl.ds(HEAD_DIM, HEAD_DIM), pl.ds(HEAD_DIM, HEAD_DIM)]


def _retention(ret, cos_f, sin_f, tables, s0, layer, *, nb, length):
    mask, q_dec, k_dec, c_dec = tables
    n_batch, t_len, _ = ret.shape
    blk = lambda g, c: (g, c, 0)
    const2 = lambda g, c: (0, 0)
    st = lambda g, c: (g, 0, 0, 0)
    return pl.pallas_call(
        functools.partial(_retention_kernel, nb=nb),
        grid=(n_batch // nb, t_len // length),
        in_specs=[pl.BlockSpec((nb, length, 4 * RET_W), blk),
                  pl.BlockSpec((length, RET_W), lambda g, c: (c, 0)),
                  pl.BlockSpec((length, RET_W), lambda g, c: (c, 0)),
                  pl.BlockSpec((RET_H, length, length), lambda g, c: (0, 0, 0)),
                  pl.BlockSpec((length, RET_W), const2),
                  pl.BlockSpec((length, RET_W), const2),
                  pl.BlockSpec((1, RET_W), const2),
                  pl.BlockSpec((None, nb, RET_H, HEAD_DIM, HEAD_DIM), lambda g, c: (layer, g, 0, 0, 0))],
        out_specs=[pl.BlockSpec((nb, length, RET_W), blk),
                   pl.BlockSpec((nb, RET_H, HEAD_DIM, HEAD_DIM), st)],
        out_shape=[jax.ShapeDtypeStruct((n_batch, t_len, RET_W), f32),
                   jax.ShapeDtypeStruct((n_batch, RET_H, HEAD_DIM, HEAD_DIM), f32)],
        scratch_shapes=[pltpu.VMEM((nb, RET_H // 2, LANES, LANES), f32)],
        compiler_params=_params("arbitrary", "arbitrary"),
        name="retention",
    )(ret, cos_f, sin_f, mask, q_dec, k_dec, c_dec, s0)


def _rwkv_kernel(rw_ref, aux_ref, valid_ref, mix_ref, vec_ref, lora_ref, tri_ref, s0_ref,
                 y_ref, s_ref, *rest, rows, length, chain):
    n_chunks = rows // length
    rw = rw_ref[...]
    if chain:
        last_ref, xs_ref, pair_ref = rest

        @pl.when(pl.program_id(1) == 0)
        def _():
            _load_head_pairs(pair_ref, s0_ref)
            xs_ref[pl.ds(0, SUBLANES), :] = jnp.broadcast_to(aux_ref[0], (SUBLANES, RWKV_PROJ))

        xs_ref[pl.ds(SUBLANES, rows), :] = rw
        prev = xs_ref[pl.ds(SUBLANES - 1, rows), :]
        xs_ref[pl.ds(0, SUBLANES), :] = rw[rows - SUBLANES:, :]
        last_ref[0] = rw[rows - 1:rows, :]
    else:
        pair_ref, = rest
        _load_head_pairs(pair_ref, s0_ref)
        prev = aux_ref[...]
    rwm = rw + (prev - rw) * mix_ref[...]

    valid = valid_ref[...]
    w0, a0, k_k, k_a, r_k, gn_g, gn_b = (vec_ref[pl.ds(i, 1), :] for i in range(7))
    lo = rwm[:, RWKV_LORA_COL:]
    lw = _mm3(jnp.tanh(lo), lora_ref[0])
    la = _mm3(lo, lora_ref[1])
    gate = _mm3(_sigmoid(lo), lora_ref[2])
    logw = -jnp.exp(-_softplus(-(w0 + lw)) - 0.5) * valid
    a = _sigmoid(a0 + la)
    r = rwm[:, :RWKV_W]
    kr = rwm[:, RWKV_W:2 * RWKV_W]
    vr = rwm[:, 2 * RWKV_W:3 * RWKV_W]
    kk_raw = kr * k_k
    kp = kr * (1.0 + (a - 1.0) * k_a) * valid
    cum = _mm3(tri_ref[...], logw)
    g_incl = jnp.exp(cum)
    g_inv = jnp.exp(-cum)
    g_prev = jnp.exp(cum - logw)
    g_end = jnp.concatenate(
        [jnp.broadcast_to(g_incl[(c + 1) * length - 1:(c + 1) * length, :], (length, RWKV_W)) for c in range(n_chunks)],
        axis=0)

    m0, m1 = _half_masks()
    bd = _block_diag_mask()
    stacked = 2 * rows
    ri = lax.broadcasted_iota(i32, (stacked, stacked), 0)
    ci = lax.broadcasted_iota(i32, (stacked, stacked), 1)
    same = (ri // length) == (ci // length)
    strict = jnp.logical_and(same, ci < ri)
    incl = jnp.logical_and(same, ci <= ri)
    eye = (ci == ri).astype(f32)
    stack = lambda t: jnp.concatenate([t * m0, t * m1], axis=0)
    unstack = lambda t: t[:rows] + t[rows:]
    pairs = range(RWKV_H // 2)

    pre = []
    for p in pairs:
        sl = slice(p * LANES, (p + 1) * LANES)
        kk2 = kk_raw[:, sl]
        kk2 = kk2 * lax.rsqrt(jnp.maximum(_seg_sum(kk2 * kk2, m0, m1), 1e-24)) * valid[:, sl]
        d = dict(sl=sl, r2=r[:, sl], v2=vr[:, sl], kp2=kp[:, sl], ge=g_end[:, sl])
        d['kh'] = kk2 * g_prev[:, sl]
        rh = d['r2'] * g_incl[:, sl]
        bt = kk2 * a[:, sl] * g_inv[:, sl]
        kt = d['kp2'] * g_inv[:, sl]
        d['b_end'] = bt * d['ge']
        d['k_end'] = kt * d['ge']
        d['kh_s'], d['rh_s'], d['v_s'] = stack(d['kh']), stack(rh), stack(d['v2'])
        gram = _mm_nt(jnp.concatenate([d['kh_s'], d['rh_s']], axis=0), jnp.concatenate([stack(bt), stack(kt)], axis=0))
        d['x'] = -jnp.where(strict, gram[:stacked, :stacked], 0.0)
        d['a_k'] = jnp.where(strict, gram[:stacked, stacked:], 0.0)
        d['r_b'] = jnp.where(incl, gram[stacked:, :stacked], 0.0)
        d['r_k'] = jnp.where(incl, gram[stacked:, stacked:], 0.0)
        d['t'] = eye + d['x']
        pre.append(d)
    cover = 2
    while cover < length:
        for d in pre:
            d['x'] = _mm(d['x'], d['x'])
            d['t'] = d['t'] + _mm(d['t'], d['x'])
        cover *= 2
    for d in pre:
        rhs0 = -_mm(d['a_k'], d['v_s'])
        tz = _mm(d['t'], jnp.concatenate([rhs0, d['kh_s']], axis=1))
        z0_s = tz[:, :LANES]
        rbz = _mm(d['r_b'], tz)
        d['z0'] = unstack(z0_s)
        d['q'] = unstack(d['rh_s'] - rbz[:, LANES:])
        d['y0'] = unstack(rbz[:, :LANES] + _mm(d['r_k'], d['v_s']))
        d['w'] = unstack(_mm_tn(d['t'], stack(d['b_end'])))

    for p, d in zip(pairs, pre):
        sl, ge, kh, v2 = d['sl'], d['ge'], d['kh'], d['v2']
        state = pair_ref[0, p] if chain else None
        ys = []
        for c in range(n_chunks):
            cr = slice(c * length, (c + 1) * length)
            if not chain:
                state = pair_ref[c, p]
            ys.append(_mm_nt(d['q'][cr], state) + d['y0'][cr])
            n_c = (_mm_tn(d['z0'][cr], d['b_end'][cr]) + _mm_tn(v2[cr], d['k_end'][cr])) * bd
            kw = _mm_tn(kh[cr], d['w'][cr]) * bd
            state = state * ge[(c + 1) * length - 1:(c + 1) * length, :] - _mm(state, kw) + n_c
            if not chain:
                pair_ref[c, p] = state
        if chain:
            pair_ref[0, p] = state
        y2 = jnp.concatenate(ys, axis=0) if n_chunks > 1 else ys[0]

        mu = _seg_mean(y2, m0, m1)
        yc = y2 - mu
        var = _seg_mean(yc * yc, m0, m1)
        yn = yc * lax.rsqrt(var + RWKV_GN_EPS) * gn_g[:, sl] + gn_b[:, sl]
        bonus = _seg_sum(d['r2'] * d['kp2'] * r_k[:, sl], m0, m1) * v2
        y_ref[:, pl.ds(p * LANES, LANES)] = (yn + bonus) * gate[:, sl]

    if chain:
        @pl.when(pl.program_id(1) == pl.num_programs(1) - 1)
        def _():
            _store_head_pairs(s_ref, pair_ref)
    else:
        _store_head_pairs(s_ref, pair_ref)


def _rwkv(rw, aux, valid, mix, vec, lora, tri, s0, layer, *, n_groups, n_steps, rows, length, chain):
    blk = lambda g, c: (g * n_steps + c, 0)
    const2 = lambda g, c: (0, 0)
    st = lambda g, c: (g, 0, 0, 0)
    n_state = 1 if chain else rows // length
    aux_spec = (pl.BlockSpec((1, 1, RWKV_PROJ), lambda g, c: (g, 0, 0)) if chain
                else pl.BlockSpec((rows, RWKV_PROJ), blk))
    return pl.pallas_call(
        functools.partial(_rwkv_kernel, rows=rows, length=length, chain=chain),
        grid=(n_groups, n_steps),
        in_specs=[pl.BlockSpec((rows, RWKV_PROJ), blk),
                  aux_spec,
                  pl.BlockSpec((rows, RWKV_W), const2),
                  pl.BlockSpec((1, RWKV_PROJ), const2),
                  pl.BlockSpec((SUBLANES, RWKV_W), const2),
                  pl.BlockSpec((3, LANES, RWKV_W), lambda g, c: (0, 0, 0)),
                  pl.BlockSpec((rows, rows), const2),
                  pl.BlockSpec((None, n_state, RWKV_H, HEAD_DIM, HEAD_DIM), lambda g, c: (layer, g, 0, 0, 0))],
        out_specs=[pl.BlockSpec((rows, RWKV_W), blk),
                   pl.BlockSpec((n_state, RWKV_H, HEAD_DIM, HEAD_DIM), st)]
        + ([pl.BlockSpec((1, 1, RWKV_PROJ), lambda g, c: (g, 0, 0))] if chain else []),
        out_shape=[jax.ShapeDtypeStruct((n_groups * n_steps * rows, RWKV_W), f32),
                   jax.ShapeDtypeStruct((n_groups * n_state, RWKV_H, HEAD_DIM, HEAD_DIM), f32)]
        + ([jax.ShapeDtypeStruct((n_groups, 1, RWKV_PROJ), f32)] if chain else []),
        scratch_shapes=([pltpu.VMEM((rows + SUBLANES, RWKV_PROJ), f32)] if chain else [])
        + [pltpu.VMEM((n_state, RWKV_H // 2, LANES, LANES), f32)],
        compiler_params=_params("arbitrary", "arbitrary"),
        name="rwkv7",
    )(rw, aux, valid, mix, vec, lora, tri, s0)


def _chunk_tri(rows, length):
    idx = jnp.arange(rows)
    same = (idx[:, None] // length) == (idx[None, :] // length)
    return jnp.logical_and(same, idx[None, :] <= idx[:, None]).astype(f32)


def _lru_kernel(lru_ref, valid_ref, vec_ref, wa_ref, wx_ref, conv0_ref, h0_ref,
                y_ref, h_ref, tail_ref, xext_ref, *, nb, length):
    @pl.when(pl.program_id(1) == 0)
    def _():
        h_ref[...] = h0_ref[...]
        xext_ref[:, pl.ds(0, SUBLANES), :] = conv0_ref[...]

    valid = valid_ref[...] > 0.5
    row = lax.broadcasted_iota(i32, (length, LRU_W), 0)
    cw = [vec_ref[pl.ds(i, 1), :] for i in range(CONV_WIDTH)]
    cb, ba, bx, lam = (vec_ref[pl.ds(i, 1), :] for i in range(CONV_WIDTH, CONV_WIDTH + 4))
    sp = _softplus(-lam)
    for j in range(nb):
        rows = pl.ds(j * length, length)
        gbr = lru_ref[rows, pl.ds(0, LRU_W)]
        x = lru_ref[rows, pl.ds(LRU_W, LRU_W)]
        xext_ref[j, pl.ds(SUBLANES, length), :] = x
        xc = cb + x * cw[CONV_WIDTH - 1]
        for t in range(CONV_WIDTH - 1):
            xc = xc + xext_ref[j, pl.ds(SUBLANES - (CONV_WIDTH - 1) + t, length), :] * cw[t]
        xext_ref[j, pl.ds(0, SUBLANES), :] = x[length - SUBLANES:, :]
        tail_ref[j] = x[length - SUBLANES:, :]
        gate_a = _sigmoid(_mm(xc, wa_ref[...]) + ba)
        gate_x = _sigmoid(_mm(xc, wx_ref[...]) + bx)
        log_a = -LRU_C * gate_a * sp
        a = jnp.exp(log_a)
        b = xc * gate_x * jnp.sqrt(1.0 - jnp.exp(2.0 * log_a))
        a = jnp.where(valid, a, 1.0)
        b = jnp.where(valid, b, 0.0)
        shift = 1
        while shift < length:
            inside = row >= shift
            b = a * jnp.where(inside, pltpu.roll(b, shift, 0), 0.0) + b
            a = a * jnp.where(inside, pltpu.roll(a, shift, 0), 1.0)
            shift *= 2
        hs = a * h_ref[j, pl.ds(0, 1), :] + b
        h_ref[j] = jnp.broadcast_to(hs[length - 1:length, :], (SUBLANES, LRU_W))
        c = 0.7978845608028654
        gelu = 0.5 * gbr * (1.0 + jnp.tanh(c * (gbr + 0.044715 * gbr * gbr * gbr)))
        y_ref[rows, :] = hs * gelu


def _lru(lru, valid, vec, wa_bd, wx_bd, conv0, h0, *, n_batch, n_chunks, nb, length, out_rows):
    blk = lambda g, c: (g * n_chunks + c, 0)
    const2 = lambda g, c: (0, 0)
    st = lambda g, c: (g, 0, 0)
    return pl.pallas_call(
        functools.partial(_lru_kernel, nb=nb, length=length),
        grid=(n_batch // nb, n_chunks),
        in_specs=[pl.BlockSpec((nb * length, 2 * LRU_W), blk),
                  pl.BlockSpec((length, LRU_W), const2),
                  pl.BlockSpec((SUBLANES, LRU_W), const2),
                  pl.BlockSpec((LRU_W, LRU_W), const2),
                  pl.BlockSpec((LRU_W, LRU_W), const2),
                  pl.BlockSpec((nb, SUBLANES, LRU_W), st),
                  pl.BlockSpec((nb, SUBLANES, LRU_W), st)],
        out_specs=[pl.BlockSpec((nb * length, LRU_W), blk),
                   pl.BlockSpec((nb, SUBLANES, LRU_W), st),
                   pl.BlockSpec((nb, SUBLANES, LRU_W), st)],
        out_shape=[jax.ShapeDtypeStruct((out_rows, LRU_W), f32),
                   jax.ShapeDtypeStruct((n_batch, SUBLANES, LRU_W), f32),
                   jax.ShapeDtypeStruct((n_batch, SUBLANES, LRU_W), f32)],
        scratch_shapes=[pltpu.VMEM((nb, length + SUBLANES, LRU_W), f32)],
        compiler_params=_params("arbitrary", "arbitrary"),
        name="rg_lru",
    )(lru, valid, vec, wa_bd, wx_bd, conv0, h0)


def _out_proj_kernel(yrp_ref, yrs_ref, ywp_ref, yws_ref, ylp_ref, yls_ref, xp_ref, xs_ref,
                     w_ref, ln_ref, wr_ref, br_ref,
                     x1_ref, lpos_ref, g_ref, meta_ref, cnt_ref, *, alpha, n_prompt_tiles):
    @pl.when(pl.program_id(0) == 0)
    def _():
        cnt_ref[...] = jnp.zeros(cnt_ref.shape, f32)

    pick = functools.partial(_pick_group, n_prompt_tiles=n_prompt_tiles)
    mixed = (_mm(pick(yrp_ref, yrs_ref), w_ref[pl.ds(0, RET_W), :])
             + _mm(pick(ywp_ref, yws_ref), w_ref[pl.ds(RET_W, RWKV_W), :])
             + _mm(pick(ylp_ref, yls_ref), w_ref[pl.ds(RET_W + RWKV_W, LRU_W), :]))
    x1 = _layer_norm_rows(alpha * pick(xp_ref, xs_ref) + mixed, ln_ref[pl.ds(0, 1), :], ln_ref[pl.ds(1, 1), :])
    x1_ref[...] = x1
    logits = _mm3(x1, wr_ref[...]) + br_ref[...]
    tm = logits.shape[0]
    lane = lax.broadcasted_iota(i32, logits.shape, 1).astype(f32)
    top_v = jnp.zeros(logits.shape, f32)
    work = logits
    v_max = None
    onehots = []
    for k in range(TOP_K):
        v = jnp.max(work, axis=-1, keepdims=True)
        idx = jnp.min(jnp.where(work == v, lane, float(LANES)), axis=-1, keepdims=True)
        if k == 0:
            v_max = v
        hit = lane == idx
        onehots.append(hit.astype(f32))
        top_v = jnp.where(lane == k, jnp.exp(v - v_max), top_v)
        work = jnp.where(hit, -jnp.inf, work)
    g_ref[...] = top_v / jnp.sum(top_v, axis=-1, keepdims=True)

    total = onehots[0] + onehots[1] + onehots[2] + onehots[3]
    ri = lax.broadcasted_iota(i32, (tm, tm), 0)
    ci = lax.broadcasted_iota(i32, (tm, tm), 1)
    before = _mm((ci < ri).astype(f32), total)
    groups = jnp.floor((jnp.sum(total, axis=0, keepdims=True) + (SUBLANES - 1.0)) * (1.0 / SUBLANES))
    er = lax.broadcasted_iota(i32, (LANES, LANES), 0)
    ec = lax.broadcasted_iota(i32, (LANES, LANES), 1)
    run_len = groups * SUBLANES
    run_start = _mm(jnp.broadcast_to(groups, (SUBLANES, LANES)), (er < ec).astype(f32))[0:1] * SUBLANES
    lpos = jnp.full(logits.shape, -1.0, f32)
    for k in range(TOP_K):
        pos = jnp.sum(onehots[k] * (before + run_start), axis=-1, keepdims=True)
        lpos = jnp.where(lane == k, pos, lpos)
    lpos_ref[...] = lpos
    row = lax.broadcasted_iota(i32, (SUBLANES, LANES), 0)
    meta = jnp.where(row == 0, run_start, jnp.where(row == 1, run_len, jnp.where(row == 2, cnt_ref[...], 0.0)))
    meta_ref[...] = meta.astype(i32)
    cnt_ref[...] = cnt_ref[...] + run_len


def _out_proj_router(y_ret, y_rwkv, y_lru, h, w_out_bf, ln, w_router, b_router, alpha):
    npt, nst = _n_tiles(*h)
    n = h[0].shape[0] + h[1].shape[0]
    tm = TOKEN_TILE
    row = lambda i: (i, 0)
    const = lambda i: (0, 0)
    return pl.pallas_call(
        functools.partial(_out_proj_kernel, alpha=alpha, n_prompt_tiles=npt),
        grid=(npt + nst,),
        in_specs=_group_specs(RET_W, npt) + _group_specs(RWKV_W, npt) + _group_specs(LRU_W, npt)
        + _group_specs(D_MODEL, npt) + [
                  pl.BlockSpec((D_MODEL, D_MODEL), const),
                  pl.BlockSpec((SUBLANES, D_MODEL), const),
                  pl.BlockSpec((D_MODEL, LANES), const),
                  pl.BlockSpec((1, LANES), const)],
        out_specs=[pl.BlockSpec((tm, D_MODEL), row), pl.BlockSpec((tm, LANES), row), pl.BlockSpec((tm, LANES), row),
                   pl.BlockSpec((SUBLANES, LANES), row), pl.BlockSpec((SUBLANES, LANES), const)],
        out_shape=[jax.ShapeDtypeStruct((n, D_MODEL), f32),
                   jax.ShapeDtypeStruct((n, LANES), f32),
                   jax.ShapeDtypeStruct((n, LANES), f32),
                   jax.ShapeDtypeStruct((n // tm * SUBLANES, LANES), i32),
                   jax.ShapeDtypeStruct((SUBLANES, LANES), f32)],
        compiler_params=_params("arbitrary"),
        name="out_proj_router",
    )(*y_ret, *y_rwkv, *y_lru, *h, w_out_bf, ln, w_router, b_router)


def _for_each_run_piece(meta_ref, pstart_ref, fn):
    for e in range(N_EXPERTS):
        start, n = meta_ref[0, e], meta_ref[1, e]
        base = pstart_ref[e] + meta_ref[2, e]
        for sz in RUN_PIECES:
            done = n & ~(2 * sz - 1)

            @pl.when((n & sz) != 0)
            def _():
                fn(pl.multiple_of(start + done, SUBLANES), pl.multiple_of(base + done, SUBLANES), sz)


def _wait_run_rows(meta_ref, make_copy):
    total = meta_ref[1, 0]
    for e in range(1, N_EXPERTS):
        total = total + meta_ref[1, e]
    for sz in WAIT_PIECES:
        @pl.when((total & sz) != 0)
        def _():
            make_copy(sz).wait()


def _dispatch_kernel(pstart_ref, cnt_ref, n_used_ref, meta_ref, meta_prev_ref, lpos_ref, x1_ref, xs_hbm,
                     xl_ref, zero_ref, sems, zsem):
    tm = TOKEN_TILE
    bm = MOE_BLOCK
    i = pl.program_id(0)
    slot = i % 2

    @pl.when(i == 0)
    def _():
        zero_ref[...] = jnp.zeros(zero_ref.shape, f32)
        tail = lambda j: pltpu.make_async_copy(zero_ref, xs_hbm.at[pl.ds(j * bm, bm), :], zsem)

        def tail_start(j, carry):
            tail(j).start()
            return carry

        def tail_wait(j, carry):
            tail(j).wait()
            return carry

        n_blocks = xs_hbm.shape[0] // bm
        lax.fori_loop(n_used_ref[0], n_blocks, tail_start, 0)
        lax.fori_loop(n_used_ref[0], n_blocks, tail_wait, 0)
        for e in range(N_EXPERTS):
            lo = pstart_ref[e] + cnt_ref[e]
            n_groups = ((cnt_ref[e] + bm - 1) // bm * bm - cnt_ref[e]) // SUBLANES
            fill = lambda g: pltpu.make_async_copy(
                zero_ref.at[pl.ds(0, SUBLANES), :],
                xs_hbm.at[pl.ds(pl.multiple_of(lo + g * SUBLANES, SUBLANES), SUBLANES), :], zsem)

            def start(g, carry):
                fill(g).start()
                return carry

            def wait(g, carry):
                fill(g).wait()
                return carry

            lax.fori_loop(0, n_groups, start, 0)
            lax.fori_loop(0, n_groups, wait, 0)

    lpos_t = lpos_ref[...].T
    srow = lax.broadcasted_iota(i32, (LOCAL_ROWS, tm), 0).astype(f32)
    perm = jnp.zeros((LOCAL_ROWS, tm), f32)
    for k in range(TOP_K):
        perm = perm + (srow == lpos_t[k:k + 1, :]).astype(f32)
    xl_ref[slot] = jnp.dot(perm.astype(bf16), x1_ref[...].astype(bf16), preferred_element_type=f32)

    def send(local_row, sorted_row, n_rows):
        pltpu.make_async_copy(xl_ref.at[slot, pl.ds(local_row, n_rows), :],
                              xs_hbm.at[pl.ds(sorted_row, n_rows), :], sems.at[slot]).start()

    _for_each_run_piece(meta_ref, pstart_ref, send)

    def sent(s):
        return lambda n_rows: pltpu.make_async_copy(xl_ref.at[s, pl.ds(0, n_rows), :],
                                                    xs_hbm.at[pl.ds(0, n_rows), :], sems.at[s])

    @pl.when(i > 0)
    def _():
        _wait_run_rows(meta_prev_ref, sent(1 - slot))

    @pl.when(i == pl.num_programs(0) - 1)
    def _():
        _wait_run_rows(meta_ref, sent(slot))


def _dispatch(pstart, counts, n_used, meta, lpos, x1, n_rows):
    n = x1.shape[0]
    tm = TOKEN_TILE
    smem_tile = lambda f: pl.BlockSpec((SUBLANES, LANES), f, memory_space=pltpu.SMEM)
    grid_spec = pltpu.PrefetchScalarGridSpec(
        num_scalar_prefetch=3,
        grid=(n // tm,),
        in_specs=[smem_tile(lambda i, ps, ct, nu: (i, 0)),
                  smem_tile(lambda i, ps, ct, nu: (jnp.maximum(i - 1, 0), 0)),
                  pl.BlockSpec((tm, LANES), lambda i, ps, ct, nu: (i, 0)),
                  pl.BlockSpec((tm, D_MODEL), lambda i, ps, ct, nu: (i, 0))],
        out_specs=pl.BlockSpec(memory_space=pl.ANY),
        scratch_shapes=[pltpu.VMEM((2, LOCAL_ROWS, D_MODEL), f32),
                        pltpu.VMEM((MOE_BLOCK, D_MODEL), f32),
                        pltpu.SemaphoreType.DMA((2,)), pltpu.SemaphoreType.DMA(())],
    )
    return pl.pallas_call(
        _dispatch_kernel,
        grid_spec=grid_spec,
        out_shape=jax.ShapeDtypeStruct((n_rows, D_MODEL), f32),
        compiler_params=_params("arbitrary"),
        name="moe_dispatch",
    )(pstart, counts, n_used, meta, meta, lpos, x1)


def _expert_kernel(blk_e_ref, n_used_ref, xs_ref, wgu_ref, bgu_ref, wdn_ref, bdn_ref,
                   out_ref, wgu_bf, wdn_bf):
    i = pl.program_id(0)
    prev = jnp.maximum(i - 1, 0)
    new_expert = jnp.logical_or(i == 0, blk_e_ref[i] != blk_e_ref[prev])
    used = i < n_used_ref[0]

    @pl.when(used)
    def _():
        @pl.when(new_expert)
        def _():
            wgu_bf[...] = wgu_ref[0, 0].astype(bf16)
            wdn_bf[...] = wdn_ref[0, 0].astype(bf16)

        gu = jnp.dot(xs_ref[...].astype(bf16), wgu_bf[...], preferred_element_type=f32) + bgu_ref[0, 0]
        g = jnp.minimum(gu[:, :D_EXPERT], SWIGLU_LIMIT)
        u = jnp.clip(gu[:, D_EXPERT:], -SWIGLU_LIMIT, SWIGLU_LIMIT)
        hdn = (u + 1.0) * g * _sigmoid(SWIGLU_ALPHA * g)
        out_ref[...] = jnp.dot(hdn.astype(bf16), wdn_bf[...], preferred_element_type=f32) + bdn_ref[0, 0]

    @pl.when(jnp.logical_not(used))
    def _():
        out_ref[...] = jnp.zeros(out_ref.shape, f32)


def _experts(blk_e, n_used, xs, w_gu, b_gu, w_down, b_down, layer):
    n_blocks = blk_e.shape[0]
    bm = MOE_BLOCK
    by_e = lambda i, be, nu: (layer, be[i], 0, 0)
    x_blk = lambda i, be, nu: (jnp.minimum(i, nu[0] - 1), 0)
    grid_spec = pltpu.PrefetchScalarGridSpec(
        num_scalar_prefetch=2,
        grid=(n_blocks,),
        in_specs=[pl.BlockSpec((bm, D_MODEL), x_blk),
                  pl.BlockSpec((1, 1, D_MODEL, 2 * D_EXPERT), by_e),
                  pl.BlockSpec((1, 1, 1, 2 * D_EXPERT), by_e),
                  pl.BlockSpec((1, 1, D_EXPERT, D_MODEL), by_e),
                  pl.BlockSpec((1, 1, 1, D_MODEL), by_e)],
        out_specs=pl.BlockSpec((bm, D_MODEL), lambda i, be, nu: (i, 0)),
        scratch_shapes=[pltpu.VMEM((D_MODEL, 2 * D_EXPERT), bf16),
                        pltpu.VMEM((D_EXPERT, D_MODEL), bf16)],
    )
    depth = w_gu.shape[0]
    return pl.pallas_call(
        _expert_kernel,
        grid_spec=grid_spec,
        out_shape=jax.ShapeDtypeStruct((n_blocks * bm, D_MODEL), f32),
        compiler_params=_params("arbitrary"),
        name="moe_experts",
    )(blk_e, n_used, xs, w_gu, b_gu.reshape(depth, N_EXPERTS, 1, 2 * D_EXPERT),
      w_down, b_down.reshape(depth, N_EXPERTS, 1, D_MODEL))


def _combine_kernel(pstart_ref, meta_ref, meta_next_ref, lpos_ref, gates_ref, x1_ref, ln_ref, yb_hbm,
                    outp_ref, outs_ref, yl_ref, sems, *, alpha, n_prompt_tiles):
    tm = TOKEN_TILE
    i = pl.program_id(0)
    slot = i % 2

    def fetch(meta, s):
        def recv(local_row, sorted_row, n_rows):
            pltpu.make_async_copy(yb_hbm.at[pl.ds(sorted_row, n_rows), :],
                                  yl_ref.at[s, pl.ds(local_row, n_rows), :], sems.at[s]).start()

        _for_each_run_piece(meta, pstart_ref, recv)

    @pl.when(i == 0)
    def _():
        yl_ref[...] = jnp.zeros(yl_ref.shape, f32)
        fetch(meta_ref, slot)

    @pl.when(i + 1 < pl.num_programs(0))
    def _():
        fetch(meta_next_ref, 1 - slot)

    _wait_run_rows(meta_ref, lambda n_rows: pltpu.make_async_copy(
        yb_hbm.at[pl.ds(0, n_rows), :], yl_ref.at[slot, pl.ds(0, n_rows), :], sems.at[slot]))

    lpos = lpos_ref[...]
    gates = gates_ref[...]
    scol = lax.broadcasted_iota(i32, (tm, LOCAL_ROWS), 1).astype(f32)
    weight = jnp.zeros((tm, LOCAL_ROWS), f32)
    for k in range(TOP_K):
        weight = weight + jnp.where(scol == lpos[:, k:k + 1], gates[:, k:k + 1], 0.0)
    y = jnp.dot(weight.astype(bf16), yl_ref[slot].astype(bf16), preferred_element_type=f32)
    x2 = _layer_norm_rows(alpha * x1_ref[...] + y, ln_ref[pl.ds(0, 1), :], ln_ref[pl.ds(1, 1), :])
    _store_group(outp_ref, outs_ref, x2, n_prompt_tiles)


def _combine(pstart, meta, lpos, gates, x1, ln, yb, alpha, n_prompt_tiles):
    n = x1.shape[0]
    tm = TOKEN_TILE
    n_tiles = n // tm
    row = lambda i, ps: (i, 0)
    smem_tile = lambda f: pl.BlockSpec((SUBLANES, LANES), f, memory_space=pltpu.SMEM)
    grid_spec = pltpu.PrefetchScalarGridSpec(
        num_scalar_prefetch=1,
        grid=(n_tiles,),
        in_specs=[smem_tile(row),
                  smem_tile(lambda i, ps: (jnp.minimum(i + 1, n_tiles - 1), 0)),
                  pl.BlockSpec((tm, LANES), row),
                  pl.BlockSpec((tm, LANES), row),
                  pl.BlockSpec((tm, D_MODEL), row),
                  pl.BlockSpec((SUBLANES, D_MODEL), lambda i, ps: (0, 0)),
                  pl.BlockSpec(memory_space=pl.ANY)],
        out_specs=_group_specs(D_MODEL, n_prompt_tiles),
        scratch_shapes=[pltpu.VMEM((2, LOCAL_ROWS, D_MODEL), f32), pltpu.SemaphoreType.DMA((2,))],
    )
    return pl.pallas_call(
        functools.partial(_combine_kernel, alpha=alpha, n_prompt_tiles=n_prompt_tiles),
        grid_spec=grid_spec,
        out_shape=[jax.ShapeDtypeStruct((n_prompt_tiles * tm, D_MODEL), f32),
                   jax.ShapeDtypeStruct((n - n_prompt_tiles * tm, D_MODEL), f32)],
        compiler_params=_params("arbitrary"),
        name="moe_combine",
    )(pstart, meta, meta, lpos, gates, x1, ln, yb)


def _block_tables(counts, n_tokens):
    bm = MOE_BLOCK
    padded = (counts + bm - 1) // bm * bm
    pad_end = jnp.cumsum(padded)
    pstart = (pad_end - padded).astype(i32)
    max_used = n_tokens * TOP_K + (n_tokens // TOKEN_TILE) * N_EXPERTS * (SUBLANES - 1)
    n_blocks = -(-(max_used + N_EXPERTS * (bm - 1)) // bm)
    first_row = jnp.arange(n_blocks, dtype=pad_end.dtype) * bm
    blk_e = jnp.minimum(jnp.sum(pad_end[None, :] <= first_row[:, None], axis=1), N_EXPERTS - 1).astype(i32)
    n_used = (pad_end[-1] // bm).astype(i32).reshape(1)
    return pstart, blk_e, n_used, n_blocks * bm


def _pad_time(t, n_batch, n_t, t_pad):
    w = t.shape[-1]
    return jnp.pad(t.reshape(n_batch, n_t, w), ((0, 0), (0, t_pad - n_t), (0, 0))).reshape(n_batch * t_pad, w)


def _block_diag_weight(w):
    h = w.shape[0]
    eye = jnp.eye(h, dtype=w.dtype)
    return (eye[:, None, :, None] * w[:, :, None, :]).reshape(h * HEAD_DIM, h * HEAD_DIM)


def _rows8(*rows):
    width = rows[0].shape[-1]
    m = jnp.stack([r.reshape(width) for r in rows])
    return jnp.pad(m, ((0, SUBLANES - m.shape[0]), (0, 0)))


def _layer(h, p, moe, layer, alpha, st_s, bp, tp, bs, ts):
    n_p, n_s = bp * tp, bs * ts
    tpad = SAMPLE_T_PAD
    (ret_p, ret_s), (rw_p, rw_s), (lru_p, lru_s) = _in_proj(*h, p['w_in'].astype(bf16))
    unpad = lambda y: y.reshape(bs, tpad, -1)[:, :ts].reshape(n_s, -1)

    c_p = RET_CHUNK if tp % RET_CHUNK == 0 else tp
    cos_p, sin_p = _rope_tables(jnp.arange(tp, dtype=f32))
    cos_s, sin_s = _rope_tables(PAST_LEN + jnp.arange(tpad, dtype=f32))
    y_ret_p, sret_p = _retention(ret_p.reshape(bp, tp, 4 * RET_W), cos_p, sin_p, _retention_tables(c_p, c_p),
                                 jnp.zeros((1, bp, RET_H, HEAD_DIM, HEAD_DIM), f32), 0,
                                 nb=2 if bp % 2 == 0 else 1, length=c_p)
    y_ret_s, sret_s = _retention(_pad_time(ret_s, bs, ts, tpad).reshape(bs, tpad, 4 * RET_W), cos_s, sin_s,
                                 _retention_tables(tpad, ts), st_s['ret'], layer, nb=SUBLANES, length=tpad)
    y_ret = (y_ret_p.reshape(n_p, RET_W), unpad(y_ret_s))

    lora = jnp.zeros((3, LANES, RWKV_W), f32)
    lora = lora.at[0, 0:32].set(p['w_up']).at[1, 32:64].set(p['a_up']).at[2, 64:128].set(p['g_up'])
    vec = _rows8(p['w0'], p['a0'], p['k_k'], p['k_a'], p['r_k'], p['gn_g'], p['gn_b'])
    mix = p['mix'].reshape(1, RWKV_PROJ)
    rows = RWKV_ROWS
    y_rwkv_p, srw_p, shift_p = _rwkv(rw_p, jnp.zeros((bp, 1, RWKV_PROJ), f32), jnp.ones((rows, RWKV_W), f32), mix,
                                     vec, lora, _chunk_tri(rows, RWKV_CHUNK),
                                     jnp.zeros((1, bp, RWKV_H, HEAD_DIM, HEAD_DIM), f32), 0,
                                     n_groups=bp, n_steps=tp // rows, rows=rows, length=RWKV_CHUNK, chain=True)
    valid_s = (jnp.arange(tpad) < ts).astype(f32)[:, None]
    rw_s3 = rw_s.reshape(bs, ts, RWKV_PROJ)
    prev_s = jnp.concatenate([st_s['shift'][:, None, :], rw_s3[:, :-1]], axis=1).reshape(n_s, RWKV_PROJ)
    seq_per_blk = rows // tpad
    y_rwkv_s, srw_s = _rwkv(_pad_time(rw_s, bs, ts, tpad), _pad_time(prev_s, bs, ts, tpad),
                            jnp.tile(jnp.broadcast_to(valid_s, (tpad, RWKV_W)), (seq_per_blk, 1)),
                            mix, vec, lora, _chunk_tri(rows, tpad), st_s['rwkv'], layer,
                            n_groups=bs // seq_per_blk, n_steps=1, rows=rows, length=tpad, chain=False)
    y_rwkv = (y_rwkv_p, unpad(y_rwkv_s))

    lvec = _rows8(p['conv_w'][0], p['conv_w'][1], p['conv_w'][2], p['conv_w'][3],
                  p['conv_b'], p['ba'], p['bx'], p['lam'])
    wa_bd = _block_diag_weight(p['wa']).astype(bf16)
    wx_bd = _block_diag_weight(p['wx']).astype(bf16)
    l_l = LRU_CHUNK if tp % LRU_CHUNK == 0 else tp
    y_lru_p, h_p, tail_p = _lru(lru_p, jnp.ones((l_l, LRU_W), f32), lvec, wa_bd, wx_bd,
                                jnp.zeros((bp, SUBLANES, LRU_W), f32), jnp.zeros((bp, SUBLANES, LRU_W), f32),
                                n_batch=bp, n_chunks=tp // l_l, nb=1, length=l_l, out_rows=n_p)
    conv0_s = jnp.pad(st_s['conv'], ((0, 0), (SUBLANES - (CONV_WIDTH - 1), 0), (0, 0)))
    h0_s = jnp.broadcast_to(st_s['lru'][:, None, :], (bs, SUBLANES, LRU_W))
    y_lru_s, h_s, _ = _lru(_pad_time(lru_s, bs, ts, tpad), jnp.broadcast_to(valid_s, (tpad, LRU_W)),
                           lvec, wa_bd, wx_bd, conv0_s, h0_s,
                           n_batch=bs, n_chunks=1, nb=SUBLANES, length=tpad, out_rows=bs * tpad)
    y_lru = (y_lru_p, unpad(y_lru_s))

    w_router = jnp.pad(p['w_router'], ((0, 0), (0, LANES - N_EXPERTS)))
    b_router = jnp.pad(p['b_router'], (0, LANES - N_EXPERTS), constant_values=-1e30).reshape(1, LANES)
    x1, lpos, gates, meta, cnt = _out_proj_router(y_ret, y_rwkv, y_lru, h, p['w_out'].astype(bf16),
                                                  _rows8(p['ln1_g'], p['ln1_b']), w_router, b_router, alpha)

    counts = cnt[0, :N_EXPERTS].astype(i32)
    pstart, blk_e, n_used, n_rows = _block_tables(counts, n_p + n_s)
    xs = _dispatch(pstart, counts, n_used, meta, lpos, x1, n_rows)
    yb = _experts(blk_e, n_used, xs, moe['w_gu'], moe['b_gu'], moe['w_down'], moe['b_down'], layer)
    x2 = _combine(pstart, meta, lpos, gates, x1, _rows8(p['ln2_g'], p['ln2_b']), yb, alpha, n_p // TOKEN_TILE)

    keep = CONV_WIDTH - 1
    assert tp >= SUBLANES and ts >= keep
    conv_s = lru_s.reshape(bs, ts, 2 * LRU_W)[:, ts - keep:, LRU_W:]
    new_p = (sret_p, srw_p, shift_p[:, 0], h_p[:, 0], tail_p[:, SUBLANES - keep:])
    new_s = (sret_s, srw_s, rw_s3[:, -1], h_s[:, 0], conv_s)
    return x2, new_p, new_s


def kernel(x_prompt, x_sample, state_ret, state_rwkv, state_rwkv_shift, state_lru, state_conv,
           w_in, w_out, ln1_g, ln1_b, ln2_g, ln2_b,
           rwkv_mix, rwkv_w0, rwkv_w_up, rwkv_a0, rwkv_a_up, rwkv_g_up, rwkv_k_k, rwkv_k_a, rwkv_r_k,
           rwkv_gn_g, rwkv_gn_b, lru_conv_w, lru_conv_b, lru_wa, lru_ba, lru_wx, lru_bx, lru_lambda,
           moe_w_router, moe_b_router, moe_w_gate_up, moe_b_gate_up, moe_w_down, moe_b_down):
    bp, tp, _ = x_prompt.shape
    bs, ts, _ = x_sample.shape
    depth = w_in.shape[0]
    alpha = (2.0 * depth) ** 0.25
    moe = {'w_gu': moe_w_gate_up, 'b_gu': moe_b_gate_up, 'w_down': moe_w_down, 'b_down': moe_b_down}
    h = (x_prompt.reshape(bp * tp, D_MODEL), x_sample.reshape(bs * ts, D_MODEL))
    new_p, new_s = [], []
    for l in range(depth):
        p = {'w_in': w_in[l], 'w_out': w_out[l], 'ln1_g': ln1_g[l], 'ln1_b': ln1_b[l],
             'ln2_g': ln2_g[l], 'ln2_b': ln2_b[l], 'mix': rwkv_mix[l], 'w0': rwkv_w0[l],
             'w_up': rwkv_w_up[l], 'a0': rwkv_a0[l], 'a_up': rwkv_a_up[l], 'g_up': rwkv_g_up[l],
             'k_k': rwkv_k_k[l], 'k_a': rwkv_k_a[l], 'r_k': rwkv_r_k[l], 'gn_g': rwkv_gn_g[l],
             'gn_b': rwkv_gn_b[l], 'conv_w': lru_conv_w[l], 'conv_b': lru_conv_b[l], 'wa': lru_wa[l],
             'ba': lru_ba[l], 'wx': lru_wx[l], 'bx': lru_bx[l], 'lam': lru_lambda[l],
             'w_router': moe_w_router[l], 'b_router': moe_b_router[l]}
        st_s = {'ret': state_ret, 'rwkv': state_rwkv, 'shift': state_rwkv_shift[l],
                'lru': state_lru[l], 'conv': state_conv[l]}
        h, sp, ss = _layer(h, p, moe, l, alpha, st_s, bp, tp, bs, ts)
        new_p.append(sp)
        new_s.append(ss)
    outs = [h[0].reshape(bp, tp, D_MODEL), h[1].reshape(bs, ts, D_MODEL)]
    for i in range(5):
        outs.append(jnp.stack([s[i] for s in new_p]))
        outs.append(jnp.stack([s[i] for s in new_s]))
    return tuple(outs)
```

```python
import functools

import jax
import jax.numpy as jnp
from jax import lax
from jax.experimental import pallas as pl
from jax.experimental.pallas import tpu as pltpu

f32 = jnp.float32
bf16 = jnp.bfloat16
i32 = jnp.int32
u32 = jnp.uint32

D_MODEL = 1024
HEAD_DIM = 64
RET_W = 256
RET_H = 4
RET_CHUNK = 128
ROPE_BASE = 10000.0
RWKV_W = 512
RWKV_H = 8
RWKV_PROJ = 1664
RWKV_LORA_COL = 1536
RWKV_GN_EPS = 64e-5
RWKV_CHUNK = 64
RWKV_ROWS = 128
LRU_W = 256
LRU_C = 8.0
LRU_CHUNK = 256
CONV_WIDTH = 4
D_PROJ = 3200
N_EXPERTS = 32
TOP_K = 4
D_EXPERT = 1024
SWIGLU_LIMIT = 7.0
SWIGLU_ALPHA = 1.702
LN_EPS = 1e-5
PAST_LEN = 16384.0

LANES = 128
SUBLANES = 8
SAMPLE_T_PAD = 8
TOKEN_TILE = 512
MOE_BLOCK = 512
LOCAL_ROWS = TOP_K * TOKEN_TILE + N_EXPERTS * SUBLANES
RUN_PIECES = tuple(SUBLANES << j for j in reversed(range((TOKEN_TILE // SUBLANES).bit_length())))
WAIT_PIECES = tuple(SUBLANES << j for j in reversed(range((LOCAL_ROWS // SUBLANES).bit_length())))
VMEM_LIMIT = 56 * 1024 * 1024

_NT = (((1,), (1,)), ((), ()))
_TN = (((0,), (0,)), ((), ()))


def _params(*sem):
    return pltpu.CompilerParams(dimension_semantics=sem, vmem_limit_bytes=VMEM_LIMIT)


def _mm(a, b):
    return jnp.dot(a.astype(bf16), b.astype(bf16), preferred_element_type=f32)


def _mm_nt(a, b):
    return lax.dot_general(a.astype(bf16), b.astype(bf16), _NT, preferred_element_type=f32)


def _mm_tn(a, b):
    return lax.dot_general(a.astype(bf16), b.astype(bf16), _TN, preferred_element_type=f32)


def _mm3(a, b):
    a_hi, b_hi = a.astype(bf16), b.astype(bf16)
    a_lo = (a - a_hi.astype(f32)).astype(bf16)
    b_lo = (b - b_hi.astype(f32)).astype(bf16)
    dot = functools.partial(jnp.dot, preferred_element_type=f32)
    return dot(a_hi, b_hi) + dot(a_hi, b_lo) + dot(a_lo, b_hi)


def _pack_halves(x):
    w = x.shape[-1] // 2
    lo = lax.bitcast_convert_type(x[:, :w], u32)
    hi = lax.bitcast_convert_type(x[:, w:], u32)
    return (lo >> 16) | hi


def _unpack_halves(p):
    return (lax.bitcast_convert_type(p << 16, f32),
            lax.bitcast_convert_type(p & jnp.uint32(0xFFFF0000), f32))


def _softplus(x):
    return jnp.maximum(x, 0.0) + jnp.log(1.0 + jnp.exp(-jnp.abs(x)))


def _sigmoid(x):
    return 1.0 / (1.0 + jnp.exp(-x))


def _half_masks():
    lane = lax.broadcasted_iota(i32, (1, LANES), 1)
    m0 = (lane < HEAD_DIM).astype(f32)
    return m0, 1.0 - m0


def _seg_mean(x, m0, m1):
    s0 = jnp.sum(x * m0, axis=-1, keepdims=True)
    s1 = jnp.sum(x * m1, axis=-1, keepdims=True)
    return (m0 * s0 + m1 * s1) * (1.0 / HEAD_DIM)


def _seg_sum(x, m0, m1):
    s0 = jnp.sum(x * m0, axis=-1, keepdims=True)
    s1 = jnp.sum(x * m1, axis=-1, keepdims=True)
    return m0 * s0 + m1 * s1


def _block_diag_mask():
    r = lax.broadcasted_iota(i32, (LANES, LANES), 0) // HEAD_DIM
    c = lax.broadcasted_iota(i32, (LANES, LANES), 1) // HEAD_DIM
    return (r == c).astype(f32)


def _layer_norm_rows(z, g, b):
    mu = jnp.mean(z, axis=-1, keepdims=True)
    zc = z - mu
    var = jnp.mean(zc * zc, axis=-1, keepdims=True)
    return zc * lax.rsqrt(var + LN_EPS) * g + b


def _group_specs(width, n_prompt_tiles):
    return [pl.BlockSpec((TOKEN_TILE, width), lambda i, *_: (jnp.minimum(i, n_prompt_tiles - 1), 0)),
            pl.BlockSpec((TOKEN_TILE, width), lambda i, *_: (jnp.maximum(i - n_prompt_tiles, 0), 0))]


def _pick_group(p_ref, s_ref, n_prompt_tiles):
    return jnp.where(pl.program_id(0) >= n_prompt_tiles, s_ref[...], p_ref[...])


def _n_tiles(h_p, h_s):
    assert h_p.shape[0] % TOKEN_TILE == 0 and h_s.shape[0] % TOKEN_TILE == 0, (h_p.shape, h_s.shape)
    return h_p.shape[0] // TOKEN_TILE, h_s.shape[0] // TOKEN_TILE


def _store_group(p_ref, s_ref, value, n_prompt_tiles):
    @pl.when(pl.program_id(0) < n_prompt_tiles)
    def _():
        p_ref[...] = value

    @pl.when(pl.program_id(0) >= n_prompt_tiles)
    def _():
        s_ref[...] = value


def _in_proj_kernel(xp_ref, xs_ref, w_ref, retp_ref, rets_ref, rwp_ref, rws_ref, lrup_ref, lrus_ref, *,
                    n_prompt_tiles):
    c0, c1 = 4 * RET_W, 4 * RET_W + RWKV_PROJ
    dot = functools.partial(jnp.dot, preferred_element_type=f32)

    def project(x_ref, ret_ref, rw_ref, lru_ref):
        xb = x_ref[...].astype(bf16)
        ret_ref[...] = dot(xb, w_ref[:, :c0])
        rw_ref[...] = dot(xb, w_ref[:, c0:c1])
        lru_ref[...] = dot(xb, w_ref[:, c1:])

    @pl.when(pl.program_id(0) < n_prompt_tiles)
    def _():
        project(xp_ref, retp_ref, rwp_ref, lrup_ref)

    @pl.when(pl.program_id(0) >= n_prompt_tiles)
    def _():
        project(xs_ref, rets_ref, rws_ref, lrus_ref)


def _in_proj(h_p, h_s, w_bf):
    npt, nst = _n_tiles(h_p, h_s)
    widths = (4 * RET_W, RWKV_PROJ, 2 * LRU_W)
    out = pl.pallas_call(
        functools.partial(_in_proj_kernel, n_prompt_tiles=npt),
        grid=(npt + nst,),
        in_specs=_group_specs(D_MODEL, npt) + [
                  pl.BlockSpec((D_MODEL, D_PROJ), lambda i: (0, 0))],
        out_specs=[s for w in widths for s in _group_specs(w, npt)],
        out_shape=[jax.ShapeDtypeStruct((rows, w), f32) for w in widths for rows in (h_p.shape[0], h_s.shape[0])],
        compiler_params=_params("arbitrary"),
        name="in_proj",
    )(h_p, h_s, w_bf)
    return out[0:2], out[2:4], out[4:6]


def _rope_tables(pos):
    half = HEAD_DIM // 2
    inv = ROPE_BASE ** (-jnp.arange(half, dtype=f32) / half)
    ang = pos[:, None] * inv[None, :]
    cos, sin = jnp.cos(ang), jnp.sin(ang)
    cos_f = jnp.tile(jnp.concatenate([cos, cos], axis=-1), (1, RET_H))
    sin_f = jnp.tile(jnp.concatenate([-sin, sin], axis=-1), (1, RET_H))
    return cos_f, sin_f


def _retention_tables(length, n_valid):
    lg = jnp.log1p(-jnp.exp2(-5.0 - jnp.arange(RET_H, dtype=f32)))
    idx = jnp.arange(length, dtype=f32)
    rel = idx[:, None] - idx[None, :]
    mask = jnp.where(rel[None] >= 0, jnp.exp(jnp.maximum(rel, 0.0)[None] * lg[:, None, None]), 0.0)
    q_dec = jnp.exp((idx[:, None] + 1.0) * lg[None, :])
    k_dec = jnp.where(idx[:, None] < n_valid, jnp.exp((n_valid - 1.0 - idx)[:, None] * lg[None, :]), 0.0)
    c_dec = jnp.exp(n_valid * lg)[None, :]
    rep = lambda t: jnp.repeat(t, HEAD_DIM, axis=-1)
    return mask, rep(q_dec), rep(k_dec), rep(c_dec)


def _retention_kernel(ret_ref, cos_ref, sin_ref, mask_ref, qdec_ref, kdec_ref, cdec_ref, s0_ref,
                      y_ref, s_ref, pair_ref, *, nb):
    @pl.when(pl.program_id(1) == 0)
    def _():
        _load_head_pairs(pair_ref, s0_ref)

    m0, m1 = _half_masks()
    lane = lax.broadcasted_iota(i32, (1, LANES), 1)
    first_half = (lane % HEAD_DIM) < (HEAD_DIM // 2)
    bd = _block_diag_mask()

    def rope(x, cs, sn):
        swapped = jnp.where(first_half, pltpu.roll(x, LANES - HEAD_DIM // 2, 1), pltpu.roll(x, HEAD_DIM // 2, 1))
        return x * cs + swapped * sn

    for j in range(nb):
        for p in range(RET_H // 2):
            cols = pl.ds(p * LANES, LANES)
            cs, sn = cos_ref[:, cols], sin_ref[:, cols]
            q2 = rope(ret_ref[j, :, pl.ds(p * LANES, LANES)], cs, sn)
            k2 = rope(ret_ref[j, :, pl.ds(RET_W + p * LANES, LANES)], cs, sn) * (HEAD_DIM ** -0.5)
            v2 = ret_ref[j, :, pl.ds(2 * RET_W + p * LANES, LANES)]
            g2 = ret_ref[j, :, pl.ds(3 * RET_W + p * LANES, LANES)]
            state = pair_ref[j, p]
            o2 = _mm(q2, state) * qdec_ref[:, cols]
            for hh, m in enumerate((m0, m1)):
                sc = _mm_nt(q2 * m, k2) * mask_ref[2 * p + hh]
                o2 = o2 + _mm(sc, v2) * m
            pair_ref[j, p] = state * cdec_ref[:, cols] + _mm_tn(k2 * kdec_ref[:, cols], v2) * bd
            mu = _seg_mean(o2, m0, m1)
            oc = o2 - mu
            var = _seg_mean(oc * oc, m0, m1)
            y_ref[j, :, cols] = g2 * _sigmoid(g2) * oc * lax.rsqrt(var + LN_EPS)

    @pl.when(pl.program_id(1) == pl.num_programs(1) - 1)
    def _():
        _store_head_pairs(s_ref, pair_ref)


def _load_head_pairs(pair_ref, s_ref):
    pair_ref[...] = jnp.zeros(pair_ref.shape, f32)
    for j in range(pair_ref.shape[0]):
        for p in range(pair_ref.shape[1]):
            pair_ref[j, p, pl.ds(0, HEAD_DIM), pl.ds(0, HEAD_DIM)] = s_ref[j, 2 * p]
            pair_ref[j, p, pl.ds(HEAD_DIM, HEAD_DIM), pl.ds(HEAD_DIM, HEAD_DIM)] = s_ref[j, 2 * p + 1]


def _store_head_pairs(s_ref, pair_ref):
    for j in range(pair_ref.shape[0]):
        for p in range(pair_ref.shape[1]):
            s_ref[j, 2 * p] = pair_ref[j, p, pl.ds(0, HEAD_DIM), pl.ds(0, HEAD_DIM)]
            s_ref[j, 2 * p + 1] = pair_ref[j, p, pl.ds(HEAD_DIM, HEAD_DIM), pl.ds(HEAD_DIM, HEAD_DIM)]


def _retention(ret, cos_f, sin_f, tables, s0, layer, *, nb, length):
    mask, q_dec, k_dec, c_dec = tables
    n_batch, t_len, _ = ret.shape
    blk = lambda g, c: (g, c, 0)
    const2 = lambda g, c: (0, 0)
    st = lambda g, c: (g, 0, 0, 0)
    return pl.pallas_call(
        functools.partial(_retention_kernel, nb=nb),
        grid=(n_batch // nb, t_len // length),
        in_specs=[pl.BlockSpec((nb, length, 4 * RET_W), blk),
                  pl.BlockSpec((length, RET_W), lambda g, c: (c, 0)),
                  pl.BlockSpec((length, RET_W), lambda g, c: (c, 0)),
                  pl.BlockSpec((RET_H, length, length), lambda g, c: (0, 0, 0)),
                  pl.BlockSpec((length, RET_W), const2),
                  pl.BlockSpec((length, RET_W), const2),
                  pl.BlockSpec((1, RET_W), const2),
                  pl.BlockSpec((None, nb, RET_H, HEAD_DIM, HEAD_DIM), lambda g, c: (layer, g, 0, 0, 0))],
        out_specs=[pl.BlockSpec((nb, length, RET_W), blk),
                   pl.BlockSpec((nb, RET_H, HEAD_DIM, HEAD_DIM), st)],
        out_shape=[jax.ShapeDtypeStruct((n_batch, t_len, RET_W), f32),
                   jax.ShapeDtypeStruct((n_batch, RET_H, HEAD_DIM, HEAD_DIM), f32)],
        scratch_shapes=[pltpu.VMEM((nb, RET_H // 2, LANES, LANES), f32)],
        compiler_params=_params("arbitrary", "arbitrary"),
        name="retention",
    )(ret, cos_f, sin_f, mask, q_dec, k_dec, c_dec, s0)


def _rwkv_kernel(rw_ref, aux_ref, valid_ref, mix_ref, vec_ref, lora_ref, tri_ref, s0_ref,
                 y_ref, s_ref, *rest, rows, length, chain):
    n_chunks = rows // length
    rw = rw_ref[...]
    if chain:
        last_ref, xs_ref, pair_ref = rest

        @pl.when(pl.program_id(1) == 0)
        def _():
            _load_head_pairs(pair_ref, s0_ref)
            xs_ref[pl.ds(0, SUBLANES), :] = jnp.broadcast_to(aux_ref[0], (SUBLANES, RWKV_PROJ))

        xs_ref[pl.ds(SUBLANES, rows), :] = rw
        prev = xs_ref[pl.ds(SUBLANES - 1, rows), :]
        xs_ref[pl.ds(0, SUBLANES), :] = rw[rows - SUBLANES:, :]
        last_ref[0] = rw[rows - 1:rows, :]
    else:
        pair_ref, = rest
        _load_head_pairs(pair_ref, s0_ref)
        prev = aux_ref[...]
    rwm = rw + (prev - rw) * mix_ref[...]

    valid = valid_ref[...]
    w0, a0, k_k, k_a, r_k, gn_g, gn_b = (vec_ref[pl.ds(i, 1), :] for i in range(7))
    lo = rwm[:, RWKV_LORA_COL:]
    lw = _mm3(jnp.tanh(lo), lora_ref[0])
    la = _mm3(lo, lora_ref[1])
    gate = _mm3(_sigmoid(lo), lora_ref[2])
    logw = -jnp.exp(-_softplus(-(w0 + lw)) - 0.5) * valid
    a = _sigmoid(a0 + la)
    r = rwm[:, :RWKV_W]
    kr = rwm[:, RWKV_W:2 * RWKV_W]
    vr = rwm[:, 2 * RWKV_W:3 * RWKV_W]
    kk_raw = kr * k_k
    kp = kr * (1.0 + (a - 1.0) * k_a) * valid
    cum = _mm3(tri_ref[...], logw)
    g_incl = jnp.exp(cum)
    g_inv = jnp.exp(-cum)
    g_prev = jnp.exp(cum - logw)
    g_end = jnp.concatenate(
        [jnp.broadcast_to(g_incl[(c + 1) * length - 1:(c + 1) * length, :], (length, RWKV_W)) for c in range(n_chunks)],
        axis=0)

    m0, m1 = _half_masks()
    bd = _block_diag_mask()
    stacked = 2 * rows
    ri = lax.broadcasted_iota(i32, (stacked, stacked), 0)
    ci = lax.broadcasted_iota(i32, (stacked, stacked), 1)
    same = (ri // length) == (ci // length)
    strict = jnp.logical_and(same, ci < ri)
    incl = jnp.logical_and(same, ci <= ri)
    eye = (ci == ri).astype(f32)
    stack = lambda t: jnp.concatenate([t * m0, t * m1], axis=0)
    unstack = lambda t: t[:rows] + t[rows:]
    pairs = range(RWKV_H // 2)

    pre = []
    for p in pairs:
        sl = slice(p * LANES, (p + 1) * LANES)
        kk2 = kk_raw[:, sl]
        kk2 = kk2 * lax.rsqrt(jnp.maximum(_seg_sum(kk2 * kk2, m0, m1), 1e-24)) * valid[:, sl]
        d = dict(sl=sl, r2=r[:, sl], v2=vr[:, sl], kp2=kp[:, sl], ge=g_end[:, sl])
        d['kh'] = kk2 * g_prev[:, sl]
        rh = d['r2'] * g_incl[:, sl]
        bt = kk2 * a[:, sl] * g_inv[:, sl]
        kt = d['kp2'] * g_inv[:, sl]
        d['b_end'] = bt * d['ge']
        d['k_end'] = kt * d['ge']
        d['kh_s'], d['rh_s'], d['v_s'] = stack(d['kh']), stack(rh), stack(d['v2'])
        gram = _mm_nt(jnp.concatenate([d['kh_s'], d['rh_s']], axis=0), jnp.concatenate([stack(bt), stack(kt)], axis=0))
        d['x'] = -jnp.where(strict, gram[:stacked, :stacked], 0.0)
        d['a_k'] = jnp.where(strict, gram[:stacked, stacked:], 0.0)
        d['r_b'] = jnp.where(incl, gram[stacked:, :stacked], 0.0)
        d['r_k'] = jnp.where(incl, gram[stacked:, stacked:], 0.0)
        d['t'] = eye + d['x']
        pre.append(d)
    cover = 2
    while cover < length:
        for d in pre:
            d['x'] = _mm(d['x'], d['x'])
            d['t'] = d['t'] + _mm(d['t'], d['x'])
        cover *= 2
    for d in pre:
        rhs0 = -_mm(d['a_k'], d['v_s'])
        tz = _mm(d['t'], jnp.concatenate([rhs0, d['kh_s']], axis=1))
        z0_s = tz[:, :LANES]
        rbz = _mm(d['r_b'], tz)
        d['z0'] = unstack(z0_s)
        d['q'] = unstack(d['rh_s'] - rbz[:, LANES:])
        d['y0'] = unstack(rbz[:, :LANES] + _mm(d['r_k'], d['v_s']))
        d['w'] = unstack(_mm_tn(d['t'], stack(d['b_end'])))

    for p, d in zip(pairs, pre):
        sl, ge, kh, v2 = d['sl'], d['ge'], d['kh'], d['v2']
        state = pair_ref[0, p] if chain else None
        ys = []
        for c in range(n_chunks):
            cr = slice(c * length, (c + 1) * length)
            if not chain:
                state = pair_ref[c, p]
            ys.append(_mm_nt(d['q'][cr], state) + d['y0'][cr])
            n_c = (_mm_tn(d['z0'][cr], d['b_end'][cr]) + _mm_tn(v2[cr], d['k_end'][cr])) * bd
            kw = _mm_tn(kh[cr], d['w'][cr]) * bd
            state = state * ge[(c + 1) * length - 1:(c + 1) * length, :] - _mm(state, kw) + n_c
            if not chain:
                pair_ref[c, p] = state
        if chain:
            pair_ref[0, p] = state
        y2 = jnp.concatenate(ys, axis=0) if n_chunks > 1 else ys[0]

        mu = _seg_mean(y2, m0, m1)
        yc = y2 - mu
        var = _seg_mean(yc * yc, m0, m1)
        yn = yc * lax.rsqrt(var + RWKV_GN_EPS) * gn_g[:, sl] + gn_b[:, sl]
        bonus = _seg_sum(d['r2'] * d['kp2'] * r_k[:, sl], m0, m1) * v2
        y_ref[:, pl.ds(p * LANES, LANES)] = (yn + bonus) * gate[:, sl]

    if chain:
        @pl.when(pl.program_id(1) == pl.num_programs(1) - 1)
        def _():
            _store_head_pairs(s_ref, pair_ref)
    else:
        _store_head_pairs(s_ref, pair_ref)


def _rwkv(rw, aux, valid, mix, vec, lora, tri, s0, layer, *, n_groups, n_steps, rows, length, chain):
    blk = lambda g, c: (g * n_steps + c, 0)
    const2 = lambda g, c: (0, 0)
    st = lambda g, c: (g, 0, 0, 0)
    n_state = 1 if chain else rows // length
    aux_spec = (pl.BlockSpec((1, 1, RWKV_PROJ), lambda g, c: (g, 0, 0)) if chain
                else pl.BlockSpec((rows, RWKV_PROJ), blk))
    return pl.pallas_call(
        functools.partial(_rwkv_kernel, rows=rows, length=length, chain=chain),
        grid=(n_groups, n_steps),
        in_specs=[pl.BlockSpec((rows, RWKV_PROJ), blk),
                  aux_spec,
                  pl.BlockSpec((rows, RWKV_W), const2),
                  pl.BlockSpec((1, RWKV_PROJ), const2),
                  pl.BlockSpec((SUBLANES, RWKV_W), const2),
                  pl.BlockSpec((3, LANES, RWKV_W), lambda g, c: (0, 0, 0)),
                  pl.BlockSpec((rows, rows), const2),
                  pl.BlockSpec((None, n_state, RWKV_H, HEAD_DIM, HEAD_DIM), lambda g, c: (layer, g, 0, 0, 0))],
        out_specs=[pl.BlockSpec((rows, RWKV_W), blk),
                   pl.BlockSpec((n_state, RWKV_H, HEAD_DIM, HEAD_DIM), st)]
        + ([pl.BlockSpec((1, 1, RWKV_PROJ), lambda g, c: (g, 0, 0))] if chain else []),
        out_shape=[jax.ShapeDtypeStruct((n_groups * n_steps * rows, RWKV_W), f32),
                   jax.ShapeDtypeStruct((n_groups * n_state, RWKV_H, HEAD_DIM, HEAD_DIM), f32)]
        + ([jax.ShapeDtypeStruct((n_groups, 1, RWKV_PROJ), f32)] if chain else []),
        scratch_shapes=([pltpu.VMEM((rows + SUBLANES, RWKV_PROJ), f32)] if chain else [])
        + [pltpu.VMEM((n_state, RWKV_H // 2, LANES, LANES), f32)],
        compiler_params=_params("arbitrary", "arbitrary"),
        name="rwkv7",
    )(rw, aux, valid, mix, vec, lora, tri, s0)


def _chunk_tri(rows, length):
    idx = jnp.arange(rows)
    same = (idx[:, None] // length) == (idx[None, :] // length)
    return jnp.logical_and(same, idx[None, :] <= idx[:, None]).astype(f32)


def _lru_kernel(lru_ref, valid_ref, vec_ref, wa_ref, wx_ref, conv0_ref, h0_ref,
                y_ref, h_ref, tail_ref, xext_ref, *, nb, length):
    @pl.when(pl.program_id(1) == 0)
    def _():
        h_ref[...] = h0_ref[...]
        xext_ref[:, pl.ds(0, SUBLANES), :] = conv0_ref[...]

    valid = valid_ref[...] > 0.5
    row = lax.broadcasted_iota(i32, (length, LRU_W), 0)
    cw = [vec_ref[pl.ds(i, 1), :] for i in range(CONV_WIDTH)]
    cb, ba, bx, lam = (vec_ref[pl.ds(i, 1), :] for i in range(CONV_WIDTH, CONV_WIDTH + 4))
    sp = _softplus(-lam)
    for j in range(nb):
        rows = pl.ds(j * length, length)
        gbr = lru_ref[rows, pl.ds(0, LRU_W)]
        x = lru_ref[rows, pl.ds(LRU_W, LRU_W)]
        xext_ref[j, pl.ds(SUBLANES, length), :] = x
        xc = cb + x * cw[CONV_WIDTH - 1]
        for t in range(CONV_WIDTH - 1):
            xc = xc + xext_ref[j, pl.ds(SUBLANES - (CONV_WIDTH - 1) + t, length), :] * cw[t]
        xext_ref[j, pl.ds(0, SUBLANES), :] = x[length - SUBLANES:, :]
        tail_ref[j] = x[length - SUBLANES:, :]
        gate_a = _sigmoid(_mm(xc, wa_ref[...]) + ba)
        gate_x = _sigmoid(_mm(xc, wx_ref[...]) + bx)
        log_a = -LRU_C * gate_a * sp
        a = jnp.exp(log_a)
        b = xc * gate_x * jnp.sqrt(1.0 - jnp.exp(2.0 * log_a))
        a = jnp.where(valid, a, 1.0)
        b = jnp.where(valid, b, 0.0)
        shift = 1
        while shift < length:
            inside = row >= shift
            b = a * jnp.where(inside, pltpu.roll(b, shift, 0), 0.0) + b
            a = a * jnp.where(inside, pltpu.roll(a, shift, 0), 1.0)
            shift *= 2
        hs = a * h_ref[j, pl.ds(0, 1), :] + b
        h_ref[j] = jnp.broadcast_to(hs[length - 1:length, :], (SUBLANES, LRU_W))
        c = 0.7978845608028654
        gelu = 0.5 * gbr * (1.0 + jnp.tanh(c * (gbr + 0.044715 * gbr * gbr * gbr)))
        y_ref[rows, :] = hs * gelu


def _lru(lru, valid, vec, wa_bd, wx_bd, conv0, h0, *, n_batch, n_chunks, nb, length, out_rows):
    blk = lambda g, c: (g * n_chunks + c, 0)
    const2 = lambda g, c: (0, 0)
    st = lambda g, c: (g, 0, 0)
    return pl.pallas_call(
        functools.partial(_lru_kernel, nb=nb, length=length),
        grid=(n_batch // nb, n_chunks),
        in_specs=[pl.BlockSpec((nb * length, 2 * LRU_W), blk),
                  pl.BlockSpec((length, LRU_W), const2),
                  pl.BlockSpec((SUBLANES, LRU_W), const2),
                  pl.BlockSpec((LRU_W, LRU_W), const2),
                  pl.BlockSpec((LRU_W, LRU_W), const2),
                  pl.BlockSpec((nb, SUBLANES, LRU_W), st),
                  pl.BlockSpec((nb, SUBLANES, LRU_W), st)],
        out_specs=[pl.BlockSpec((nb * length, LRU_W), blk),
                   pl.BlockSpec((nb, SUBLANES, LRU_W), st),
                   pl.BlockSpec((nb, SUBLANES, LRU_W), st)],
        out_shape=[jax.ShapeDtypeStruct((out_rows, LRU_W), f32),
                   jax.ShapeDtypeStruct((n_batch, SUBLANES, LRU_W), f32),
                   jax.ShapeDtypeStruct((n_batch, SUBLANES, LRU_W), f32)],
        scratch_shapes=[pltpu.VMEM((nb, length + SUBLANES, LRU_W), f32)],
        compiler_params=_params("arbitrary", "arbitrary"),
        name="rg_lru",
    )(lru, valid, vec, wa_bd, wx_bd, conv0, h0)


def _out_proj_kernel(yrp_ref, yrs_ref, ywp_ref, yws_ref, ylp_ref, yls_ref, xp_ref, xs_ref,
                     w_ref, ln_ref, wr_ref, br_ref,
                     x1_ref, lpos_ref, g_ref, meta_ref, cnt_ref, *, alpha, n_prompt_tiles):
    @pl.when(pl.program_id(0) == 0)
    def _():
        cnt_ref[...] = jnp.zeros(cnt_ref.shape, f32)

    pick = functools.partial(_pick_group, n_prompt_tiles=n_prompt_tiles)
    mixed = (_mm(pick(yrp_ref, yrs_ref), w_ref[pl.ds(0, RET_W), :])
             + _mm(pick(ywp_ref, yws_ref), w_ref[pl.ds(RET_W, RWKV_W), :])
             + _mm(pick(ylp_ref, yls_ref), w_ref[pl.ds(RET_W + RWKV_W, LRU_W), :]))
    x1 = _layer_norm_rows(alpha * pick(xp_ref, xs_ref) + mixed, ln_ref[pl.ds(0, 1), :], ln_ref[pl.ds(1, 1), :])
    x1_ref[...] = x1
    logits = _mm3(x1, wr_ref[...]) + br_ref[...]
    tm = logits.shape[0]
    lane = lax.broadcasted_iota(i32, logits.shape, 1).astype(f32)
    top_v = jnp.zeros(logits.shape, f32)
    work = logits
    v_max = None
    onehots = []
    for k in range(TOP_K):
        v = jnp.max(work, axis=-1, keepdims=True)
        idx = jnp.min(jnp.where(work == v, lane, float(LANES)), axis=-1, keepdims=True)
        if k == 0:
            v_max = v
        hit = lane == idx
        onehots.append(hit.astype(f32))
        top_v = jnp.where(lane == k, jnp.exp(v - v_max), top_v)
        work = jnp.where(hit, -jnp.inf, work)
    g_ref[...] = top_v / jnp.sum(top_v, axis=-1, keepdims=True)

    total = onehots[0] + onehots[1] + onehots[2] + onehots[3]
    ri = lax.broadcasted_iota(i32, (tm, tm), 0)
    ci = lax.broadcasted_iota(i32, (tm, tm), 1)
    before = _mm((ci < ri).astype(f32), total)
    groups = jnp.floor((jnp.sum(total, axis=0, keepdims=True) + (SUBLANES - 1.0)) * (1.0 / SUBLANES))
    er = lax.broadcasted_iota(i32, (LANES, LANES), 0)
    ec = lax.broadcasted_iota(i32, (LANES, LANES), 1)
    run_len = groups * SUBLANES
    run_start = _mm(jnp.broadcast_to(groups, (SUBLANES, LANES)), (er < ec).astype(f32))[0:1] * SUBLANES
    lpos = jnp.full(logits.shape, -1.0, f32)
    for k in range(TOP_K):
        pos = jnp.sum(onehots[k] * (before + run_start), axis=-1, keepdims=True)
        lpos = jnp.where(lane == k, pos, lpos)
    lpos_ref[...] = lpos
    row = lax.broadcasted_iota(i32, (SUBLANES, LANES), 0)
    meta = jnp.where(row == 0, run_start, jnp.where(row == 1, run_len, jnp.where(row == 2, cnt_ref[...], 0.0)))
    meta_ref[...] = meta.astype(i32)
    cnt_ref[...] = cnt_ref[...] + run_len


def _out_proj_router(y_ret, y_rwkv, y_lru, h, w_out_bf, ln, w_router, b_router, alpha):
    npt, nst = _n_tiles(*h)
    n = h[0].shape[0] + h[1].shape[0]
    tm = TOKEN_TILE
    row = lambda i: (i, 0)
    const = lambda i: (0, 0)
    return pl.pallas_call(
        functools.partial(_out_proj_kernel, alpha=alpha, n_prompt_tiles=npt),
        grid=(npt + nst,),
        in_specs=_group_specs(RET_W, npt) + _group_specs(RWKV_W, npt) + _group_specs(LRU_W, npt)
        + _group_specs(D_MODEL, npt) + [
                  pl.BlockSpec((D_MODEL, D_MODEL), const),
                  pl.BlockSpec((SUBLANES, D_MODEL), const),
                  pl.BlockSpec((D_MODEL, LANES), const),
                  pl.BlockSpec((1, LANES), const)],
        out_specs=[pl.BlockSpec((tm, D_MODEL), row), pl.BlockSpec((tm, LANES), row), pl.BlockSpec((tm, LANES), row),
                   pl.BlockSpec((SUBLANES, LANES), row), pl.BlockSpec((SUBLANES, LANES), const)],
        out_shape=[jax.ShapeDtypeStruct((n, D_MODEL), f32),
                   jax.ShapeDtypeStruct((n, LANES), f32),
                   jax.ShapeDtypeStruct((n, LANES), f32),
                   jax.ShapeDtypeStruct((n // tm * SUBLANES, LANES), i32),
                   jax.ShapeDtypeStruct((SUBLANES, LANES), f32)],
        compiler_params=_params("arbitrary"),
        name="out_proj_router",
    )(*y_ret, *y_rwkv, *y_lru, *h, w_out_bf, ln, w_router, b_router)


def _for_each_run_piece(meta_ref, pstart_ref, fn):
    for e in range(N_EXPERTS):
        start, n = meta_ref[0, e], meta_ref[1, e]
        base = pstart_ref[e] + meta_ref[2, e]
        for sz in RUN_PIECES:
            done = n & ~(2 * sz - 1)

            @pl.when((n & sz) != 0)
            def _():
                fn(pl.multiple_of(start + done, SUBLANES), pl.multiple_of(base + done, SUBLANES), sz)


def _wait_run_rows(meta_ref, make_copy):
    total = meta_ref[1, 0]
    for e in range(1, N_EXPERTS):
        total = total + meta_ref[1, e]
    for sz in WAIT_PIECES:
        @pl.when((total & sz) != 0)
        def _():
            make_copy(sz).wait()


def _dispatch_kernel(pstart_ref, cnt_ref, n_used_ref, meta_ref, meta_prev_ref, lpos_ref, x1_ref, xs_hbm,
                     xl_ref, zero_ref, sems, zsem):
    tm = TOKEN_TILE
    bm = MOE_BLOCK
    i = pl.program_id(0)
    slot = i % 2

    @pl.when(i == 0)
    def _():
        zero_ref[...] = jnp.zeros(zero_ref.shape, u32)
        tail = lambda j: pltpu.make_async_copy(zero_ref, xs_hbm.at[pl.ds(j * bm, bm), :], zsem)

        def tail_start(j, carry):
            tail(j).start()
            return carry

        def tail_wait(j, carry):
            tail(j).wait()
            return carry

        n_blocks = xs_hbm.shape[0] // bm
        lax.fori_loop(n_used_ref[0], n_blocks, tail_start, 0)
        lax.fori_loop(n_used_ref[0], n_blocks, tail_wait, 0)
        for e in range(N_EXPERTS):
            lo = pstart_ref[e] + cnt_ref[e]
            n_groups = ((cnt_ref[e] + bm - 1) // bm * bm - cnt_ref[e]) // SUBLANES
            fill = lambda g: pltpu.make_async_copy(
                zero_ref.at[pl.ds(0, SUBLANES), :],
                xs_hbm.at[pl.ds(pl.multiple_of(lo + g * SUBLANES, SUBLANES), SUBLANES), :], zsem)

            def start(g, carry):
                fill(g).start()
                return carry

            def wait(g, carry):
                fill(g).wait()
                return carry

            lax.fori_loop(0, n_groups, start, 0)
            lax.fori_loop(0, n_groups, wait, 0)

    lpos_t = lpos_ref[...].T
    srow = lax.broadcasted_iota(i32, (LOCAL_ROWS, tm), 0).astype(f32)
    perm = jnp.zeros((LOCAL_ROWS, tm), f32)
    for k in range(TOP_K):
        perm = jnp.where(srow == lpos_t[k:k + 1, :], 1.0, perm)
    xl_ref[slot] = _pack_halves(jnp.dot(perm.astype(bf16), x1_ref[...].astype(bf16), preferred_element_type=f32))

    def send(local_row, sorted_row, n_rows):
        pltpu.make_async_copy(xl_ref.at[slot, pl.ds(local_row, n_rows), :],
                              xs_hbm.at[pl.ds(sorted_row, n_rows), :], sems.at[slot]).start()

    _for_each_run_piece(meta_ref, pstart_ref, send)

    def sent(s):
        return lambda n_rows: pltpu.make_async_copy(xl_ref.at[s, pl.ds(0, n_rows), :],
                                                    xs_hbm.at[pl.ds(0, n_rows), :], sems.at[s])

    @pl.when(i > 0)
    def _():
        _wait_run_rows(meta_prev_ref, sent(1 - slot))

    @pl.when(i == pl.num_programs(0) - 1)
    def _():
        _wait_run_rows(meta_ref, sent(slot))


def _dispatch(pstart, counts, n_used, meta, lpos, x1, n_rows):
    n = x1.shape[0]
    tm = TOKEN_TILE
    smem_tile = lambda f: pl.BlockSpec((SUBLANES, LANES), f, memory_space=pltpu.SMEM)
    grid_spec = pltpu.PrefetchScalarGridSpec(
        num_scalar_prefetch=3,
        grid=(n // tm,),
        in_specs=[smem_tile(lambda i, ps, ct, nu: (i, 0)),
                  smem_tile(lambda i, ps, ct, nu: (jnp.maximum(i - 1, 0), 0)),
                  pl.BlockSpec((tm, LANES), lambda i, ps, ct, nu: (i, 0)),
                  pl.BlockSpec((tm, D_MODEL), lambda i, ps, ct, nu: (i, 0))],
        out_specs=pl.BlockSpec(memory_space=pl.ANY),
        scratch_shapes=[pltpu.VMEM((2, LOCAL_ROWS, D_MODEL // 2), u32),
                        pltpu.VMEM((MOE_BLOCK, D_MODEL // 2), u32),
                        pltpu.SemaphoreType.DMA((2,)), pltpu.SemaphoreType.DMA(())],
    )
    return pl.pallas_call(
        _dispatch_kernel,
        grid_spec=grid_spec,
        out_shape=jax.ShapeDtypeStruct((n_rows, D_MODEL // 2), u32),
        compiler_params=_params("arbitrary"),
        name="moe_dispatch",
    )(pstart, counts, n_used, meta, meta, lpos, x1)


def _expert_kernel(blk_e_ref, n_used_ref, xs_ref, wgu_ref, bgu_ref, wdn_ref, bdn_ref,
                   out_ref, wgu_bf, wdn_bf):
    i = pl.program_id(0)
    prev = jnp.maximum(i - 1, 0)
    new_expert = jnp.logical_or(i == 0, blk_e_ref[i] != blk_e_ref[prev])
    used = i < n_used_ref[0]

    @pl.when(used)
    def _():
        @pl.when(new_expert)
        def _():
            wgu_bf[...] = wgu_ref[0, 0].astype(bf16)
            wdn_bf[...] = wdn_ref[0, 0].astype(bf16)

        x_lo, x_hi = _unpack_halves(xs_ref[...])
        half = D_MODEL // 2
        gu = (jnp.dot(x_lo.astype(bf16), wgu_bf[pl.ds(0, half), :], preferred_element_type=f32)
              + jnp.dot(x_hi.astype(bf16), wgu_bf[pl.ds(half, half), :], preferred_element_type=f32) + bgu_ref[0, 0])
        g = jnp.minimum(gu[:, :D_EXPERT], SWIGLU_LIMIT)
        u = jnp.clip(gu[:, D_EXPERT:], -SWIGLU_LIMIT, SWIGLU_LIMIT)
        hdn = (u + 1.0) * g * _sigmoid(SWIGLU_ALPHA * g)
        y = jnp.dot(hdn.astype(bf16), wdn_bf[...], preferred_element_type=f32) + bdn_ref[0, 0]
        out_ref[...] = _pack_halves(y.astype(bf16).astype(f32))

    @pl.when(jnp.logical_not(used))
    def _():
        out_ref[...] = jnp.zeros(out_ref.shape, u32)


def _experts(blk_e, n_used, xs, w_gu, b_gu, w_down, b_down, layer):
    n_blocks = blk_e.shape[0]
    bm = MOE_BLOCK
    by_e = lambda i, be, nu: (layer, be[i], 0, 0)
    x_blk = lambda i, be, nu: (jnp.minimum(i, nu[0] - 1), 0)
    grid_spec = pltpu.PrefetchScalarGridSpec(
        num_scalar_prefetch=2,
        grid=(n_blocks,),
        in_specs=[pl.BlockSpec((bm, D_MODEL // 2), x_blk),
                  pl.BlockSpec((1, 1, D_MODEL, 2 * D_EXPERT), by_e),
                  pl.BlockSpec((1, 1, 1, 2 * D_EXPERT), by_e),
                  pl.BlockSpec((1, 1, D_EXPERT, D_MODEL), by_e),
                  pl.BlockSpec((1, 1, 1, D_MODEL), by_e)],
        out_specs=pl.BlockSpec((bm, D_MODEL // 2), lambda i, be, nu: (i, 0)),
        scratch_shapes=[pltpu.VMEM((D_MODEL, 2 * D_EXPERT), bf16),
                        pltpu.VMEM((D_EXPERT, D_MODEL), bf16)],
    )
    depth = w_gu.shape[0]
    return pl.pallas_call(
        _expert_kernel,
        grid_spec=grid_spec,
        out_shape=jax.ShapeDtypeStruct((n_blocks * bm, D_MODEL // 2), u32),
        compiler_params=_params("arbitrary"),
        name="moe_experts",
    )(blk_e, n_used, xs, w_gu, b_gu.reshape(depth, N_EXPERTS, 1, 2 * D_EXPERT),
      w_down, b_down.reshape(depth, N_EXPERTS, 1, D_MODEL))


def _combine_kernel(pstart_ref, meta_ref, meta_next_ref, lpos_ref, gates_ref, x1_ref, ln_ref, yb_hbm,
                    outp_ref, outs_ref, yl_ref, sems, *, alpha, n_prompt_tiles):
    tm = TOKEN_TILE
    i = pl.program_id(0)
    slot = i % 2

    def fetch(meta, s):
        def recv(local_row, sorted_row, n_rows):
            pltpu.make_async_copy(yb_hbm.at[pl.ds(sorted_row, n_rows), :],
                                  yl_ref.at[s, pl.ds(local_row, n_rows), :], sems.at[s]).start()

        _for_each_run_piece(meta, pstart_ref, recv)

    @pl.when(i == 0)
    def _():
        yl_ref[...] = jnp.zeros(yl_ref.shape, u32)
        fetch(meta_ref, slot)

    @pl.when(i + 1 < pl.num_programs(0))
    def _():
        fetch(meta_next_ref, 1 - slot)

    _wait_run_rows(meta_ref, lambda n_rows: pltpu.make_async_copy(
        yb_hbm.at[pl.ds(0, n_rows), :], yl_ref.at[slot, pl.ds(0, n_rows), :], sems.at[slot]))

    lpos = lpos_ref[...]
    gates = gates_ref[...]
    scol = lax.broadcasted_iota(i32, (tm, LOCAL_ROWS), 1).astype(f32)
    weight = jnp.zeros((tm, LOCAL_ROWS), f32)
    for k in range(TOP_K):
        weight = jnp.where(scol == lpos[:, k:k + 1], gates[:, k:k + 1], weight)
    y_lo, y_hi = _unpack_halves(yl_ref[slot])
    weight = weight.astype(bf16)
    y = jnp.concatenate([jnp.dot(weight, y_lo.astype(bf16), preferred_element_type=f32),
                         jnp.dot(weight, y_hi.astype(bf16), preferred_element_type=f32)], axis=1)
    x2 = _layer_norm_rows(alpha * x1_ref[...] + y, ln_ref[pl.ds(0, 1), :], ln_ref[pl.ds(1, 1), :])
    _store_group(outp_ref, outs_ref, x2, n_prompt_tiles)


def _combine(pstart, meta, lpos, gates, x1, ln, yb, alpha, n_prompt_tiles):
    n = x1.shape[0]
    tm = TOKEN_TILE
    n_tiles = n // tm
    row = lambda i, ps: (i, 0)
    smem_tile = lambda f: pl.BlockSpec((SUBLANES, LANES), f, memory_space=pltpu.SMEM)
    grid_spec = pltpu.PrefetchScalarGridSpec(
        num_scalar_prefetch=1,
        grid=(n_tiles,),
        in_specs=[smem_tile(row),
                  smem_tile(lambda i, ps: (jnp.minimum(i + 1, n_tiles - 1), 0)),
                  pl.BlockSpec((tm, LANES), row),
                  pl.BlockSpec((tm, LANES), row),
                  pl.BlockSpec((tm, D_MODEL), row),
                  pl.BlockSpec((SUBLANES, D_MODEL), lambda i, ps: (0, 0)),
                  pl.BlockSpec(memory_space=pl.ANY)],
        out_specs=_group_specs(D_MODEL, n_prompt_tiles),
        scratch_shapes=[pltpu.VMEM((2, LOCAL_ROWS, D_MODEL // 2), u32), pltpu.SemaphoreType.DMA((2,))],
    )
    return pl.pallas_call(
        functools.partial(_combine_kernel, alpha=alpha, n_prompt_tiles=n_prompt_tiles),
        grid_spec=grid_spec,
        out_shape=[jax.ShapeDtypeStruct((n_prompt_tiles * tm, D_MODEL), f32),
                   jax.ShapeDtypeStruct((n - n_prompt_tiles * tm, D_MODEL), f32)],
        compiler_params=_params("arbitrary"),
        name="moe_combine",
    )(pstart, meta, meta, lpos, gates, x1, ln, yb)


def _block_tables(counts, n_tokens):
    bm = MOE_BLOCK
    padded = (counts + bm - 1) // bm * bm
    pad_end = jnp.cumsum(padded)
    pstart = (pad_end - padded).astype(i32)
    max_used = n_tokens * TOP_K + (n_tokens // TOKEN_TILE) * N_EXPERTS * (SUBLANES - 1)
    n_blocks = -(-(max_used + N_EXPERTS * (bm - 1)) // bm)
    first_row = jnp.arange(n_blocks, dtype=pad_end.dtype) * bm
    blk_e = jnp.minimum(jnp.sum(pad_end[None, :] <= first_row[:, None], axis=1), N_EXPERTS - 1).astype(i32)
    n_used = (pad_end[-1] // bm).astype(i32).reshape(1)
    return pstart, blk_e, n_used, n_blocks * bm


def _pad_time(t, n_batch, n_t, t_pad):
    w = t.shape[-1]
    return jnp.pad(t.reshape(n_batch, n_t, w), ((0, 0), (0, t_pad - n_t), (0, 0))).reshape(n_batch * t_pad, w)


def _block_diag_weight(w):
    h = w.shape[0]
    eye = jnp.eye(h, dtype=w.dtype)
    return (eye[:, None, :, None] * w[:, :, None, :]).reshape(h * HEAD_DIM, h * HEAD_DIM)


def _rows8(*rows):
    width = rows[0].shape[-1]
    m = jnp.stack([r.reshape(width) for r in rows])
    return jnp.pad(m, ((0, SUBLANES - m.shape[0]), (0, 0)))


def _layer(h, p, moe, layer, alpha, st_s, bp, tp, bs, ts):
    n_p, n_s = bp * tp, bs * ts
    tpad = SAMPLE_T_PAD
    (ret_p, ret_s), (rw_p, rw_s), (lru_p, lru_s) = _in_proj(*h, p['w_in'].astype(bf16))
    unpad = lambda y: y.reshape(bs, tpad, -1)[:, :ts].reshape(n_s, -1)

    c_p = RET_CHUNK if tp % RET_CHUNK == 0 else tp
    cos_p, sin_p = _rope_tables(jnp.arange(tp, dtype=f32))
    cos_s, sin_s = _rope_tables(PAST_LEN + jnp.arange(tpad, dtype=f32))
    y_ret_p, sret_p = _retention(ret_p.reshape(bp, tp, 4 * RET_W), cos_p, sin_p, _retention_tables(c_p, c_p),
                                 jnp.zeros((1, bp, RET_H, HEAD_DIM, HEAD_DIM), f32), 0,
                                 nb=max(d for d in (1, 2, 4, 8) if bp % d == 0), length=c_p)
    y_ret_s, sret_s = _retention(_pad_time(ret_s, bs, ts, tpad).reshape(bs, tpad, 4 * RET_W), cos_s, sin_s,
                                 _retention_tables(tpad, ts), st_s['ret'], layer, nb=SUBLANES, length=tpad)
    y_ret = (y_ret_p.reshape(n_p, RET_W), unpad(y_ret_s))

    lora = jnp.zeros((3, LANES, RWKV_W), f32)
    lora = lora.at[0, 0:32].set(p['w_up']).at[1, 32:64].set(p['a_up']).at[2, 64:128].set(p['g_up'])
    vec = _rows8(p['w0'], p['a0'], p['k_k'], p['k_a'], p['r_k'], p['gn_g'], p['gn_b'])
    mix = p['mix'].reshape(1, RWKV_PROJ)
    rows = RWKV_ROWS
    y_rwkv_p, srw_p, shift_p = _rwkv(rw_p, jnp.zeros((bp, 1, RWKV_PROJ), f32), jnp.ones((rows, RWKV_W), f32), mix,
                                     vec, lora, _chunk_tri(rows, RWKV_CHUNK),
                                     jnp.zeros((1, bp, RWKV_H, HEAD_DIM, HEAD_DIM), f32), 0,
                                     n_groups=bp, n_steps=tp // rows, rows=rows, length=RWKV_CHUNK, chain=True)
    valid_s = (jnp.arange(tpad) < ts).astype(f32)[:, None]
    rw_s3 = rw_s.reshape(bs, ts, RWKV_PROJ)
    prev_s = jnp.concatenate([st_s['shift'][:, None, :], rw_s3[:, :-1]], axis=1).reshape(n_s, RWKV_PROJ)
    seq_per_blk = rows // tpad
    y_rwkv_s, srw_s = _rwkv(_pad_time(rw_s, bs, ts, tpad), _pad_time(prev_s, bs, ts, tpad),
                            jnp.tile(jnp.broadcast_to(valid_s, (tpad, RWKV_W)), (seq_per_blk, 1)),
                            mix, vec, lora, _chunk_tri(rows, tpad), st_s['rwkv'], layer,
                            n_groups=bs // seq_per_blk, n_steps=1, rows=rows, length=tpad, chain=False)
    y_rwkv = (y_rwkv_p, unpad(y_rwkv_s))

    lvec = _rows8(p['conv_w'][0], p['conv_w'][1], p['conv_w'][2], p['conv_w'][3],
                  p['conv_b'], p['ba'], p['bx'], p['lam'])
    wa_bd = _block_diag_weight(p['wa']).astype(bf16)
    wx_bd = _block_diag_weight(p['wx']).astype(bf16)
    l_l = LRU_CHUNK if tp % LRU_CHUNK == 0 else tp
    y_lru_p, h_p, tail_p = _lru(lru_p, jnp.ones((l_l, LRU_W), f32), lvec, wa_bd, wx_bd,
                                jnp.zeros((bp, SUBLANES, LRU_W), f32), jnp.zeros((bp, SUBLANES, LRU_W), f32),
                                n_batch=bp, n_chunks=tp // l_l, nb=1, length=l_l, out_rows=n_p)
    conv0_s = jnp.pad(st_s['conv'], ((0, 0), (SUBLANES - (CONV_WIDTH - 1), 0), (0, 0)))
    h0_s = jnp.broadcast_to(st_s['lru'][:, None, :], (bs, SUBLANES, LRU_W))
    y_lru_s, h_s, _ = _lru(_pad_time(lru_s, bs, ts, tpad), jnp.broadcast_to(valid_s, (tpad, LRU_W)),
                           lvec, wa_bd, wx_bd, conv0_s, h0_s,
                           n_batch=bs, n_chunks=1, nb=SUBLANES, length=tpad, out_rows=bs * tpad)
    y_lru = (y_lru_p, unpad(y_lru_s))

    w_router = jnp.pad(p['w_router'], ((0, 0), (0, LANES - N_EXPERTS)))
    b_router = jnp.pad(p['b_router'], (0, LANES - N_EXPERTS), constant_values=-1e30).reshape(1, LANES)
    x1, lpos, gates, meta, cnt = _out_proj_router(y_ret, y_rwkv, y_lru, h, p['w_out'].astype(bf16),
                                                  _rows8(p['ln1_g'], p['ln1_b']), w_router, b_router, alpha)

    counts = cnt[0, :N_EXPERTS].astype(i32)
    pstart, blk_e, n_used, n_rows = _block_tables(counts, n_p + n_s)
    xs = _dispatch(pstart, counts, n_used, meta, lpos, x1, n_rows)
    yb = _experts(blk_e, n_used, xs, moe['w_gu'], moe['b_gu'], moe['w_down'], moe['b_down'], layer)
    x2 = _combine(pstart, meta, lpos, gates, x1, _rows8(p['ln2_g'], p['ln2_b']), yb, alpha, n_p // TOKEN_TILE)

    keep = CONV_WIDTH - 1
    assert tp >= SUBLANES and ts >= keep
    conv_s = lru_s.reshape(bs, ts, 2 * LRU_W)[:, ts - keep:, LRU_W:]
    new_p = (sret_p, srw_p, shift_p[:, 0], h_p[:, 0], tail_p[:, SUBLANES - keep:])
    new_s = (sret_s, srw_s, rw_s3[:, -1], h_s[:, 0], conv_s)
    return x2, new_p, new_s


def kernel(x_prompt, x_sample, state_ret, state_rwkv, state_rwkv_shift, state_lru, state_conv,
           w_in, w_out, ln1_g, ln1_b, ln2_g, ln2_b,
           rwkv_mix, rwkv_w0, rwkv_w_up, rwkv_a0, rwkv_a_up, rwkv_g_up, rwkv_k_k, rwkv_k_a, rwkv_r_k,
           rwkv_gn_g, rwkv_gn_b, lru_conv_w, lru_conv_b, lru_wa, lru_ba, lru_wx, lru_bx, lru_lambda,
           moe_w_router, moe_b_router, moe_w_gate_up, moe_b_gate_up, moe_w_down, moe_b_down):
    bp, tp, _ = x_prompt.shape
    bs, ts, _ = x_sample.shape
    depth = w_in.shape[0]
    alpha = (2.0 * depth) ** 0.25
    moe = {'w_gu': moe_w_gate_up, 'b_gu': moe_b_gate_up, 'w_down': moe_w_down, 'b_down': moe_b_down}
    h = (x_prompt.reshape(bp * tp, D_MODEL), x_sample.reshape(bs * ts, D_MODEL))
    new_p, new_s = [], []
    for l in range(depth):
        p = {'w_in': w_in[l], 'w_out': w_out[l], 'ln1_g': ln1_g[l], 'ln1_b': ln1_b[l],
             'ln2_g': ln2_g[l], 'ln2_b': ln2_b[l], 'mix': rwkv_mix[l], 'w0': rwkv_w0[l],
             'w_up': rwkv_w_up[l], 'a0': rwkv_a0[l], 'a_up': rwkv_a_up[l], 'g_up': rwkv_g_up[l],
             'k_k': rwkv_k_k[l], 'k_a': rwkv_k_a[l], 'r_k': rwkv_r_k[l], 'gn_g': rwkv_gn_g[l],
             'gn_b': rwkv_gn_b[l], 'conv_w': lru_conv_w[l], 'conv_b': lru_conv_b[l], 'wa': lru_wa[l],
             'ba': lru_ba[l], 'wx': lru_wx[l], 'bx': lru_bx[l], 'lam': lru_lambda[l],
             'w_router': moe_w_router[l], 'b_router': moe_b_router[l]}
        st_s = {'ret': state_ret, 'rwkv': state_rwkv, 'shift': state_rwkv_shift[l],
                'lru': state_lru[l], 'conv': state_conv[l]}
        h, sp, ss = _layer(h, p, moe, l, alpha, st_s, bp, tp, bs, ts)
        new_p.append(sp)
        new_s.append(ss)
    outs = [h[0].reshape(bp, tp, D_MODEL), h[1].reshape(bs, ts, D_MODEL)]
    for i in range(5):
        outs.append(jnp.stack([s[i] for s in new_p]))
        outs.append(jnp.stack([s[i] for s in new_s]))
    return tuple(outs)
```

```python
import functools

import jax
import jax.numpy as jnp
from jax import lax
from jax.experimental import pallas as pl
from jax.experimental.pallas import tpu as pltpu

f32 = jnp.float32
bf16 = jnp.bfloat16
i32 = jnp.int32
u32 = jnp.uint32

D_MODEL = 1024
HEAD_DIM = 64
RET_W = 256
RET_H = 4
RET_CHUNK = 128
ROPE_BASE = 10000.0
RWKV_W = 512
RWKV_H = 8
RWKV_PROJ = 1664
RWKV_LORA_COL = 1536
RWKV_GN_EPS = 64e-5
RWKV_CHUNK = 64
RWKV_ROWS = 128
LRU_W = 256
LRU_C = 8.0
LRU_CHUNK = 256
CONV_WIDTH = 4
D_PROJ = 3200
N_EXPERTS = 32
TOP_K = 4
D_EXPERT = 1024
SWIGLU_LIMIT = 7.0
SWIGLU_ALPHA = 1.702
LN_EPS = 1e-5
PAST_LEN = 16384.0

LANES = 128
SUBLANES = 8
SAMPLE_T_PAD = 8
TOKEN_TILE = 512
MOE_BLOCK = 512
LOCAL_ROWS = TOP_K * TOKEN_TILE + N_EXPERTS * SUBLANES
RUN_PIECES = tuple(SUBLANES << j for j in reversed(range((TOKEN_TILE // SUBLANES).bit_length())))
WAIT_PIECES = tuple(SUBLANES << j for j in reversed(range((LOCAL_ROWS // SUBLANES).bit_length())))
VMEM_LIMIT = 56 * 1024 * 1024

_NT = (((1,), (1,)), ((), ()))
_TN = (((0,), (0,)), ((), ()))


def _params(*sem):
    return pltpu.CompilerParams(dimension_semantics=sem, vmem_limit_bytes=VMEM_LIMIT)


def _mm(a, b):
    return jnp.dot(a.astype(bf16), b.astype(bf16), preferred_element_type=f32)


def _mm_nt(a, b):
    return lax.dot_general(a.astype(bf16), b.astype(bf16), _NT, preferred_element_type=f32)


def _mm_tn(a, b):
    return lax.dot_general(a.astype(bf16), b.astype(bf16), _TN, preferred_element_type=f32)


def _mm3(a, b):
    a_hi, b_hi = a.astype(bf16), b.astype(bf16)
    a_lo = (a - a_hi.astype(f32)).astype(bf16)
    b_lo = (b - b_hi.astype(f32)).astype(bf16)
    dot = functools.partial(jnp.dot, preferred_element_type=f32)
    return dot(a_hi, b_hi) + dot(a_hi, b_lo) + dot(a_lo, b_hi)


def _pack_halves(x):
    w = x.shape[-1] // 2
    lo = lax.bitcast_convert_type(x[:, :w], u32)
    hi = lax.bitcast_convert_type(x[:, w:], u32)
    return (lo >> 16) | hi


def _unpack_halves(p):
    return (lax.bitcast_convert_type(p << 16, f32),
            lax.bitcast_convert_type(p & jnp.uint32(0xFFFF0000), f32))


def _softplus(x):
    return jnp.maximum(x, 0.0) + jnp.log(1.0 + jnp.exp(-jnp.abs(x)))


def _sigmoid(x):
    return 1.0 / (1.0 + jnp.exp(-x))


def _half_masks():
    lane = lax.broadcasted_iota(i32, (1, LANES), 1)
    m0 = (lane < HEAD_DIM).astype(f32)
    return m0, 1.0 - m0


def _seg_mean(x, m0, m1):
    s0 = jnp.sum(x * m0, axis=-1, keepdims=True)
    s1 = jnp.sum(x * m1, axis=-1, keepdims=True)
    return (m0 * s0 + m1 * s1) * (1.0 / HEAD_DIM)


def _seg_sum(x, m0, m1):
    s0 = jnp.sum(x * m0, axis=-1, keepdims=True)
    s1 = jnp.sum(x * m1, axis=-1, keepdims=True)
    return m0 * s0 + m1 * s1


def _block_diag_mask():
    r = lax.broadcasted_iota(i32, (LANES, LANES), 0) // HEAD_DIM
    c = lax.broadcasted_iota(i32, (LANES, LANES), 1) // HEAD_DIM
    return (r == c).astype(f32)


def _layer_norm_rows(z, g, b):
    mu = jnp.mean(z, axis=-1, keepdims=True)
    zc = z - mu
    var = jnp.mean(zc * zc, axis=-1, keepdims=True)
    return zc * lax.rsqrt(var + LN_EPS) * g + b


def _group_specs(width, n_prompt_tiles):
    return [pl.BlockSpec((TOKEN_TILE, width), lambda i, *_: (jnp.minimum(i, n_prompt_tiles - 1), 0)),
            pl.BlockSpec((TOKEN_TILE, width), lambda i, *_: (jnp.maximum(i - n_prompt_tiles, 0), 0))]


def _pick_group(p_ref, s_ref, n_prompt_tiles):
    return jnp.where(pl.program_id(0) >= n_prompt_tiles, s_ref[...], p_ref[...])


def _n_tiles(h_p, h_s):
    assert h_p.shape[0] % TOKEN_TILE == 0 and h_s.shape[0] % TOKEN_TILE == 0, (h_p.shape, h_s.shape)
    return h_p.shape[0] // TOKEN_TILE, h_s.shape[0] // TOKEN_TILE


def _store_group(p_ref, s_ref, value, n_prompt_tiles):
    @pl.when(pl.program_id(0) < n_prompt_tiles)
    def _():
        p_ref[...] = value

    @pl.when(pl.program_id(0) >= n_prompt_tiles)
    def _():
        s_ref[...] = value


def _in_proj_kernel(xp_ref, xs_ref, w_ref, retp_ref, rets_ref, rwp_ref, rws_ref, lrup_ref, lrus_ref, *,
                    n_prompt_tiles):
    c0, c1 = 4 * RET_W, 4 * RET_W + RWKV_PROJ
    dot = functools.partial(jnp.dot, preferred_element_type=f32)

    def project(x_ref, ret_ref, rw_ref, lru_ref):
        xb = x_ref[...].astype(bf16)
        ret_ref[...] = dot(xb, w_ref[:, :c0])
        rw_ref[...] = dot(xb, w_ref[:, c0:c1])
        lru_ref[...] = dot(xb, w_ref[:, c1:])

    @pl.when(pl.program_id(0) < n_prompt_tiles)
    def _():
        project(xp_ref, retp_ref, rwp_ref, lrup_ref)

    @pl.when(pl.program_id(0) >= n_prompt_tiles)
    def _():
        project(xs_ref, rets_ref, rws_ref, lrus_ref)


def _in_proj(h_p, h_s, w_bf):
    npt, nst = _n_tiles(h_p, h_s)
    widths = (4 * RET_W, RWKV_PROJ, 2 * LRU_W)
    out = pl.pallas_call(
        functools.partial(_in_proj_kernel, n_prompt_tiles=npt),
        grid=(npt + nst,),
        in_specs=_group_specs(D_MODEL, npt) + [
                  pl.BlockSpec((D_MODEL, D_PROJ), lambda i: (0, 0))],
        out_specs=[s for w in widths for s in _group_specs(w, npt)],
        out_shape=[jax.ShapeDtypeStruct((rows, w), f32) for w in widths for rows in (h_p.shape[0], h_s.shape[0])],
        compiler_params=_params("arbitrary"),
        name="in_proj",
    )(h_p, h_s, w_bf)
    return out[0:2], out[2:4], out[4:6]


def _rope_tables(pos):
    half = HEAD_DIM // 2
    inv = ROPE_BASE ** (-jnp.arange(half, dtype=f32) / half)
    ang = pos[:, None] * inv[None, :]
    cos, sin = jnp.cos(ang), jnp.sin(ang)
    cos_f = jnp.tile(jnp.concatenate([cos, cos], axis=-1), (1, RET_H))
    sin_f = jnp.tile(jnp.concatenate([-sin, sin], axis=-1), (1, RET_H))
    return cos_f, sin_f


def _retention_tables(length, n_valid):
    lg = jnp.log1p(-jnp.exp2(-5.0 - jnp.arange(RET_H, dtype=f32)))
    idx = jnp.arange(length, dtype=f32)
    rel = idx[:, None] - idx[None, :]
    mask = jnp.where(rel[None] >= 0, jnp.exp(jnp.maximum(rel, 0.0)[None] * lg[:, None, None]), 0.0)
    q_dec = jnp.exp((idx[:, None] + 1.0) * lg[None, :])
    k_dec = jnp.where(idx[:, None] < n_valid, jnp.exp((n_valid - 1.0 - idx)[:, None] * lg[None, :]), 0.0)
    c_dec = jnp.exp(n_valid * lg)[None, :]
    rep = lambda t: jnp.repeat(t, HEAD_DIM, axis=-1)
    return mask, rep(q_dec), rep(k_dec), rep(c_dec)


def _retention_kernel(ret_ref, cos_ref, sin_ref, mask_ref, qdec_ref, kdec_ref, cdec_ref, s0_ref,
                      y_ref, s_ref, pair_ref, *, nb):
    @pl.when(pl.program_id(1) == 0)
    def _():
        _load_head_pairs(pair_ref, s0_ref)

    m0, m1 = _half_masks()
    lane = lax.broadcasted_iota(i32, (1, LANES), 1)
    first_half = (lane % HEAD_DIM) < (HEAD_DIM // 2)
    bd = _block_diag_mask()

    def rope(x, cs, sn):
        swapped = jnp.where(first_half, pltpu.roll(x, LANES - HEAD_DIM // 2, 1), pltpu.roll(x, HEAD_DIM // 2, 1))
        return x * cs + swapped * sn

    for j in range(nb):
        for p in range(RET_H // 2):
            cols = pl.ds(p * LANES, LANES)
            cs, sn = cos_ref[:, cols], sin_ref[:, cols]
            q2 = rope(ret_ref[j, :, pl.ds(p * LANES, LANES)], cs, sn)
            k2 = rope(ret_ref[j, :, pl.ds(RET_W + p * LANES, LANES)], cs, sn) * (HEAD_DIM ** -0.5)
            v2 = ret_ref[j, :, pl.ds(2 * RET_W + p * LANES, LANES)]
            g2 = ret_ref[j, :, pl.ds(3 * RET_W + p * LANES, LANES)]
            state = pair_ref[j, p]
            o2 = _mm(q2, state) * qdec_ref[:, cols]
            for hh, m in enumerate((m0, m1)):
                sc = _mm_nt(q2 * m, k2) * mask_ref[2 * p + hh]
                o2 = o2 + _mm(sc, v2) * m
            pair_ref[j, p] = state * cdec_ref[:, cols] + _mm_tn(k2 * kdec_ref[:, cols], v2) * bd
            mu = _seg_mean(o2, m0, m1)
            oc = o2 - mu
            var = _seg_mean(oc * oc, m0, m1)
            y_ref[j, :, cols] = g2 * _sigmoid(g2) * oc * lax.rsqrt(var + LN_EPS)

    @pl.when(pl.program_id(1) == pl.num_programs(1) - 1)
    def _():
        _store_head_pairs(s_ref, pair_ref)


def _load_head_pairs(pair_ref, s_ref):
    pair_ref[...] = jnp.zeros(pair_ref.shape, f32)
    for j in range(pair_ref.shape[0]):
        for p in range(pair_ref.shape[1]):
            pair_ref[j, p, pl.ds(0, HEAD_DIM), pl.ds(0, HEAD_DIM)] = s_ref[j, 2 * p]
            pair_ref[j, p, pl.ds(HEAD_DIM, HEAD_DIM), pl.ds(HEAD_DIM, HEAD_DIM)] = s_ref[j, 2 * p + 1]


def _store_head_pairs(s_ref, pair_ref):
    for j in range(pair_ref.shape[0]):
        for p in range(pair_ref.shape[1]):
            s_ref[j, 2 * p] = pair_ref[j, p, pl.ds(0, HEAD_DIM), pl.ds(0, HEAD_DIM)]
            s_ref[j, 2 * p + 1] = pair_ref[j, p, pl.ds(HEAD_DIM, HEAD_DIM), pl.ds(HEAD_DIM, HEAD_DIM)]


def _retention(ret, cos_f, sin_f, tables, s0, layer, *, nb, length):
    mask, q_dec, k_dec, c_dec = tables
    n_batch, t_len, _ = ret.shape
    blk = lambda g, c: (g, c, 0)
    const2 = lambda g, c: (0, 0)
    st = lambda g, c: (g, 0, 0, 0)
    return pl.pallas_call(
        functools.partial(_retention_kernel, nb=nb),
        grid=(n_batch // nb, t_len // length),
        in_specs=[pl.BlockSpec((nb, length, 4 * RET_W), blk),
                  pl.BlockSpec((length, RET_W), lambda g, c: (c, 0)),
                  pl.BlockSpec((length, RET_W), lambda g, c: (c, 0)),
                  pl.BlockSpec((RET_H, length, length), lambda g, c: (0, 0, 0)),
                  pl.BlockSpec((length, RET_W), const2),
                  pl.BlockSpec((length, RET_W), const2),
                  pl.BlockSpec((1, RET_W), const2),
                  pl.BlockSpec((None, nb, RET_H, HEAD_DIM, HEAD_DIM), lambda g, c: (layer, g, 0, 0, 0))],
        out_specs=[pl.BlockSpec((nb, length, RET_W), blk),
                   pl.BlockSpec((nb, RET_H, HEAD_DIM, HEAD_DIM), st)],
        out_shape=[jax.ShapeDtypeStruct((n_batch, t_len, RET_W), f32),
                   jax.ShapeDtypeStruct((n_batch, RET_H, HEAD_DIM, HEAD_DIM), f32)],
        scratch_shapes=[pltpu.VMEM((nb, RET_H // 2, LANES, LANES), f32)],
        compiler_params=_params("arbitrary", "arbitrary"),
        name="retention",
    )(ret, cos_f, sin_f, mask, q_dec, k_dec, c_dec, s0)


def _rwkv_kernel(rw_ref, aux_ref, valid_ref, mix_ref, vec_ref, lora_ref, tri_ref, s0_ref,
                 y_ref, s_ref, *rest, rows, length, chain):
    n_chunks = rows // length
    rw = rw_ref[...]
    if chain:
        last_ref, xs_ref, pair_ref = rest

        @pl.when(pl.program_id(1) == 0)
        def _():
            _load_head_pairs(pair_ref, s0_ref)
            xs_ref[pl.ds(0, SUBLANES), :] = jnp.broadcast_to(aux_ref[0], (SUBLANES, RWKV_PROJ))

        xs_ref[pl.ds(SUBLANES, rows), :] = rw
        prev = xs_ref[pl.ds(SUBLANES - 1, rows), :]
        xs_ref[pl.ds(0, SUBLANES), :] = rw[rows - SUBLANES:, :]
        last_ref[0] = rw[rows - 1:rows, :]
    else:
        pair_ref, = rest
        _load_head_pairs(pair_ref, s0_ref)
        prev = aux_ref[...]
    rwm = rw + (prev - rw) * mix_ref[...]

    valid = valid_ref[...]
    w0, a0, k_k, k_a, r_k, gn_g, gn_b = (vec_ref[pl.ds(i, 1), :] for i in range(7))
    lo = rwm[:, RWKV_LORA_COL:]
    lw = _mm3(jnp.tanh(lo), lora_ref[0])
    la = _mm3(lo, lora_ref[1])
    gate = _mm3(_sigmoid(lo), lora_ref[2])
    logw = -jnp.exp(-_softplus(-(w0 + lw)) - 0.5) * valid
    a = _sigmoid(a0 + la)
    r = rwm[:, :RWKV_W]
    kr = rwm[:, RWKV_W:2 * RWKV_W]
    vr = rwm[:, 2 * RWKV_W:3 * RWKV_W]
    kk_raw = kr * k_k
    kp = kr * (1.0 + (a - 1.0) * k_a) * valid
    cum = _mm3(tri_ref[...], logw)
    g_incl = jnp.exp(cum)
    g_inv = jnp.exp(-cum)
    g_prev = jnp.exp(cum - logw)
    g_end = jnp.concatenate(
        [jnp.broadcast_to(g_incl[(c + 1) * length - 1:(c + 1) * length, :], (length, RWKV_W)) for c in range(n_chunks)],
        axis=0)

    m0, m1 = _half_masks()
    bd = _block_diag_mask()
    stacked = 2 * rows
    ri = lax.broadcasted_iota(i32, (stacked, stacked), 0)
    ci = lax.broadcasted_iota(i32, (stacked, stacked), 1)
    same = (ri // length) == (ci // length)
    strict = jnp.logical_and(same, ci < ri)
    incl = jnp.logical_and(same, ci <= ri)
    eye = (ci == ri).astype(f32)
    stack = lambda t: jnp.concatenate([t * m0, t * m1], axis=0)
    unstack = lambda t: t[:rows] + t[rows:]
    pairs = range(RWKV_H // 2)

    pre = []
    for p in pairs:
        sl = slice(p * LANES, (p + 1) * LANES)
        kk2 = kk_raw[:, sl]
        kk2 = kk2 * lax.rsqrt(jnp.maximum(_seg_sum(kk2 * kk2, m0, m1), 1e-24)) * valid[:, sl]
        d = dict(sl=sl, r2=r[:, sl], v2=vr[:, sl], kp2=kp[:, sl], ge=g_end[:, sl])
        d['kh'] = kk2 * g_prev[:, sl]
        rh = d['r2'] * g_incl[:, sl]
        bt = kk2 * a[:, sl] * g_inv[:, sl]
        kt = d['kp2'] * g_inv[:, sl]
        d['b_end'] = bt * d['ge']
        d['k_end'] = kt * d['ge']
        d['kh_s'], d['rh_s'], d['v_s'] = stack(d['kh']), stack(rh), stack(d['v2'])
        gram = _mm_nt(jnp.concatenate([d['kh_s'], d['rh_s']], axis=0), jnp.concatenate([stack(bt), stack(kt)], axis=0))
        d['x'] = -jnp.where(strict, gram[:stacked, :stacked], 0.0)
        d['a_k'] = jnp.where(strict, gram[:stacked, stacked:], 0.0)
        d['r_b'] = jnp.where(incl, gram[stacked:, :stacked], 0.0)
        d['r_k'] = jnp.where(incl, gram[stacked:, stacked:], 0.0)
        d['t'] = eye + d['x']
        pre.append(d)
    cover = 2
    while cover < length:
        for d in pre:
            d['x'] = _mm(d['x'], d['x'])
            d['t'] = d['t'] + _mm(d['t'], d['x'])
        cover *= 2
    for d in pre:
        rhs0 = -_mm(d['a_k'], d['v_s'])
        tz = _mm(d['t'], jnp.concatenate([rhs0, d['kh_s']], axis=1))
        z0_s = tz[:, :LANES]
        rbz = _mm(d['r_b'], tz)
        d['z0'] = unstack(z0_s)
        d['q'] = unstack(d['rh_s'] - rbz[:, LANES:])
        d['y0'] = unstack(rbz[:, :LANES] + _mm(d['r_k'], d['v_s']))
        d['w'] = unstack(_mm_tn(d['t'], stack(d['b_end'])))

    for p, d in zip(pairs, pre):
        sl, ge, kh, v2 = d['sl'], d['ge'], d['kh'], d['v2']
        state = pair_ref[0, p] if chain else None
        ys = []
        for c in range(n_chunks):
            cr = slice(c * length, (c + 1) * length)
            if not chain:
                state = pair_ref[c, p]
            ys.append(_mm_nt(d['q'][cr], state) + d['y0'][cr])
            n_c = (_mm_tn(d['z0'][cr], d['b_end'][cr]) + _mm_tn(v2[cr], d['k_end'][cr])) * bd
            kw = _mm_tn(kh[cr], d['w'][cr]) * bd
            state = state * ge[(c + 1) * length - 1:(c + 1) * length, :] - _mm(state, kw) + n_c
            if not chain:
                pair_ref[c, p] = state
        if chain:
            pair_ref[0, p] = state
        y2 = jnp.concatenate(ys, axis=0) if n_chunks > 1 else ys[0]

        mu = _seg_mean(y2, m0, m1)
        yc = y2 - mu
        var = _seg_mean(yc * yc, m0, m1)
        yn = yc * lax.rsqrt(var + RWKV_GN_EPS) * gn_g[:, sl] + gn_b[:, sl]
        bonus = _seg_sum(d['r2'] * d['kp2'] * r_k[:, sl], m0, m1) * v2
        y_ref[:, pl.ds(p * LANES, LANES)] = (yn + bonus) * gate[:, sl]

    if chain:
        @pl.when(pl.program_id(1) == pl.num_programs(1) - 1)
        def _():
            _store_head_pairs(s_ref, pair_ref)
    else:
        _store_head_pairs(s_ref, pair_ref)


def _rwkv(rw, aux, valid, mix, vec, lora, tri, s0, layer, *, n_groups, n_steps, rows, length, chain):
    blk = lambda g, c: (g * n_steps + c, 0)
    const2 = lambda g, c: (0, 0)
    st = lambda g, c: (g, 0, 0, 0)
    n_state = 1 if chain else rows // length
    aux_spec = (pl.BlockSpec((1, 1, RWKV_PROJ), lambda g, c: (g, 0, 0)) if chain
                else pl.BlockSpec((rows, RWKV_PROJ), blk))
    return pl.pallas_call(
        functools.partial(_rwkv_kernel, rows=rows, length=length, chain=chain),
        grid=(n_groups, n_steps),
        in_specs=[pl.BlockSpec((rows, RWKV_PROJ), blk),
                  aux_spec,
                  pl.BlockSpec((rows, RWKV_W), const2),
                  pl.BlockSpec((1, RWKV_PROJ), const2),
                  pl.BlockSpec((SUBLANES, RWKV_W), const2),
                  pl.BlockSpec((3, LANES, RWKV_W), lambda g, c: (0, 0, 0)),
                  pl.BlockSpec((rows, rows), const2),
                  pl.BlockSpec((None, n_state, RWKV_H, HEAD_DIM, HEAD_DIM), lambda g, c: (layer, g, 0, 0, 0))],
        out_specs=[pl.BlockSpec((rows, RWKV_W), blk),
                   pl.BlockSpec((n_state, RWKV_H, HEAD_DIM, HEAD_DIM), st)]
        + ([pl.BlockSpec((1, 1, RWKV_PROJ), lambda g, c: (g, 0, 0))] if chain else []),
        out_shape=[jax.ShapeDtypeStruct((n_groups * n_steps * rows, RWKV_W), f32),
                   jax.ShapeDtypeStruct((n_groups * n_state, RWKV_H, HEAD_DIM, HEAD_DIM), f32)]
        + ([jax.ShapeDtypeStruct((n_groups, 1, RWKV_PROJ), f32)] if chain else []),
        scratch_shapes=([pltpu.VMEM((rows + SUBLANES, RWKV_PROJ), f32)] if chain else [])
        + [pltpu.VMEM((n_state, RWKV_H // 2, LANES, LANES), f32)],
        compiler_params=_params("arbitrary", "arbitrary"),
        name="rwkv7",
    )(rw, aux, valid, mix, vec, lora, tri, s0)


def _chunk_tri(rows, length):
    idx = jnp.arange(rows)
    same = (idx[:, None] // length) == (idx[None, :] // length)
    return jnp.logical_and(same, idx[None, :] <= idx[:, None]).astype(f32)


def _lru_kernel(lru_ref, valid_ref, vec_ref, wa_ref, wx_ref, conv0_ref, h0_ref,
                y_ref, h_ref, tail_ref, xext_ref, *, nb, length):
    @pl.when(pl.program_id(1) == 0)
    def _():
        h_ref[...] = h0_ref[...]
        xext_ref[:, pl.ds(0, SUBLANES), :] = conv0_ref[...]

    valid = valid_ref[...] > 0.5
    row = lax.broadcasted_iota(i32, (length, LRU_W), 0)
    cw = [vec_ref[pl.ds(i, 1), :] for i in range(CONV_WIDTH)]
    cb, ba, bx, lam = (vec_ref[pl.ds(i, 1), :] for i in range(CONV_WIDTH, CONV_WIDTH + 4))
    sp = _softplus(-lam)
    for j in range(nb):
        rows = pl.ds(j * length, length)
        gbr = lru_ref[rows, pl.ds(0, LRU_W)]
        x = lru_ref[rows, pl.ds(LRU_W, LRU_W)]
        xext_ref[j, pl.ds(SUBLANES, length), :] = x
        xc = cb + x * cw[CONV_WIDTH - 1]
        for t in range(CONV_WIDTH - 1):
            xc = xc + xext_ref[j, pl.ds(SUBLANES - (CONV_WIDTH - 1) + t, length), :] * cw[t]
        xext_ref[j, pl.ds(0, SUBLANES), :] = x[length - SUBLANES:, :]
        tail_ref[j] = x[length - SUBLANES:, :]
        gate_a = _sigmoid(_mm(xc, wa_ref[...]) + ba)
        gate_x = _sigmoid(_mm(xc, wx_ref[...]) + bx)
        log_a = -LRU_C * gate_a * sp
        a = jnp.exp(log_a)
        b = xc * gate_x * jnp.sqrt(1.0 - jnp.exp(2.0 * log_a))
        a = jnp.where(valid, a, 1.0)
        b = jnp.where(valid, b, 0.0)
        shift = 1
        while shift < length:
            inside = row >= shift
            b = a * jnp.where(inside, pltpu.roll(b, shift, 0), 0.0) + b
            a = a * jnp.where(inside, pltpu.roll(a, shift, 0), 1.0)
            shift *= 2
        hs = a * h_ref[j, pl.ds(0, 1), :] + b
        h_ref[j] = jnp.broadcast_to(hs[length - 1:length, :], (SUBLANES, LRU_W))
        c = 0.7978845608028654
        gelu = 0.5 * gbr * (1.0 + jnp.tanh(c * (gbr + 0.044715 * gbr * gbr * gbr)))
        y_ref[rows, :] = hs * gelu


def _lru(lru, valid, vec, wa_bd, wx_bd, conv0, h0, *, n_batch, n_chunks, nb, length, out_rows):
    blk = lambda g, c: (g * n_chunks + c, 0)
    const2 = lambda g, c: (0, 0)
    st = lambda g, c: (g, 0, 0)
    return pl.pallas_call(
        functools.partial(_lru_kernel, nb=nb, length=length),
        grid=(n_batch // nb, n_chunks),
        in_specs=[pl.BlockSpec((nb * length, 2 * LRU_W), blk),
                  pl.BlockSpec((length, LRU_W), const2),
                  pl.BlockSpec((SUBLANES, LRU_W), const2),
                  pl.BlockSpec((LRU_W, LRU_W), const2),
                  pl.BlockSpec((LRU_W, LRU_W), const2),
                  pl.BlockSpec((nb, SUBLANES, LRU_W), st),
                  pl.BlockSpec((nb, SUBLANES, LRU_W), st)],
        out_specs=[pl.BlockSpec((nb * length, LRU_W), blk),
                   pl.BlockSpec((nb, SUBLANES, LRU_W), st),
                   pl.BlockSpec((nb, SUBLANES, LRU_W), st)],
        out_shape=[jax.ShapeDtypeStruct((out_rows, LRU_W), f32),
                   jax.ShapeDtypeStruct((n_batch, SUBLANES, LRU_W), f32),
                   jax.ShapeDtypeStruct((n_batch, SUBLANES, LRU_W), f32)],
        scratch_shapes=[pltpu.VMEM((nb, length + SUBLANES, LRU_W), f32)],
        compiler_params=_params("arbitrary", "arbitrary"),
        name="rg_lru",
    )(lru, valid, vec, wa_bd, wx_bd, conv0, h0)


def _out_proj_kernel(yrp_ref, yrs_ref, ywp_ref, yws_ref, ylp_ref, yls_ref, xp_ref, xs_ref,
                     w_ref, ln_ref, wr_ref, br_ref,
                     x1_ref, lpos_ref, g_ref, meta_ref, cnt_ref, *, alpha, n_prompt_tiles):
    @pl.when(pl.program_id(0) == 0)
    def _():
        cnt_ref[...] = jnp.zeros(cnt_ref.shape, f32)

    pick = functools.partial(_pick_group, n_prompt_tiles=n_prompt_tiles)
    mixed = (_mm(pick(yrp_ref, yrs_ref), w_ref[pl.ds(0, RET_W), :])
             + _mm(pick(ywp_ref, yws_ref), w_ref[pl.ds(RET_W, RWKV_W), :])
             + _mm(pick(ylp_ref, yls_ref), w_ref[pl.ds(RET_W + RWKV_W, LRU_W), :]))
    x1 = _layer_norm_rows(alpha * pick(xp_ref, xs_ref) + mixed, ln_ref[pl.ds(0, 1), :], ln_ref[pl.ds(1, 1), :])
    x1_ref[...] = x1
    logits = _mm3(x1, wr_ref[...]) + br_ref[...]
    tm = logits.shape[0]
    lane = lax.broadcasted_iota(i32, logits.shape, 1).astype(f32)
    top_v = jnp.zeros(logits.shape, f32)
    work = logits
    v_max = None
    onehots = []
    for k in range(TOP_K):
        v = jnp.max(work, axis=-1, keepdims=True)
        idx = jnp.min(jnp.where(work == v, lane, float(LANES)), axis=-1, keepdims=True)
        if k == 0:
            v_max = v
        hit = lane == idx
        onehots.append(hit.astype(f32))
        top_v = jnp.where(lane == k, jnp.exp(v - v_max), top_v)
        work = jnp.where(hit, -jnp.inf, work)
    g_ref[...] = top_v / jnp.sum(top_v, axis=-1, keepdims=True)

    total = onehots[0] + onehots[1] + onehots[2] + onehots[3]
    ri = lax.broadcasted_iota(i32, (tm, tm), 0)
    ci = lax.broadcasted_iota(i32, (tm, tm), 1)
    before = _mm((ci < ri).astype(f32), total)
    groups = jnp.floor((jnp.sum(total, axis=0, keepdims=True) + (SUBLANES - 1.0)) * (1.0 / SUBLANES))
    er = lax.broadcasted_iota(i32, (LANES, LANES), 0)
    ec = lax.broadcasted_iota(i32, (LANES, LANES), 1)
    run_len = groups * SUBLANES
    run_start = _mm(jnp.broadcast_to(groups, (SUBLANES, LANES)), (er < ec).astype(f32))[0:1] * SUBLANES
    lpos = jnp.full(logits.shape, -1.0, f32)
    for k in range(TOP_K):
        pos = jnp.sum(onehots[k] * (before + run_start), axis=-1, keepdims=True)
        lpos = jnp.where(lane == k, pos, lpos)
    lpos_ref[...] = lpos
    row = lax.broadcasted_iota(i32, (SUBLANES, LANES), 0)
    meta = jnp.where(row == 0, run_start, jnp.where(row == 1, run_len, jnp.where(row == 2, cnt_ref[...], 0.0)))
    meta_ref[...] = meta.astype(i32)
    cnt_ref[...] = cnt_ref[...] + run_len


def _out_proj_router(y_ret, y_rwkv, y_lru, h, w_out_bf, ln, w_router, b_router, alpha):
    npt, nst = _n_tiles(*h)
    n = h[0].shape[0] + h[1].shape[0]
    tm = TOKEN_TILE
    row = lambda i: (i, 0)
    const = lambda i: (0, 0)
    return pl.pallas_call(
        functools.partial(_out_proj_kernel, alpha=alpha, n_prompt_tiles=npt),
        grid=(npt + nst,),
        in_specs=_group_specs(RET_W, npt) + _group_specs(RWKV_W, npt) + _group_specs(LRU_W, npt)
        + _group_specs(D_MODEL, npt) + [
                  pl.BlockSpec((D_MODEL, D_MODEL), const),
                  pl.BlockSpec((SUBLANES, D_MODEL), const),
                  pl.BlockSpec((D_MODEL, LANES), const),
                  pl.BlockSpec((1, LANES), const)],
        out_specs=[pl.BlockSpec((tm, D_MODEL), row), pl.BlockSpec((tm, LANES), row), pl.BlockSpec((tm, LANES), row),
                   pl.BlockSpec((SUBLANES, LANES), row), pl.BlockSpec((SUBLANES, LANES), const)],
        out_shape=[jax.ShapeDtypeStruct((n, D_MODEL), f32),
                   jax.ShapeDtypeStruct((n, LANES), f32),
                   jax.ShapeDtypeStruct((n, LANES), f32),
                   jax.ShapeDtypeStruct((n // tm * SUBLANES, LANES), i32),
                   jax.ShapeDtypeStruct((SUBLANES, LANES), f32)],
        compiler_params=_params("arbitrary"),
        name="out_proj_router",
    )(*y_ret, *y_rwkv, *y_lru, *h, w_out_bf, ln, w_router, b_router)


def _for_each_run_piece(meta_ref, pstart_ref, fn):
    for e in range(N_EXPERTS):
        start, n = meta_ref[0, e], meta_ref[1, e]
        base = pstart_ref[e] + meta_ref[2, e]
        for sz in RUN_PIECES:
            done = n & ~(2 * sz - 1)

            @pl.when((n & sz) != 0)
            def _():
                fn(pl.multiple_of(start + done, SUBLANES), pl.multiple_of(base + done, SUBLANES), sz)


def _wait_run_rows(meta_ref, make_copy):
    total = meta_ref[1, 0]
    for e in range(1, N_EXPERTS):
        total = total + meta_ref[1, e]
    for sz in WAIT_PIECES:
        @pl.when((total & sz) != 0)
        def _():
            make_copy(sz).wait()


def _dispatch_kernel(pstart_ref, cnt_ref, n_used_ref, meta_ref, meta_prev_ref, lpos_ref, x1_ref, xs_hbm,
                     xl_ref, zero_ref, sems, zsem):
    tm = TOKEN_TILE
    bm = MOE_BLOCK
    i = pl.program_id(0)
    slot = i % 2

    @pl.when(i == 0)
    def _():
        zero_ref[...] = jnp.zeros(zero_ref.shape, u32)
        tail = lambda j: pltpu.make_async_copy(zero_ref, xs_hbm.at[pl.ds(j * bm, bm), :], zsem)

        def tail_start(j, carry):
            tail(j).start()
            return carry

        def tail_wait(j, carry):
            tail(j).wait()
            return carry

        n_blocks = xs_hbm.shape[0] // bm
        lax.fori_loop(n_used_ref[0], n_blocks, tail_start, 0)
        lax.fori_loop(n_used_ref[0], n_blocks, tail_wait, 0)
        for e in range(N_EXPERTS):
            lo = pstart_ref[e] + cnt_ref[e]
            n_groups = ((cnt_ref[e] + bm - 1) // bm * bm - cnt_ref[e]) // SUBLANES
            fill = lambda g: pltpu.make_async_copy(
                zero_ref.at[pl.ds(0, SUBLANES), :],
                xs_hbm.at[pl.ds(pl.multiple_of(lo + g * SUBLANES, SUBLANES), SUBLANES), :], zsem)

            def start(g, carry):
                fill(g).start()
                return carry

            def wait(g, carry):
                fill(g).wait()
                return carry

            lax.fori_loop(0, n_groups, start, 0)
            lax.fori_loop(0, n_groups, wait, 0)

    lpos_t = lpos_ref[...].T
    srow = lax.broadcasted_iota(i32, (LOCAL_ROWS, tm), 0).astype(f32)
    perm = jnp.zeros((LOCAL_ROWS, tm), f32)
    for k in range(TOP_K):
        perm = jnp.where(srow == lpos_t[k:k + 1, :], 1.0, perm)
    xl_ref[slot] = _pack_halves(jnp.dot(perm.astype(bf16), x1_ref[...].astype(bf16), preferred_element_type=f32))

    def send(local_row, sorted_row, n_rows):
        pltpu.make_async_copy(xl_ref.at[slot, pl.ds(local_row, n_rows), :],
                              xs_hbm.at[pl.ds(sorted_row, n_rows), :], sems.at[slot]).start()

    _for_each_run_piece(meta_ref, pstart_ref, send)

    def sent(s):
        return lambda n_rows: pltpu.make_async_copy(xl_ref.at[s, pl.ds(0, n_rows), :],
                                                    xs_hbm.at[pl.ds(0, n_rows), :], sems.at[s])

    @pl.when(i > 0)
    def _():
        _wait_run_rows(meta_prev_ref, sent(1 - slot))

    @pl.when(i == pl.num_programs(0) - 1)
    def _():
        _wait_run_rows(meta_ref, sent(slot))


def _dispatch(pstart, counts, n_used, meta, lpos, x1, n_rows):
    n = x1.shape[0]
    tm = TOKEN_TILE
    smem_tile = lambda f: pl.BlockSpec((SUBLANES, LANES), f, memory_space=pltpu.SMEM)
    grid_spec = pltpu.PrefetchScalarGridSpec(
        num_scalar_prefetch=3,
        grid=(n // tm,),
        in_specs=[smem_tile(lambda i, ps, ct, nu: (i, 0)),
                  smem_tile(lambda i, ps, ct, nu: (jnp.maximum(i - 1, 0), 0)),
                  pl.BlockSpec((tm, LANES), lambda i, ps, ct, nu: (i, 0)),
                  pl.BlockSpec((tm, D_MODEL), lambda i, ps, ct, nu: (i, 0))],
        out_specs=pl.BlockSpec(memory_space=pl.ANY),
        scratch_shapes=[pltpu.VMEM((2, LOCAL_ROWS, D_MODEL // 2), u32),
                        pltpu.VMEM((MOE_BLOCK, D_MODEL // 2), u32),
                        pltpu.SemaphoreType.DMA((2,)), pltpu.SemaphoreType.DMA(())],
    )
    return pl.pallas_call(
        _dispatch_kernel,
        grid_spec=grid_spec,
        out_shape=jax.ShapeDtypeStruct((n_rows, D_MODEL // 2), u32),
        compiler_params=_params("arbitrary"),
        name="moe_dispatch",
    )(pstart, counts, n_used, meta, meta, lpos, x1)


def _expert_kernel(blk_e_ref, n_used_ref, next_e_ref, xs_ref, wgu_hbm, bgu_ref, wdn_hbm, bdn_ref,
                   out_ref, wgu_f32, wdn_f32, wgu_bf, wdn_bf, sems, *, layer):
    i = pl.program_id(0)
    prev = jnp.maximum(i - 1, 0)
    expert = blk_e_ref[i]
    new_expert = jnp.logical_or(i == 0, expert != blk_e_ref[prev])
    used = i < n_used_ref[0]

    def fetch(e):
        return (pltpu.make_async_copy(wgu_hbm.at[layer, e], wgu_f32, sems.at[0]),
                pltpu.make_async_copy(wdn_hbm.at[layer, e], wdn_f32, sems.at[1]))

    @pl.when(used)
    def _():
        @pl.when(new_expert)
        def _():
            @pl.when(i == 0)
            def _():
                for c in fetch(expert):
                    c.start()

            for c in fetch(expert):
                c.wait()
            wgu_bf[...] = wgu_f32[...].astype(bf16)
            wdn_bf[...] = wdn_f32[...].astype(bf16)
            nxt = next_e_ref[expert]

            @pl.when(nxt >= 0)
            def _():
                for c in fetch(nxt):
                    c.start()

        x_lo, x_hi = _unpack_halves(xs_ref[...])
        half = D_MODEL // 2
        gu = (jnp.dot(x_lo.astype(bf16), wgu_bf[pl.ds(0, half), :], preferred_element_type=f32)
              + jnp.dot(x_hi.astype(bf16), wgu_bf[pl.ds(half, half), :], preferred_element_type=f32) + bgu_ref[0, 0])
        g = jnp.minimum(gu[:, :D_EXPERT], SWIGLU_LIMIT)
        u = jnp.clip(gu[:, D_EXPERT:], -SWIGLU_LIMIT, SWIGLU_LIMIT)
        hdn = (u + 1.0) * g * _sigmoid(SWIGLU_ALPHA * g)
        y = jnp.dot(hdn.astype(bf16), wdn_bf[...], preferred_element_type=f32) + bdn_ref[0, 0]
        out_ref[...] = _pack_halves(y.astype(bf16).astype(f32))

    @pl.when(jnp.logical_not(used))
    def _():
        out_ref[...] = jnp.zeros(out_ref.shape, u32)


def _experts(blk_e, n_used, next_e, xs, w_gu, b_gu, w_down, b_down, layer):
    n_blocks = blk_e.shape[0]
    bm = MOE_BLOCK
    by_e = lambda i, be, nu, ne: (layer, be[i], 0, 0)
    x_blk = lambda i, be, nu, ne: (jnp.minimum(i, nu[0] - 1), 0)
    grid_spec = pltpu.PrefetchScalarGridSpec(
        num_scalar_prefetch=3,
        grid=(n_blocks,),
        in_specs=[pl.BlockSpec((bm, D_MODEL // 2), x_blk),
                  pl.BlockSpec(memory_space=pl.ANY),
                  pl.BlockSpec((1, 1, 1, 2 * D_EXPERT), by_e),
                  pl.BlockSpec(memory_space=pl.ANY),
                  pl.BlockSpec((1, 1, 1, D_MODEL), by_e)],
        out_specs=pl.BlockSpec((bm, D_MODEL // 2), lambda i, be, nu, ne: (i, 0)),
        scratch_shapes=[pltpu.VMEM((D_MODEL, 2 * D_EXPERT), f32),
                        pltpu.VMEM((D_EXPERT, D_MODEL), f32),
                        pltpu.VMEM((D_MODEL, 2 * D_EXPERT), bf16),
                        pltpu.VMEM((D_EXPERT, D_MODEL), bf16),
                        pltpu.SemaphoreType.DMA((2,))],
    )
    depth = w_gu.shape[0]
    return pl.pallas_call(
        functools.partial(_expert_kernel, layer=layer),
        grid_spec=grid_spec,
        out_shape=jax.ShapeDtypeStruct((n_blocks * bm, D_MODEL // 2), u32),
        compiler_params=_params("arbitrary"),
        name="moe_experts",
    )(blk_e, n_used, next_e, xs, w_gu, b_gu.reshape(depth, N_EXPERTS, 1, 2 * D_EXPERT),
      w_down, b_down.reshape(depth, N_EXPERTS, 1, D_MODEL))


def _combine_kernel(pstart_ref, meta_ref, meta_next_ref, lpos_ref, gates_ref, x1_ref, ln_ref, yb_hbm,
                    outp_ref, outs_ref, yl_ref, sems, *, alpha, n_prompt_tiles):
    tm = TOKEN_TILE
    i = pl.program_id(0)
    slot = i % 2

    def fetch(meta, s):
        def recv(local_row, sorted_row, n_rows):
            pltpu.make_async_copy(yb_hbm.at[pl.ds(sorted_row, n_rows), :],
                                  yl_ref.at[s, pl.ds(local_row, n_rows), :], sems.at[s]).start()

        _for_each_run_piece(meta, pstart_ref, recv)

    @pl.when(i == 0)
    def _():
        yl_ref[...] = jnp.zeros(yl_ref.shape, u32)
        fetch(meta_ref, slot)

    @pl.when(i + 1 < pl.num_programs(0))
    def _():
        fetch(meta_next_ref, 1 - slot)

    _wait_run_rows(meta_ref, lambda n_rows: pltpu.make_async_copy(
        yb_hbm.at[pl.ds(0, n_rows), :], yl_ref.at[slot, pl.ds(0, n_rows), :], sems.at[slot]))

    lpos = lpos_ref[...]
    gates = gates_ref[...]
    scol = lax.broadcasted_iota(i32, (tm, LOCAL_ROWS), 1).astype(f32)
    weight = jnp.zeros((tm, LOCAL_ROWS), f32)
    for k in range(TOP_K):
        weight = jnp.where(scol == lpos[:, k:k + 1], gates[:, k:k + 1], weight)
    y_lo, y_hi = _unpack_halves(yl_ref[slot])
    weight = weight.astype(bf16)
    y = jnp.concatenate([jnp.dot(weight, y_lo.astype(bf16), preferred_element_type=f32),
                         jnp.dot(weight, y_hi.astype(bf16), preferred_element_type=f32)], axis=1)
    x2 = _layer_norm_rows(alpha * x1_ref[...] + y, ln_ref[pl.ds(0, 1), :], ln_ref[pl.ds(1, 1), :])
    _store_group(outp_ref, outs_ref, x2, n_prompt_tiles)


def _combine(pstart, meta, lpos, gates, x1, ln, yb, alpha, n_prompt_tiles):
    n = x1.shape[0]
    tm = TOKEN_TILE
    n_tiles = n // tm
    row = lambda i, ps: (i, 0)
    smem_tile = lambda f: pl.BlockSpec((SUBLANES, LANES), f, memory_space=pltpu.SMEM)
    grid_spec = pltpu.PrefetchScalarGridSpec(
        num_scalar_prefetch=1,
        grid=(n_tiles,),
        in_specs=[smem_tile(row),
                  smem_tile(lambda i, ps: (jnp.minimum(i + 1, n_tiles - 1), 0)),
                  pl.BlockSpec((tm, LANES), row),
                  pl.BlockSpec((tm, LANES), row),
                  pl.BlockSpec((tm, D_MODEL), row),
                  pl.BlockSpec((SUBLANES, D_MODEL), lambda i, ps: (0, 0)),
                  pl.BlockSpec(memory_space=pl.ANY)],
        out_specs=_group_specs(D_MODEL, n_prompt_tiles),
        scratch_shapes=[pltpu.VMEM((2, LOCAL_ROWS, D_MODEL // 2), u32), pltpu.SemaphoreType.DMA((2,))],
    )
    return pl.pallas_call(
        functools.partial(_combine_kernel, alpha=alpha, n_prompt_tiles=n_prompt_tiles),
        grid_spec=grid_spec,
        out_shape=[jax.ShapeDtypeStruct((n_prompt_tiles * tm, D_MODEL), f32),
                   jax.ShapeDtypeStruct((n - n_prompt_tiles * tm, D_MODEL), f32)],
        compiler_params=_params("arbitrary"),
        name="moe_combine",
    )(pstart, meta, meta, lpos, gates, x1, ln, yb)


def _block_tables(counts, n_tokens):
    bm = MOE_BLOCK
    padded = (counts + bm - 1) // bm * bm
    pad_end = jnp.cumsum(padded)
    pstart = (pad_end - padded).astype(i32)
    max_used = n_tokens * TOP_K + (n_tokens // TOKEN_TILE) * N_EXPERTS * (SUBLANES - 1)
    n_blocks = -(-(max_used + N_EXPERTS * (bm - 1)) // bm)
    first_row = jnp.arange(n_blocks, dtype=pad_end.dtype) * bm
    blk_e = jnp.minimum(jnp.sum(pad_end[None, :] <= first_row[:, None], axis=1), N_EXPERTS - 1).astype(i32)
    n_used = (pad_end[-1] // bm).astype(i32).reshape(1)
    idx = jnp.arange(N_EXPERTS)
    later = jnp.logical_and(idx[None, :] > idx[:, None], counts[None, :] > 0)
    next_e = jnp.min(jnp.where(later, idx[None, :], N_EXPERTS), axis=1)
    next_e = jnp.where(next_e == N_EXPERTS, -1, next_e).astype(i32)
    return pstart, blk_e, n_used, next_e, n_blocks * bm


def _pad_time(t, n_batch, n_t, t_pad):
    w = t.shape[-1]
    return jnp.pad(t.reshape(n_batch, n_t, w), ((0, 0), (0, t_pad - n_t), (0, 0))).reshape(n_batch * t_pad, w)


def _block_diag_weight(w):
    h = w.shape[0]
    eye = jnp.eye(h, dtype=w.dtype)
    return (eye[:, None, :, None] * w[:, :, None, :]).reshape(h * HEAD_DIM, h * HEAD_DIM)


def _rows8(*rows):
    width = rows[0].shape[-1]
    m = jnp.stack([r.reshape(width) for r in rows])
    return jnp.pad(m, ((0, SUBLANES - m.shape[0]), (0, 0)))


def _layer(h, p, moe, layer, alpha, st_s, bp, tp, bs, ts):
    n_p, n_s = bp * tp, bs * ts
    tpad = SAMPLE_T_PAD
    (ret_p, ret_s), (rw_p, rw_s), (lru_p, lru_s) = _in_proj(*h, p['w_in'].astype(bf16))
    unpad = lambda y: y.reshape(bs, tpad, -1)[:, :ts].reshape(n_s, -1)

    c_p = RET_CHUNK if tp % RET_CHUNK == 0 else tp
    cos_p, sin_p = _rope_tables(jnp.arange(tp, dtype=f32))
    cos_s, sin_s = _rope_tables(PAST_LEN + jnp.arange(tpad, dtype=f32))
    y_ret_p, sret_p = _retention(ret_p.reshape(bp, tp, 4 * RET_W), cos_p, sin_p, _retention_tables(c_p, c_p),
                                 jnp.zeros((1, bp, RET_H, HEAD_DIM, HEAD_DIM), f32), 0,
                                 nb=max(d for d in (1, 2, 4, 8) if bp % d == 0), length=c_p)
    y_ret_s, sret_s = _retention(_pad_time(ret_s, bs, ts, tpad).reshape(bs, tpad, 4 * RET_W), cos_s, sin_s,
                                 _retention_tables(tpad, ts), st_s['ret'], layer, nb=SUBLANES, length=tpad)
    y_ret = (y_ret_p.reshape(n_p, RET_W), unpad(y_ret_s))

    lora = jnp.zeros((3, LANES, RWKV_W), f32)
    lora = lora.at[0, 0:32].set(p['w_up']).at[1, 32:64].set(p['a_up']).at[2, 64:128].set(p['g_up'])
    vec = _rows8(p['w0'], p['a0'], p['k_k'], p['k_a'], p['r_k'], p['gn_g'], p['gn_b'])
    mix = p['mix'].reshape(1, RWKV_PROJ)
    rows = RWKV_ROWS
    y_rwkv_p, srw_p, shift_p = _rwkv(rw_p, jnp.zeros((bp, 1, RWKV_PROJ), f32), jnp.ones((rows, RWKV_W), f32), mix,
                                     vec, lora, _chunk_tri(rows, RWKV_CHUNK),
                                     jnp.zeros((1, bp, RWKV_H, HEAD_DIM, HEAD_DIM), f32), 0,
                                     n_groups=bp, n_steps=tp // rows, rows=rows, length=RWKV_CHUNK, chain=True)
    valid_s = (jnp.arange(tpad) < ts).astype(f32)[:, None]
    rw_s3 = rw_s.reshape(bs, ts, RWKV_PROJ)
    prev_s = jnp.concatenate([st_s['shift'][:, None, :], rw_s3[:, :-1]], axis=1).reshape(n_s, RWKV_PROJ)
    seq_per_blk = rows // tpad
    y_rwkv_s, srw_s = _rwkv(_pad_time(rw_s, bs, ts, tpad), _pad_time(prev_s, bs, ts, tpad),
                            jnp.tile(jnp.broadcast_to(valid_s, (tpad, RWKV_W)), (seq_per_blk, 1)),
                            mix, vec, lora, _chunk_tri(rows, tpad), st_s['rwkv'], layer,
                            n_groups=bs // seq_per_blk, n_steps=1, rows=rows, length=tpad, chain=False)
    y_rwkv = (y_rwkv_p, unpad(y_rwkv_s))

    lvec = _rows8(p['conv_w'][0], p['conv_w'][1], p['conv_w'][2], p['conv_w'][3],
                  p['conv_b'], p['ba'], p['bx'], p['lam'])
    wa_bd = _block_diag_weight(p['wa']).astype(bf16)
    wx_bd = _block_diag_weight(p['wx']).astype(bf16)
    l_l = LRU_CHUNK if tp % LRU_CHUNK == 0 else tp
    y_lru_p, h_p, tail_p = _lru(lru_p, jnp.ones((l_l, LRU_W), f32), lvec, wa_bd, wx_bd,
                                jnp.zeros((bp, SUBLANES, LRU_W), f32), jnp.zeros((bp, SUBLANES, LRU_W), f32),
                                n_batch=bp, n_chunks=tp // l_l, nb=1, length=l_l, out_rows=n_p)
    conv0_s = jnp.pad(st_s['conv'], ((0, 0), (SUBLANES - (CONV_WIDTH - 1), 0), (0, 0)))
    h0_s = jnp.broadcast_to(st_s['lru'][:, None, :], (bs, SUBLANES, LRU_W))
    y_lru_s, h_s, _ = _lru(_pad_time(lru_s, bs, ts, tpad), jnp.broadcast_to(valid_s, (tpad, LRU_W)),
                           lvec, wa_bd, wx_bd, conv0_s, h0_s,
                           n_batch=bs, n_chunks=1, nb=SUBLANES, length=tpad, out_rows=bs * tpad)
    y_lru = (y_lru_p, unpad(y_lru_s))

    w_router = jnp.pad(p['w_router'], ((0, 0), (0, LANES - N_EXPERTS)))
    b_router = jnp.pad(p['b_router'], (0, LANES - N_EXPERTS), constant_values=-1e30).reshape(1, LANES)
    x1, lpos, gates, meta, cnt = _out_proj_router(y_ret, y_rwkv, y_lru, h, p['w_out'].astype(bf16),
                                                  _rows8(p['ln1_g'], p['ln1_b']), w_router, b_router, alpha)

    counts = cnt[0, :N_EXPERTS].astype(i32)
    pstart, blk_e, n_used, next_e, n_rows = _block_tables(counts, n_p + n_s)
    xs = _dispatch(pstart, counts, n_used, meta, lpos, x1, n_rows)
    yb = _experts(blk_e, n_used, next_e, xs, moe['w_gu'], moe['b_gu'], moe['w_down'], moe['b_down'], layer)
    x2 = _combine(pstart, meta, lpos, gates, x1, _rows8(p['ln2_g'], p['ln2_b']), yb, alpha, n_p // TOKEN_TILE)

    keep = CONV_WIDTH - 1
    assert tp >= SUBLANES and ts >= keep
    conv_s = lru_s.reshape(bs, ts, 2 * LRU_W)[:, ts - keep:, LRU_W:]
    new_p = (sret_p, srw_p, shift_p[:, 0], h_p[:, 0], tail_p[:, SUBLANES - keep:])
    new_s = (sret_s, srw_s, rw_s3[:, -1], h_s[:, 0], conv_s)
    return x2, new_p, new_s


def kernel(x_prompt, x_sample, state_ret, state_rwkv, state_rwkv_shift, state_lru, state_conv,
           w_in, w_out, ln1_g, ln1_b, ln2_g, ln2_b,
           rwkv_mix, rwkv_w0, rwkv_w_up, rwkv_a0, rwkv_a_up, rwkv_g_up, rwkv_k_k, rwkv_k_a, rwkv_r_k,
           rwkv_gn_g, rwkv_gn_b, lru_conv_w, lru_conv_b, lru_wa, lru_ba, lru_wx, lru_bx, lru_lambda,
           moe_w_router, moe_b_router, moe_w_gate_up, moe_b_gate_up, moe_w_down, moe_b_down):
    bp, tp, _ = x_prompt.shape
    bs, ts, _ = x_sample.shape
    depth = w_in.shape[0]
    alpha = (2.0 * depth) ** 0.25
    moe = {'w_gu': moe_w_gate_up, 'b_gu': moe_b_gate_up, 'w_down': moe_w_down, 'b_down': moe_b_down}
    h = (x_prompt.reshape(bp * tp, D_MODEL), x_sample.reshape(bs * ts, D_MODEL))
    new_p, new_s = [], []
    for l in range(depth):
        p = {'w_in': w_in[l], 'w_out': w_out[l], 'ln1_g': ln1_g[l], 'ln1_b': ln1_b[l],
             'ln2_g': ln2_g[l], 'ln2_b': ln2_b[l], 'mix': rwkv_mix[l], 'w0': rwkv_w0[l],
             'w_up': rwkv_w_up[l], 'a0': rwkv_a0[l], 'a_up': rwkv_a_up[l], 'g_up': rwkv_g_up[l],
             'k_k': rwkv_k_k[l], 'k_a': rwkv_k_a[l], 'r_k': rwkv_r_k[l], 'gn_g': rwkv_gn_g[l],
             'gn_b': rwkv_gn_b[l], 'conv_w': lru_conv_w[l], 'conv_b': lru_conv_b[l], 'wa': lru_wa[l],
             'ba': lru_ba[l], 'wx': lru_wx[l], 'bx': lru_bx[l], 'lam': lru_lambda[l],
             'w_router': moe_w_router[l], 'b_router': moe_b_router[l]}
        st_s = {'ret': state_ret, 'rwkv': state_rwkv, 'shift': state_rwkv_shift[l],
                'lru': state_lru[l], 'conv': state_conv[l]}
        h, sp, ss = _layer(h, p, moe, l, alpha, st_s, bp, tp, bs, ts)
        new_p.append(sp)
        new_s.append(ss)
    outs = [h[0].reshape(bp, tp, D_MODEL), h[1].reshape(bs, ts, D_MODEL)]
    for i in range(5):
        outs.append(jnp.stack([s[i] for s in new_p]))
        outs.append(jnp.stack([s[i] for s in new_s]))
    return tuple(outs)
```

```python
import functools

import jax
import jax.numpy as jnp
from jax import lax
from jax.experimental import pallas as pl
from jax.experimental.pallas import tpu as pltpu

f32 = jnp.float32
bf16 = jnp.bfloat16
i32 = jnp.int32
u32 = jnp.uint32

D_MODEL = 1024
HEAD_DIM = 64
RET_W = 256
RET_H = 4
RET_CHUNK = 128
ROPE_BASE = 10000.0
RWKV_W = 512
RWKV_H = 8
RWKV_PROJ = 1664
RWKV_LORA_COL = 1536
RWKV_GN_EPS = 64e-5
RWKV_CHUNK = 64
RWKV_ROWS = 128
LRU_W = 256
LRU_C = 8.0
LRU_CHUNK = 256
CONV_WIDTH = 4
D_PROJ = 3200
N_EXPERTS = 32
TOP_K = 4
D_EXPERT = 1024
SWIGLU_LIMIT = 7.0
SWIGLU_ALPHA = 1.702
LN_EPS = 1e-5
PAST_LEN = 16384.0

LANES = 128
SUBLANES = 8
SAMPLE_T_PAD = 8
TOKEN_TILE = 512
MOE_BLOCK = 512
LOCAL_ROWS = TOP_K * TOKEN_TILE + N_EXPERTS * SUBLANES
RUN_PIECES = tuple(SUBLANES << j for j in reversed(range((TOKEN_TILE // SUBLANES).bit_length())))
WAIT_PIECES = tuple(SUBLANES << j for j in reversed(range((LOCAL_ROWS // SUBLANES).bit_length())))
VMEM_LIMIT = 56 * 1024 * 1024

_NT = (((1,), (1,)), ((), ()))
_TN = (((0,), (0,)), ((), ()))


def _params(*sem):
    return pltpu.CompilerParams(dimension_semantics=sem, vmem_limit_bytes=VMEM_LIMIT)


def _mm(a, b):
    return jnp.dot(a.astype(bf16), b.astype(bf16), preferred_element_type=f32)


def _mm_nt(a, b):
    return lax.dot_general(a.astype(bf16), b.astype(bf16), _NT, preferred_element_type=f32)


def _mm_tn(a, b):
    return lax.dot_general(a.astype(bf16), b.astype(bf16), _TN, preferred_element_type=f32)


def _mm3(a, b):
    a_hi, b_hi = a.astype(bf16), b.astype(bf16)
    a_lo = (a - a_hi.astype(f32)).astype(bf16)
    b_lo = (b - b_hi.astype(f32)).astype(bf16)
    dot = functools.partial(jnp.dot, preferred_element_type=f32)
    return dot(a_hi, b_hi) + dot(a_hi, b_lo) + dot(a_lo, b_hi)


def _pack_halves(x):
    w = x.shape[-1] // 2
    lo = lax.bitcast_convert_type(x[:, :w], u32)
    hi = lax.bitcast_convert_type(x[:, w:], u32)
    return (lo >> 16) | hi


def _unpack_halves(p):
    return (lax.bitcast_convert_type(p << 16, f32),
            lax.bitcast_convert_type(p & jnp.uint32(0xFFFF0000), f32))


def _softplus(x):
    return jnp.maximum(x, 0.0) + jnp.log(1.0 + jnp.exp(-jnp.abs(x)))


def _sigmoid(x):
    return 1.0 / (1.0 + jnp.exp(-x))


def _half_masks():
    lane = lax.broadcasted_iota(i32, (1, LANES), 1)
    m0 = (lane < HEAD_DIM).astype(f32)
    return m0, 1.0 - m0


def _seg_mean(x, m0, m1):
    s0 = jnp.sum(x * m0, axis=-1, keepdims=True)
    s1 = jnp.sum(x * m1, axis=-1, keepdims=True)
    return (m0 * s0 + m1 * s1) * (1.0 / HEAD_DIM)


def _seg_sum(x, m0, m1):
    s0 = jnp.sum(x * m0, axis=-1, keepdims=True)
    s1 = jnp.sum(x * m1, axis=-1, keepdims=True)
    return m0 * s0 + m1 * s1


def _block_diag_mask():
    r = lax.broadcasted_iota(i32, (LANES, LANES), 0) // HEAD_DIM
    c = lax.broadcasted_iota(i32, (LANES, LANES), 1) // HEAD_DIM
    return (r == c).astype(f32)


def _layer_norm_rows(z, g, b):
    mu = jnp.mean(z, axis=-1, keepdims=True)
    zc = z - mu
    var = jnp.mean(zc * zc, axis=-1, keepdims=True)
    return zc * lax.rsqrt(var + LN_EPS) * g + b


def _group_specs(width, n_prompt_tiles):
    return [pl.BlockSpec((TOKEN_TILE, width), lambda i, *_: (jnp.minimum(i, n_prompt_tiles - 1), 0)),
            pl.BlockSpec((TOKEN_TILE, width), lambda i, *_: (jnp.maximum(i - n_prompt_tiles, 0), 0))]


def _pick_group(p_ref, s_ref, n_prompt_tiles):
    return jnp.where(pl.program_id(0) >= n_prompt_tiles, s_ref[...], p_ref[...])


def _n_tiles(h_p, h_s):
    assert h_p.shape[0] % TOKEN_TILE == 0 and h_s.shape[0] % TOKEN_TILE == 0, (h_p.shape, h_s.shape)
    return h_p.shape[0] // TOKEN_TILE, h_s.shape[0] // TOKEN_TILE


def _store_group(p_ref, s_ref, value, n_prompt_tiles):
    @pl.when(pl.program_id(0) < n_prompt_tiles)
    def _():
        p_ref[...] = value

    @pl.when(pl.program_id(0) >= n_prompt_tiles)
    def _():
        s_ref[...] = value


def _in_proj_kernel(xp_ref, xs_ref, w_ref, retp_ref, rets_ref, rwp_ref, rws_ref, lrup_ref, lrus_ref, *,
                    n_prompt_tiles):
    c0, c1 = 4 * RET_W, 4 * RET_W + RWKV_PROJ
    dot = functools.partial(jnp.dot, preferred_element_type=f32)

    def project(x_ref, ret_ref, rw_ref, lru_ref):
        xb = x_ref[...].astype(bf16)
        ret_ref[...] = dot(xb, w_ref[:, :c0])
        rw_ref[...] = dot(xb, w_ref[:, c0:c1])
        lru_ref[...] = dot(xb, w_ref[:, c1:])

    @pl.when(pl.program_id(0) < n_prompt_tiles)
    def _():
        project(xp_ref, retp_ref, rwp_ref, lrup_ref)

    @pl.when(pl.program_id(0) >= n_prompt_tiles)
    def _():
        project(xs_ref, rets_ref, rws_ref, lrus_ref)


def _in_proj(h_p, h_s, w_bf):
    npt, nst = _n_tiles(h_p, h_s)
    widths = (4 * RET_W, RWKV_PROJ, 2 * LRU_W)
    out = pl.pallas_call(
        functools.partial(_in_proj_kernel, n_prompt_tiles=npt),
        grid=(npt + nst,),
        in_specs=_group_specs(D_MODEL, npt) + [
                  pl.BlockSpec((D_MODEL, D_PROJ), lambda i: (0, 0))],
        out_specs=[s for w in widths for s in _group_specs(w, npt)],
        out_shape=[jax.ShapeDtypeStruct((rows, w), f32) for w in widths for rows in (h_p.shape[0], h_s.shape[0])],
        compiler_params=_params("arbitrary"),
        name="in_proj",
    )(h_p, h_s, w_bf)
    return out[0:2], out[2:4], out[4:6]


def _rope_tables(pos):
    half = HEAD_DIM // 2
    inv = ROPE_BASE ** (-jnp.arange(half, dtype=f32) / half)
    ang = pos[:, None] * inv[None, :]
    cos, sin = jnp.cos(ang), jnp.sin(ang)
    cos_f = jnp.tile(jnp.concatenate([cos, cos], axis=-1), (1, RET_H))
    sin_f = jnp.tile(jnp.concatenate([-sin, sin], axis=-1), (1, RET_H))
    return cos_f, sin_f


def _retention_tables(length, n_valid):
    lg = jnp.log1p(-jnp.exp2(-5.0 - jnp.arange(RET_H, dtype=f32)))
    idx = jnp.arange(length, dtype=f32)
    rel = idx[:, None] - idx[None, :]
    mask = jnp.where(rel[None] >= 0, jnp.exp(jnp.maximum(rel, 0.0)[None] * lg[:, None, None]), 0.0)
    q_dec = jnp.exp((idx[:, None] + 1.0) * lg[None, :])
    k_dec = jnp.where(idx[:, None] < n_valid, jnp.exp((n_valid - 1.0 - idx)[:, None] * lg[None, :]), 0.0)
    c_dec = jnp.exp(n_valid * lg)[None, :]
    rep = lambda t: jnp.repeat(t, HEAD_DIM, axis=-1)
    return mask, rep(q_dec), rep(k_dec), rep(c_dec)


def _retention_kernel(ret_ref, cos_ref, sin_ref, mask_ref, qdec_ref, kdec_ref, cdec_ref, s0_ref,
                      y_ref, s_ref, pair_ref, *, nb):
    @pl.when(pl.program_id(1) == 0)
    def _():
        _load_head_pairs(pair_ref, s0_ref)

    m0, m1 = _half_masks()
    lane = lax.broadcasted_iota(i32, (1, LANES), 1)
    first_half = (lane % HEAD_DIM) < (HEAD_DIM // 2)
    bd = _block_diag_mask()

    def rope(x, cs, sn):
        swapped = jnp.where(first_half, pltpu.roll(x, LANES - HEAD_DIM // 2, 1), pltpu.roll(x, HEAD_DIM // 2, 1))
        return x * cs + swapped * sn

    for j in range(nb):
        for p in range(RET_H // 2):
            cols = pl.ds(p * LANES, LANES)
            cs, sn = cos_ref[:, cols], sin_ref[:, cols]
            q2 = rope(ret_ref[j, :, pl.ds(p * LANES, LANES)], cs, sn)
            k2 = rope(ret_ref[j, :, pl.ds(RET_W + p * LANES, LANES)], cs, sn) * (HEAD_DIM ** -0.5)
            v2 = ret_ref[j, :, pl.ds(2 * RET_W + p * LANES, LANES)]
            g2 = ret_ref[j, :, pl.ds(3 * RET_W + p * LANES, LANES)]
            state = pair_ref[j, p]
            o2 = _mm(q2, state) * qdec_ref[:, cols]
            for hh, m in enumerate((m0, m1)):
                sc = _mm_nt(q2 * m, k2) * mask_ref[2 * p + hh]
                o2 = o2 + _mm(sc, v2) * m
            pair_ref[j, p] = state * cdec_ref[:, cols] + _mm_tn(k2 * kdec_ref[:, cols], v2) * bd
            mu = _seg_mean(o2, m0, m1)
            oc = o2 - mu
            var = _seg_mean(oc * oc, m0, m1)
            y_ref[j, :, cols] = g2 * _sigmoid(g2) * oc * lax.rsqrt(var + LN_EPS)

    @pl.when(pl.program_id(1) == pl.num_programs(1) - 1)
    def _():
        _store_head_pairs(s_ref, pair_ref)


def _load_head_pairs(pair_ref, s_ref):
    pair_ref[...] = jnp.zeros(pair_ref.shape, f32)
    for j in range(pair_ref.shape[0]):
        for p in range(pair_ref.shape[1]):
            pair_ref[j, p, pl.ds(0, HEAD_DIM), pl.ds(0, HEAD_DIM)] = s_ref[j, 2 * p]
            pair_ref[j, p, pl.ds(HEAD_DIM, HEAD_DIM), pl.ds(HEAD_DIM, HEAD_DIM)] = s_ref[j, 2 * p + 1]


def _store_head_pairs(s_ref, pair_ref):
    for j in range(pair_ref.shape[0]):
        for p in range(pair_ref.shape[1]):
            s_ref[j, 2 * p] = pair_ref[j, p, pl.ds(0, HEAD_DIM), pl.ds(0, HEAD_DIM)]
            s_ref[j, 2 * p + 1] = pair_ref[j, p, pl.ds(HEAD_DIM, HEAD_DIM), pl.ds(HEAD_DIM, HEAD_DIM)]


def _retention(ret, cos_f, sin_f, tables, s0, layer, *, nb, length):
    mask, q_dec, k_dec, c_dec = tables
    n_batch, t_len, _ = ret.shape
    blk = lambda g, c: (g, c, 0)
    const2 = lambda g, c: (0, 0)
    st = lambda g, c: (g, 0, 0, 0)
    return pl.pallas_call(
        functools.partial(_retention_kernel, nb=nb),
        grid=(n_batch // nb, t_len // length),
        in_specs=[pl.BlockSpec((nb, length, 4 * RET_W), blk),
                  pl.BlockSpec((length, RET_W), lambda g, c: (c, 0)),
                  pl.BlockSpec((length, RET_W), lambda g, c: (c, 0)),
                  pl.BlockSpec((RET_H, length, length), lambda g, c: (0, 0, 0)),
                  pl.BlockSpec((length, RET_W), const2),
                  pl.BlockSpec((length, RET_W), const2),
                  pl.BlockSpec((1, RET_W), const2),
                  pl.BlockSpec((None, nb, RET_H, HEAD_DIM, HEAD_DIM), lambda g, c: (layer, g, 0, 0, 0))],
        out_specs=[pl.BlockSpec((nb, length, RET_W), blk),
                   pl.BlockSpec((nb, RET_H, HEAD_DIM, HEAD_DIM), st)],
        out_shape=[jax.ShapeDtypeStruct((n_batch, t_len, RET_W), f32),
                   jax.ShapeDtypeStruct((n_batch, RET_H, HEAD_DIM, HEAD_DIM), f32)],
        scratch_shapes=[pltpu.VMEM((nb, RET_H // 2, LANES, LANES), f32)],
        compiler_params=_params("arbitrary", "arbitrary"),
        name="retention",
    )(ret, cos_f, sin_f, mask, q_dec, k_dec, c_dec, s0)


def _rwkv_kernel(rw_ref, aux_ref, valid_ref, mix_ref, vec_ref, lora_ref, tri_ref, s0_ref,
                 y_ref, s_ref, *rest, rows, length, chain):
    n_chunks = rows // length
    rw = rw_ref[...]
    if chain:
        last_ref, xs_ref, pair_ref = rest

        @pl.when(pl.program_id(1) == 0)
        def _():
            _load_head_pairs(pair_ref, s0_ref)
            xs_ref[pl.ds(0, SUBLANES), :] = jnp.broadcast_to(aux_ref[0], (SUBLANES, RWKV_PROJ))

        xs_ref[pl.ds(SUBLANES, rows), :] = rw
        prev = xs_ref[pl.ds(SUBLANES - 1, rows), :]
        xs_ref[pl.ds(0, SUBLANES), :] = rw[rows - SUBLANES:, :]
        last_ref[0] = rw[rows - 1:rows, :]
    else:
        pair_ref, = rest
        _load_head_pairs(pair_ref, s0_ref)
        prev = aux_ref[...]
    rwm = rw + (prev - rw) * mix_ref[...]

    valid = valid_ref[...]
    w0, a0, k_k, k_a, r_k, gn_g, gn_b = (vec_ref[pl.ds(i, 1), :] for i in range(7))
    lo = rwm[:, RWKV_LORA_COL:]
    lw = _mm3(jnp.tanh(lo), lora_ref[0])
    la = _mm3(lo, lora_ref[1])
    gate = _mm3(_sigmoid(lo), lora_ref[2])
    logw = -jnp.exp(-_softplus(-(w0 + lw)) - 0.5) * valid
    a = _sigmoid(a0 + la)
    r = rwm[:, :RWKV_W]
    kr = rwm[:, RWKV_W:2 * RWKV_W]
    vr = rwm[:, 2 * RWKV_W:3 * RWKV_W]
    kk_raw = kr * k_k
    kp = kr * (1.0 + (a - 1.0) * k_a) * valid
    cum = _mm3(tri_ref[...], logw)
    g_incl = jnp.exp(cum)
    g_inv = jnp.exp(-cum)
    g_prev = jnp.exp(cum - logw)
    g_end = jnp.concatenate(
        [jnp.broadcast_to(g_incl[(c + 1) * length - 1:(c + 1) * length, :], (length, RWKV_W)) for c in range(n_chunks)],
        axis=0)

    m0, m1 = _half_masks()
    bd = _block_diag_mask()
    stacked = 2 * rows
    ri = lax.broadcasted_iota(i32, (stacked, stacked), 0)
    ci = lax.broadcasted_iota(i32, (stacked, stacked), 1)
    same = (ri // length) == (ci // length)
    strict = jnp.logical_and(same, ci < ri)
    incl = jnp.logical_and(same, ci <= ri)
    eye = (ci == ri).astype(f32)
    stack = lambda t: jnp.concatenate([t * m0, t * m1], axis=0)
    unstack = lambda t: t[:rows] + t[rows:]
    pairs = range(RWKV_H // 2)

    pre = []
    for p in pairs:
        sl = slice(p * LANES, (p + 1) * LANES)
        kk2 = kk_raw[:, sl]
        kk2 = kk2 * lax.rsqrt(jnp.maximum(_seg_sum(kk2 * kk2, m0, m1), 1e-24)) * valid[:, sl]
        d = dict(sl=sl, r2=r[:, sl], v2=vr[:, sl], kp2=kp[:, sl], ge=g_end[:, sl])
        d['kh'] = kk2 * g_prev[:, sl]
        rh = d['r2'] * g_incl[:, sl]
        bt = kk2 * a[:, sl] * g_inv[:, sl]
        kt = d['kp2'] * g_inv[:, sl]
        d['b_end'] = bt * d['ge']
        d['k_end'] = kt * d['ge']
        d['kh_s'], d['rh_s'], d['v_s'] = stack(d['kh']), stack(rh), stack(d['v2'])
        gram = _mm_nt(jnp.concatenate([d['kh_s'], d['rh_s']], axis=0), jnp.concatenate([stack(bt), stack(kt)], axis=0))
        d['x'] = -jnp.where(strict, gram[:stacked, :stacked], 0.0)
        d['a_k'] = jnp.where(strict, gram[:stacked, stacked:], 0.0)
        d['r_b'] = jnp.where(incl, gram[stacked:, :stacked], 0.0)
        d['r_k'] = jnp.where(incl, gram[stacked:, stacked:], 0.0)
        d['t'] = eye + d['x']
        pre.append(d)
    cover = 2
    while cover < length:
        for d in pre:
            d['x'] = _mm(d['x'], d['x'])
            d['t'] = d['t'] + _mm(d['t'], d['x'])
        cover *= 2
    for d in pre:
        rhs0 = -_mm(d['a_k'], d['v_s'])
        if chain:
            tz = _mm(d['t'], jnp.concatenate([rhs0, d['kh_s']], axis=1))
            rbz = _mm(d['r_b'], tz)
            d['z0'] = unstack(tz[:, :LANES])
            d['q'] = unstack(d['rh_s'] - rbz[:, LANES:])
            d['y0'] = unstack(rbz[:, :LANES] + _mm(d['r_k'], d['v_s']))
            d['w'] = unstack(_mm_tn(d['t'], stack(d['b_end'])))
        else:
            d['z0'] = unstack(_mm(d['t'], rhs0))

    for p, d in zip(pairs, pre):
        sl, ge, kh, v2 = d['sl'], d['ge'], d['kh'], d['v2']
        chunk = lambda c: slice(c * length, (c + 1) * length)
        g_last = lambda c: ge[(c + 1) * length - 1:(c + 1) * length, :]
        if chain:
            state = pair_ref[0, p]
            ys = []
            for c in range(n_chunks):
                cr = chunk(c)
                ys.append(_mm_nt(d['q'][cr], state) + d['y0'][cr])
                n_c = (_mm_tn(d['z0'][cr], d['b_end'][cr]) + _mm_tn(v2[cr], d['k_end'][cr])) * bd
                kw = _mm_tn(kh[cr], d['w'][cr]) * bd
                state = state * g_last(c) - _mm(state, kw) + n_c
            pair_ref[0, p] = state
            y2 = jnp.concatenate(ys, axis=0) if n_chunks > 1 else ys[0]
        else:
            rh = unstack(d['rh_s'])
            ks, rs = [], []
            for c in range(n_chunks):
                cr = chunk(c)
                both = _mm_nt(jnp.concatenate([kh[cr], rh[cr]], axis=0), pair_ref[c, p])
                ks.append(both[:length])
                rs.append(both[length:])
            z = d['z0'] - unstack(_mm(d['t'], stack(jnp.concatenate(ks, axis=0))))
            y2 = jnp.concatenate(rs, axis=0) + unstack(_mm(d['r_b'], stack(z)) + _mm(d['r_k'], d['v_s']))
            for c in range(n_chunks):
                cr = chunk(c)
                update = _mm_tn(jnp.concatenate([z[cr], v2[cr]], axis=0),
                                jnp.concatenate([d['b_end'][cr], d['k_end'][cr]], axis=0)) * bd
                pair_ref[c, p] = pair_ref[c, p] * g_last(c) + update

        mu = _seg_mean(y2, m0, m1)
        yc = y2 - mu
        var = _seg_mean(yc * yc, m0, m1)
        yn = yc * lax.rsqrt(var + RWKV_GN_EPS) * gn_g[:, sl] + gn_b[:, sl]
        bonus = _seg_sum(d['r2'] * d['kp2'] * r_k[:, sl], m0, m1) * v2
        y_ref[:, pl.ds(p * LANES, LANES)] = (yn + bonus) * gate[:, sl]

    if chain:
        @pl.when(pl.program_id(1) == pl.num_programs(1) - 1)
        def _():
            _store_head_pairs(s_ref, pair_ref)
    else:
        _store_head_pairs(s_ref, pair_ref)


def _rwkv(rw, aux, valid, mix, vec, lora, tri, s0, layer, *, n_groups, n_steps, rows, length, chain):
    blk = lambda g, c: (g * n_steps + c, 0)
    const2 = lambda g, c: (0, 0)
    st = lambda g, c: (g, 0, 0, 0)
    n_state = 1 if chain else rows // length
    aux_spec = (pl.BlockSpec((1, 1, RWKV_PROJ), lambda g, c: (g, 0, 0)) if chain
                else pl.BlockSpec((rows, RWKV_PROJ), blk))
    return pl.pallas_call(
        functools.partial(_rwkv_kernel, rows=rows, length=length, chain=chain),
        grid=(n_groups, n_steps),
        in_specs=[pl.BlockSpec((rows, RWKV_PROJ), blk),
                  aux_spec,
                  pl.BlockSpec((rows, RWKV_W), const2),
                  pl.BlockSpec((1, RWKV_PROJ), const2),
                  pl.BlockSpec((SUBLANES, RWKV_W), const2),
                  pl.BlockSpec((3, LANES, RWKV_W), lambda g, c: (0, 0, 0)),
                  pl.BlockSpec((rows, rows), const2),
                  pl.BlockSpec((None, n_state, RWKV_H, HEAD_DIM, HEAD_DIM), lambda g, c: (layer, g, 0, 0, 0))],
        out_specs=[pl.BlockSpec((rows, RWKV_W), blk),
                   pl.BlockSpec((n_state, RWKV_H, HEAD_DIM, HEAD_DIM), st)]
        + ([pl.BlockSpec((1, 1, RWKV_PROJ), lambda g, c: (g, 0, 0))] if chain else []),
        out_shape=[jax.ShapeDtypeStruct((n_groups * n_steps * rows, RWKV_W), f32),
                   jax.ShapeDtypeStruct((n_groups * n_state, RWKV_H, HEAD_DIM, HEAD_DIM), f32)]
        + ([jax.ShapeDtypeStruct((n_groups, 1, RWKV_PROJ), f32)] if chain else []),
        scratch_shapes=([pltpu.VMEM((rows + SUBLANES, RWKV_PROJ), f32)] if chain else [])
        + [pltpu.VMEM((n_state, RWKV_H // 2, LANES, LANES), f32)],
        compiler_params=_params("arbitrary", "arbitrary"),
        name="rwkv7",
    )(rw, aux, valid, mix, vec, lora, tri, s0)


def _chunk_tri(rows, length):
    idx = jnp.arange(rows)
    same = (idx[:, None] // length) == (idx[None, :] // length)
    return jnp.logical_and(same, idx[None, :] <= idx[:, None]).astype(f32)


def _lru_kernel(lru_ref, valid_ref, vec_ref, wa_ref, wx_ref, conv0_ref, h0_ref,
                y_ref, h_ref, tail_ref, xext_ref, *, nb, length):
    @pl.when(pl.program_id(1) == 0)
    def _():
        h_ref[...] = h0_ref[...]
        xext_ref[:, pl.ds(0, SUBLANES), :] = conv0_ref[...]

    valid = valid_ref[...] > 0.5
    row = lax.broadcasted_iota(i32, (length, LRU_W), 0)
    cw = [vec_ref[pl.ds(i, 1), :] for i in range(CONV_WIDTH)]
    cb, ba, bx, lam = (vec_ref[pl.ds(i, 1), :] for i in range(CONV_WIDTH, CONV_WIDTH + 4))
    sp = _softplus(-lam)
    for j in range(nb):
        rows = pl.ds(j * length, length)
        gbr = lru_ref[rows, pl.ds(0, LRU_W)]
        x = lru_ref[rows, pl.ds(LRU_W, LRU_W)]
        xext_ref[j, pl.ds(SUBLANES, length), :] = x
        xc = cb + x * cw[CONV_WIDTH - 1]
        for t in range(CONV_WIDTH - 1):
            xc = xc + xext_ref[j, pl.ds(SUBLANES - (CONV_WIDTH - 1) + t, length), :] * cw[t]
        xext_ref[j, pl.ds(0, SUBLANES), :] = x[length - SUBLANES:, :]
        tail_ref[j] = x[length - SUBLANES:, :]
        gate_a = _sigmoid(_mm(xc, wa_ref[...]) + ba)
        gate_x = _sigmoid(_mm(xc, wx_ref[...]) + bx)
        log_a = -LRU_C * gate_a * sp
        a = jnp.exp(log_a)
        b = xc * gate_x * jnp.sqrt(1.0 - jnp.exp(2.0 * log_a))
        a = jnp.where(valid, a, 1.0)
        b = jnp.where(valid, b, 0.0)
        shift = 1
        while shift < length:
            inside = row >= shift
            b = a * jnp.where(inside, pltpu.roll(b, shift, 0), 0.0) + b
            a = a * jnp.where(inside, pltpu.roll(a, shift, 0), 1.0)
            shift *= 2
        hs = a * h_ref[j, pl.ds(0, 1), :] + b
        h_ref[j] = jnp.broadcast_to(hs[length - 1:length, :], (SUBLANES, LRU_W))
        c = 0.7978845608028654
        gelu = 0.5 * gbr * (1.0 + jnp.tanh(c * (gbr + 0.044715 * gbr * gbr * gbr)))
        y_ref[rows, :] = hs * gelu


def _lru(lru, valid, vec, wa_bd, wx_bd, conv0, h0, *, n_batch, n_chunks, nb, length, out_rows):
    blk = lambda g, c: (g * n_chunks + c, 0)
    const2 = lambda g, c: (0, 0)
    st = lambda g, c: (g, 0, 0)
    return pl.pallas_call(
        functools.partial(_lru_kernel, nb=nb, length=length),
        grid=(n_batch // nb, n_chunks),
        in_specs=[pl.BlockSpec((nb * length, 2 * LRU_W), blk),
                  pl.BlockSpec((length, LRU_W), const2),
                  pl.BlockSpec((SUBLANES, LRU_W), const2),
                  pl.BlockSpec((LRU_W, LRU_W), const2),
                  pl.BlockSpec((LRU_W, LRU_W), const2),
                  pl.BlockSpec((nb, SUBLANES, LRU_W), st),
                  pl.BlockSpec((nb, SUBLANES, LRU_W), st)],
        out_specs=[pl.BlockSpec((nb * length, LRU_W), blk),
                   pl.BlockSpec((nb, SUBLANES, LRU_W), st),
                   pl.BlockSpec((nb, SUBLANES, LRU_W), st)],
        out_shape=[jax.ShapeDtypeStruct((out_rows, LRU_W), f32),
                   jax.ShapeDtypeStruct((n_batch, SUBLANES, LRU_W), f32),
                   jax.ShapeDtypeStruct((n_batch, SUBLANES, LRU_W), f32)],
        scratch_shapes=[pltpu.VMEM((nb, length + SUBLANES, LRU_W), f32)],
        compiler_params=_params("arbitrary", "arbitrary"),
        name="rg_lru",
    )(lru, valid, vec, wa_bd, wx_bd, conv0, h0)


def _out_proj_kernel(yrp_ref, yrs_ref, ywp_ref, yws_ref, ylp_ref, yls_ref, xp_ref, xs_ref,
                     w_ref, ln_ref, wr_ref, br_ref,
                     x1_ref, lpos_ref, g_ref, meta_ref, cnt_ref, *, alpha, n_prompt_tiles):
    @pl.when(pl.program_id(0) == 0)
    def _():
        cnt_ref[...] = jnp.zeros(cnt_ref.shape, f32)

    pick = functools.partial(_pick_group, n_prompt_tiles=n_prompt_tiles)
    mixed = (_mm(pick(yrp_ref, yrs_ref), w_ref[pl.ds(0, RET_W), :])
             + _mm(pick(ywp_ref, yws_ref), w_ref[pl.ds(RET_W, RWKV_W), :])
             + _mm(pick(ylp_ref, yls_ref), w_ref[pl.ds(RET_W + RWKV_W, LRU_W), :]))
    x1 = _layer_norm_rows(alpha * pick(xp_ref, xs_ref) + mixed, ln_ref[pl.ds(0, 1), :], ln_ref[pl.ds(1, 1), :])
    x1_ref[...] = x1
    logits = _mm3(x1, wr_ref[...]) + br_ref[...]
    tm = logits.shape[0]
    lane = lax.broadcasted_iota(i32, logits.shape, 1).astype(f32)
    top_v = jnp.zeros(logits.shape, f32)
    work = logits
    v_max = None
    onehots = []
    for k in range(TOP_K):
        v = jnp.max(work, axis=-1, keepdims=True)
        idx = jnp.min(jnp.where(work == v, lane, float(LANES)), axis=-1, keepdims=True)
        if k == 0:
            v_max = v
        hit = lane == idx
        onehots.append(hit.astype(f32))
        top_v = jnp.where(lane == k, jnp.exp(v - v_max), top_v)
        work = jnp.where(hit, -jnp.inf, work)
    g_ref[...] = top_v / jnp.sum(top_v, axis=-1, keepdims=True)

    total = onehots[0] + onehots[1] + onehots[2] + onehots[3]
    ri = lax.broadcasted_iota(i32, (tm, tm), 0)
    ci = lax.broadcasted_iota(i32, (tm, tm), 1)
    before = _mm((ci < ri).astype(f32), total)
    groups = jnp.floor((jnp.sum(total, axis=0, keepdims=True) + (SUBLANES - 1.0)) * (1.0 / SUBLANES))
    er = lax.broadcasted_iota(i32, (LANES, LANES), 0)
    ec = lax.broadcasted_iota(i32, (LANES, LANES), 1)
    run_len = groups * SUBLANES
    run_start = _mm(jnp.broadcast_to(groups, (SUBLANES, LANES)), (er < ec).astype(f32))[0:1] * SUBLANES
    lpos = jnp.full(logits.shape, -1.0, f32)
    for k in range(TOP_K):
        pos = jnp.sum(onehots[k] * (before + run_start), axis=-1, keepdims=True)
        lpos = jnp.where(lane == k, pos, lpos)
    lpos_ref[...] = lpos
    row = lax.broadcasted_iota(i32, (SUBLANES, LANES), 0)
    meta = jnp.where(row == 0, run_start, jnp.where(row == 1, run_len, jnp.where(row == 2, cnt_ref[...], 0.0)))
    meta_ref[...] = meta.astype(i32)
    cnt_ref[...] = cnt_ref[...] + run_len


def _out_proj_router(y_ret, y_rwkv, y_lru, h, w_out_bf, ln, w_router, b_router, alpha):
    npt, nst = _n_tiles(*h)
    n = h[0].shape[0] + h[1].shape[0]
    tm = TOKEN_TILE
    row = lambda i: (i, 0)
    const = lambda i: (0, 0)
    return pl.pallas_call(
        functools.partial(_out_proj_kernel, alpha=alpha, n_prompt_tiles=npt),
        grid=(npt + nst,),
        in_specs=_group_specs(RET_W, npt) + _group_specs(RWKV_W, npt) + _group_specs(LRU_W, npt)
        + _group_specs(D_MODEL, npt) + [
                  pl.BlockSpec((D_MODEL, D_MODEL), const),
                  pl.BlockSpec((SUBLANES, D_MODEL), const),
                  pl.BlockSpec((D_MODEL, LANES), const),
                  pl.BlockSpec((1, LANES), const)],
        out_specs=[pl.BlockSpec((tm, D_MODEL), row), pl.BlockSpec((tm, LANES), row), pl.BlockSpec((tm, LANES), row),
                   pl.BlockSpec((SUBLANES, LANES), row), pl.BlockSpec((SUBLANES, LANES), const)],
        out_shape=[jax.ShapeDtypeStruct((n, D_MODEL), f32),
                   jax.ShapeDtypeStruct((n, LANES), f32),
                   jax.ShapeDtypeStruct((n, LANES), f32),
                   jax.ShapeDtypeStruct((n // tm * SUBLANES, LANES), i32),
                   jax.ShapeDtypeStruct((SUBLANES, LANES), f32)],
        compiler_params=_params("arbitrary"),
        name="out_proj_router",
    )(*y_ret, *y_rwkv, *y_lru, *h, w_out_bf, ln, w_router, b_router)


def _for_each_run_piece(meta_ref, pstart_ref, fn):
    for e in range(N_EXPERTS):
        start, n = meta_ref[0, e], meta_ref[1, e]
        base = pstart_ref[e] + meta_ref[2, e]
        for sz in RUN_PIECES:
            done = n & ~(2 * sz - 1)

            @pl.when((n & sz) != 0)
            def _():
                fn(pl.multiple_of(start + done, SUBLANES), pl.multiple_of(base + done, SUBLANES), sz)


def _wait_run_rows(meta_ref, make_copy):
    total = meta_ref[1, 0]
    for e in range(1, N_EXPERTS):
        total = total + meta_ref[1, e]
    for sz in WAIT_PIECES:
        @pl.when((total & sz) != 0)
        def _():
            make_copy(sz).wait()


def _dispatch_kernel(pstart_ref, cnt_ref, n_used_ref, meta_ref, meta_prev_ref, lpos_ref, x1_ref, xs_hbm,
                     xl_ref, zero_ref, sems, zsem):
    tm = TOKEN_TILE
    bm = MOE_BLOCK
    i = pl.program_id(0)
    slot = i % 2

    @pl.when(i == 0)
    def _():
        zero_ref[...] = jnp.zeros(zero_ref.shape, u32)
        tail = lambda j: pltpu.make_async_copy(zero_ref, xs_hbm.at[pl.ds(j * bm, bm), :], zsem)

        def tail_start(j, carry):
            tail(j).start()
            return carry

        def tail_wait(j, carry):
            tail(j).wait()
            return carry

        n_blocks = xs_hbm.shape[0] // bm
        lax.fori_loop(n_used_ref[0], n_blocks, tail_start, 0)
        lax.fori_loop(n_used_ref[0], n_blocks, tail_wait, 0)
        for e in range(N_EXPERTS):
            lo = pstart_ref[e] + cnt_ref[e]
            n_groups = ((cnt_ref[e] + bm - 1) // bm * bm - cnt_ref[e]) // SUBLANES
            fill = lambda g: pltpu.make_async_copy(
                zero_ref.at[pl.ds(0, SUBLANES), :],
                xs_hbm.at[pl.ds(pl.multiple_of(lo + g * SUBLANES, SUBLANES), SUBLANES), :], zsem)

            def start(g, carry):
                fill(g).start()
                return carry

            def wait(g, carry):
                fill(g).wait()
                return carry

            lax.fori_loop(0, n_groups, start, 0)
            lax.fori_loop(0, n_groups, wait, 0)

    lpos_t = lpos_ref[...].T
    srow = lax.broadcasted_iota(i32, (LOCAL_ROWS, tm), 0).astype(f32)
    perm = jnp.zeros((LOCAL_ROWS, tm), f32)
    for k in range(TOP_K):
        perm = jnp.where(srow == lpos_t[k:k + 1, :], 1.0, perm)
    xl_ref[slot] = _pack_halves(jnp.dot(perm.astype(bf16), x1_ref[...].astype(bf16), preferred_element_type=f32))

    def send(local_row, sorted_row, n_rows):
        pltpu.make_async_copy(xl_ref.at[slot, pl.ds(local_row, n_rows), :],
                              xs_hbm.at[pl.ds(sorted_row, n_rows), :], sems.at[slot]).start()

    _for_each_run_piece(meta_ref, pstart_ref, send)

    def sent(s):
        return lambda n_rows: pltpu.make_async_copy(xl_ref.at[s, pl.ds(0, n_rows), :],
                                                    xs_hbm.at[pl.ds(0, n_rows), :], sems.at[s])

    @pl.when(i > 0)
    def _():
        _wait_run_rows(meta_prev_ref, sent(1 - slot))

    @pl.when(i == pl.num_programs(0) - 1)
    def _():
        _wait_run_rows(meta_ref, sent(slot))


def _dispatch(pstart, counts, n_used, meta, lpos, x1, n_rows):
    n = x1.shape[0]
    tm = TOKEN_TILE
    smem_tile = lambda f: pl.BlockSpec((SUBLANES, LANES), f, memory_space=pltpu.SMEM)
    grid_spec = pltpu.PrefetchScalarGridSpec(
        num_scalar_prefetch=3,
        grid=(n // tm,),
        in_specs=[smem_tile(lambda i, ps, ct, nu: (i, 0)),
                  smem_tile(lambda i, ps, ct, nu: (jnp.maximum(i - 1, 0), 0)),
                  pl.BlockSpec((tm, LANES), lambda i, ps, ct, nu: (i, 0)),
                  pl.BlockSpec((tm, D_MODEL), lambda i, ps, ct, nu: (i, 0))],
        out_specs=pl.BlockSpec(memory_space=pl.ANY),
        scratch_shapes=[pltpu.VMEM((2, LOCAL_ROWS, D_MODEL // 2), u32),
                        pltpu.VMEM((MOE_BLOCK, D_MODEL // 2), u32),
                        pltpu.SemaphoreType.DMA((2,)), pltpu.SemaphoreType.DMA(())],
    )
    return pl.pallas_call(
        _dispatch_kernel,
        grid_spec=grid_spec,
        out_shape=jax.ShapeDtypeStruct((n_rows, D_MODEL // 2), u32),
        compiler_params=_params("arbitrary"),
        name="moe_dispatch",
    )(pstart, counts, n_used, meta, meta, lpos, x1)


def _expert_kernel(blk_e_ref, n_used_ref, next_e_ref, xs_ref, wgu_hbm, bgu_ref, wdn_hbm, bdn_ref,
                   out_ref, wgu_f32, wdn_f32, wgu_bf, wdn_bf, sems, *, layer):
    i = pl.program_id(0)
    prev = jnp.maximum(i - 1, 0)
    expert = blk_e_ref[i]
    new_expert = jnp.logical_or(i == 0, expert != blk_e_ref[prev])
    used = i < n_used_ref[0]

    def fetch(e):
        return (pltpu.make_async_copy(wgu_hbm.at[layer, e], wgu_f32, sems.at[0]),
                pltpu.make_async_copy(wdn_hbm.at[layer, e], wdn_f32, sems.at[1]))

    @pl.when(used)
    def _():
        @pl.when(new_expert)
        def _():
            @pl.when(i == 0)
            def _():
                for c in fetch(expert):
                    c.start()

            for c in fetch(expert):
                c.wait()
            wgu_bf[...] = wgu_f32[...].astype(bf16)
            wdn_bf[...] = wdn_f32[...].astype(bf16)
            nxt = next_e_ref[expert]

            @pl.when(nxt >= 0)
            def _():
                for c in fetch(nxt):
                    c.start()

        x_lo, x_hi = _unpack_halves(xs_ref[...])
        half = D_MODEL // 2
        gu = (jnp.dot(x_lo.astype(bf16), wgu_bf[pl.ds(0, half), :], preferred_element_type=f32)
              + jnp.dot(x_hi.astype(bf16), wgu_bf[pl.ds(half, half), :], preferred_element_type=f32) + bgu_ref[0, 0])
        g = jnp.minimum(gu[:, :D_EXPERT], SWIGLU_LIMIT)
        u = jnp.clip(gu[:, D_EXPERT:], -SWIGLU_LIMIT, SWIGLU_LIMIT)
        hdn = (u + 1.0) * g * _sigmoid(SWIGLU_ALPHA * g)
        y = jnp.dot(hdn.astype(bf16), wdn_bf[...], preferred_element_type=f32) + bdn_ref[0, 0]
        out_ref[...] = _pack_halves(y.astype(bf16).astype(f32))

    @pl.when(jnp.logical_not(used))
    def _():
        out_ref[...] = jnp.zeros(out_ref.shape, u32)


def _experts(blk_e, n_used, next_e, xs, w_gu, b_gu, w_down, b_down, layer):
    n_blocks = blk_e.shape[0]
    bm = MOE_BLOCK
    by_e = lambda i, be, nu, ne: (layer, be[i], 0, 0)
    x_blk = lambda i, be, nu, ne: (jnp.minimum(i, nu[0] - 1), 0)
    grid_spec = pltpu.PrefetchScalarGridSpec(
        num_scalar_prefetch=3,
        grid=(n_blocks,),
        in_specs=[pl.BlockSpec((bm, D_MODEL // 2), x_blk),
                  pl.BlockSpec(memory_space=pl.ANY),
                  pl.BlockSpec((1, 1, 1, 2 * D_EXPERT), by_e),
                  pl.BlockSpec(memory_space=pl.ANY),
                  pl.BlockSpec((1, 1, 1, D_MODEL), by_e)],
        out_specs=pl.BlockSpec((bm, D_MODEL // 2), lambda i, be, nu, ne: (i, 0)),
        scratch_shapes=[pltpu.VMEM((D_MODEL, 2 * D_EXPERT), f32),
                        pltpu.VMEM((D_EXPERT, D_MODEL), f32),
                        pltpu.VMEM((D_MODEL, 2 * D_EXPERT), bf16),
                        pltpu.VMEM((D_EXPERT, D_MODEL), bf16),
                        pltpu.SemaphoreType.DMA((2,))],
    )
    depth = w_gu.shape[0]
    return pl.pallas_call(
        functools.partial(_expert_kernel, layer=layer),
        grid_spec=grid_spec,
        out_shape=jax.ShapeDtypeStruct((n_blocks * bm, D_MODEL // 2), u32),
        compiler_params=_params("arbitrary"),
        name="moe_experts",
    )(blk_e, n_used, next_e, xs, w_gu, b_gu.reshape(depth, N_EXPERTS, 1, 2 * D_EXPERT),
      w_down, b_down.reshape(depth, N_EXPERTS, 1, D_MODEL))


def _combine_kernel(pstart_ref, meta_ref, meta_next_ref, lpos_ref, gates_ref, x1_ref, ln_ref, yb_hbm,
                    outp_ref, outs_ref, yl_ref, sems, *, alpha, n_prompt_tiles):
    tm = TOKEN_TILE
    i = pl.program_id(0)
    slot = i % 2

    def fetch(meta, s):
        def recv(local_row, sorted_row, n_rows):
            pltpu.make_async_copy(yb_hbm.at[pl.ds(sorted_row, n_rows), :],
                                  yl_ref.at[s, pl.ds(local_row, n_rows), :], sems.at[s]).start()

        _for_each_run_piece(meta, pstart_ref, recv)

    @pl.when(i == 0)
    def _():
        yl_ref[...] = jnp.zeros(yl_ref.shape, u32)
        fetch(meta_ref, slot)

    @pl.when(i + 1 < pl.num_programs(0))
    def _():
        fetch(meta_next_ref, 1 - slot)

    _wait_run_rows(meta_ref, lambda n_rows: pltpu.make_async_copy(
        yb_hbm.at[pl.ds(0, n_rows), :], yl_ref.at[slot, pl.ds(0, n_rows), :], sems.at[slot]))

    lpos = lpos_ref[...]
    gates = gates_ref[...]
    scol = lax.broadcasted_iota(i32, (tm, LOCAL_ROWS), 1).astype(f32)
    weight = jnp.zeros((tm, LOCAL_ROWS), f32)
    for k in range(TOP_K):
        weight = jnp.where(scol == lpos[:, k:k + 1], gates[:, k:k + 1], weight)
    y_lo, y_hi = _unpack_halves(yl_ref[slot])
    weight = weight.astype(bf16)
    y = jnp.concatenate([jnp.dot(weight, y_lo.astype(bf16), preferred_element_type=f32),
                         jnp.dot(weight, y_hi.astype(bf16), preferred_element_type=f32)], axis=1)
    x2 = _layer_norm_rows(alpha * x1_ref[...] + y, ln_ref[pl.ds(0, 1), :], ln_ref[pl.ds(1, 1), :])
    _store_group(outp_ref, outs_ref, x2, n_prompt_tiles)


def _combine(pstart, meta, lpos, gates, x1, ln, yb, alpha, n_prompt_tiles):
    n = x1.shape[0]
    tm = TOKEN_TILE
    n_tiles = n // tm
    row = lambda i, ps: (i, 0)
    smem_tile = lambda f: pl.BlockSpec((SUBLANES, LANES), f, memory_space=pltpu.SMEM)
    grid_spec = pltpu.PrefetchScalarGridSpec(
        num_scalar_prefetch=1,
        grid=(n_tiles,),
        in_specs=[smem_tile(row),
                  smem_tile(lambda i, ps: (jnp.minimum(i + 1, n_tiles - 1), 0)),
                  pl.BlockSpec((tm, LANES), row),
                  pl.BlockSpec((tm, LANES), row),
                  pl.BlockSpec((tm, D_MODEL), row),
                  pl.BlockSpec((SUBLANES, D_MODEL), lambda i, ps: (0, 0)),
                  pl.BlockSpec(memory_space=pl.ANY)],
        out_specs=_group_specs(D_MODEL, n_prompt_tiles),
        scratch_shapes=[pltpu.VMEM((2, LOCAL_ROWS, D_MODEL // 2), u32), pltpu.SemaphoreType.DMA((2,))],
    )
    return pl.pallas_call(
        functools.partial(_combine_kernel, alpha=alpha, n_prompt_tiles=n_prompt_tiles),
        grid_spec=grid_spec,
        out_shape=[jax.ShapeDtypeStruct((n_prompt_tiles * tm, D_MODEL), f32),
                   jax.ShapeDtypeStruct((n - n_prompt_tiles * tm, D_MODEL), f32)],
        compiler_params=_params("arbitrary"),
        name="moe_combine",
    )(pstart, meta, meta, lpos, gates, x1, ln, yb)


def _block_tables(counts, n_tokens):
    bm = MOE_BLOCK
    padded = (counts + bm - 1) // bm * bm
    pad_end = jnp.cumsum(padded)
    pstart = (pad_end - padded).astype(i32)
    max_used = n_tokens * TOP_K + (n_tokens // TOKEN_TILE) * N_EXPERTS * (SUBLANES - 1)
    n_blocks = -(-(max_used + N_EXPERTS * (bm - 1)) // bm)
    first_row = jnp.arange(n_blocks, dtype=pad_end.dtype) * bm
    blk_e = jnp.minimum(jnp.sum(pad_end[None, :] <= first_row[:, None], axis=1), N_EXPERTS - 1).astype(i32)
    n_used = (pad_end[-1] // bm).astype(i32).reshape(1)
    idx = jnp.arange(N_EXPERTS)
    later = jnp.logical_and(idx[None, :] > idx[:, None], counts[None, :] > 0)
    next_e = jnp.min(jnp.where(later, idx[None, :], N_EXPERTS), axis=1)
    next_e = jnp.where(next_e == N_EXPERTS, -1, next_e).astype(i32)
    return pstart, blk_e, n_used, next_e, n_blocks * bm


def _pad_time(t, n_batch, n_t, t_pad):
    w = t.shape[-1]
    return jnp.pad(t.reshape(n_batch, n_t, w), ((0, 0), (0, t_pad - n_t), (0, 0))).reshape(n_batch * t_pad, w)


def _block_diag_weight(w):
    h = w.shape[0]
    eye = jnp.eye(h, dtype=w.dtype)
    return (eye[:, None, :, None] * w[:, :, None, :]).reshape(h * HEAD_DIM, h * HEAD_DIM)


def _rows8(*rows):
    width = rows[0].shape[-1]
    m = jnp.stack([r.reshape(width) for r in rows])
    return jnp.pad(m, ((0, SUBLANES - m.shape[0]), (0, 0)))


def _layer(h, p, moe, layer, alpha, st_s, bp, tp, bs, ts):
    n_p, n_s = bp * tp, bs * ts
    tpad = SAMPLE_T_PAD
    (ret_p, ret_s), (rw_p, rw_s), (lru_p, lru_s) = _in_proj(*h, p['w_in'].astype(bf16))
    unpad = lambda y: y.reshape(bs, tpad, -1)[:, :ts].reshape(n_s, -1)

    c_p = RET_CHUNK if tp % RET_CHUNK == 0 else tp
    cos_p, sin_p = _rope_tables(jnp.arange(tp, dtype=f32))
    cos_s, sin_s = _rope_tables(PAST_LEN + jnp.arange(tpad, dtype=f32))
    y_ret_p, sret_p = _retention(ret_p.reshape(bp, tp, 4 * RET_W), cos_p, sin_p, _retention_tables(c_p, c_p),
                                 jnp.zeros((1, bp, RET_H, HEAD_DIM, HEAD_DIM), f32), 0,
                                 nb=max(d for d in (1, 2, 4, 8) if bp % d == 0), length=c_p)
    y_ret_s, sret_s = _retention(_pad_time(ret_s, bs, ts, tpad).reshape(bs, tpad, 4 * RET_W), cos_s, sin_s,
                                 _retention_tables(tpad, ts), st_s['ret'], layer, nb=SUBLANES, length=tpad)
    y_ret = (y_ret_p.reshape(n_p, RET_W), unpad(y_ret_s))

    lora = jnp.zeros((3, LANES, RWKV_W), f32)
    lora = lora.at[0, 0:32].set(p['w_up']).at[1, 32:64].set(p['a_up']).at[2, 64:128].set(p['g_up'])
    vec = _rows8(p['w0'], p['a0'], p['k_k'], p['k_a'], p['r_k'], p['gn_g'], p['gn_b'])
    mix = p['mix'].reshape(1, RWKV_PROJ)
    rows = RWKV_ROWS
    y_rwkv_p, srw_p, shift_p = _rwkv(rw_p, jnp.zeros((bp, 1, RWKV_PROJ), f32), jnp.ones((rows, RWKV_W), f32), mix,
                                     vec, lora, _chunk_tri(rows, RWKV_CHUNK),
                                     jnp.zeros((1, bp, RWKV_H, HEAD_DIM, HEAD_DIM), f32), 0,
                                     n_groups=bp, n_steps=tp // rows, rows=rows, length=RWKV_CHUNK, chain=True)
    valid_s = (jnp.arange(tpad) < ts).astype(f32)[:, None]
    rw_s3 = rw_s.reshape(bs, ts, RWKV_PROJ)
    prev_s = jnp.concatenate([st_s['shift'][:, None, :], rw_s3[:, :-1]], axis=1).reshape(n_s, RWKV_PROJ)
    seq_per_blk = rows // tpad
    y_rwkv_s, srw_s = _rwkv(_pad_time(rw_s, bs, ts, tpad), _pad_time(prev_s, bs, ts, tpad),
                            jnp.tile(jnp.broadcast_to(valid_s, (tpad, RWKV_W)), (seq_per_blk, 1)),
                            mix, vec, lora, _chunk_tri(rows, tpad), st_s['rwkv'], layer,
                            n_groups=bs // seq_per_blk, n_steps=1, rows=rows, length=tpad, chain=False)
    y_rwkv = (y_rwkv_p, unpad(y_rwkv_s))

    lvec = _rows8(p['conv_w'][0], p['conv_w'][1], p['conv_w'][2], p['conv_w'][3],
                  p['conv_b'], p['ba'], p['bx'], p['lam'])
    wa_bd = _block_diag_weight(p['wa']).astype(bf16)
    wx_bd = _block_diag_weight(p['wx']).astype(bf16)
    l_l = LRU_CHUNK if tp % LRU_CHUNK == 0 else tp
    y_lru_p, h_p, tail_p = _lru(lru_p, jnp.ones((l_l, LRU_W), f32), lvec, wa_bd, wx_bd,
                                jnp.zeros((bp, SUBLANES, LRU_W), f32), jnp.zeros((bp, SUBLANES, LRU_W), f32),
                                n_batch=bp, n_chunks=tp // l_l, nb=1, length=l_l, out_rows=n_p)
    conv0_s = jnp.pad(st_s['conv'], ((0, 0), (SUBLANES - (CONV_WIDTH - 1), 0), (0, 0)))
    h0_s = jnp.broadcast_to(st_s['lru'][:, None, :], (bs, SUBLANES, LRU_W))
    y_lru_s, h_s, _ = _lru(_pad_time(lru_s, bs, ts, tpad), jnp.broadcast_to(valid_s, (tpad, LRU_W)),
                           lvec, wa_bd, wx_bd, conv0_s, h0_s,
                           n_batch=bs, n_chunks=1, nb=SUBLANES, length=tpad, out_rows=bs * tpad)
    y_lru = (y_lru_p, unpad(y_lru_s))

    w_router = jnp.pad(p['w_router'], ((0, 0), (0, LANES - N_EXPERTS)))
    b_router = jnp.pad(p['b_router'], (0, LANES - N_EXPERTS), constant_values=-1e30).reshape(1, LANES)
    x1, lpos, gates, meta, cnt = _out_proj_router(y_ret, y_rwkv, y_lru, h, p['w_out'].astype(bf16),
                                                  _rows8(p['ln1_g'], p['ln1_b']), w_router, b_router, alpha)

    counts = cnt[0, :N_EXPERTS].astype(i32)
    pstart, blk_e, n_used, next_e, n_rows = _block_tables(counts, n_p + n_s)
    xs = _dispatch(pstart, counts, n_used, meta, lpos, x1, n_rows)
    yb = _experts(blk_e, n_used, next_e, xs, moe['w_gu'], moe['b_gu'], moe['w_down'], moe['b_down'], layer)
    x2 = _combine(pstart, meta, lpos, gates, x1, _rows8(p['ln2_g'], p['ln2_b']), yb, alpha, n_p // TOKEN_TILE)

    keep = CONV_WIDTH - 1
    assert tp >= SUBLANES and ts >= keep
    conv_s = lru_s.reshape(bs, ts, 2 * LRU_W)[:, ts - keep:, LRU_W:]
    new_p = (sret_p, srw_p, shift_p[:, 0], h_p[:, 0], tail_p[:, SUBLANES - keep:])
    new_s = (sret_s, srw_s, rw_s3[:, -1], h_s[:, 0], conv_s)
    return x2, new_p, new_s


def kernel(x_prompt, x_sample, state_ret, state_rwkv, state_rwkv_shift, state_lru, state_conv,
           w_in, w_out, ln1_g, ln1_b, ln2_g, ln2_b,
           rwkv_mix, rwkv_w0, rwkv_w_up, rwkv_a0, rwkv_a_up, rwkv_g_up, rwkv_k_k, rwkv_k_a, rwkv_r_k,
           rwkv_gn_g, rwkv_gn_b, lru_conv_w, lru_conv_b, lru_wa, lru_ba, lru_wx, lru_bx, lru_lambda,
           moe_w_router, moe_b_router, moe_w_gate_up, moe_b_gate_up, moe_w_down, moe_b_down):
    bp, tp, _ = x_prompt.shape
    bs, ts, _ = x_sample.shape
    depth = w_in.shape[0]
    alpha = (2.0 * depth) ** 0.25
    moe = {'w_gu': moe_w_gate_up, 'b_gu': moe_b_gate_up, 'w_down': moe_w_down, 'b_down': moe_b_down}
    h = (x_prompt.reshape(bp * tp, D_MODEL), x_sample.reshape(bs * ts, D_MODEL))
    new_p, new_s = [], []
    for l in range(depth):
        p = {'w_in': w_in[l], 'w_out': w_out[l], 'ln1_g': ln1_g[l], 'ln1_b': ln1_b[l],
             'ln2_g': ln2_g[l], 'ln2_b': ln2_b[l], 'mix': rwkv_mix[l], 'w0': rwkv_w0[l],
             'w_up': rwkv_w_up[l], 'a0': rwkv_a0[l], 'a_up': rwkv_a_up[l], 'g_up': rwkv_g_up[l],
             'k_k': rwkv_k_k[l], 'k_a': rwkv_k_a[l], 'r_k': rwkv_r_k[l], 'gn_g': rwkv_gn_g[l],
             'gn_b': rwkv_gn_b[l], 'conv_w': lru_conv_w[l], 'conv_b': lru_conv_b[l], 'wa': lru_wa[l],
             'ba': lru_ba[l], 'wx': lru_wx[l], 'bx': lru_bx[l], 'lam': lru_lambda[l],
             'w_router': moe_w_router[l], 'b_router': moe_b_router[l]}
        st_s = {'ret': state_ret, 'rwkv': state_rwkv, 'shift': state_rwkv_shift[l],
                'lru': state_lru[l], 'conv': state_conv[l]}
        h, sp, ss = _layer(h, p, moe, l, alpha, st_s, bp, tp, bs, ts)
        new_p.append(sp)
        new_s.append(ss)
    outs = [h[0].reshape(bp, tp, D_MODEL), h[1].reshape(bs, ts, D_MODEL)]
    for i in range(5):
        outs.append(jnp.stack([s[i] for s in new_p]))
        outs.append(jnp.stack([s[i] for s in new_s]))
    return tuple(outs)
```

```python
import functools

import jax
import jax.numpy as jnp
from jax import lax
from jax.experimental import pallas as pl
from jax.experimental.pallas import tpu as pltpu

f32 = jnp.float32
bf16 = jnp.bfloat16
i32 = jnp.int32
u32 = jnp.uint32

D_MODEL = 1024
HEAD_DIM = 64
RET_W = 256
RET_H = 4
RET_CHUNK = 128
ROPE_BASE = 10000.0
RWKV_W = 512
RWKV_H = 8
RWKV_PROJ = 1664
RWKV_LORA_COL = 1536
RWKV_GN_EPS = 64e-5
RWKV_CHUNK = 64
RWKV_ROWS = 128
LRU_W = 256
LRU_C = 8.0
LRU_CHUNK = 256
CONV_WIDTH = 4
D_PROJ = 3200
N_EXPERTS = 32
TOP_K = 4
D_EXPERT = 1024
SWIGLU_LIMIT = 7.0
SWIGLU_ALPHA = 1.702
LN_EPS = 1e-5
PAST_LEN = 16384.0

LANES = 128
SUBLANES = 8
SAMPLE_T_PAD = 8
TOKEN_TILE = 512
MOE_BLOCK = 512
LOCAL_ROWS = TOP_K * TOKEN_TILE + N_EXPERTS * SUBLANES
RUN_PIECES = tuple(SUBLANES << j for j in reversed(range((TOKEN_TILE // SUBLANES).bit_length())))
WAIT_PIECES = tuple(SUBLANES << j for j in reversed(range((LOCAL_ROWS // SUBLANES).bit_length())))
VMEM_LIMIT = 56 * 1024 * 1024

_NT = (((1,), (1,)), ((), ()))
_TN = (((0,), (0,)), ((), ()))


def _params(*sem):
    return pltpu.CompilerParams(dimension_semantics=sem, vmem_limit_bytes=VMEM_LIMIT)


def _mm(a, b):
    return jnp.dot(a.astype(bf16), b.astype(bf16), preferred_element_type=f32)


def _mm_nt(a, b):
    return lax.dot_general(a.astype(bf16), b.astype(bf16), _NT, preferred_element_type=f32)


def _mm_tn(a, b):
    return lax.dot_general(a.astype(bf16), b.astype(bf16), _TN, preferred_element_type=f32)


def _mm3(a, b):
    a_hi, b_hi = a.astype(bf16), b.astype(bf16)
    a_lo = (a - a_hi.astype(f32)).astype(bf16)
    b_lo = (b - b_hi.astype(f32)).astype(bf16)
    dot = functools.partial(jnp.dot, preferred_element_type=f32)
    return dot(a_hi, b_hi) + dot(a_hi, b_lo) + dot(a_lo, b_hi)


def _pack_halves(x):
    w = x.shape[-1] // 2
    lo = lax.bitcast_convert_type(x[:, :w], u32)
    hi = lax.bitcast_convert_type(x[:, w:], u32)
    return (lo >> 16) | hi


def _unpack_halves(p):
    return (lax.bitcast_convert_type(p << 16, f32),
            lax.bitcast_convert_type(p & jnp.uint32(0xFFFF0000), f32))


def _softplus(x):
    return jnp.maximum(x, 0.0) + jnp.log(1.0 + jnp.exp(-jnp.abs(x)))


def _sigmoid(x):
    return 1.0 / (1.0 + jnp.exp(-x))


def _half_masks():
    lane = lax.broadcasted_iota(i32, (1, LANES), 1)
    m0 = (lane < HEAD_DIM).astype(f32)
    return m0, 1.0 - m0


def _seg_mean(x, m0, m1):
    s0 = jnp.sum(x * m0, axis=-1, keepdims=True)
    s1 = jnp.sum(x * m1, axis=-1, keepdims=True)
    return (m0 * s0 + m1 * s1) * (1.0 / HEAD_DIM)


def _seg_sum(x, m0, m1):
    s0 = jnp.sum(x * m0, axis=-1, keepdims=True)
    s1 = jnp.sum(x * m1, axis=-1, keepdims=True)
    return m0 * s0 + m1 * s1


def _block_diag_mask():
    r = lax.broadcasted_iota(i32, (LANES, LANES), 0) // HEAD_DIM
    c = lax.broadcasted_iota(i32, (LANES, LANES), 1) // HEAD_DIM
    return (r == c).astype(f32)


def _layer_norm_rows(z, g, b):
    mu = jnp.mean(z, axis=-1, keepdims=True)
    zc = z - mu
    var = jnp.mean(zc * zc, axis=-1, keepdims=True)
    return zc * lax.rsqrt(var + LN_EPS) * g + b


def _group_specs(width, n_prompt_tiles):
    return [pl.BlockSpec((TOKEN_TILE, width), lambda i, *_: (jnp.minimum(i, n_prompt_tiles - 1), 0)),
            pl.BlockSpec((TOKEN_TILE, width), lambda i, *_: (jnp.maximum(i - n_prompt_tiles, 0), 0))]


def _pick_group(p_ref, s_ref, n_prompt_tiles):
    return jnp.where(pl.program_id(0) >= n_prompt_tiles, s_ref[...], p_ref[...])


def _n_tiles(h_p, h_s):
    assert h_p.shape[0] % TOKEN_TILE == 0 and h_s.shape[0] % TOKEN_TILE == 0, (h_p.shape, h_s.shape)
    return h_p.shape[0] // TOKEN_TILE, h_s.shape[0] // TOKEN_TILE


def _store_group(p_ref, s_ref, value, n_prompt_tiles):
    @pl.when(pl.program_id(0) < n_prompt_tiles)
    def _():
        p_ref[...] = value

    @pl.when(pl.program_id(0) >= n_prompt_tiles)
    def _():
        s_ref[...] = value


def _in_proj_kernel(xp_ref, xs_ref, w_ref, retp_ref, rets_ref, rwp_ref, rws_ref, lrup_ref, lrus_ref, *,
                    n_prompt_tiles):
    c0, c1 = 4 * RET_W, 4 * RET_W + RWKV_PROJ
    dot = functools.partial(jnp.dot, preferred_element_type=f32)

    def project(x_ref, ret_ref, rw_ref, lru_ref):
        xb = x_ref[...].astype(bf16)
        ret_ref[...] = dot(xb, w_ref[:, :c0])
        rw_ref[...] = dot(xb, w_ref[:, c0:c1])
        lru_ref[...] = dot(xb, w_ref[:, c1:])

    @pl.when(pl.program_id(0) < n_prompt_tiles)
    def _():
        project(xp_ref, retp_ref, rwp_ref, lrup_ref)

    @pl.when(pl.program_id(0) >= n_prompt_tiles)
    def _():
        project(xs_ref, rets_ref, rws_ref, lrus_ref)


def _in_proj(h_p, h_s, w_bf):
    npt, nst = _n_tiles(h_p, h_s)
    widths = (4 * RET_W, RWKV_PROJ, 2 * LRU_W)
    out = pl.pallas_call(
        functools.partial(_in_proj_kernel, n_prompt_tiles=npt),
        grid=(npt + nst,),
        in_specs=_group_specs(D_MODEL, npt) + [
                  pl.BlockSpec((D_MODEL, D_PROJ), lambda i: (0, 0))],
        out_specs=[s for w in widths for s in _group_specs(w, npt)],
        out_shape=[jax.ShapeDtypeStruct((rows, w), f32) for w in widths for rows in (h_p.shape[0], h_s.shape[0])],
        compiler_params=_params("arbitrary"),
        name="in_proj",
    )(h_p, h_s, w_bf)
    return out[0:2], out[2:4], out[4:6]


def _rope_tables(pos):
    half = HEAD_DIM // 2
    inv = ROPE_BASE ** (-jnp.arange(half, dtype=f32) / half)
    ang = pos[:, None] * inv[None, :]
    cos, sin = jnp.cos(ang), jnp.sin(ang)
    cos_f = jnp.tile(jnp.concatenate([cos, cos], axis=-1), (1, RET_H))
    sin_f = jnp.tile(jnp.concatenate([-sin, sin], axis=-1), (1, RET_H))
    return cos_f, sin_f


def _retention_tables(length, n_valid):
    lg = jnp.log1p(-jnp.exp2(-5.0 - jnp.arange(RET_H, dtype=f32)))
    idx = jnp.arange(length, dtype=f32)
    rel = idx[:, None] - idx[None, :]
    mask = jnp.where(rel[None] >= 0, jnp.exp(jnp.maximum(rel, 0.0)[None] * lg[:, None, None]), 0.0)
    q_dec = jnp.exp((idx[:, None] + 1.0) * lg[None, :])
    k_dec = jnp.where(idx[:, None] < n_valid, jnp.exp((n_valid - 1.0 - idx)[:, None] * lg[None, :]), 0.0)
    c_dec = jnp.exp(n_valid * lg)[None, :]
    rep = lambda t: jnp.repeat(t, HEAD_DIM, axis=-1)
    return mask, rep(q_dec), rep(k_dec), rep(c_dec)


def _retention_kernel(ret_ref, cos_ref, sin_ref, mask_ref, qdec_ref, kdec_ref, cdec_ref, s0_ref,
                      y_ref, s_ref, pair_ref, *, nb):
    @pl.when(pl.program_id(1) == 0)
    def _():
        _load_head_pairs(pair_ref, s0_ref)

    m0, m1 = _half_masks()
    lane = lax.broadcasted_iota(i32, (1, LANES), 1)
    first_half = (lane % HEAD_DIM) < (HEAD_DIM // 2)
    bd = _block_diag_mask()

    def rope(x, cs, sn):
        swapped = jnp.where(first_half, pltpu.roll(x, LANES - HEAD_DIM // 2, 1), pltpu.roll(x, HEAD_DIM // 2, 1))
        return x * cs + swapped * sn

    for j in range(nb):
        for p in range(RET_H // 2):
            cols = pl.ds(p * LANES, LANES)
            cs, sn = cos_ref[:, cols], sin_ref[:, cols]
            q2 = rope(ret_ref[j, :, pl.ds(p * LANES, LANES)], cs, sn)
            k2 = rope(ret_ref[j, :, pl.ds(RET_W + p * LANES, LANES)], cs, sn) * (HEAD_DIM ** -0.5)
            v2 = ret_ref[j, :, pl.ds(2 * RET_W + p * LANES, LANES)]
            g2 = ret_ref[j, :, pl.ds(3 * RET_W + p * LANES, LANES)]
            state = pair_ref[j, p]
            o2 = _mm(q2, state) * qdec_ref[:, cols]
            for hh, m in enumerate((m0, m1)):
                sc = _mm_nt(q2 * m, k2) * mask_ref[2 * p + hh]
                o2 = o2 + _mm(sc, v2) * m
            pair_ref[j, p] = state * cdec_ref[:, cols] + _mm_tn(k2 * kdec_ref[:, cols], v2) * bd
            mu = _seg_mean(o2, m0, m1)
            oc = o2 - mu
            var = _seg_mean(oc * oc, m0, m1)
            y_ref[j, :, cols] = g2 * _sigmoid(g2) * oc * lax.rsqrt(var + LN_EPS)

    @pl.when(pl.program_id(1) == pl.num_programs(1) - 1)
    def _():
        _store_head_pairs(s_ref, pair_ref)


def _load_head_pairs(pair_ref, s_ref):
    pair_ref[...] = jnp.zeros(pair_ref.shape, f32)
    for j in range(pair_ref.shape[0]):
        for p in range(pair_ref.shape[1]):
            pair_ref[j, p, pl.ds(0, HEAD_DIM), pl.ds(0, HEAD_DIM)] = s_ref[j, 2 * p]
            pair_ref[j, p, pl.ds(HEAD_DIM, HEAD_DIM), pl.ds(HEAD_DIM, HEAD_DIM)] = s_ref[j, 2 * p + 1]


def _store_head_pairs(s_ref, pair_ref):
    for j in range(pair_ref.shape[0]):
        for p in range(pair_ref.shape[1]):
            s_ref[j, 2 * p] = pair_ref[j, p, pl.ds(0, HEAD_DIM), pl.ds(0, HEAD_DIM)]
            s_ref[j, 2 * p + 1] = pair_ref[j, p, pl.ds(HEAD_DIM, HEAD_DIM), pl.ds(HEAD_DIM, HEAD_DIM)]


def _retention(ret, cos_f, sin_f, tables, s0, layer, *, nb, length):
    mask, q_dec, k_dec, c_dec = tables
    n_batch, t_len, _ = ret.shape
    blk = lambda g, c: (g, c, 0)
    const2 = lambda g, c: (0, 0)
    st = lambda g, c: (g, 0, 0, 0)
    return pl.pallas_call(
        functools.partial(_retention_kernel, nb=nb),
        grid=(n_batch // nb, t_len // length),
        in_specs=[pl.BlockSpec((nb, length, 4 * RET_W), blk),
                  pl.BlockSpec((length, RET_W), lambda g, c: (c, 0)),
                  pl.BlockSpec((length, RET_W), lambda g, c: (c, 0)),
                  pl.BlockSpec((RET_H, length, length), lambda g, c: (0, 0, 0)),
                  pl.BlockSpec((length, RET_W), const2),
                  pl.BlockSpec((length, RET_W), const2),
                  pl.BlockSpec((1, RET_W), const2),
                  pl.BlockSpec((None, nb, RET_H, HEAD_DIM, HEAD_DIM), lambda g, c: (layer, g, 0, 0, 0))],
        out_specs=[pl.BlockSpec((nb, length, RET_W), blk),
                   pl.BlockSpec((nb, RET_H, HEAD_DIM, HEAD_DIM), st)],
        out_shape=[jax.ShapeDtypeStruct((n_batch, t_len, RET_W), f32),
                   jax.ShapeDtypeStruct((n_batch, RET_H, HEAD_DIM, HEAD_DIM), f32)],
        scratch_shapes=[pltpu.VMEM((nb, RET_H // 2, LANES, LANES), f32)],
        compiler_params=_params("arbitrary", "arbitrary"),
        name="retention",
    )(ret, cos_f, sin_f, mask, q_dec, k_dec, c_dec, s0)


def _rwkv_kernel(rw_ref, aux_ref, valid_ref, mix_ref, vec_ref, lora_ref, tri_ref, s0_ref,
                 y_ref, s_ref, *rest, rows, length, chain):
    n_chunks = rows // length
    rw = rw_ref[...]
    if chain:
        last_ref, xs_ref, pair_ref = rest

        @pl.when(pl.program_id(1) == 0)
        def _():
            _load_head_pairs(pair_ref, s0_ref)
            xs_ref[pl.ds(0, SUBLANES), :] = jnp.broadcast_to(aux_ref[0], (SUBLANES, RWKV_PROJ))

        xs_ref[pl.ds(SUBLANES, rows), :] = rw
        prev = xs_ref[pl.ds(SUBLANES - 1, rows), :]
        xs_ref[pl.ds(0, SUBLANES), :] = rw[rows - SUBLANES:, :]
        last_ref[0] = rw[rows - 1:rows, :]
    else:
        pair_ref, = rest
        _load_head_pairs(pair_ref, s0_ref)
        prev = aux_ref[...]
    rwm = rw + (prev - rw) * mix_ref[...]

    valid = valid_ref[...]
    w0, a0, k_k, k_a, r_k, gn_g, gn_b = (vec_ref[pl.ds(i, 1), :] for i in range(7))
    lo = rwm[:, RWKV_LORA_COL:]
    lw = _mm3(jnp.tanh(lo), lora_ref[0])
    la = _mm3(lo, lora_ref[1])
    gate = _mm3(_sigmoid(lo), lora_ref[2])
    logw = -jnp.exp(-_softplus(-(w0 + lw)) - 0.5) * valid
    a = _sigmoid(a0 + la)
    r = rwm[:, :RWKV_W]
    kr = rwm[:, RWKV_W:2 * RWKV_W]
    vr = rwm[:, 2 * RWKV_W:3 * RWKV_W]
    kk_raw = kr * k_k
    kp = kr * (1.0 + (a - 1.0) * k_a) * valid
    cum = _mm3(tri_ref[...], logw)
    g_incl = jnp.exp(cum)
    g_inv = jnp.exp(-cum)
    g_prev = jnp.exp(cum - logw)
    g_end = jnp.concatenate(
        [jnp.broadcast_to(g_incl[(c + 1) * length - 1:(c + 1) * length, :], (length, RWKV_W)) for c in range(n_chunks)],
        axis=0)

    m0, m1 = _half_masks()
    bd = _block_diag_mask()
    stacked = 2 * rows
    ri = lax.broadcasted_iota(i32, (stacked, stacked), 0)
    ci = lax.broadcasted_iota(i32, (stacked, stacked), 1)
    same = (ri // length) == (ci // length)
    strict = jnp.logical_and(same, ci < ri)
    incl = jnp.logical_and(same, ci <= ri)
    eye = (ci == ri).astype(f32)
    stack = lambda t: jnp.concatenate([t * m0, t * m1], axis=0)
    unstack = lambda t: t[:rows] + t[rows:]
    pairs = range(RWKV_H // 2)

    pre = []
    for p in pairs:
        sl = slice(p * LANES, (p + 1) * LANES)
        kk2 = kk_raw[:, sl]
        kk2 = kk2 * lax.rsqrt(jnp.maximum(_seg_sum(kk2 * kk2, m0, m1), 1e-24)) * valid[:, sl]
        d = dict(sl=sl, r2=r[:, sl], v2=vr[:, sl], kp2=kp[:, sl], ge=g_end[:, sl])
        d['kh'] = kk2 * g_prev[:, sl]
        rh = d['r2'] * g_incl[:, sl]
        bt = kk2 * a[:, sl] * g_inv[:, sl]
        kt = d['kp2'] * g_inv[:, sl]
        d['b_end'] = bt * d['ge']
        d['k_end'] = kt * d['ge']
        d['kh_s'], d['rh_s'], d['v_s'] = stack(d['kh']), stack(rh), stack(d['v2'])
        gram = _mm_nt(jnp.concatenate([d['kh_s'], d['rh_s']], axis=0), jnp.concatenate([stack(bt), stack(kt)], axis=0))
        d['x'] = -jnp.where(strict, gram[:stacked, :stacked], 0.0)
        d['a_k'] = jnp.where(strict, gram[:stacked, stacked:], 0.0)
        d['r_b'] = jnp.where(incl, gram[stacked:, :stacked], 0.0)
        d['r_k'] = jnp.where(incl, gram[stacked:, stacked:], 0.0)
        d['t'] = eye + d['x']
        pre.append(d)
    cover = 2
    while cover < length:
        for d in pre:
            d['x'] = _mm(d['x'], d['x'])
            d['t'] = d['t'] + _mm(d['t'], d['x'])
        cover *= 2
    for d in pre:
        rhs0 = -_mm(d['a_k'], d['v_s'])
        if chain:
            tz = _mm(d['t'], jnp.concatenate([rhs0, d['kh_s']], axis=1))
            rbz = _mm(d['r_b'], tz)
            d['z0'] = unstack(tz[:, :LANES])
            d['q'] = unstack(d['rh_s'] - rbz[:, LANES:])
            d['y0'] = unstack(rbz[:, :LANES] + _mm(d['r_k'], d['v_s']))
            d['w'] = unstack(_mm_tn(d['t'], stack(d['b_end'])))
        else:
            d['z0'] = unstack(_mm(d['t'], rhs0))

    for p, d in zip(pairs, pre):
        sl, ge, kh, v2 = d['sl'], d['ge'], d['kh'], d['v2']
        chunk = lambda c: slice(c * length, (c + 1) * length)
        g_last = lambda c: ge[(c + 1) * length - 1:(c + 1) * length, :]
        if chain:
            state = pair_ref[0, p]
            ys = []
            for c in range(n_chunks):
                cr = chunk(c)
                ys.append(_mm_nt(d['q'][cr], state) + d['y0'][cr])
                n_c = (_mm_tn(d['z0'][cr], d['b_end'][cr]) + _mm_tn(v2[cr], d['k_end'][cr])) * bd
                kw = _mm_tn(kh[cr], d['w'][cr]) * bd
                state = state * g_last(c) - _mm(state, kw) + n_c
            pair_ref[0, p] = state
            y2 = jnp.concatenate(ys, axis=0) if n_chunks > 1 else ys[0]
        else:
            rh = unstack(d['rh_s'])
            ks, rs = [], []
            for c in range(n_chunks):
                cr = chunk(c)
                both = _mm_nt(jnp.concatenate([kh[cr], rh[cr]], axis=0), pair_ref[c, p])
                ks.append(both[:length])
                rs.append(both[length:])
            z = d['z0'] - unstack(_mm(d['t'], stack(jnp.concatenate(ks, axis=0))))
            y2 = jnp.concatenate(rs, axis=0) + unstack(_mm(d['r_b'], stack(z)) + _mm(d['r_k'], d['v_s']))
            for c in range(n_chunks):
                cr = chunk(c)
                update = _mm_tn(jnp.concatenate([z[cr], v2[cr]], axis=0),
                                jnp.concatenate([d['b_end'][cr], d['k_end'][cr]], axis=0)) * bd
                pair_ref[c, p] = pair_ref[c, p] * g_last(c) + update

        mu = _seg_mean(y2, m0, m1)
        yc = y2 - mu
        var = _seg_mean(yc * yc, m0, m1)
        yn = yc * lax.rsqrt(var + RWKV_GN_EPS) * gn_g[:, sl] + gn_b[:, sl]
        bonus = _seg_sum(d['r2'] * d['kp2'] * r_k[:, sl], m0, m1) * v2
        y_ref[:, pl.ds(p * LANES, LANES)] = (yn + bonus) * gate[:, sl]

    if chain:
        @pl.when(pl.program_id(1) == pl.num_programs(1) - 1)
        def _():
            _store_head_pairs(s_ref, pair_ref)
    else:
        _store_head_pairs(s_ref, pair_ref)


def _rwkv(rw, aux, valid, mix, vec, lora, tri, s0, layer, *, n_groups, n_steps, rows, length, chain):
    blk = lambda g, c: (g * n_steps + c, 0)
    const2 = lambda g, c: (0, 0)
    st = lambda g, c: (g, 0, 0, 0)
    n_state = 1 if chain else rows // length
    aux_spec = (pl.BlockSpec((1, 1, RWKV_PROJ), lambda g, c: (g, 0, 0)) if chain
                else pl.BlockSpec((rows, RWKV_PROJ), blk))
    return pl.pallas_call(
        functools.partial(_rwkv_kernel, rows=rows, length=length, chain=chain),
        grid=(n_groups, n_steps),
        in_specs=[pl.BlockSpec((rows, RWKV_PROJ), blk),
                  aux_spec,
                  pl.BlockSpec((rows, RWKV_W), const2),
                  pl.BlockSpec((1, RWKV_PROJ), const2),
                  pl.BlockSpec((SUBLANES, RWKV_W), const2),
                  pl.BlockSpec((3, LANES, RWKV_W), lambda g, c: (0, 0, 0)),
                  pl.BlockSpec((rows, rows), const2),
                  pl.BlockSpec((None, n_state, RWKV_H, HEAD_DIM, HEAD_DIM), lambda g, c: (layer, g, 0, 0, 0))],
        out_specs=[pl.BlockSpec((rows, RWKV_W), blk),
                   pl.BlockSpec((n_state, RWKV_H, HEAD_DIM, HEAD_DIM), st)]
        + ([pl.BlockSpec((1, 1, RWKV_PROJ), lambda g, c: (g, 0, 0))] if chain else []),
        out_shape=[jax.ShapeDtypeStruct((n_groups * n_steps * rows, RWKV_W), f32),
                   jax.ShapeDtypeStruct((n_groups * n_state, RWKV_H, HEAD_DIM, HEAD_DIM), f32)]
        + ([jax.ShapeDtypeStruct((n_groups, 1, RWKV_PROJ), f32)] if chain else []),
        scratch_shapes=([pltpu.VMEM((rows + SUBLANES, RWKV_PROJ), f32)] if chain else [])
        + [pltpu.VMEM((n_state, RWKV_H // 2, LANES, LANES), f32)],
        compiler_params=_params("arbitrary", "arbitrary"),
        name="rwkv7",
    )(rw, aux, valid, mix, vec, lora, tri, s0)


def _chunk_tri(rows, length):
    idx = jnp.arange(rows)
    same = (idx[:, None] // length) == (idx[None, :] // length)
    return jnp.logical_and(same, idx[None, :] <= idx[:, None]).astype(f32)


def _lru_kernel(lru_ref, valid_ref, vec_ref, wa_ref, wx_ref, conv0_ref, h0_ref,
                y_ref, h_ref, tail_ref, xext_ref, *, nb, length):
    @pl.when(pl.program_id(1) == 0)
    def _():
        h_ref[...] = h0_ref[...]
        xext_ref[:, pl.ds(0, SUBLANES), :] = conv0_ref[...]

    valid = valid_ref[...] > 0.5
    row = lax.broadcasted_iota(i32, (length, LRU_W), 0)
    cw = [vec_ref[pl.ds(i, 1), :] for i in range(CONV_WIDTH)]
    cb, ba, bx, lam = (vec_ref[pl.ds(i, 1), :] for i in range(CONV_WIDTH, CONV_WIDTH + 4))
    sp = _softplus(-lam)
    for j in range(nb):
        rows = pl.ds(j * length, length)
        gbr = lru_ref[rows, pl.ds(0, LRU_W)]
        x = lru_ref[rows, pl.ds(LRU_W, LRU_W)]
        xext_ref[j, pl.ds(SUBLANES, length), :] = x
        xc = cb + x * cw[CONV_WIDTH - 1]
        for t in range(CONV_WIDTH - 1):
            xc = xc + xext_ref[j, pl.ds(SUBLANES - (CONV_WIDTH - 1) + t, length), :] * cw[t]
        xext_ref[j, pl.ds(0, SUBLANES), :] = x[length - SUBLANES:, :]
        tail_ref[j] = x[length - SUBLANES:, :]
        gate_a = _sigmoid(_mm(xc, wa_ref[...]) + ba)
        gate_x = _sigmoid(_mm(xc, wx_ref[...]) + bx)
        log_a = -LRU_C * gate_a * sp
        a = jnp.exp(log_a)
        b = xc * gate_x * jnp.sqrt(1.0 - jnp.exp(2.0 * log_a))
        a = jnp.where(valid, a, 1.0)
        b = jnp.where(valid, b, 0.0)
        shift = 1
        while shift < length:
            inside = row >= shift
            b = a * jnp.where(inside, pltpu.roll(b, shift, 0), 0.0) + b
            a = a * jnp.where(inside, pltpu.roll(a, shift, 0), 1.0)
            shift *= 2
        hs = a * h_ref[j, pl.ds(0, 1), :] + b
        h_ref[j] = jnp.broadcast_to(hs[length - 1:length, :], (SUBLANES, LRU_W))
        c = 0.7978845608028654
        gelu = 0.5 * gbr * (1.0 + jnp.tanh(c * (gbr + 0.044715 * gbr * gbr * gbr)))
        y_ref[rows, :] = hs * gelu


def _lru(lru, valid, vec, wa_bd, wx_bd, conv0, h0, *, n_batch, n_chunks, nb, length, out_rows):
    blk = lambda g, c: (g * n_chunks + c, 0)
    const2 = lambda g, c: (0, 0)
    st = lambda g, c: (g, 0, 0)
    return pl.pallas_call(
        functools.partial(_lru_kernel, nb=nb, length=length),
        grid=(n_batch // nb, n_chunks),
        in_specs=[pl.BlockSpec((nb * length, 2 * LRU_W), blk),
                  pl.BlockSpec((length, LRU_W), const2),
                  pl.BlockSpec((SUBLANES, LRU_W), const2),
                  pl.BlockSpec((LRU_W, LRU_W), const2),
                  pl.BlockSpec((LRU_W, LRU_W), const2),
                  pl.BlockSpec((nb, SUBLANES, LRU_W), st),
                  pl.BlockSpec((nb, SUBLANES, LRU_W), st)],
        out_specs=[pl.BlockSpec((nb * length, LRU_W), blk),
                   pl.BlockSpec((nb, SUBLANES, LRU_W), st),
                   pl.BlockSpec((nb, SUBLANES, LRU_W), st)],
        out_shape=[jax.ShapeDtypeStruct((out_rows, LRU_W), f32),
                   jax.ShapeDtypeStruct((n_batch, SUBLANES, LRU_W), f32),
                   jax.ShapeDtypeStruct((n_batch, SUBLANES, LRU_W), f32)],
        scratch_shapes=[pltpu.VMEM((nb, length + SUBLANES, LRU_W), f32)],
        compiler_params=_params("arbitrary", "arbitrary"),
        name="rg_lru",
    )(lru, valid, vec, wa_bd, wx_bd, conv0, h0)


def _out_proj_kernel(yrp_ref, yrs_ref, ywp_ref, yws_ref, ylp_ref, yls_ref, xp_ref, xs_ref,
                     w_ref, ln_ref, wr_ref, br_ref,
                     x1_ref, lpos_ref, g_ref, meta_ref, cnt_ref, *, alpha, n_prompt_tiles):
    @pl.when(pl.program_id(0) == 0)
    def _():
        cnt_ref[...] = jnp.zeros(cnt_ref.shape, f32)

    pick = functools.partial(_pick_group, n_prompt_tiles=n_prompt_tiles)
    mixed = (_mm(pick(yrp_ref, yrs_ref), w_ref[pl.ds(0, RET_W), :])
             + _mm(pick(ywp_ref, yws_ref), w_ref[pl.ds(RET_W, RWKV_W), :])
             + _mm(pick(ylp_ref, yls_ref), w_ref[pl.ds(RET_W + RWKV_W, LRU_W), :]))
    x1 = _layer_norm_rows(alpha * pick(xp_ref, xs_ref) + mixed, ln_ref[pl.ds(0, 1), :], ln_ref[pl.ds(1, 1), :])
    x1_ref[...] = x1
    logits = _mm3(x1, wr_ref[...]) + br_ref[...]
    tm = logits.shape[0]
    lane = lax.broadcasted_iota(i32, logits.shape, 1).astype(f32)
    top_v = jnp.zeros(logits.shape, f32)
    work = logits
    v_max = None
    onehots = []
    for k in range(TOP_K):
        v = jnp.max(work, axis=-1, keepdims=True)
        idx = jnp.min(jnp.where(work == v, lane, float(LANES)), axis=-1, keepdims=True)
        if k == 0:
            v_max = v
        hit = lane == idx
        onehots.append(hit.astype(f32))
        top_v = jnp.where(lane == k, jnp.exp(v - v_max), top_v)
        work = jnp.where(hit, -jnp.inf, work)
    g_ref[...] = top_v / jnp.sum(top_v, axis=-1, keepdims=True)

    total = onehots[0] + onehots[1] + onehots[2] + onehots[3]
    ri = lax.broadcasted_iota(i32, (tm, tm), 0)
    ci = lax.broadcasted_iota(i32, (tm, tm), 1)
    before = _mm((ci < ri).astype(f32), total)
    groups = jnp.floor((jnp.sum(total, axis=0, keepdims=True) + (SUBLANES - 1.0)) * (1.0 / SUBLANES))
    er = lax.broadcasted_iota(i32, (LANES, LANES), 0)
    ec = lax.broadcasted_iota(i32, (LANES, LANES), 1)
    run_len = groups * SUBLANES
    run_start = _mm(jnp.broadcast_to(groups, (SUBLANES, LANES)), (er < ec).astype(f32))[0:1] * SUBLANES
    lpos = jnp.full(logits.shape, -1.0, f32)
    for k in range(TOP_K):
        pos = jnp.sum(onehots[k] * (before + run_start), axis=-1, keepdims=True)
        lpos = jnp.where(lane == k, pos, lpos)
    lpos_ref[...] = lpos
    row = lax.broadcasted_iota(i32, (SUBLANES, LANES), 0)
    meta = jnp.where(row == 0, run_start, jnp.where(row == 1, run_len, jnp.where(row == 2, cnt_ref[...], 0.0)))
    meta_ref[...] = meta.astype(i32)
    cnt_ref[...] = cnt_ref[...] + run_len


def _out_proj_router(y_ret, y_rwkv, y_lru, h, w_out_bf, ln, w_router, b_router, alpha):
    npt, nst = _n_tiles(*h)
    n = h[0].shape[0] + h[1].shape[0]
    tm = TOKEN_TILE
    row = lambda i: (i, 0)
    const = lambda i: (0, 0)
    return pl.pallas_call(
        functools.partial(_out_proj_kernel, alpha=alpha, n_prompt_tiles=npt),
        grid=(npt + nst,),
        in_specs=_group_specs(RET_W, npt) + _group_specs(RWKV_W, npt) + _group_specs(LRU_W, npt)
        + _group_specs(D_MODEL, npt) + [
                  pl.BlockSpec((D_MODEL, D_MODEL), const),
                  pl.BlockSpec((SUBLANES, D_MODEL), const),
                  pl.BlockSpec((D_MODEL, LANES), const),
                  pl.BlockSpec((1, LANES), const)],
        out_specs=[pl.BlockSpec((tm, D_MODEL), row), pl.BlockSpec((tm, LANES), row), pl.BlockSpec((tm, LANES), row),
                   pl.BlockSpec((SUBLANES, LANES), row), pl.BlockSpec((SUBLANES, LANES), const)],
        out_shape=[jax.ShapeDtypeStruct((n, D_MODEL), f32),
                   jax.ShapeDtypeStruct((n, LANES), f32),
                   jax.ShapeDtypeStruct((n, LANES), f32),
                   jax.ShapeDtypeStruct((n // tm * SUBLANES, LANES), i32),
                   jax.ShapeDtypeStruct((SUBLANES, LANES), f32)],
        compiler_params=_params("arbitrary"),
        name="out_proj_router",
    )(*y_ret, *y_rwkv, *y_lru, *h, w_out_bf, ln, w_router, b_router)


def _for_each_run_piece(meta_ref, pstart_ref, fn):
    for e in range(N_EXPERTS):
        start, n = meta_ref[0, e], meta_ref[1, e]
        base = pstart_ref[e] + meta_ref[2, e]
        for sz in RUN_PIECES:
            done = n & ~(2 * sz - 1)

            @pl.when((n & sz) != 0)
            def _():
                fn(pl.multiple_of(start + done, SUBLANES), pl.multiple_of(base + done, SUBLANES), sz)


def _wait_run_rows(meta_ref, make_copy):
    total = meta_ref[1, 0]
    for e in range(1, N_EXPERTS):
        total = total + meta_ref[1, e]
    for sz in WAIT_PIECES:
        @pl.when((total & sz) != 0)
        def _():
            make_copy(sz).wait()


def _dispatch_kernel(pstart_ref, cnt_ref, n_used_ref, meta_ref, meta_prev_ref, lpos_ref, x1_ref, xs_hbm,
                     xl_ref, zero_ref, sems, zsem):
    tm = TOKEN_TILE
    bm = MOE_BLOCK
    i = pl.program_id(0)
    slot = i % 2

    @pl.when(i == 0)
    def _():
        zero_ref[...] = jnp.zeros(zero_ref.shape, u32)
        tail = lambda j: pltpu.make_async_copy(zero_ref, xs_hbm.at[pl.ds(j * bm, bm), :], zsem)

        def tail_start(j, carry):
            tail(j).start()
            return carry

        def tail_wait(j, carry):
            tail(j).wait()
            return carry

        n_blocks = xs_hbm.shape[0] // bm
        lax.fori_loop(n_used_ref[0], n_blocks, tail_start, 0)
        lax.fori_loop(n_used_ref[0], n_blocks, tail_wait, 0)
        for e in range(N_EXPERTS):
            lo = pstart_ref[e] + cnt_ref[e]
            n_groups = ((cnt_ref[e] + bm - 1) // bm * bm - cnt_ref[e]) // SUBLANES
            fill = lambda g: pltpu.make_async_copy(
                zero_ref.at[pl.ds(0, SUBLANES), :],
                xs_hbm.at[pl.ds(pl.multiple_of(lo + g * SUBLANES, SUBLANES), SUBLANES), :], zsem)

            def start(g, carry):
                fill(g).start()
                return carry

            def wait(g, carry):
                fill(g).wait()
                return carry

            lax.fori_loop(0, n_groups, start, 0)
            lax.fori_loop(0, n_groups, wait, 0)

    lpos_t = lpos_ref[...].T
    srow = lax.broadcasted_iota(i32, (LOCAL_ROWS, tm), 0).astype(f32)
    perm = jnp.zeros((LOCAL_ROWS, tm), f32)
    for k in range(TOP_K):
        perm = jnp.where(srow == lpos_t[k:k + 1, :], 1.0, perm)
    xl_ref[slot] = _pack_halves(jnp.dot(perm.astype(bf16), x1_ref[...].astype(bf16), preferred_element_type=f32))

    def send(local_row, sorted_row, n_rows):
        pltpu.make_async_copy(xl_ref.at[slot, pl.ds(local_row, n_rows), :],
                              xs_hbm.at[pl.ds(sorted_row, n_rows), :], sems.at[slot]).start()

    _for_each_run_piece(meta_ref, pstart_ref, send)

    def sent(s):
        return lambda n_rows: pltpu.make_async_copy(xl_ref.at[s, pl.ds(0, n_rows), :],
                                                    xs_hbm.at[pl.ds(0, n_rows), :], sems.at[s])

    @pl.when(i > 0)
    def _():
        _wait_run_rows(meta_prev_ref, sent(1 - slot))

    @pl.when(i == pl.num_programs(0) - 1)
    def _():
        _wait_run_rows(meta_ref, sent(slot))


def _dispatch(pstart, counts, n_used, meta, lpos, x1, n_rows):
    n = x1.shape[0]
    tm = TOKEN_TILE
    smem_tile = lambda f: pl.BlockSpec((SUBLANES, LANES), f, memory_space=pltpu.SMEM)
    grid_spec = pltpu.PrefetchScalarGridSpec(
        num_scalar_prefetch=3,
        grid=(n // tm,),
        in_specs=[smem_tile(lambda i, ps, ct, nu: (i, 0)),
                  smem_tile(lambda i, ps, ct, nu: (jnp.maximum(i - 1, 0), 0)),
                  pl.BlockSpec((tm, LANES), lambda i, ps, ct, nu: (i, 0)),
                  pl.BlockSpec((tm, D_MODEL), lambda i, ps, ct, nu: (i, 0))],
        out_specs=pl.BlockSpec(memory_space=pl.ANY),
        scratch_shapes=[pltpu.VMEM((2, LOCAL_ROWS, D_MODEL // 2), u32),
                        pltpu.VMEM((MOE_BLOCK, D_MODEL // 2), u32),
                        pltpu.SemaphoreType.DMA((2,)), pltpu.SemaphoreType.DMA(())],
    )
    return pl.pallas_call(
        _dispatch_kernel,
        grid_spec=grid_spec,
        out_shape=jax.ShapeDtypeStruct((n_rows, D_MODEL // 2), u32),
        compiler_params=_params("arbitrary"),
        name="moe_dispatch",
    )(pstart, counts, n_used, meta, meta, lpos, x1)


def _expert_kernel(blk_e_ref, n_used_ref, next_e_ref, rows_ref, xs_ref, wgu_hbm, bgu_ref, wdn_hbm, bdn_ref,
                   out_ref, wgu_f32, wdn_f32, wgu_bf, wdn_bf, sems, *, layer):
    i = pl.program_id(0)
    prev = jnp.maximum(i - 1, 0)
    expert = blk_e_ref[i]
    new_expert = jnp.logical_or(i == 0, expert != blk_e_ref[prev])
    used = i < n_used_ref[0]

    def fetch(e):
        return (pltpu.make_async_copy(wgu_hbm.at[layer, e], wgu_f32, sems.at[0]),
                pltpu.make_async_copy(wdn_hbm.at[layer, e], wdn_f32, sems.at[1]))

    @pl.when(used)
    def _():
        @pl.when(new_expert)
        def _():
            @pl.when(i == 0)
            def _():
                for c in fetch(expert):
                    c.start()

            for c in fetch(expert):
                c.wait()
            wgu_bf[...] = wgu_f32[...].astype(bf16)
            wdn_bf[...] = wdn_f32[...].astype(bf16)
            nxt = next_e_ref[expert]

            @pl.when(nxt >= 0)
            def _():
                for c in fetch(nxt):
                    c.start()

        def run(n_rows):
            rows = pl.ds(0, n_rows)
            x_lo, x_hi = _unpack_halves(xs_ref[rows, :])
            half = D_MODEL // 2
            gu = (jnp.dot(x_lo.astype(bf16), wgu_bf[pl.ds(0, half), :], preferred_element_type=f32)
                  + jnp.dot(x_hi.astype(bf16), wgu_bf[pl.ds(half, half), :], preferred_element_type=f32)
                  + bgu_ref[0, 0])
            g = jnp.minimum(gu[:, :D_EXPERT], SWIGLU_LIMIT)
            u = jnp.clip(gu[:, D_EXPERT:], -SWIGLU_LIMIT, SWIGLU_LIMIT)
            hdn = (u + 1.0) * g * _sigmoid(SWIGLU_ALPHA * g)
            y = jnp.dot(hdn.astype(bf16), wdn_bf[...], preferred_element_type=f32) + bdn_ref[0, 0]
            out_ref[rows, :] = _pack_halves(y.astype(bf16).astype(f32))
            if n_rows < MOE_BLOCK:
                out_ref[pl.ds(n_rows, MOE_BLOCK - n_rows), :] = jnp.zeros((MOE_BLOCK - n_rows, D_MODEL // 2), u32)

        @pl.when(rows_ref[i] > MOE_BLOCK // 2)
        def _():
            run(MOE_BLOCK)

        @pl.when(rows_ref[i] <= MOE_BLOCK // 2)
        def _():
            run(MOE_BLOCK // 2)

    @pl.when(jnp.logical_not(used))
    def _():
        out_ref[...] = jnp.zeros(out_ref.shape, u32)


def _experts(blk_e, n_used, next_e, blk_rows, xs, w_gu, b_gu, w_down, b_down, layer):
    n_blocks = blk_e.shape[0]
    bm = MOE_BLOCK
    by_e = lambda i, be, nu, ne, br: (layer, be[i], 0, 0)
    x_blk = lambda i, be, nu, ne, br: (jnp.minimum(i, nu[0] - 1), 0)
    grid_spec = pltpu.PrefetchScalarGridSpec(
        num_scalar_prefetch=4,
        grid=(n_blocks,),
        in_specs=[pl.BlockSpec((bm, D_MODEL // 2), x_blk),
                  pl.BlockSpec(memory_space=pl.ANY),
                  pl.BlockSpec((1, 1, 1, 2 * D_EXPERT), by_e),
                  pl.BlockSpec(memory_space=pl.ANY),
                  pl.BlockSpec((1, 1, 1, D_MODEL), by_e)],
        out_specs=pl.BlockSpec((bm, D_MODEL // 2), lambda i, be, nu, ne, br: (i, 0)),
        scratch_shapes=[pltpu.VMEM((D_MODEL, 2 * D_EXPERT), f32),
                        pltpu.VMEM((D_EXPERT, D_MODEL), f32),
                        pltpu.VMEM((D_MODEL, 2 * D_EXPERT), bf16),
                        pltpu.VMEM((D_EXPERT, D_MODEL), bf16),
                        pltpu.SemaphoreType.DMA((2,))],
    )
    depth = w_gu.shape[0]
    return pl.pallas_call(
        functools.partial(_expert_kernel, layer=layer),
        grid_spec=grid_spec,
        out_shape=jax.ShapeDtypeStruct((n_blocks * bm, D_MODEL // 2), u32),
        compiler_params=_params("arbitrary"),
        name="moe_experts",
    )(blk_e, n_used, next_e, blk_rows, xs, w_gu, b_gu.reshape(depth, N_EXPERTS, 1, 2 * D_EXPERT),
      w_down, b_down.reshape(depth, N_EXPERTS, 1, D_MODEL))


def _combine_kernel(pstart_ref, meta_ref, meta_next_ref, lpos_ref, gates_ref, x1_ref, ln_ref, yb_hbm,
                    outp_ref, outs_ref, yl_ref, sems, *, alpha, n_prompt_tiles):
    tm = TOKEN_TILE
    i = pl.program_id(0)
    slot = i % 2

    def fetch(meta, s):
        def recv(local_row, sorted_row, n_rows):
            pltpu.make_async_copy(yb_hbm.at[pl.ds(sorted_row, n_rows), :],
                                  yl_ref.at[s, pl.ds(local_row, n_rows), :], sems.at[s]).start()

        _for_each_run_piece(meta, pstart_ref, recv)

    @pl.when(i == 0)
    def _():
        yl_ref[...] = jnp.zeros(yl_ref.shape, u32)
        fetch(meta_ref, slot)

    @pl.when(i + 1 < pl.num_programs(0))
    def _():
        fetch(meta_next_ref, 1 - slot)

    _wait_run_rows(meta_ref, lambda n_rows: pltpu.make_async_copy(
        yb_hbm.at[pl.ds(0, n_rows), :], yl_ref.at[slot, pl.ds(0, n_rows), :], sems.at[slot]))

    lpos = lpos_ref[...]
    gates = gates_ref[...]
    scol = lax.broadcasted_iota(i32, (tm, LOCAL_ROWS), 1).astype(f32)
    weight = jnp.zeros((tm, LOCAL_ROWS), f32)
    for k in range(TOP_K):
        weight = jnp.where(scol == lpos[:, k:k + 1], gates[:, k:k + 1], weight)
    y_lo, y_hi = _unpack_halves(yl_ref[slot])
    weight = weight.astype(bf16)
    y = jnp.concatenate([jnp.dot(weight, y_lo.astype(bf16), preferred_element_type=f32),
                         jnp.dot(weight, y_hi.astype(bf16), preferred_element_type=f32)], axis=1)
    x2 = _layer_norm_rows(alpha * x1_ref[...] + y, ln_ref[pl.ds(0, 1), :], ln_ref[pl.ds(1, 1), :])
    _store_group(outp_ref, outs_ref, x2, n_prompt_tiles)


def _combine(pstart, meta, lpos, gates, x1, ln, yb, alpha, n_prompt_tiles):
    n = x1.shape[0]
    tm = TOKEN_TILE
    n_tiles = n // tm
    row = lambda i, ps: (i, 0)
    smem_tile = lambda f: pl.BlockSpec((SUBLANES, LANES), f, memory_space=pltpu.SMEM)
    grid_spec = pltpu.PrefetchScalarGridSpec(
        num_scalar_prefetch=1,
        grid=(n_tiles,),
        in_specs=[smem_tile(row),
                  smem_tile(lambda i, ps: (jnp.minimum(i + 1, n_tiles - 1), 0)),
                  pl.BlockSpec((tm, LANES), row),
                  pl.BlockSpec((tm, LANES), row),
                  pl.BlockSpec((tm, D_MODEL), row),
                  pl.BlockSpec((SUBLANES, D_MODEL), lambda i, ps: (0, 0)),
                  pl.BlockSpec(memory_space=pl.ANY)],
        out_specs=_group_specs(D_MODEL, n_prompt_tiles),
        scratch_shapes=[pltpu.VMEM((2, LOCAL_ROWS, D_MODEL // 2), u32), pltpu.SemaphoreType.DMA((2,))],
    )
    return pl.pallas_call(
        functools.partial(_combine_kernel, alpha=alpha, n_prompt_tiles=n_prompt_tiles),
        grid_spec=grid_spec,
        out_shape=[jax.ShapeDtypeStruct((n_prompt_tiles * tm, D_MODEL), f32),
                   jax.ShapeDtypeStruct((n - n_prompt_tiles * tm, D_MODEL), f32)],
        compiler_params=_params("arbitrary"),
        name="moe_combine",
    )(pstart, meta, meta, lpos, gates, x1, ln, yb)


def _block_tables(counts, n_tokens):
    bm = MOE_BLOCK
    padded = (counts + bm - 1) // bm * bm
    pad_end = jnp.cumsum(padded)
    pstart = (pad_end - padded).astype(i32)
    max_used = n_tokens * TOP_K + (n_tokens // TOKEN_TILE) * N_EXPERTS * (SUBLANES - 1)
    n_blocks = -(-(max_used + N_EXPERTS * (bm - 1)) // bm)
    first_row = jnp.arange(n_blocks, dtype=pad_end.dtype) * bm
    blk_e = jnp.minimum(jnp.sum(pad_end[None, :] <= first_row[:, None], axis=1), N_EXPERTS - 1).astype(i32)
    n_used = (pad_end[-1] // bm).astype(i32).reshape(1)
    idx = jnp.arange(N_EXPERTS)
    later = jnp.logical_and(idx[None, :] > idx[:, None], counts[None, :] > 0)
    next_e = jnp.min(jnp.where(later, idx[None, :], N_EXPERTS), axis=1)
    next_e = jnp.where(next_e == N_EXPERTS, -1, next_e).astype(i32)
    used_end = pstart + counts
    blk_rows = jnp.sum(jnp.where(idx[None, :] == blk_e[:, None], used_end[None, :], 0), axis=1) - first_row
    blk_rows = jnp.clip(blk_rows, 0, bm).astype(i32)
    return pstart, blk_e, n_used, next_e, blk_rows, n_blocks * bm


def _pad_time(t, n_batch, n_t, t_pad):
    w = t.shape[-1]
    return jnp.pad(t.reshape(n_batch, n_t, w), ((0, 0), (0, t_pad - n_t), (0, 0))).reshape(n_batch * t_pad, w)


def _block_diag_weight(w):
    h = w.shape[0]
    eye = jnp.eye(h, dtype=w.dtype)
    return (eye[:, None, :, None] * w[:, :, None, :]).reshape(h * HEAD_DIM, h * HEAD_DIM)


def _rows8(*rows):
    width = rows[0].shape[-1]
    m = jnp.stack([r.reshape(width) for r in rows])
    return jnp.pad(m, ((0, SUBLANES - m.shape[0]), (0, 0)))


def _layer(h, p, moe, layer, alpha, st_s, bp, tp, bs, ts):
    n_p, n_s = bp * tp, bs * ts
    tpad = SAMPLE_T_PAD
    (ret_p, ret_s), (rw_p, rw_s), (lru_p, lru_s) = _in_proj(*h, p['w_in'].astype(bf16))
    unpad = lambda y: y.reshape(bs, tpad, -1)[:, :ts].reshape(n_s, -1)

    c_p = RET_CHUNK if tp % RET_CHUNK == 0 else tp
    cos_p, sin_p = _rope_tables(jnp.arange(tp, dtype=f32))
    cos_s, sin_s = _rope_tables(PAST_LEN + jnp.arange(tpad, dtype=f32))
    y_ret_p, sret_p = _retention(ret_p.reshape(bp, tp, 4 * RET_W), cos_p, sin_p, _retention_tables(c_p, c_p),
                                 jnp.zeros((1, bp, RET_H, HEAD_DIM, HEAD_DIM), f32), 0,
                                 nb=max(d for d in (1, 2, 4, 8) if bp % d == 0), length=c_p)
    y_ret_s, sret_s = _retention(_pad_time(ret_s, bs, ts, tpad).reshape(bs, tpad, 4 * RET_W), cos_s, sin_s,
                                 _retention_tables(tpad, ts), st_s['ret'], layer, nb=SUBLANES, length=tpad)
    y_ret = (y_ret_p.reshape(n_p, RET_W), unpad(y_ret_s))

    lora = jnp.zeros((3, LANES, RWKV_W), f32)
    lora = lora.at[0, 0:32].set(p['w_up']).at[1, 32:64].set(p['a_up']).at[2, 64:128].set(p['g_up'])
    vec = _rows8(p['w0'], p['a0'], p['k_k'], p['k_a'], p['r_k'], p['gn_g'], p['gn_b'])
    mix = p['mix'].reshape(1, RWKV_PROJ)
    rows = RWKV_ROWS
    y_rwkv_p, srw_p, shift_p = _rwkv(rw_p, jnp.zeros((bp, 1, RWKV_PROJ), f32), jnp.ones((rows, RWKV_W), f32), mix,
                                     vec, lora, _chunk_tri(rows, RWKV_CHUNK),
                                     jnp.zeros((1, bp, RWKV_H, HEAD_DIM, HEAD_DIM), f32), 0,
                                     n_groups=bp, n_steps=tp // rows, rows=rows, length=RWKV_CHUNK, chain=True)
    valid_s = (jnp.arange(tpad) < ts).astype(f32)[:, None]
    rw_s3 = rw_s.reshape(bs, ts, RWKV_PROJ)
    prev_s = jnp.concatenate([st_s['shift'][:, None, :], rw_s3[:, :-1]], axis=1).reshape(n_s, RWKV_PROJ)
    seq_per_blk = rows // tpad
    y_rwkv_s, srw_s = _rwkv(_pad_time(rw_s, bs, ts, tpad), _pad_time(prev_s, bs, ts, tpad),
                            jnp.tile(jnp.broadcast_to(valid_s, (tpad, RWKV_W)), (seq_per_blk, 1)),
                            mix, vec, lora, _chunk_tri(rows, tpad), st_s['rwkv'], layer,
                            n_groups=bs // seq_per_blk, n_steps=1, rows=rows, length=tpad, chain=False)
    y_rwkv = (y_rwkv_p, unpad(y_rwkv_s))

    lvec = _rows8(p['conv_w'][0], p['conv_w'][1], p['conv_w'][2], p['conv_w'][3],
                  p['conv_b'], p['ba'], p['bx'], p['lam'])
    wa_bd = _block_diag_weight(p['wa']).astype(bf16)
    wx_bd = _block_diag_weight(p['wx']).astype(bf16)
    l_l = LRU_CHUNK if tp % LRU_CHUNK == 0 else tp
    y_lru_p, h_p, tail_p = _lru(lru_p, jnp.ones((l_l, LRU_W), f32), lvec, wa_bd, wx_bd,
                                jnp.zeros((bp, SUBLANES, LRU_W), f32), jnp.zeros((bp, SUBLANES, LRU_W), f32),
                                n_batch=bp, n_chunks=tp // l_l, nb=1, length=l_l, out_rows=n_p)
    conv0_s = jnp.pad(st_s['conv'], ((0, 0), (SUBLANES - (CONV_WIDTH - 1), 0), (0, 0)))
    h0_s = jnp.broadcast_to(st_s['lru'][:, None, :], (bs, SUBLANES, LRU_W))
    y_lru_s, h_s, _ = _lru(_pad_time(lru_s, bs, ts, tpad), jnp.broadcast_to(valid_s, (tpad, LRU_W)),
                           lvec, wa_bd, wx_bd, conv0_s, h0_s,
                           n_batch=bs, n_chunks=1, nb=SUBLANES, length=tpad, out_rows=bs * tpad)
    y_lru = (y_lru_p, unpad(y_lru_s))

    w_router = jnp.pad(p['w_router'], ((0, 0), (0, LANES - N_EXPERTS)))
    b_router = jnp.pad(p['b_router'], (0, LANES - N_EXPERTS), constant_values=-1e30).reshape(1, LANES)
    x1, lpos, gates, meta, cnt = _out_proj_router(y_ret, y_rwkv, y_lru, h, p['w_out'].astype(bf16),
                                                  _rows8(p['ln1_g'], p['ln1_b']), w_router, b_router, alpha)

    counts = cnt[0, :N_EXPERTS].astype(i32)
    pstart, blk_e, n_used, next_e, blk_rows, n_rows = _block_tables(counts, n_p + n_s)
    xs = _dispatch(pstart, counts, n_used, meta, lpos, x1, n_rows)
    yb = _experts(blk_e, n_used, next_e, blk_rows, xs, moe['w_gu'], moe['b_gu'], moe['w_down'], moe['b_down'],
                  layer)
    x2 = _combine(pstart, meta, lpos, gates, x1, _rows8(p['ln2_g'], p['ln2_b']), yb, alpha, n_p // TOKEN_TILE)

    keep = CONV_WIDTH - 1
    assert tp >= SUBLANES and ts >= keep
    conv_s = lru_s.reshape(bs, ts, 2 * LRU_W)[:, ts - keep:, LRU_W:]
    new_p = (sret_p, srw_p, shift_p[:, 0], h_p[:, 0], tail_p[:, SUBLANES - keep:])
    new_s = (sret_s, srw_s, rw_s3[:, -1], h_s[:, 0], conv_s)
    return x2, new_p, new_s


def kernel(x_prompt, x_sample, state_ret, state_rwkv, state_rwkv_shift, state_lru, state_conv,
           w_in, w_out, ln1_g, ln1_b, ln2_g, ln2_b,
           rwkv_mix, rwkv_w0, rwkv_w_up, rwkv_a0, rwkv_a_up, rwkv_g_up, rwkv_k_k, rwkv_k_a, rwkv_r_k,
           rwkv_gn_g, rwkv_gn_b, lru_conv_w, lru_conv_b, lru_wa, lru_ba, lru_wx, lru_bx, lru_lambda,
           moe_w_router, moe_b_router, moe_w_gate_up, moe_b_gate_up, moe_w_down, moe_b_down):
    bp, tp, _ = x_prompt.shape
    bs, ts, _ = x_sample.shape
    depth = w_in.shape[0]
    alpha = (2.0 * depth) ** 0.25
    moe = {'w_gu': moe_w_gate_up, 'b_gu': moe_b_gate_up, 'w_down': moe_w_down, 'b_down': moe_b_down}
    h = (x_prompt.reshape(bp * tp, D_MODEL), x_sample.reshape(bs * ts, D_MODEL))
    new_p, new_s = [], []
    for l in range(depth):
        p = {'w_in': w_in[l], 'w_out': w_out[l], 'ln1_g': ln1_g[l], 'ln1_b': ln1_b[l],
             'ln2_g': ln2_g[l], 'ln2_b': ln2_b[l], 'mix': rwkv_mix[l], 'w0': rwkv_w0[l],
             'w_up': rwkv_w_up[l], 'a0': rwkv_a0[l], 'a_up': rwkv_a_up[l], 'g_up': rwkv_g_up[l],
             'k_k': rwkv_k_k[l], 'k_a': rwkv_k_a[l], 'r_k': rwkv_r_k[l], 'gn_g': rwkv_gn_g[l],
             'gn_b': rwkv_gn_b[l], 'conv_w': lru_conv_w[l], 'conv_b': lru_conv_b[l], 'wa': lru_wa[l],
             'ba': lru_ba[l], 'wx': lru_wx[l], 'bx': lru_bx[l], 'lam': lru_lambda[l],
             'w_router': moe_w_router[l], 'b_router': moe_b_router[l]}
        st_s = {'ret': state_ret, 'rwkv': state_rwkv, 'shift': state_rwkv_shift[l],
                'lru': state_lru[l], 'conv': state_conv[l]}
        h, sp, ss = _layer(h, p, moe, l, alpha, st_s, bp, tp, bs, ts)
        new_p.append(sp)
        new_s.append(ss)
    outs = [h[0].reshape(bp, tp, D_MODEL), h[1].reshape(bs, ts, D_MODEL)]
    for i in range(5):
        outs.append(jnp.stack([s[i] for s in new_p]))
        outs.append(jnp.stack([s[i] for s in new_s]))
    return tuple(outs)
```

```python
import functools

import jax
import jax.numpy as jnp
from jax import lax
from jax.experimental import pallas as pl
from jax.experimental.pallas import tpu as pltpu

f32 = jnp.float32
bf16 = jnp.bfloat16
i32 = jnp.int32
u32 = jnp.uint32

D_MODEL = 1024
HEAD_DIM = 64
RET_W = 256
RET_H = 4
RET_CHUNK = 128
ROPE_BASE = 10000.0
RWKV_W = 512
RWKV_H = 8
RWKV_PROJ = 1664
RWKV_LORA_COL = 1536
RWKV_GN_EPS = 64e-5
RWKV_CHUNK = 64
RWKV_ROWS = 128
LRU_W = 256
LRU_C = 8.0
LRU_CHUNK = 256
CONV_WIDTH = 4
D_PROJ = 3200
N_EXPERTS = 32
TOP_K = 4
D_EXPERT = 1024
SWIGLU_LIMIT = 7.0
SWIGLU_ALPHA = 1.702
LN_EPS = 1e-5
PAST_LEN = 16384.0

LANES = 128
SUBLANES = 8
SAMPLE_T_PAD = 8
TOKEN_TILE = 512
MOE_BLOCK = 512
LOCAL_ROWS = TOP_K * TOKEN_TILE + N_EXPERTS * SUBLANES
RUN_PIECES = tuple(SUBLANES << j for j in reversed(range((TOKEN_TILE // SUBLANES).bit_length())))
WAIT_PIECES = tuple(SUBLANES << j for j in reversed(range((LOCAL_ROWS // SUBLANES).bit_length())))
VMEM_LIMIT = 56 * 1024 * 1024

_NT = (((1,), (1,)), ((), ()))
_TN = (((0,), (0,)), ((), ()))


def _params(*sem):
    return pltpu.CompilerParams(dimension_semantics=sem, vmem_limit_bytes=VMEM_LIMIT)


def _mm(a, b):
    return jnp.dot(a.astype(bf16), b.astype(bf16), preferred_element_type=f32)


def _mm_nt(a, b):
    return lax.dot_general(a.astype(bf16), b.astype(bf16), _NT, preferred_element_type=f32)


def _mm_tn(a, b):
    return lax.dot_general(a.astype(bf16), b.astype(bf16), _TN, preferred_element_type=f32)


def _mm3(a, b):
    a_hi, b_hi = a.astype(bf16), b.astype(bf16)
    a_lo = (a - a_hi.astype(f32)).astype(bf16)
    b_lo = (b - b_hi.astype(f32)).astype(bf16)
    dot = functools.partial(jnp.dot, preferred_element_type=f32)
    return dot(a_hi, b_hi) + dot(a_hi, b_lo) + dot(a_lo, b_hi)


def _pack_halves(x):
    w = x.shape[-1] // 2
    lo = lax.bitcast_convert_type(x[:, :w], u32)
    hi = lax.bitcast_convert_type(x[:, w:], u32)
    return (lo >> 16) | hi


def _unpack_halves(p):
    return (lax.bitcast_convert_type(p << 16, f32),
            lax.bitcast_convert_type(p & jnp.uint32(0xFFFF0000), f32))


def _softplus(x):
    return jnp.maximum(x, 0.0) + jnp.log(1.0 + jnp.exp(-jnp.abs(x)))


def _sigmoid(x):
    return 1.0 / (1.0 + jnp.exp(-x))


def _half_masks():
    lane = lax.broadcasted_iota(i32, (1, LANES), 1)
    m0 = (lane < HEAD_DIM).astype(f32)
    return m0, 1.0 - m0


def _seg_mean(x, m0, m1):
    s0 = jnp.sum(x * m0, axis=-1, keepdims=True)
    s1 = jnp.sum(x * m1, axis=-1, keepdims=True)
    return (m0 * s0 + m1 * s1) * (1.0 / HEAD_DIM)


def _seg_sum(x, m0, m1):
    s0 = jnp.sum(x * m0, axis=-1, keepdims=True)
    s1 = jnp.sum(x * m1, axis=-1, keepdims=True)
    return m0 * s0 + m1 * s1


def _block_diag_mask():
    r = lax.broadcasted_iota(i32, (LANES, LANES), 0) // HEAD_DIM
    c = lax.broadcasted_iota(i32, (LANES, LANES), 1) // HEAD_DIM
    return (r == c).astype(f32)


def _layer_norm_rows(z, g, b):
    mu = jnp.mean(z, axis=-1, keepdims=True)
    zc = z - mu
    var = jnp.mean(zc * zc, axis=-1, keepdims=True)
    return zc * lax.rsqrt(var + LN_EPS) * g + b


def _group_specs(width, n_prompt_tiles):
    return [pl.BlockSpec((TOKEN_TILE, width), lambda i, *_: (jnp.minimum(i, n_prompt_tiles - 1), 0)),
            pl.BlockSpec((TOKEN_TILE, width), lambda i, *_: (jnp.maximum(i - n_prompt_tiles, 0), 0))]


def _pick_group(p_ref, s_ref, n_prompt_tiles):
    return jnp.where(pl.program_id(0) >= n_prompt_tiles, s_ref[...], p_ref[...])


def _n_tiles(h_p, h_s):
    assert h_p.shape[0] % TOKEN_TILE == 0 and h_s.shape[0] % TOKEN_TILE == 0, (h_p.shape, h_s.shape)
    return h_p.shape[0] // TOKEN_TILE, h_s.shape[0] // TOKEN_TILE


def _store_group(p_ref, s_ref, value, n_prompt_tiles):
    @pl.when(pl.program_id(0) < n_prompt_tiles)
    def _():
        p_ref[...] = value

    @pl.when(pl.program_id(0) >= n_prompt_tiles)
    def _():
        s_ref[...] = value


def _in_proj_kernel(xp_ref, xs_ref, w_ref, retp_ref, rets_ref, rwp_ref, rws_ref, lrup_ref, lrus_ref, *,
                    n_prompt_tiles):
    c0, c1 = 4 * RET_W, 4 * RET_W + RWKV_PROJ
    dot = functools.partial(jnp.dot, preferred_element_type=f32)

    def project(x_ref, ret_ref, rw_ref, lru_ref):
        xb = x_ref[...].astype(bf16)
        ret_ref[...] = dot(xb, w_ref[:, :c0])
        rw_ref[...] = dot(xb, w_ref[:, c0:c1])
        lru_ref[...] = dot(xb, w_ref[:, c1:])

    @pl.when(pl.program_id(0) < n_prompt_tiles)
    def _():
        project(xp_ref, retp_ref, rwp_ref, lrup_ref)

    @pl.when(pl.program_id(0) >= n_prompt_tiles)
    def _():
        project(xs_ref, rets_ref, rws_ref, lrus_ref)


def _in_proj(h_p, h_s, w_bf):
    npt, nst = _n_tiles(h_p, h_s)
    widths = (4 * RET_W, RWKV_PROJ, 2 * LRU_W)
    out = pl.pallas_call(
        functools.partial(_in_proj_kernel, n_prompt_tiles=npt),
        grid=(npt + nst,),
        in_specs=_group_specs(D_MODEL, npt) + [
                  pl.BlockSpec((D_MODEL, D_PROJ), lambda i: (0, 0))],
        out_specs=[s for w in widths for s in _group_specs(w, npt)],
        out_shape=[jax.ShapeDtypeStruct((rows, w), f32) for w in widths for rows in (h_p.shape[0], h_s.shape[0])],
        compiler_params=_params("arbitrary"),
        name="in_proj",
    )(h_p, h_s, w_bf)
    return out[0:2], out[2:4], out[4:6]


def _rope_tables(pos):
    half = HEAD_DIM // 2
    inv = ROPE_BASE ** (-jnp.arange(half, dtype=f32) / half)
    ang = pos[:, None] * inv[None, :]
    cos, sin = jnp.cos(ang), jnp.sin(ang)
    cos_f = jnp.tile(jnp.concatenate([cos, cos], axis=-1), (1, RET_H))
    sin_f = jnp.tile(jnp.concatenate([-sin, sin], axis=-1), (1, RET_H))
    return cos_f, sin_f


def _retention_tables(length, n_valid):
    lg = jnp.log1p(-jnp.exp2(-5.0 - jnp.arange(RET_H, dtype=f32)))
    idx = jnp.arange(length, dtype=f32)
    rel = idx[:, None] - idx[None, :]
    mask = jnp.where(rel[None] >= 0, jnp.exp(jnp.maximum(rel, 0.0)[None] * lg[:, None, None]), 0.0)
    q_dec = jnp.exp((idx[:, None] + 1.0) * lg[None, :])
    k_dec = jnp.where(idx[:, None] < n_valid, jnp.exp((n_valid - 1.0 - idx)[:, None] * lg[None, :]), 0.0)
    c_dec = jnp.exp(n_valid * lg)[None, :]
    rep = lambda t: jnp.repeat(t, HEAD_DIM, axis=-1)
    return mask, rep(q_dec), rep(k_dec), rep(c_dec)


def _retention_kernel(ret_ref, cos_ref, sin_ref, mask_ref, qdec_ref, kdec_ref, cdec_ref, s0_ref,
                      y_ref, s_ref, pair_ref, *, nb):
    @pl.when(pl.program_id(1) == 0)
    def _():
        _load_head_pairs(pair_ref, s0_ref)

    m0, m1 = _half_masks()
    lane = lax.broadcasted_iota(i32, (1, LANES), 1)
    first_half = (lane % HEAD_DIM) < (HEAD_DIM // 2)
    bd = _block_diag_mask()

    def rope(x, cs, sn):
        swapped = jnp.where(first_half, pltpu.roll(x, LANES - HEAD_DIM // 2, 1), pltpu.roll(x, HEAD_DIM // 2, 1))
        return x * cs + swapped * sn

    for j in range(nb):
        for p in range(RET_H // 2):
            cols = pl.ds(p * LANES, LANES)
            cs, sn = cos_ref[:, cols], sin_ref[:, cols]
            q2 = rope(ret_ref[j, :, pl.ds(p * LANES, LANES)], cs, sn)
            k2 = rope(ret_ref[j, :, pl.ds(RET_W + p * LANES, LANES)], cs, sn) * (HEAD_DIM ** -0.5)
            v2 = ret_ref[j, :, pl.ds(2 * RET_W + p * LANES, LANES)]
            g2 = ret_ref[j, :, pl.ds(3 * RET_W + p * LANES, LANES)]
            state = pair_ref[j, p]
            o2 = _mm(q2, state) * qdec_ref[:, cols]
            for hh, m in enumerate((m0, m1)):
                sc = _mm_nt(q2 * m, k2) * mask_ref[2 * p + hh]
                o2 = o2 + _mm(sc, v2) * m
            pair_ref[j, p] = state * cdec_ref[:, cols] + _mm_tn(k2 * kdec_ref[:, cols], v2) * bd
            mu = _seg_mean(o2, m0, m1)
            oc = o2 - mu
            var = _seg_mean(oc * oc, m0, m1)
            y_ref[j, :, cols] = g2 * _sigmoid(g2) * oc * lax.rsqrt(var + LN_EPS)

    @pl.when(pl.program_id(1) == pl.num_programs(1) - 1)
    def _():
        _store_head_pairs(s_ref, pair_ref)


def _load_head_pairs(pair_ref, s_ref):
    pair_ref[...] = jnp.zeros(pair_ref.shape, f32)
    for j in range(pair_ref.shape[0]):
        for p in range(pair_ref.shape[1]):
            pair_ref[j, p, pl.ds(0, HEAD_DIM), pl.ds(0, HEAD_DIM)] = s_ref[j, 2 * p]
            pair_ref[j, p, pl.ds(HEAD_DIM, HEAD_DIM), pl.ds(HEAD_DIM, HEAD_DIM)] = s_ref[j, 2 * p + 1]


def _store_head_pairs(s_ref, pair_ref):
    for j in range(pair_ref.shape[0]):
        for p in range(pair_ref.shape[1]):
            s_ref[j, 2 * p] = pair_ref[j, p, pl.ds(0, HEAD_DIM), pl.ds(0, HEAD_DIM)]
            s_ref[j, 2 * p + 1] = pair_ref[j, p, pl.ds(HEAD_DIM, HEAD_DIM), pl.ds(HEAD_DIM, HEAD_DIM)]


def _retention(ret, cos_f, sin_f, tables, s0, layer, *, nb, length):
    mask, q_dec, k_dec, c_dec = tables
    n_batch, t_len, _ = ret.shape
    blk = lambda g, c: (g, c, 0)
    const2 = lambda g, c: (0, 0)
    st = lambda g, c: (g, 0, 0, 0)
    return pl.pallas_call(
        functools.partial(_retention_kernel, nb=nb),
        grid=(n_batch // nb, t_len // length),
        in_specs=[pl.BlockSpec((nb, length, 4 * RET_W), blk),
                  pl.BlockSpec((length, RET_W), lambda g, c: (c, 0)),
                  pl.BlockSpec((length, RET_W), lambda g, c: (c, 0)),
                  pl.BlockSpec((RET_H, length, length), lambda g, c: (0, 0, 0)),
                  pl.BlockSpec((length, RET_W), const2),
                  pl.BlockSpec((length, RET_W), const2),
                  pl.BlockSpec((1, RET_W), const2),
                  pl.BlockSpec((None, nb, RET_H, HEAD_DIM, HEAD_DIM), lambda g, c: (layer, g, 0, 0, 0))],
        out_specs=[pl.BlockSpec((nb, length, RET_W), blk),
                   pl.BlockSpec((nb, RET_H, HEAD_DIM, HEAD_DIM), st)],
        out_shape=[jax.ShapeDtypeStruct((n_batch, t_len, RET_W), f32),
                   jax.ShapeDtypeStruct((n_batch, RET_H, HEAD_DIM, HEAD_DIM), f32)],
        scratch_shapes=[pltpu.VMEM((nb, RET_H // 2, LANES, LANES), f32)],
        compiler_params=_params("arbitrary", "arbitrary"),
        name="retention",
    )(ret, cos_f, sin_f, mask, q_dec, k_dec, c_dec, s0)


def _rwkv_kernel(rw_ref, aux_ref, valid_ref, mix_ref, vec_ref, lora_ref, tri_ref, s0_ref,
                 y_ref, s_ref, *rest, rows, length, chain):
    n_chunks = rows // length
    rw = rw_ref[...]
    if chain:
        last_ref, xs_ref, pair_ref = rest

        @pl.when(pl.program_id(1) == 0)
        def _():
            _load_head_pairs(pair_ref, s0_ref)
            xs_ref[pl.ds(0, SUBLANES), :] = jnp.broadcast_to(aux_ref[0], (SUBLANES, RWKV_PROJ))

        xs_ref[pl.ds(SUBLANES, rows), :] = rw
        prev = xs_ref[pl.ds(SUBLANES - 1, rows), :]
        xs_ref[pl.ds(0, SUBLANES), :] = rw[rows - SUBLANES:, :]
        last_ref[0] = rw[rows - 1:rows, :]
    else:
        pair_ref, = rest
        _load_head_pairs(pair_ref, s0_ref)
        prev = aux_ref[...]
    rwm = rw + (prev - rw) * mix_ref[...]

    valid = valid_ref[...]
    w0, a0, k_k, k_a, r_k, gn_g, gn_b = (vec_ref[pl.ds(i, 1), :] for i in range(7))
    lo = rwm[:, RWKV_LORA_COL:]
    lw = _mm3(jnp.tanh(lo), lora_ref[0])
    la = _mm3(lo, lora_ref[1])
    gate = _mm3(_sigmoid(lo), lora_ref[2])
    logw = -jnp.exp(-_softplus(-(w0 + lw)) - 0.5) * valid
    a = _sigmoid(a0 + la)
    r = rwm[:, :RWKV_W]
    kr = rwm[:, RWKV_W:2 * RWKV_W]
    vr = rwm[:, 2 * RWKV_W:3 * RWKV_W]
    kk_raw = kr * k_k
    kp = kr * (1.0 + (a - 1.0) * k_a) * valid
    cum = _mm3(tri_ref[...], logw)
    g_incl = jnp.exp(cum)
    g_inv = jnp.exp(-cum)
    g_prev = jnp.exp(cum - logw)
    g_end = jnp.concatenate(
        [jnp.broadcast_to(g_incl[(c + 1) * length - 1:(c + 1) * length, :], (length, RWKV_W)) for c in range(n_chunks)],
        axis=0)

    m0, m1 = _half_masks()
    bd = _block_diag_mask()
    stacked = 2 * rows
    ri = lax.broadcasted_iota(i32, (stacked, stacked), 0)
    ci = lax.broadcasted_iota(i32, (stacked, stacked), 1)
    same = (ri // length) == (ci // length)
    strict = jnp.logical_and(same, ci < ri)
    incl = jnp.logical_and(same, ci <= ri)
    eye = (ci == ri).astype(f32)
    stack = lambda t: jnp.concatenate([t * m0, t * m1], axis=0)
    unstack = lambda t: t[:rows] + t[rows:]
    pairs = range(RWKV_H // 2)

    pre = []
    for p in pairs:
        sl = slice(p * LANES, (p + 1) * LANES)
        kk2 = kk_raw[:, sl]
        kk2 = kk2 * lax.rsqrt(jnp.maximum(_seg_sum(kk2 * kk2, m0, m1), 1e-24)) * valid[:, sl]
        d = dict(sl=sl, r2=r[:, sl], v2=vr[:, sl], kp2=kp[:, sl], ge=g_end[:, sl])
        d['kh'] = kk2 * g_prev[:, sl]
        rh = d['r2'] * g_incl[:, sl]
        bt = kk2 * a[:, sl] * g_inv[:, sl]
        kt = d['kp2'] * g_inv[:, sl]
        d['b_end'] = bt * d['ge']
        d['k_end'] = kt * d['ge']
        d['kh_s'], d['rh_s'], d['v_s'] = stack(d['kh']), stack(rh), stack(d['v2'])
        gram = _mm_nt(jnp.concatenate([d['kh_s'], d['rh_s']], axis=0), jnp.concatenate([stack(bt), stack(kt)], axis=0))
        d['x'] = -jnp.where(strict, gram[:stacked, :stacked], 0.0)
        d['a_k'] = jnp.where(strict, gram[:stacked, stacked:], 0.0)
        d['r_b'] = jnp.where(incl, gram[stacked:, :stacked], 0.0)
        d['r_k'] = jnp.where(incl, gram[stacked:, stacked:], 0.0)
        d['t'] = eye + d['x']
        pre.append(d)
    cover = 2
    while cover < length:
        for d in pre:
            d['x'] = _mm(d['x'], d['x'])
            d['t'] = d['t'] + _mm(d['t'], d['x'])
        cover *= 2
    for d in pre:
        rhs0 = -_mm(d['a_k'], d['v_s'])
        if chain:
            tz = _mm(d['t'], jnp.concatenate([rhs0, d['kh_s']], axis=1))
            rbz = _mm(d['r_b'], tz)
            d['z0'] = unstack(tz[:, :LANES])
            d['q'] = unstack(d['rh_s'] - rbz[:, LANES:])
            d['y0'] = unstack(rbz[:, :LANES] + _mm(d['r_k'], d['v_s']))
            d['w'] = unstack(_mm_tn(d['t'], stack(d['b_end'])))
        else:
            d['z0'] = unstack(_mm(d['t'], rhs0))

    for p, d in zip(pairs, pre):
        sl, ge, kh, v2 = d['sl'], d['ge'], d['kh'], d['v2']
        chunk = lambda c: slice(c * length, (c + 1) * length)
        g_last = lambda c: ge[(c + 1) * length - 1:(c + 1) * length, :]
        if chain:
            state = pair_ref[0, p]
            ys = []
            for c in range(n_chunks):
                cr = chunk(c)
                ys.append(_mm_nt(d['q'][cr], state) + d['y0'][cr])
                n_c = (_mm_tn(d['z0'][cr], d['b_end'][cr]) + _mm_tn(v2[cr], d['k_end'][cr])) * bd
                kw = _mm_tn(kh[cr], d['w'][cr]) * bd
                state = state * g_last(c) - _mm(state, kw) + n_c
            pair_ref[0, p] = state
            y2 = jnp.concatenate(ys, axis=0) if n_chunks > 1 else ys[0]
        else:
            rh = unstack(d['rh_s'])
            ks, rs = [], []
            for c in range(n_chunks):
                cr = chunk(c)
                both = _mm_nt(jnp.concatenate([kh[cr], rh[cr]], axis=0), pair_ref[c, p])
                ks.append(both[:length])
                rs.append(both[length:])
            z = d['z0'] - unstack(_mm(d['t'], stack(jnp.concatenate(ks, axis=0))))
            y2 = jnp.concatenate(rs, axis=0) + unstack(_mm(d['r_b'], stack(z)) + _mm(d['r_k'], d['v_s']))
            for c in range(n_chunks):
                cr = chunk(c)
                update = _mm_tn(jnp.concatenate([z[cr], v2[cr]], axis=0),
                                jnp.concatenate([d['b_end'][cr], d['k_end'][cr]], axis=0)) * bd
                pair_ref[c, p] = pair_ref[c, p] * g_last(c) + update

        mu = _seg_mean(y2, m0, m1)
        yc = y2 - mu
        var = _seg_mean(yc * yc, m0, m1)
        yn = yc * lax.rsqrt(var + RWKV_GN_EPS) * gn_g[:, sl] + gn_b[:, sl]
        bonus = _seg_sum(d['r2'] * d['kp2'] * r_k[:, sl], m0, m1) * v2
        y_ref[:, pl.ds(p * LANES, LANES)] = (yn + bonus) * gate[:, sl]

    if chain:
        @pl.when(pl.program_id(1) == pl.num_programs(1) - 1)
        def _():
            _store_head_pairs(s_ref, pair_ref)
    else:
        _store_head_pairs(s_ref, pair_ref)


def _rwkv(rw, aux, valid, mix, vec, lora, tri, s0, layer, *, n_groups, n_steps, rows, length, chain):
    blk = lambda g, c: (g * n_steps + c, 0)
    const2 = lambda g, c: (0, 0)
    st = lambda g, c: (g, 0, 0, 0)
    n_state = 1 if chain else rows // length
    aux_spec = (pl.BlockSpec((1, 1, RWKV_PROJ), lambda g, c: (g, 0, 0)) if chain
                else pl.BlockSpec((rows, RWKV_PROJ), blk))
    return pl.pallas_call(
        functools.partial(_rwkv_kernel, rows=rows, length=length, chain=chain),
        grid=(n_groups, n_steps),
        in_specs=[pl.BlockSpec((rows, RWKV_PROJ), blk),
                  aux_spec,
                  pl.BlockSpec((rows, RWKV_W), const2),
                  pl.BlockSpec((1, RWKV_PROJ), const2),
                  pl.BlockSpec((SUBLANES, RWKV_W), const2),
                  pl.BlockSpec((3, LANES, RWKV_W), lambda g, c: (0, 0, 0)),
                  pl.BlockSpec((rows, rows), const2),
                  pl.BlockSpec((None, n_state, RWKV_H, HEAD_DIM, HEAD_DIM), lambda g, c: (layer, g, 0, 0, 0))],
        out_specs=[pl.BlockSpec((rows, RWKV_W), blk),
                   pl.BlockSpec((n_state, RWKV_H, HEAD_DIM, HEAD_DIM), st)]
        + ([pl.BlockSpec((1, 1, RWKV_PROJ), lambda g, c: (g, 0, 0))] if chain else []),
        out_shape=[jax.ShapeDtypeStruct((n_groups * n_steps * rows, RWKV_W), f32),
                   jax.ShapeDtypeStruct((n_groups * n_state, RWKV_H, HEAD_DIM, HEAD_DIM), f32)]
        + ([jax.ShapeDtypeStruct((n_groups, 1, RWKV_PROJ), f32)] if chain else []),
        scratch_shapes=([pltpu.VMEM((rows + SUBLANES, RWKV_PROJ), f32)] if chain else [])
        + [pltpu.VMEM((n_state, RWKV_H // 2, LANES, LANES), f32)],
        compiler_params=_params("arbitrary", "arbitrary"),
        name="rwkv7",
    )(rw, aux, valid, mix, vec, lora, tri, s0)


def _chunk_tri(rows, length):
    idx = jnp.arange(rows)
    same = (idx[:, None] // length) == (idx[None, :] // length)
    return jnp.logical_and(same, idx[None, :] <= idx[:, None]).astype(f32)


def _lru_kernel(lru_ref, valid_ref, vec_ref, wa_ref, wx_ref, conv0_ref, h0_ref,
                y_ref, h_ref, tail_ref, xext_ref, *, nb, length):
    @pl.when(pl.program_id(1) == 0)
    def _():
        h_ref[...] = h0_ref[...]
        xext_ref[:, pl.ds(0, SUBLANES), :] = conv0_ref[...]

    valid = valid_ref[...] > 0.5
    row = lax.broadcasted_iota(i32, (length, LRU_W), 0)
    cw = [vec_ref[pl.ds(i, 1), :] for i in range(CONV_WIDTH)]
    cb, ba, bx, lam = (vec_ref[pl.ds(i, 1), :] for i in range(CONV_WIDTH, CONV_WIDTH + 4))
    sp = _softplus(-lam)
    for j in range(nb):
        rows = pl.ds(j * length, length)
        gbr = lru_ref[rows, pl.ds(0, LRU_W)]
        x = lru_ref[rows, pl.ds(LRU_W, LRU_W)]
        xext_ref[j, pl.ds(SUBLANES, length), :] = x
        xc = cb + x * cw[CONV_WIDTH - 1]
        for t in range(CONV_WIDTH - 1):
            xc = xc + xext_ref[j, pl.ds(SUBLANES - (CONV_WIDTH - 1) + t, length), :] * cw[t]
        xext_ref[j, pl.ds(0, SUBLANES), :] = x[length - SUBLANES:, :]
        tail_ref[j] = x[length - SUBLANES:, :]
        gate_a = _sigmoid(_mm(xc, wa_ref[...]) + ba)
        gate_x = _sigmoid(_mm(xc, wx_ref[...]) + bx)
        log_a = -LRU_C * gate_a * sp
        a = jnp.exp(log_a)
        b = xc * gate_x * jnp.sqrt(1.0 - jnp.exp(2.0 * log_a))
        a = jnp.where(valid, a, 1.0)
        b = jnp.where(valid, b, 0.0)
        shift = 1
        while shift < length:
            inside = row >= shift
            b = a * jnp.where(inside, pltpu.roll(b, shift, 0), 0.0) + b
            a = a * jnp.where(inside, pltpu.roll(a, shift, 0), 1.0)
            shift *= 2
        hs = a * h_ref[j, pl.ds(0, 1), :] + b
        h_ref[j] = jnp.broadcast_to(hs[length - 1:length, :], (SUBLANES, LRU_W))
        c = 0.7978845608028654
        gelu = 0.5 * gbr * (1.0 + jnp.tanh(c * (gbr + 0.044715 * gbr * gbr * gbr)))
        y_ref[rows, :] = hs * gelu


def _lru(lru, valid, vec, wa_bd, wx_bd, conv0, h0, *, n_batch, n_chunks, nb, length, out_rows):
    blk = lambda g, c: (g * n_chunks + c, 0)
    const2 = lambda g, c: (0, 0)
    st = lambda g, c: (g, 0, 0)
    return pl.pallas_call(
        functools.partial(_lru_kernel, nb=nb, length=length),
        grid=(n_batch // nb, n_chunks),
        in_specs=[pl.BlockSpec((nb * length, 2 * LRU_W), blk),
                  pl.BlockSpec((length, LRU_W), const2),
                  pl.BlockSpec((SUBLANES, LRU_W), const2),
                  pl.BlockSpec((LRU_W, LRU_W), const2),
                  pl.BlockSpec((LRU_W, LRU_W), const2),
                  pl.BlockSpec((nb, SUBLANES, LRU_W), st),
                  pl.BlockSpec((nb, SUBLANES, LRU_W), st)],
        out_specs=[pl.BlockSpec((nb * length, LRU_W), blk),
                   pl.BlockSpec((nb, SUBLANES, LRU_W), st),
                   pl.BlockSpec((nb, SUBLANES, LRU_W), st)],
        out_shape=[jax.ShapeDtypeStruct((out_rows, LRU_W), f32),
                   jax.ShapeDtypeStruct((n_batch, SUBLANES, LRU_W), f32),
                   jax.ShapeDtypeStruct((n_batch, SUBLANES, LRU_W), f32)],
        scratch_shapes=[pltpu.VMEM((nb, length + SUBLANES, LRU_W), f32)],
        compiler_params=_params("arbitrary", "arbitrary"),
        name="rg_lru",
    )(lru, valid, vec, wa_bd, wx_bd, conv0, h0)


def _out_proj_kernel(yrp_ref, yrs_ref, ywp_ref, yws_ref, ylp_ref, yls_ref, xp_ref, xs_ref,
                     w_ref, ln_ref, wr_ref, br_ref,
                     x1_ref, lpos_ref, g_ref, meta_ref, cnt_ref, *, alpha, n_prompt_tiles):
    @pl.when(pl.program_id(0) == 0)
    def _():
        cnt_ref[...] = jnp.zeros(cnt_ref.shape, f32)

    pick = functools.partial(_pick_group, n_prompt_tiles=n_prompt_tiles)
    mixed = (_mm(pick(yrp_ref, yrs_ref), w_ref[pl.ds(0, RET_W), :])
             + _mm(pick(ywp_ref, yws_ref), w_ref[pl.ds(RET_W, RWKV_W), :])
             + _mm(pick(ylp_ref, yls_ref), w_ref[pl.ds(RET_W + RWKV_W, LRU_W), :]))
    x1 = _layer_norm_rows(alpha * pick(xp_ref, xs_ref) + mixed, ln_ref[pl.ds(0, 1), :], ln_ref[pl.ds(1, 1), :])
    x1_ref[...] = x1
    logits = _mm3(x1, wr_ref[...]) + br_ref[...]
    tm = logits.shape[0]
    lane = lax.broadcasted_iota(i32, logits.shape, 1).astype(f32)
    top_v = jnp.zeros(logits.shape, f32)
    work = logits
    v_max = None
    onehots = []
    for k in range(TOP_K):
        v = jnp.max(work, axis=-1, keepdims=True)
        idx = jnp.min(jnp.where(work == v, lane, float(LANES)), axis=-1, keepdims=True)
        if k == 0:
            v_max = v
        hit = lane == idx
        onehots.append(hit.astype(f32))
        top_v = jnp.where(lane == k, jnp.exp(v - v_max), top_v)
        work = jnp.where(hit, -jnp.inf, work)
    g_ref[...] = top_v / jnp.sum(top_v, axis=-1, keepdims=True)

    total = onehots[0] + onehots[1] + onehots[2] + onehots[3]
    ri = lax.broadcasted_iota(i32, (tm, tm), 0)
    ci = lax.broadcasted_iota(i32, (tm, tm), 1)
    before = _mm((ci < ri).astype(f32), total)
    groups = jnp.floor((jnp.sum(total, axis=0, keepdims=True) + (SUBLANES - 1.0)) * (1.0 / SUBLANES))
    er = lax.broadcasted_iota(i32, (LANES, LANES), 0)
    ec = lax.broadcasted_iota(i32, (LANES, LANES), 1)
    run_len = groups * SUBLANES
    run_start = _mm(jnp.broadcast_to(groups, (SUBLANES, LANES)), (er < ec).astype(f32))[0:1] * SUBLANES
    lpos = jnp.full(logits.shape, -1.0, f32)
    for k in range(TOP_K):
        pos = jnp.sum(onehots[k] * (before + run_start), axis=-1, keepdims=True)
        lpos = jnp.where(lane == k, pos, lpos)
    lpos_ref[...] = lpos
    row = lax.broadcasted_iota(i32, (SUBLANES, LANES), 0)
    meta = jnp.where(row == 0, run_start, jnp.where(row == 1, run_len, jnp.where(row == 2, cnt_ref[...], 0.0)))
    meta_ref[...] = meta.astype(i32)
    cnt_ref[...] = cnt_ref[...] + run_len


def _out_proj_router(y_ret, y_rwkv, y_lru, h, w_out_bf, ln, w_router, b_router, alpha):
    npt, nst = _n_tiles(*h)
    n = h[0].shape[0] + h[1].shape[0]
    tm = TOKEN_TILE
    row = lambda i: (i, 0)
    const = lambda i: (0, 0)
    return pl.pallas_call(
        functools.partial(_out_proj_kernel, alpha=alpha, n_prompt_tiles=npt),
        grid=(npt + nst,),
        in_specs=_group_specs(RET_W, npt) + _group_specs(RWKV_W, npt) + _group_specs(LRU_W, npt)
        + _group_specs(D_MODEL, npt) + [
                  pl.BlockSpec((D_MODEL, D_MODEL), const),
                  pl.BlockSpec((SUBLANES, D_MODEL), const),
                  pl.BlockSpec((D_MODEL, LANES), const),
                  pl.BlockSpec((1, LANES), const)],
        out_specs=[pl.BlockSpec((tm, D_MODEL), row), pl.BlockSpec((tm, LANES), row), pl.BlockSpec((tm, LANES), row),
                   pl.BlockSpec((SUBLANES, LANES), row), pl.BlockSpec((SUBLANES, LANES), const)],
        out_shape=[jax.ShapeDtypeStruct((n, D_MODEL), f32),
                   jax.ShapeDtypeStruct((n, LANES), f32),
                   jax.ShapeDtypeStruct((n, LANES), f32),
                   jax.ShapeDtypeStruct((n // tm * SUBLANES, LANES), i32),
                   jax.ShapeDtypeStruct((SUBLANES, LANES), f32)],
        compiler_params=_params("arbitrary"),
        name="out_proj_router",
    )(*y_ret, *y_rwkv, *y_lru, *h, w_out_bf, ln, w_router, b_router)


def _for_each_run_piece(meta_ref, pstart_ref, fn):
    for e in range(N_EXPERTS):
        start, n = meta_ref[0, e], meta_ref[1, e]
        base = pstart_ref[e] + meta_ref[2, e]
        for sz in RUN_PIECES:
            done = n & ~(2 * sz - 1)

            @pl.when((n & sz) != 0)
            def _():
                fn(pl.multiple_of(start + done, SUBLANES), pl.multiple_of(base + done, SUBLANES), sz, e % 2)


def _wait_run_rows(meta_ref, make_copy):
    total = meta_ref[1, 0]
    for e in range(1, N_EXPERTS):
        total = total + meta_ref[1, e]
    for sz in WAIT_PIECES:
        @pl.when((total & sz) != 0)
        def _():
            make_copy(sz).wait()


def _dispatch_kernel(pstart_ref, cnt_ref, n_used_ref, meta_ref, meta_prev_ref, lpos_ref, x1_ref, xs_hbm,
                     xl_ref, zero_ref, sems, zsem):
    tm = TOKEN_TILE
    bm = MOE_BLOCK
    i = pl.program_id(0)
    slot = i % 2

    @pl.when(i == 0)
    def _():
        zero_ref[...] = jnp.zeros(zero_ref.shape, u32)
        tail = lambda j: pltpu.make_async_copy(zero_ref, xs_hbm.at[pl.ds(j * bm, bm), :], zsem)

        def tail_start(j, carry):
            tail(j).start()
            return carry

        def tail_wait(j, carry):
            tail(j).wait()
            return carry

        n_blocks = xs_hbm.shape[0] // bm
        lax.fori_loop(n_used_ref[0], n_blocks, tail_start, 0)
        lax.fori_loop(n_used_ref[0], n_blocks, tail_wait, 0)
        for e in range(N_EXPERTS):
            lo = pstart_ref[e] + cnt_ref[e]
            n_groups = ((cnt_ref[e] + bm - 1) // bm * bm - cnt_ref[e]) // SUBLANES
            fill = lambda g: pltpu.make_async_copy(
                zero_ref.at[pl.ds(0, SUBLANES), :],
                xs_hbm.at[pl.ds(pl.multiple_of(lo + g * SUBLANES, SUBLANES), SUBLANES), :], zsem)

            def start(g, carry):
                fill(g).start()
                return carry

            def wait(g, carry):
                fill(g).wait()
                return carry

            lax.fori_loop(0, n_groups, start, 0)
            lax.fori_loop(0, n_groups, wait, 0)

    lpos_t = lpos_ref[...].T
    srow = lax.broadcasted_iota(i32, (LOCAL_ROWS, tm), 0).astype(f32)
    perm = jnp.zeros((LOCAL_ROWS, tm), f32)
    for k in range(TOP_K):
        perm = jnp.where(srow == lpos_t[k:k + 1, :], 1.0, perm)
    xl_ref[slot] = _pack_halves(jnp.dot(perm.astype(bf16), x1_ref[...].astype(bf16), preferred_element_type=f32))

    def send(local_row, sorted_row, n_rows, priority):
        pltpu.make_async_copy(xl_ref.at[slot, pl.ds(local_row, n_rows), :],
                              xs_hbm.at[pl.ds(sorted_row, n_rows), :], sems.at[slot]).start(priority)

    _for_each_run_piece(meta_ref, pstart_ref, send)

    def sent(s):
        return lambda n_rows: pltpu.make_async_copy(xl_ref.at[s, pl.ds(0, n_rows), :],
                                                    xs_hbm.at[pl.ds(0, n_rows), :], sems.at[s])

    @pl.when(i > 0)
    def _():
        _wait_run_rows(meta_prev_ref, sent(1 - slot))

    @pl.when(i == pl.num_programs(0) - 1)
    def _():
        _wait_run_rows(meta_ref, sent(slot))


def _dispatch(pstart, counts, n_used, meta, lpos, x1, n_rows):
    n = x1.shape[0]
    tm = TOKEN_TILE
    smem_tile = lambda f: pl.BlockSpec((SUBLANES, LANES), f, memory_space=pltpu.SMEM)
    grid_spec = pltpu.PrefetchScalarGridSpec(
        num_scalar_prefetch=3,
        grid=(n // tm,),
        in_specs=[smem_tile(lambda i, ps, ct, nu: (i, 0)),
                  smem_tile(lambda i, ps, ct, nu: (jnp.maximum(i - 1, 0), 0)),
                  pl.BlockSpec((tm, LANES), lambda i, ps, ct, nu: (i, 0)),
                  pl.BlockSpec((tm, D_MODEL), lambda i, ps, ct, nu: (i, 0))],
        out_specs=pl.BlockSpec(memory_space=pl.ANY),
        scratch_shapes=[pltpu.VMEM((2, LOCAL_ROWS, D_MODEL // 2), u32),
                        pltpu.VMEM((MOE_BLOCK, D_MODEL // 2), u32),
                        pltpu.SemaphoreType.DMA((2,)), pltpu.SemaphoreType.DMA(())],
    )
    return pl.pallas_call(
        _dispatch_kernel,
        grid_spec=grid_spec,
        out_shape=jax.ShapeDtypeStruct((n_rows, D_MODEL // 2), u32),
        compiler_params=_params("arbitrary"),
        name="moe_dispatch",
    )(pstart, counts, n_used, meta, meta, lpos, x1)


def _expert_kernel(blk_e_ref, n_used_ref, next_e_ref, rows_ref, xs_ref, wgu_hbm, bgu_ref, wdn_hbm, bdn_ref,
                   out_ref, wgu_f32, wdn_f32, wgu_bf, wdn_bf, sems, *, layer):
    i = pl.program_id(0)
    prev = jnp.maximum(i - 1, 0)
    expert = blk_e_ref[i]
    new_expert = jnp.logical_or(i == 0, expert != blk_e_ref[prev])
    used = i < n_used_ref[0]

    def fetch(e):
        return (pltpu.make_async_copy(wgu_hbm.at[layer, e], wgu_f32, sems.at[0]),
                pltpu.make_async_copy(wdn_hbm.at[layer, e], wdn_f32, sems.at[1]))

    @pl.when(used)
    def _():
        @pl.when(new_expert)
        def _():
            @pl.when(i == 0)
            def _():
                for c in fetch(expert):
                    c.start()

            for c in fetch(expert):
                c.wait()
            wgu_bf[...] = wgu_f32[...].astype(bf16)
            wdn_bf[...] = wdn_f32[...].astype(bf16)
            nxt = next_e_ref[expert]

            @pl.when(nxt >= 0)
            def _():
                for c in fetch(nxt):
                    c.start()

        def run(n_rows):
            rows = pl.ds(0, n_rows)
            x_lo, x_hi = _unpack_halves(xs_ref[rows, :])
            half = D_MODEL // 2
            gu = (jnp.dot(x_lo.astype(bf16), wgu_bf[pl.ds(0, half), :], preferred_element_type=f32)
                  + jnp.dot(x_hi.astype(bf16), wgu_bf[pl.ds(half, half), :], preferred_element_type=f32)
                  + bgu_ref[0, 0])
            g = jnp.minimum(gu[:, :D_EXPERT], SWIGLU_LIMIT)
            u = jnp.clip(gu[:, D_EXPERT:], -SWIGLU_LIMIT, SWIGLU_LIMIT)
            hdn = (u + 1.0) * g * _sigmoid(SWIGLU_ALPHA * g)
            y = jnp.dot(hdn.astype(bf16), wdn_bf[...], preferred_element_type=f32) + bdn_ref[0, 0]
            out_ref[rows, :] = _pack_halves(y.astype(bf16).astype(f32))
            if n_rows < MOE_BLOCK:
                out_ref[pl.ds(n_rows, MOE_BLOCK - n_rows), :] = jnp.zeros((MOE_BLOCK - n_rows, D_MODEL // 2), u32)

        @pl.when(rows_ref[i] > MOE_BLOCK // 2)
        def _():
            run(MOE_BLOCK)

        @pl.when(rows_ref[i] <= MOE_BLOCK // 2)
        def _():
            run(MOE_BLOCK // 2)

    @pl.when(jnp.logical_not(used))
    def _():
        out_ref[...] = jnp.zeros(out_ref.shape, u32)


def _experts(blk_e, n_used, next_e, blk_rows, xs, w_gu, b_gu, w_down, b_down, layer):
    n_blocks = blk_e.shape[0]
    bm = MOE_BLOCK
    by_e = lambda i, be, nu, ne, br: (layer, be[i], 0, 0)
    x_blk = lambda i, be, nu, ne, br: (jnp.minimum(i, nu[0] - 1), 0)
    grid_spec = pltpu.PrefetchScalarGridSpec(
        num_scalar_prefetch=4,
        grid=(n_blocks,),
        in_specs=[pl.BlockSpec((bm, D_MODEL // 2), x_blk),
                  pl.BlockSpec(memory_space=pl.ANY),
                  pl.BlockSpec((1, 1, 1, 2 * D_EXPERT), by_e),
                  pl.BlockSpec(memory_space=pl.ANY),
                  pl.BlockSpec((1, 1, 1, D_MODEL), by_e)],
        out_specs=pl.BlockSpec((bm, D_MODEL // 2), lambda i, be, nu, ne, br: (i, 0)),
        scratch_shapes=[pltpu.VMEM((D_MODEL, 2 * D_EXPERT), f32),
                        pltpu.VMEM((D_EXPERT, D_MODEL), f32),
                        pltpu.VMEM((D_MODEL, 2 * D_EXPERT), bf16),
                        pltpu.VMEM((D_EXPERT, D_MODEL), bf16),
                        pltpu.SemaphoreType.DMA((2,))],
    )
    depth = w_gu.shape[0]
    return pl.pallas_call(
        functools.partial(_expert_kernel, layer=layer),
        grid_spec=grid_spec,
        out_shape=jax.ShapeDtypeStruct((n_blocks * bm, D_MODEL // 2), u32),
        compiler_params=_params("arbitrary"),
        name="moe_experts",
    )(blk_e, n_used, next_e, blk_rows, xs, w_gu, b_gu.reshape(depth, N_EXPERTS, 1, 2 * D_EXPERT),
      w_down, b_down.reshape(depth, N_EXPERTS, 1, D_MODEL))


def _combine_kernel(pstart_ref, meta_ref, meta_next_ref, lpos_ref, gates_ref, x1_ref, ln_ref, yb_hbm,
                    outp_ref, outs_ref, yl_ref, sems, *, alpha, n_prompt_tiles):
    tm = TOKEN_TILE
    i = pl.program_id(0)
    slot = i % 2

    def fetch(meta, s):
        def recv(local_row, sorted_row, n_rows, priority):
            pltpu.make_async_copy(yb_hbm.at[pl.ds(sorted_row, n_rows), :],
                                  yl_ref.at[s, pl.ds(local_row, n_rows), :], sems.at[s]).start(priority)

        _for_each_run_piece(meta, pstart_ref, recv)

    @pl.when(i == 0)
    def _():
        yl_ref[...] = jnp.zeros(yl_ref.shape, u32)
        fetch(meta_ref, slot)

    @pl.when(i + 1 < pl.num_programs(0))
    def _():
        fetch(meta_next_ref, 1 - slot)

    _wait_run_rows(meta_ref, lambda n_rows: pltpu.make_async_copy(
        yb_hbm.at[pl.ds(0, n_rows), :], yl_ref.at[slot, pl.ds(0, n_rows), :], sems.at[slot]))

    lpos = lpos_ref[...]
    gates = gates_ref[...]
    scol = lax.broadcasted_iota(i32, (tm, LOCAL_ROWS), 1).astype(f32)
    weight = jnp.zeros((tm, LOCAL_ROWS), f32)
    for k in range(TOP_K):
        weight = jnp.where(scol == lpos[:, k:k + 1], gates[:, k:k + 1], weight)
    y_lo, y_hi = _unpack_halves(yl_ref[slot])
    weight = weight.astype(bf16)
    y = jnp.concatenate([jnp.dot(weight, y_lo.astype(bf16), preferred_element_type=f32),
                         jnp.dot(weight, y_hi.astype(bf16), preferred_element_type=f32)], axis=1)
    x2 = _layer_norm_rows(alpha * x1_ref[...] + y, ln_ref[pl.ds(0, 1), :], ln_ref[pl.ds(1, 1), :])
    _store_group(outp_ref, outs_ref, x2, n_prompt_tiles)


def _combine(pstart, meta, lpos, gates, x1, ln, yb, alpha, n_prompt_tiles):
    n = x1.shape[0]
    tm = TOKEN_TILE
    n_tiles = n // tm
    row = lambda i, ps: (i, 0)
    smem_tile = lambda f: pl.BlockSpec((SUBLANES, LANES), f, memory_space=pltpu.SMEM)
    grid_spec = pltpu.PrefetchScalarGridSpec(
        num_scalar_prefetch=1,
        grid=(n_tiles,),
        in_specs=[smem_tile(row),
                  smem_tile(lambda i, ps: (jnp.minimum(i + 1, n_tiles - 1), 0)),
                  pl.BlockSpec((tm, LANES), row),
                  pl.BlockSpec((tm, LANES), row),
                  pl.BlockSpec((tm, D_MODEL), row),
                  pl.BlockSpec((SUBLANES, D_MODEL), lambda i, ps: (0, 0)),
                  pl.BlockSpec(memory_space=pl.ANY)],
        out_specs=_group_specs(D_MODEL, n_prompt_tiles),
        scratch_shapes=[pltpu.VMEM((2, LOCAL_ROWS, D_MODEL // 2), u32), pltpu.SemaphoreType.DMA((2,))],
    )
    return pl.pallas_call(
        functools.partial(_combine_kernel, alpha=alpha, n_prompt_tiles=n_prompt_tiles),
        grid_spec=grid_spec,
        out_shape=[jax.ShapeDtypeStruct((n_prompt_tiles * tm, D_MODEL), f32),
                   jax.ShapeDtypeStruct((n - n_prompt_tiles * tm, D_MODEL), f32)],
        compiler_params=_params("arbitrary"),
        name="moe_combine",
    )(pstart, meta, meta, lpos, gates, x1, ln, yb)


def _block_tables(counts, n_tokens):
    bm = MOE_BLOCK
    padded = (counts + bm - 1) // bm * bm
    pad_end = jnp.cumsum(padded)
    pstart = (pad_end - padded).astype(i32)
    max_used = n_tokens * TOP_K + (n_tokens // TOKEN_TILE) * N_EXPERTS * (SUBLANES - 1)
    n_blocks = -(-(max_used + N_EXPERTS * (bm - 1)) // bm)
    first_row = jnp.arange(n_blocks, dtype=pad_end.dtype) * bm
    blk_e = jnp.minimum(jnp.sum(pad_end[None, :] <= first_row[:, None], axis=1), N_EXPERTS - 1).astype(i32)
    n_used = (pad_end[-1] // bm).astype(i32).reshape(1)
    idx = jnp.arange(N_EXPERTS)
    later = jnp.logical_and(idx[None, :] > idx[:, None], counts[None, :] > 0)
    next_e = jnp.min(jnp.where(later, idx[None, :], N_EXPERTS), axis=1)
    next_e = jnp.where(next_e == N_EXPERTS, -1, next_e).astype(i32)
    used_end = pstart + counts
    blk_rows = jnp.sum(jnp.where(idx[None, :] == blk_e[:, None], used_end[None, :], 0), axis=1) - first_row
    blk_rows = jnp.clip(blk_rows, 0, bm).astype(i32)
    return pstart, blk_e, n_used, next_e, blk_rows, n_blocks * bm


def _pad_time(t, n_batch, n_t, t_pad):
    w = t.shape[-1]
    return jnp.pad(t.reshape(n_batch, n_t, w), ((0, 0), (0, t_pad - n_t), (0, 0))).reshape(n_batch * t_pad, w)


def _block_diag_weight(w):
    h = w.shape[0]
    eye = jnp.eye(h, dtype=w.dtype)
    return (eye[:, None, :, None] * w[:, :, None, :]).reshape(h * HEAD_DIM, h * HEAD_DIM)


def _rows8(*rows):
    width = rows[0].shape[-1]
    m = jnp.stack([r.reshape(width) for r in rows])
    return jnp.pad(m, ((0, SUBLANES - m.shape[0]), (0, 0)))


def _layer(h, p, moe, layer, alpha, st_s, bp, tp, bs, ts):
    n_p, n_s = bp * tp, bs * ts
    tpad = SAMPLE_T_PAD
    (ret_p, ret_s), (rw_p, rw_s), (lru_p, lru_s) = _in_proj(*h, p['w_in'].astype(bf16))
    unpad = lambda y: y.reshape(bs, tpad, -1)[:, :ts].reshape(n_s, -1)

    c_p = RET_CHUNK if tp % RET_CHUNK == 0 else tp
    cos_p, sin_p = _rope_tables(jnp.arange(tp, dtype=f32))
    cos_s, sin_s = _rope_tables(PAST_LEN + jnp.arange(tpad, dtype=f32))
    y_ret_p, sret_p = _retention(ret_p.reshape(bp, tp, 4 * RET_W), cos_p, sin_p, _retention_tables(c_p, c_p),
                                 jnp.zeros((1, bp, RET_H, HEAD_DIM, HEAD_DIM), f32), 0,
                                 nb=max(d for d in (1, 2, 4, 8) if bp % d == 0), length=c_p)
    y_ret_s, sret_s = _retention(_pad_time(ret_s, bs, ts, tpad).reshape(bs, tpad, 4 * RET_W), cos_s, sin_s,
                                 _retention_tables(tpad, ts), st_s['ret'], layer, nb=SUBLANES, length=tpad)
    y_ret = (y_ret_p.reshape(n_p, RET_W), unpad(y_ret_s))

    lora = jnp.zeros((3, LANES, RWKV_W), f32)
    lora = lora.at[0, 0:32].set(p['w_up']).at[1, 32:64].set(p['a_up']).at[2, 64:128].set(p['g_up'])
    vec = _rows8(p['w0'], p['a0'], p['k_k'], p['k_a'], p['r_k'], p['gn_g'], p['gn_b'])
    mix = p['mix'].reshape(1, RWKV_PROJ)
    rows = RWKV_ROWS
    y_rwkv_p, srw_p, shift_p = _rwkv(rw_p, jnp.zeros((bp, 1, RWKV_PROJ), f32), jnp.ones((rows, RWKV_W), f32), mix,
                                     vec, lora, _chunk_tri(rows, RWKV_CHUNK),
                                     jnp.zeros((1, bp, RWKV_H, HEAD_DIM, HEAD_DIM), f32), 0,
                                     n_groups=bp, n_steps=tp // rows, rows=rows, length=RWKV_CHUNK, chain=True)
    valid_s = (jnp.arange(tpad) < ts).astype(f32)[:, None]
    rw_s3 = rw_s.reshape(bs, ts, RWKV_PROJ)
    prev_s = jnp.concatenate([st_s['shift'][:, None, :], rw_s3[:, :-1]], axis=1).reshape(n_s, RWKV_PROJ)
    seq_per_blk = rows // tpad
    y_rwkv_s, srw_s = _rwkv(_pad_time(rw_s, bs, ts, tpad), _pad_time(prev_s, bs, ts, tpad),
                            jnp.tile(jnp.broadcast_to(valid_s, (tpad, RWKV_W)), (seq_per_blk, 1)),
                            mix, vec, lora, _chunk_tri(rows, tpad), st_s['rwkv'], layer,
                            n_groups=bs // seq_per_blk, n_steps=1, rows=rows, length=tpad, chain=False)
    y_rwkv = (y_rwkv_p, unpad(y_rwkv_s))

    lvec = _rows8(p['conv_w'][0], p['conv_w'][1], p['conv_w'][2], p['conv_w'][3],
                  p['conv_b'], p['ba'], p['bx'], p['lam'])
    wa_bd = _block_diag_weight(p['wa']).astype(bf16)
    wx_bd = _block_diag_weight(p['wx']).astype(bf16)
    l_l = LRU_CHUNK if tp % LRU_CHUNK == 0 else tp
    y_lru_p, h_p, tail_p = _lru(lru_p, jnp.ones((l_l, LRU_W), f32), lvec, wa_bd, wx_bd,
                                jnp.zeros((bp, SUBLANES, LRU_W), f32), jnp.zeros((bp, SUBLANES, LRU_W), f32),
                                n_batch=bp, n_chunks=tp // l_l, nb=1, length=l_l, out_rows=n_p)
    conv0_s = jnp.pad(st_s['conv'], ((0, 0), (SUBLANES - (CONV_WIDTH - 1), 0), (0, 0)))
    h0_s = jnp.broadcast_to(st_s['lru'][:, None, :], (bs, SUBLANES, LRU_W))
    y_lru_s, h_s, _ = _lru(_pad_time(lru_s, bs, ts, tpad), jnp.broadcast_to(valid_s, (tpad, LRU_W)),
                           lvec, wa_bd, wx_bd, conv0_s, h0_s,
                           n_batch=bs, n_chunks=1, nb=SUBLANES, length=tpad, out_rows=bs * tpad)
    y_lru = (y_lru_p, unpad(y_lru_s))

    w_router = jnp.pad(p['w_router'], ((0, 0), (0, LANES - N_EXPERTS)))
    b_router = jnp.pad(p['b_router'], (0, LANES - N_EXPERTS), constant_values=-1e30).reshape(1, LANES)
    x1, lpos, gates, meta, cnt = _out_proj_router(y_ret, y_rwkv, y_lru, h, p['w_out'].astype(bf16),
                                                  _rows8(p['ln1_g'], p['ln1_b']), w_router, b_router, alpha)

    counts = cnt[0, :N_EXPERTS].astype(i32)
    pstart, blk_e, n_used, next_e, blk_rows, n_rows = _block_tables(counts, n_p + n_s)
    xs = _dispatch(pstart, counts, n_used, meta, lpos, x1, n_rows)
    yb = _experts(blk_e, n_used, next_e, blk_rows, xs, moe['w_gu'], moe['b_gu'], moe['w_down'], moe['b_down'],
                  layer)
    x2 = _combine(pstart, meta, lpos, gates, x1, _rows8(p['ln2_g'], p['ln2_b']), yb, alpha, n_p // TOKEN_TILE)

    keep = CONV_WIDTH - 1
    assert tp >= SUBLANES and ts >= keep
    conv_s = lru_s.reshape(bs, ts, 2 * LRU_W)[:, ts - keep:, LRU_W:]
    new_p = (sret_p, srw_p, shift_p[:, 0], h_p[:, 0], tail_p[:, SUBLANES - keep:])
    new_s = (sret_s, srw_s, rw_s3[:, -1], h_s[:, 0], conv_s)
    return x2, new_p, new_s


def kernel(x_prompt, x_sample, state_ret, state_rwkv, state_rwkv_shift, state_lru, state_conv,
           w_in, w_out, ln1_g, ln1_b, ln2_g, ln2_b,
           rwkv_mix, rwkv_w0, rwkv_w_up, rwkv_a0, rwkv_a_up, rwkv_g_up, rwkv_k_k, rwkv_k_a, rwkv_r_k,
           rwkv_gn_g, rwkv_gn_b, lru_conv_w, lru_conv_b, lru_wa, lru_ba, lru_wx, lru_bx, lru_lambda,
           moe_w_router, moe_b_router, moe_w_gate_up, moe_b_gate_up, moe_w_down, moe_b_down):
    bp, tp, _ = x_prompt.shape
    bs, ts, _ = x_sample.shape
    depth = w_in.shape[0]
    alpha = (2.0 * depth) ** 0.25
    moe = {'w_gu': moe_w_gate_up, 'b_gu': moe_b_gate_up, 'w_down': moe_w_down, 'b_down': moe_b_down}
    h = (x_prompt.reshape(bp * tp, D_MODEL), x_sample.reshape(bs * ts, D_MODEL))
    new_p, new_s = [], []
    for l in range(depth):
        p = {'w_in': w_in[l], 'w_out': w_out[l], 'ln1_g': ln1_g[l], 'ln1_b': ln1_b[l],
             'ln2_g': ln2_g[l], 'ln2_b': ln2_b[l], 'mix': rwkv_mix[l], 'w0': rwkv_w0[l],
             'w_up': rwkv_w_up[l], 'a0': rwkv_a0[l], 'a_up': rwkv_a_up[l], 'g_up': rwkv_g_up[l],
             'k_k': rwkv_k_k[l], 'k_a': rwkv_k_a[l], 'r_k': rwkv_r_k[l], 'gn_g': rwkv_gn_g[l],
             'gn_b': rwkv_gn_b[l], 'conv_w': lru_conv_w[l], 'conv_b': lru_conv_b[l], 'wa': lru_wa[l],
             'ba': lru_ba[l], 'wx': lru_wx[l], 'bx': lru_bx[l], 'lam': lru_lambda[l],
             'w_router': moe_w_router[l], 'b_router': moe_b_router[l]}
        st_s = {'ret': state_ret, 'rwkv': state_rwkv, 'shift': state_rwkv_shift[l],
                'lru': state_lru[l], 'conv': state_conv[l]}
        h, sp, ss = _layer(h, p, moe, l, alpha, st_s, bp, tp, bs, ts)
        new_p.append(sp)
        new_s.append(ss)
    outs = [h[0].reshape(bp, tp, D_MODEL), h[1].reshape(bs, ts, D_MODEL)]
    for i in range(5):
        outs.append(jnp.stack([s[i] for s in new_p]))
        outs.append(jnp.stack([s[i] for s in new_s]))
    return tuple(outs)
```
